```python
import jax, jax.numpy as jnp
from jax import lax
import numpy as np

D_MODEL = 2048
BATCH = 2
SEQ = 4096
DEPTH = 2
DEC_BATCH = 8
DEC_SEQ = 1
PAST_LEN = 16384
PAGE_SIZE = 128

HEAD_DIM = 128
N_HEADS = D_MODEL // HEAD_DIM
H_HGRN = N_HEADS // 2
H_FOX = N_HEADS - H_HGRN
HGRN_DK = 128
HGRN_CHUNK = 64
FOX_QBLK = 128
H_NSA = N_HEADS
NSA_GROUPS = 4
NSA_CMP_STRIDE = 16
NSA_CMP_LEN = 2 * NSA_CMP_STRIDE
NSA_SEL_LEN = 64
NSA_TOPN = 16
NSA_WINDOW = 512
NSA_QBLK = 32
D_FF = 5632
N_EXPERTS = 8
TOP_K = 2
D_FF_EXPERT = 2816
ROPE_THETA = 10000.0
EPS = 1e-6
N_EVEN = (DEPTH + 1) // 2
N_ODD = DEPTH // 2
A_WIDTH = H_HGRN * HGRN_DK
A_VWIDTH = H_HGRN * HEAD_DIM
B_WIDTH = H_FOX * HEAD_DIM
KV_WIDTH = NSA_GROUPS * HEAD_DIM
EVEN_SPLITS = (A_WIDTH, A_WIDTH, A_VWIDTH, A_VWIDTH, B_WIDTH, B_WIDTH, B_WIDTH, H_FOX)
ODD_SPLITS = (H_NSA * HEAD_DIM,) + (KV_WIDTH,) * 6 + (3 * H_NSA,)
EVEN_MIX = A_VWIDTH + B_WIDTH
ODD_MIX = H_NSA * HEAD_DIM

kernel_name = 'hybrid_hgrn2_fox_nsa_decoder_step'


def rmsnorm(x, g):
    xf = x.astype(jnp.float32)
    y = xf * lax.rsqrt(jnp.mean(xf * xf, axis=-1, keepdims=True) + EPS)
    return (y * g.astype(jnp.float32)).astype(x.dtype)


def rope(x, pos):
    half = HEAD_DIM // 2
    inv = ROPE_THETA ** (-jnp.arange(half, dtype=jnp.float32) / half)
    ang = pos.astype(jnp.float32)[:, None] * inv[None, :]
    cos, sin = jnp.cos(ang)[:, None, :], jnp.sin(ang)[:, None, :]
    xf = x.astype(jnp.float32)
    x1, x2 = xf[..., :half], xf[..., half:]
    return jnp.concatenate([x1 * cos - x2 * sin, x2 * cos + x1 * sin], axis=-1).astype(x.dtype)


def split_cols(h, widths):
    return jnp.split(h, np.cumsum(widths)[:-1].tolist(), axis=-1)


def masked_softmax(s, mask):
    s = jnp.where(mask, s.astype(jnp.float32), -jnp.inf)
    m = jnp.max(s, axis=-1, keepdims=True)
    m = jnp.where(jnp.isfinite(m), m, 0.0)
    e = jnp.where(mask, jnp.exp(s - m), 0.0)
    return e / jnp.maximum(jnp.sum(e, axis=-1, keepdims=True), 1.0)


def swiglu(x, w_gate, w_up, w_down):
    return (jax.nn.silu(x @ w_gate) * (x @ w_up)) @ w_down


def moe_swiglu(x, router_w, router_b, w_gate, w_up, w_down):
    B, S, D = x.shape
    xt = x.reshape(B * S, D)
    logits = xt.astype(jnp.float32) @ router_w.astype(jnp.float32) + router_b.astype(jnp.float32)
    top_logit, top_idx = lax.top_k(logits, TOP_K)
    weights = jax.nn.softmax(top_logit, axis=-1)
    gate = jnp.einsum('nk,nke->ne', weights, jax.nn.one_hot(top_idx, N_EXPERTS, dtype=jnp.float32))
    y = jnp.zeros_like(xt)
    for e in range(N_EXPERTS):
        y = y + gate[:, e:e + 1].astype(x.dtype) * swiglu(xt, w_gate[e], w_up[e], w_down[e])
    return y.reshape(B, S, D)


def hgrn2_chunk(state, inputs):
    q, k, v, logf = inputs
    C = q.shape[1]
    b = jnp.cumsum(logf, axis=1)
    causal = jnp.tril(jnp.ones((C, C), dtype=bool))[None, :, :, None, None]
    decay = jnp.exp(jnp.where(causal, b[:, :, None] - b[:, None, :], -jnp.inf))
    scores = jnp.einsum('bthk,bshk,btshk->bhts', q, k, decay)
    o = jnp.einsum('bhts,bshv->bthv', scores, v) + jnp.einsum('bthk,bhkv->bthv', q * jnp.exp(b), state)
    b_end = b[:, -1]
    k_end = k * jnp.exp(b_end[:, None] - b)
    new_state = state * jnp.exp(b_end)[..., None] + jnp.einsum('bshk,bshv->bhkv', k_end, v)
    return new_state, o


def hgrn2_prompt(q, k, v, logf):
    B, S, H, K = q.shape
    nc = S // HGRN_CHUNK
    chunks = lambda t: jnp.moveaxis(t.reshape((B, nc, HGRN_CHUNK) + t.shape[2:]), 1, 0)
    s0 = jnp.zeros((B, H, K, v.shape[-1]), jnp.float32)
    s_fin, o = lax.scan(hgrn2_chunk, s0, (chunks(q), chunks(k), chunks(v), chunks(logf)))
    return s_fin, jnp.moveaxis(o, 0, 1).reshape(B, S, H, v.shape[-1])


def fox_attend(q, qpos, cq, k, v, kpos, ck):
    s = jnp.einsum('bqhd,bkhd->bhqk', q, k).astype(jnp.float32) * (HEAD_DIM ** -0.5)
    s = s + jnp.swapaxes(cq, 1, 2)[..., :, None] - jnp.swapaxes(ck, 1, 2)[..., None, :]
    p = masked_softmax(s, kpos[None, :] <= qpos[:, None])
    return jnp.einsum('bhqk,bkhd->bqhd', p.astype(v.dtype), v)


def fox_prompt(q, k, v, logf):
    B, S = q.shape[:2]
    c = jnp.cumsum(logf, axis=1)
    nb = S // FOX_QBLK
    kpos = jnp.arange(S)
    blocks = lambda t: jnp.moveaxis(t.reshape((B, nb, FOX_QBLK) + t.shape[2:]), 1, 0)

    def one(a):
        qb, cb, pb = a
        return fox_attend(qb, pb, cb, k, v, kpos, c)

    o = lax.map(one, (blocks(q), blocks(c), kpos.reshape(nb, FOX_QBLK)))
    return jnp.moveaxis(o, 0, 1).reshape(q.shape)


def fox_sample(page_table, q, k, v, logf, c_k, c_v, c_lf, qpos):
    T = q.shape[1]
    past = page_table.shape[1] * c_k.shape[1]
    kpos = jnp.arange(past + T)

    def one(a):
        pt, q1, k1, v1, lf1 = a
        kk = jnp.concatenate([c_k[pt].reshape(past, H_FOX, HEAD_DIM), k1], axis=0)
        vv = jnp.concatenate([c_v[pt].reshape(past, H_FOX, HEAD_DIM), v1], axis=0)
        lf = jnp.concatenate([c_lf[pt].reshape(past, H_FOX).astype(jnp.float32), lf1], axis=0)
        c = jnp.cumsum(lf, axis=0)
        return fox_attend(q1[None], qpos, c[past:][None], kk[None], vv[None], kpos, c[None])[0]

    return lax.map(one, (page_table, q, k, v, logf))


def nsa_compress(rows, pe, w1, w2):
    B, L, G, D = rows.shape
    n = L // NSA_CMP_STRIDE
    r = rows.reshape(B, n, NSA_CMP_STRIDE, G, D)
    blocks = jnp.concatenate([r[:, :-1], r[:, 1:]], axis=2) + pe[:, None, :]
    flat = blocks.transpose(0, 1, 3, 2, 4).reshape(B, n - 1, G, NSA_CMP_LEN * D)
    return jax.nn.silu(flat @ w1) @ w2


def sel_blocks(rows):
    B, L, G, D = rows.shape
    return rows.reshape(B, L // NSA_SEL_LEN, NSA_SEL_LEN, G, D).transpose(0, 3, 1, 2, 4)


def block_overlap(n_cmp, n_sel):
    cs = jnp.arange(n_cmp)[:, None] * NSA_CMP_STRIDE
    ss = jnp.arange(n_sel)[None, :] * NSA_SEL_LEN
    ov = jnp.minimum(cs + NSA_CMP_LEN, ss + NSA_SEL_LEN) - jnp.maximum(cs, ss)
    return jnp.maximum(ov, 0).astype(jnp.float32) / NSA_CMP_LEN


def nsa_attend(qn, qr, qpos, gates, kc, vc, cend, kb, vb, kw, vw, wpos):
    B, Q = qn.shape[:2]
    G, R = NSA_GROUPS, H_NSA // NSA_GROUPS
    n_sel = kb.shape[2]
    scale = HEAD_DIM ** -0.5
    qn = qn.reshape(B, Q, G, R, HEAD_DIM)
    qr = qr.reshape(B, Q, G, R, HEAD_DIM)
    s = jnp.einsum('bqgrd,bcgd->bgrqc', qn, kc) * scale
    p_cmp = masked_softmax(s, cend[None, :] <= qpos[:, None])
    o_cmp = jnp.einsum('bgrqc,bcgd->bqgrd', p_cmp.astype(vc.dtype), vc)
    score = jnp.einsum('bgrqc,cn->bgqn', p_cmp, block_overlap(kc.shape[1], n_sel))
    blk = jnp.arange(n_sel)[None, :]
    cur = (qpos // NSA_SEL_LEN)[:, None]
    valid = blk * NSA_SEL_LEN <= qpos[:, None]
    forced = (blk == 0) | (blk == cur) | (blk == cur - 1)
    score = jnp.where(valid, jnp.where(forced, jnp.inf, score), -jnp.inf)
    _, idx = lax.top_k(score, min(NSA_TOPN, n_sel))
    bi = jnp.arange(B)[:, None, None, None]
    gi = jnp.arange(G)[None, :, None, None]
    k_sel, v_sel = kb[bi, gi, idx], vb[bi, gi, idx]
    kpos = idx[..., None] * NSA_SEL_LEN + jnp.arange(NSA_SEL_LEN)
    s = jnp.einsum('bqgrd,bgqnkd->bgrqnk', qr, k_sel) * scale
    mask = (kpos <= qpos[:, None, None])[:, :, None]
    nk = s.shape[-2] * s.shape[-1]
    p = masked_softmax(s.reshape(B, G, R, Q, nk), mask.reshape(B, G, 1, Q, nk)).reshape(s.shape)
    o_sel = jnp.einsum('bgrqnk,bgqnkd->bqgrd', p.astype(v_sel.dtype), v_sel)
    s = jnp.einsum('bqgrd,bkgd->bgrqk', qr, kw) * scale
    dpos = qpos[:, None] - wpos[None, :]
    p = masked_softmax(s, (dpos >= 0) & (dpos < NSA_WINDOW) & (wpos[None, :] >= 0))
    o_win = jnp.einsum('bgrqk,bkgd->bqgrd', p.astype(vw.dtype), vw)
    g = gates.reshape(B, Q, G, R, 3)
    o = g[..., 0:1] * o_cmp + g[..., 1:2] * o_sel + g[..., 2:3] * o_win
    return o.reshape(B, Q, H_NSA, HEAD_DIM).astype(qn.dtype)


def nsa_prompt(qn, qr, g, ck, cv, sk, sv, wk, wv, pe_k, w1_k, w2_k, pe_v, w1_v, w2_v):
    B, S = qn.shape[:2]
    kc = nsa_compress(ck, pe_k, w1_k, w2_k)
    vc = nsa_compress(cv, pe_v, w1_v, w2_v)
    cend = jnp.arange(kc.shape[1]) * NSA_CMP_STRIDE + NSA_CMP_LEN - 1
    kb, vb = sel_blocks(sk), sel_blocks(sv)
    pad = ((0, 0), (NSA_WINDOW, 0), (0, 0), (0, 0))
    kw_pad, vw_pad = jnp.pad(wk, pad), jnp.pad(wv, pad)
    band = NSA_WINDOW + NSA_QBLK
    nb = S // NSA_QBLK
    blocks = lambda t: jnp.moveaxis(t.reshape((B, nb, NSA_QBLK) + t.shape[2:]), 1, 0)

    def one(a):
        qn_b, qr_b, g_b, i = a
        s0 = i * NSA_QBLK
        qpos = s0 + jnp.arange(NSA_QBLK)
        kw = lax.dynamic_slice_in_dim(kw_pad, s0, band, axis=1)
        vw = lax.dynamic_slice_in_dim(vw_pad, s0, band, axis=1)
        wpos = s0 - NSA_WINDOW + jnp.arange(band)
        return nsa_attend(qn_b, qr_b, qpos, g_b, kc, vc, cend, kb, vb, kw, vw, wpos)

    o = lax.map(one, (blocks(qn), blocks(qr), blocks(g), jnp.arange(nb)))
    return jnp.moveaxis(o, 0, 1).reshape(B, S, H_NSA, HEAD_DIM)


def nsa_sample(page_table, qn, qr, g, ck, cv, sk, sv, wk, wv, wbk, wbv, c_ck, c_cv, c_sk, c_sv,
               pe_k, w1_k, w2_k, pe_v, w1_v, w2_v, qpos):
    T = qn.shape[1]
    past = page_table.shape[1] * c_ck.shape[1]
    L = past + T
    Lp = -(-L // NSA_SEL_LEN) * NSA_SEL_LEN
    cend = jnp.arange(Lp // NSA_CMP_STRIDE - 1) * NSA_CMP_STRIDE + NSA_CMP_LEN - 1
    wbuf = wbk.shape[1]
    wpos = past - wbuf + jnp.arange(wbuf + T)

    def rows(cache, pt, new):
        r = jnp.concatenate([cache[pt].reshape(past, NSA_GROUPS, HEAD_DIM), new], axis=0)
        return jnp.pad(r, ((0, Lp - L), (0, 0), (0, 0)))[None]

    def one(a):
        pt, qn1, qr1, g1, ck1, cv1, sk1, sv1, wk1, wv1, wbk1, wbv1 = a
        kc = nsa_compress(rows(c_ck, pt, ck1), pe_k, w1_k, w2_k)
        vc = nsa_compress(rows(c_cv, pt, cv1), pe_v, w1_v, w2_v)
        kb = sel_blocks(rows(c_sk, pt, sk1))
        vb = sel_blocks(rows(c_sv, pt, sv1))
        kw = jnp.concatenate([wbk1, wk1], axis=0)[None]
        vw = jnp.concatenate([wbv1, wv1], axis=0)[None]
        return nsa_attend(qn1[None], qr1[None], qpos, g1[None], kc, vc, cend, kb, vb, kw, vw, wpos)[0]

    return lax.map(one, (page_table, qn, qr, g, ck, cv, sk, sv, wk, wv, wbk, wbv))


def even_features(x, norm_g, w_in, lb, f_bias, q_norm, k_norm):
    B, S, _ = x.shape
    aq, af, ai, ag, fq, fk, fv, fl = split_cols(rmsnorm(x, norm_g) @ w_in, EVEN_SPLITS)
    heads = lambda t, n: t.reshape(B, S, n, -1)
    gate = lb + (1.0 - lb) * jax.nn.sigmoid(af.astype(jnp.float32))
    hgrn = (heads(jax.nn.silu(aq.astype(jnp.float32)), H_HGRN), heads(1.0 - gate, H_HGRN),
            heads(ai.astype(jnp.float32), H_HGRN), heads(jnp.log(gate), H_HGRN))
    fox = (rmsnorm(heads(fq, H_FOX), q_norm), rmsnorm(heads(fk, H_FOX), k_norm), heads(fv, H_FOX),
           jax.nn.log_sigmoid(fl.astype(jnp.float32) + f_bias.astype(jnp.float32)))
    return hgrn, heads(ag, H_HGRN), fox


def even_mix_out(x, o_hgrn, a_gate, out_norm, o_fox, w_out):
    B, S, _ = x.shape
    oa = (rmsnorm(o_hgrn, out_norm) * jax.nn.silu(a_gate.astype(jnp.float32))).astype(x.dtype)
    o = jnp.concatenate([oa.reshape(B, S, -1), o_fox.reshape(B, S, -1).astype(x.dtype)], axis=-1)
    return x + o @ w_out


def odd_features(x, pos, norm_g, w_in, q_norm, ck_norm, sk_norm, wk_norm):
    B, S, _ = x.shape
    q, ck, cv, sk, sv, wk, wv, gl = split_cols(rmsnorm(x, norm_g) @ w_in, ODD_SPLITS)
    kv = lambda t: t.reshape(B, S, NSA_GROUPS, HEAD_DIM)
    q = rmsnorm(q.reshape(B, S, H_NSA, HEAD_DIM), q_norm)
    gates = jax.nn.sigmoid(gl.astype(jnp.float32)).reshape(B, S, H_NSA, 3)
    return (q, rope(q, pos), gates, rmsnorm(kv(ck), ck_norm), kv(cv),
            rope(rmsnorm(kv(sk), sk_norm), pos), kv(sv), rope(rmsnorm(kv(wk), wk_norm), pos), kv(wv))


def setup_inputs(seed: int = 0) -> dict:
    keys = iter(jax.random.split(jax.random.key(seed), 64))
    f32 = jnp.float32

    def nrm(shape, scale=1.0):
        return jax.random.normal(next(keys), shape, f32) * scale

    def gain(shape):
        return 1.0 + 0.02 * nrm(shape)

    d = D_MODEL
    n_pages = PAST_LEN // PAGE_SIZE
    n_pool = (5 * DEC_BATCH * n_pages + 3) // 4
    w_buf = min(NSA_WINDOW, PAST_LEN)
    perm = jax.random.permutation(next(keys), n_pool)
    page_table = perm[:DEC_BATCH * n_pages].reshape(DEC_BATCH, n_pages).astype(jnp.int32)
    fox_pool = (N_EVEN, n_pool, PAGE_SIZE, H_FOX, HEAD_DIM)
    nsa_pool = (N_ODD, n_pool, PAGE_SIZE, NSA_GROUPS, HEAD_DIM)
    win_buf = (N_ODD, DEC_BATCH, w_buf, NSA_GROUPS, HEAD_DIM)
    return {
        'x_prompt': nrm((BATCH, SEQ, d)),
        'x_sample': nrm((DEC_BATCH, DEC_SEQ, d)),
        'state_hgrn': nrm((N_EVEN, DEC_BATCH, H_HGRN, HGRN_DK, HEAD_DIM), 0.5),
        'cache_fox_k': nrm(fox_pool),
        'cache_fox_v': nrm(fox_pool),
        'cache_fox_logf': jax.nn.log_sigmoid(3.0 + nrm((N_EVEN, n_pool, PAGE_SIZE, H_FOX), 0.5)),
        'cache_nsa_cmp_k': nrm(nsa_pool),
        'cache_nsa_cmp_v': nrm(nsa_pool),
        'cache_nsa_sel_k': nrm(nsa_pool),
        'cache_nsa_sel_v': nrm(nsa_pool),
        'cache_nsa_win_k': nrm(win_buf),
        'cache_nsa_win_v': nrm(win_buf),
        'page_table': page_table,
        'e_norm_mix': gain((N_EVEN, d)),
        'e_w_in': nrm((N_EVEN, d, sum(EVEN_SPLITS)), d ** -0.5),
        'hgrn_lb_logits': nrm((N_EVEN + 1, A_WIDTH), 0.5),
        'hgrn_out_norm': gain((N_EVEN, HEAD_DIM)),
        'fox_f_bias': 3.0 + nrm((N_EVEN, H_FOX), 0.5),
        'fox_q_norm': gain((N_EVEN, HEAD_DIM)),
        'fox_k_norm': gain((N_EVEN, HEAD_DIM)),
        'e_w_out': nrm((N_EVEN, EVEN_MIX, d), EVEN_MIX ** -0.5),
        'e_norm_ffn': gain((N_EVEN, d)),
        'ffn_w_gate': nrm((N_EVEN, d, D_FF), d ** -0.5),
        'ffn_w_up': nrm((N_EVEN, d, D_FF), d ** -0.5),
        'ffn_w_down': nrm((N_EVEN, D_FF, d), D_FF ** -0.5),
        'o_norm_mix': gain((N_ODD, d)),
        'o_w_in': nrm((N_ODD, d, sum(ODD_SPLITS)), d ** -0.5),
        'nsa_q_norm': gain((N_ODD, HEAD_DIM)),
        'nsa_cmp_k_norm': gain((N_ODD, HEAD_DIM)),
        'nsa_sel_k_norm': gain((N_ODD, HEAD_DIM)),
        'nsa_win_k_norm': gain((N_ODD, HEAD_DIM)),
        'cmp_pe_k': nrm((N_ODD, NSA_CMP_LEN, HEAD_DIM), 0.1),
        'cmp_w1_k': nrm((N_ODD, NSA_CMP_LEN * HEAD_DIM, HEAD_DIM), (NSA_CMP_LEN * HEAD_DIM) ** -0.5),
        'cmp_w2_k': nrm((N_ODD, HEAD_DIM, HEAD_DIM), HEAD_DIM ** -0.5),
        'cmp_pe_v': nrm((N_ODD, NSA_CMP_LEN, HEAD_DIM), 0.1),
        'cmp_w1_v': nrm((N_ODD, NSA_CMP_LEN * HEAD_DIM, HEAD_DIM), (NSA_CMP_LEN * HEAD_DIM) ** -0.5),
        'cmp_w2_v': nrm((N_ODD, HEAD_DIM, HEAD_DIM), HEAD_DIM ** -0.5),
        'o_w_out': nrm((N_ODD, ODD_MIX, d), ODD_MIX ** -0.5),
        'o_norm_ffn': gain((N_ODD, d)),
        'router_w': nrm((N_ODD, d, N_EXPERTS), d ** -0.5),
        'router_b': nrm((N_ODD, N_EXPERTS), 0.01),
        'moe_w_gate': nrm((N_ODD, N_EXPERTS, d, D_FF_EXPERT), d ** -0.5),
        'moe_w_up': nrm((N_ODD, N_EXPERTS, d, D_FF_EXPERT), d ** -0.5),
        'moe_w_down': nrm((N_ODD, N_EXPERTS, D_FF_EXPERT, d), D_FF_EXPERT ** -0.5),
    }


def reference(x_prompt, x_sample, state_hgrn, cache_fox_k, cache_fox_v, cache_fox_logf,
              cache_nsa_cmp_k, cache_nsa_cmp_v, cache_nsa_sel_k, cache_nsa_sel_v,
              cache_nsa_win_k, cache_nsa_win_v, page_table,
              e_norm_mix, e_w_in, hgrn_lb_logits, hgrn_out_norm, fox_f_bias, fox_q_norm, fox_k_norm,
              e_w_out, e_norm_ffn, ffn_w_gate, ffn_w_up, ffn_w_down,
              o_norm_mix, o_w_in, nsa_q_norm, nsa_cmp_k_norm, nsa_sel_k_norm, nsa_win_k_norm,
              cmp_pe_k, cmp_w1_k, cmp_w2_k, cmp_pe_v, cmp_w1_v, cmp_w2_v,
              o_w_out, o_norm_ffn, router_w, router_b, moe_w_gate, moe_w_up, moe_w_down):
    f32 = jnp.float32
    S = x_prompt.shape[1]
    T = x_sample.shape[1]
    past = page_table.shape[1] * cache_fox_k.shape[2]
    pos_p = jnp.arange(S)
    pos_d = past + jnp.arange(T)
    lbs = jnp.cumsum(jax.nn.softmax(hgrn_lb_logits.astype(f32), axis=0), axis=0)
    w_buf = cache_nsa_win_k.shape[2]
    n_win_p = min(NSA_WINDOW, S)
    xp, xd = x_prompt, x_sample
    hs_p, hs_d, fox_p, fox_d, nsa_p, nsa_d = [], [], [], [], [], []
    for layer in range(DEPTH):
        li = layer // 2
        if layer % 2 == 0:
            ev = (e_norm_mix[li], e_w_in[li], lbs[li], fox_f_bias[li], fox_q_norm[li], fox_k_norm[li])
            ffn = (ffn_w_gate[li], ffn_w_up[li], ffn_w_down[li])
            hp, gp, (qp, kp, vp, lfp) = even_features(xp, *ev)
            s_fin, oa = hgrn2_prompt(*hp)
            ob = fox_prompt(qp, kp, vp, lfp)
            xp = even_mix_out(xp, oa, gp, hgrn_out_norm[li], ob, e_w_out[li])
            xp = xp + swiglu(rmsnorm(xp, e_norm_ffn[li]), *ffn)
            hd, gd, (qd, kd, vd, lfd) = even_features(xd, *ev)
            s_new, oa = hgrn2_chunk(state_hgrn[li].astype(f32), hd)
            ob = fox_sample(page_table, qd, kd, vd, lfd, cache_fox_k[li], cache_fox_v[li],
                            cache_fox_logf[li], pos_d)
            xd = even_mix_out(xd, oa, gd, hgrn_out_norm[li], ob, e_w_out[li])
            xd = xd + swiglu(rmsnorm(xd, e_norm_ffn[li]), *ffn)
            hs_p.append(s_fin.astype(state_hgrn.dtype))
            hs_d.append(s_new.astype(state_hgrn.dtype))
            fox_p.append((kp, vp, lfp))
            fox_d.append((kd, vd, lfd))
        else:
            od = (o_norm_mix[li], o_w_in[li], nsa_q_norm[li], nsa_cmp_k_norm[li], nsa_sel_k_norm[li],
                  nsa_win_k_norm[li])
            comp = (cmp_pe_k[li], cmp_w1_k[li], cmp_w2_k[li], cmp_pe_v[li], cmp_w1_v[li], cmp_w2_v[li])
            moe_w = (router_w[li], router_b[li], moe_w_gate[li], moe_w_up[li], moe_w_down[li])
            fp = odd_features(xp, pos_p, *od)
            o = nsa_prompt(*fp, *comp)
            xp = xp + o.reshape(xp.shape[0], S, ODD_MIX) @ o_w_out[li]
            xp = xp + moe_swiglu(rmsnorm(xp, o_norm_ffn[li]), *moe_w)
            fd = odd_features(xd, pos_d, *od)
            o = nsa_sample(page_table, *fd, cache_nsa_win_k[li], cache_nsa_win_v[li],
                           cache_nsa_cmp_k[li], cache_nsa_cmp_v[li], cache_nsa_sel_k[li], cache_nsa_sel_v[li],
                           *comp, pos_d)
            xd = xd + o.reshape(xd.shape[0], T, ODD_MIX) @ o_w_out[li]
            xd = xd + moe_swiglu(rmsnorm(xd, o_norm_ffn[li]), *moe_w)
            nsa_p.append((fp[3], fp[4], fp[5], fp[6], fp[7][:, S - n_win_p:], fp[8][:, S - n_win_p:]))
            nsa_d.append((fd[3], fd[4], fd[5], fd[6],
                          jnp.concatenate([cache_nsa_win_k[li], fd[7]], axis=1)[:, -w_buf:],
                          jnp.concatenate([cache_nsa_win_v[li], fd[8]], axis=1)[:, -w_buf:]))
    y_prompt, y_sample = xp, xd
    hgrn_state_prompt = jnp.stack(hs_p)
    hgrn_state_sample = jnp.stack(hs_d)
    fox_k_prompt, fox_v_prompt, fox_logf_prompt = [jnp.stack(t) for t in zip(*fox_p)]
    fox_k_sample, fox_v_sample, fox_logf_sample = [jnp.stack(t) for t in zip(*fox_d)]
    (nsa_cmp_k_prompt, nsa_cmp_v_prompt, nsa_sel_k_prompt, nsa_sel_v_prompt,
     nsa_win_k_prompt, nsa_win_v_prompt) = [jnp.stack(t) for t in zip(*nsa_p)]
    (nsa_cmp_k_sample, nsa_cmp_v_sample, nsa_sel_k_sample, nsa_sel_v_sample,
     nsa_win_k_sample, nsa_win_v_sample) = [jnp.stack(t) for t in zip(*nsa_d)]
    return (y_prompt, y_sample, hgrn_state_prompt, hgrn_state_sample,
            fox_k_prompt, fox_v_prompt, fox_logf_prompt, fox_k_sample, fox_v_sample, fox_logf_sample,
            nsa_cmp_k_prompt, nsa_cmp_v_prompt, nsa_sel_k_prompt, nsa_sel_v_prompt,
            nsa_win_k_prompt, nsa_win_v_prompt,
            nsa_cmp_k_sample, nsa_cmp_v_sample, nsa_sel_k_sample, nsa_sel_v_sample,
            nsa_win_k_sample, nsa_win_v_sample)
```

```python
import functools

import numpy as np
import jax
import jax.numpy as jnp
from jax import lax
from jax.experimental import pallas as pl
from jax.experimental.pallas import tpu as pltpu

F32 = jnp.float32
BF16 = jnp.bfloat16

LANES = 128
HEAD_DIM = 128
EPS = 1e-6
ROPE_THETA = 10000.0
NEG = -1e30
M_INIT = -1e20
VMEM_LIMIT = 56 * 1024 * 1024

H_HGRN = 8
H_FOX = 8
H_NSA = 16
NSA_GROUPS = 4
NSA_R = H_NSA // NSA_GROUPS
CMP_STRIDE = 16
CMP_LEN = 32
SEL_LEN = 64
SEL_SHIFT = 6
TOPN = 16
WINDOW = 512
N_EXPERTS = 8
PAGE = 128


def _cparams(sem):
    return pltpu.CompilerParams(dimension_semantics=sem, vmem_limit_bytes=VMEM_LIMIT)


def _dot(a, b):
    return jnp.dot(a, b, preferred_element_type=F32)


def _dot_nt(a, b):
    return lax.dot_general(a, b, (((1,), (1,)), ((), ())), preferred_element_type=F32)


def _dot_tn(a, b):
    return lax.dot_general(a, b, (((0,), (0,)), ((), ())), preferred_element_type=F32)


def _split3(x):
    hi = x.astype(BF16)
    r1 = x - hi.astype(F32)
    mid = r1.astype(BF16)
    lo = (r1 - mid.astype(F32)).astype(BF16)
    return hi, mid, lo


def _dot3(w_bf16, x):
    hi, mid, lo = _split3(x)
    return _dot(w_bf16, hi) + _dot(w_bf16, mid) + _dot(w_bf16, lo)


def _sigmoid(x):
    return 1.0 / (1.0 + jnp.exp(-x))


def _silu(x):
    return x * _sigmoid(x)


def _log_sigmoid(x):
    return jnp.minimum(x, 0.0) - jnp.log(1.0 + jnp.exp(-jnp.abs(x)))


def _tril_bf16(n):
    r = lax.broadcasted_iota(jnp.int32, (n, n), 0)
    c = lax.broadcasted_iota(jnp.int32, (n, n), 1)
    return jnp.where(c <= r, 1.0, 0.0).astype(BF16)


def _head_rms(a, gain):
    ms = jnp.mean(a * a, axis=-1, keepdims=True)
    return a * lax.rsqrt(ms + EPS) * gain


def _rope(y, cosf, sinf):
    return y * cosf + pltpu.roll(y, HEAD_DIM // 2, 1) * sinf


def _rope_tables(pos):
    half = HEAD_DIM // 2
    inv = ROPE_THETA ** (-jnp.arange(half, dtype=F32) / half)
    ang = pos.astype(F32)[:, None] * inv[None, :]
    cos, sin = jnp.cos(ang), jnp.sin(ang)
    return jnp.concatenate([cos, cos], axis=-1), jnp.concatenate([-sin, sin], axis=-1)


def _proj_kernel(segs, tn, *refs):
    n_out = sum(2 if s[2] == 'hnorm_both' else 1 for s in segs)
    x_ref, g_ref, w_ref, gain_ref, bias_ref, cos_ref, sin_ref = refs[:7]
    out_refs = refs[7:7 + n_out]
    xn_ref = refs[7 + n_out]
    j = pl.program_id(1)

    @pl.when(j == 0)
    def _():
        xf = x_ref[...]
        ms = jnp.mean(xf * xf, axis=-1, keepdims=True)
        xn_ref[...] = (xf * lax.rsqrt(ms + EPS) * g_ref[...]).astype(BF16)

    acc = _dot(xn_ref[...], w_ref[...])
    oi = 0
    for (t0, nt, mode) in segs:
        outs = out_refs[oi:oi + (2 if mode == 'hnorm_both' else 1)]
        oi += len(outs)

        @pl.when((j >= t0) & (j < t0 + nt))
        def _(mode=mode, outs=outs):
            if mode == 'raw':
                outs[0][...] = acc
            elif mode == 'sigmoid':
                outs[0][...] = _sigmoid(acc)
            elif mode == 'logsig':
                outs[0][...] = _log_sigmoid(acc + bias_ref[...])
            else:
                for h in range(tn // HEAD_DIM):
                    sl = slice(h * HEAD_DIM, (h + 1) * HEAD_DIM)
                    y = _head_rms(acc[:, sl], gain_ref[:, sl])
                    if mode == 'hnorm':
                        outs[0][:, sl] = y
                    elif mode == 'hnorm_rope':
                        outs[0][:, sl] = _rope(y, cos_ref[...], sin_ref[...])
                    else:
                        outs[0][:, sl] = y
                        outs[1][:, sl] = _rope(y, cos_ref[...], sin_ref[...])


def _proj(x, norm_g, w, gain_all, bias_all, cosf, sinf, segs, *, tm, tn, name):
    M, K = x.shape
    ntiles = w.shape[1] // tn
    assert M % tm == 0 and w.shape[1] == ntiles * tn and segs[-1][0] + segs[-1][1] == ntiles
    nseq = cosf.shape[0] // tm
    out_shape, out_specs = [], []
    for (t0, nt, mode) in segs:
        for _ in range(2 if mode == 'hnorm_both' else 1):
            out_shape.append(jax.ShapeDtypeStruct((M, nt * tn), F32))
            out_specs.append(pl.BlockSpec(
                (tm, tn), lambda i, j, t0=t0, nt=nt: (i, jnp.clip(j - t0, 0, nt - 1))))
    return pl.pallas_call(
        functools.partial(_proj_kernel, tuple(segs), tn),
        out_shape=out_shape,
        grid=(M // tm, ntiles),
        in_specs=[
            pl.BlockSpec((tm, K), lambda i, j: (i, 0)),
            pl.BlockSpec((1, K), lambda i, j: (0, 0)),
            pl.BlockSpec((K, tn), lambda i, j: (0, j)),
            pl.BlockSpec((1, tn), lambda i, j: (0, j)),
            pl.BlockSpec((1, tn), lambda i, j: (0, j)),
            pl.BlockSpec((tm, HEAD_DIM), lambda i, j: (i % nseq, 0)),
            pl.BlockSpec((tm, HEAD_DIM), lambda i, j: (i % nseq, 0)),
        ],
        out_specs=out_specs,
        scratch_shapes=[pltpu.VMEM((tm, K), BF16)],
        compiler_params=_cparams(("arbitrary", "arbitrary")),
        name=name,
    )(x, norm_g, w, gain_all, bias_all, cosf, sinf)


def _outproj_kernel(n_lhs, *refs):
    lhs = refs[:n_lhs]
    w_ref, res_ref, o_ref, xs_ref = refs[n_lhs:n_lhs + 4]

    @pl.when(pl.program_id(1) == 0)
    def _():
        x = lhs[0][...]
        for r in lhs[1:]:
            x = x + r[...]
        xs_ref[...] = x.astype(BF16)

    o_ref[...] = res_ref[...] + _dot(xs_ref[...], w_ref[...])


def _outproj(lhs_list, w, res, *, tm, tn, name):
    M, K = lhs_list[0].shape
    N = w.shape[1]
    n = len(lhs_list)
    return pl.pallas_call(
        functools.partial(_outproj_kernel, n),
        out_shape=jax.ShapeDtypeStruct((M, N), F32),
        grid=(M // tm, N // tn),
        in_specs=[pl.BlockSpec((tm, K), lambda i, j: (i, 0)) for _ in range(n)] + [
            pl.BlockSpec((K, tn), lambda i, j: (0, j)),
            pl.BlockSpec((tm, tn), lambda i, j: (i, j)),
        ],
        out_specs=pl.BlockSpec((tm, tn), lambda i, j: (i, j)),
        scratch_shapes=[pltpu.VMEM((tm, K), BF16)],
        compiler_params=_cparams(("arbitrary", "arbitrary")),
        name=name,
    )(*lhs_list, w, res)


def _ffn_kernel(x_ref, g_ref, wg_ref, wu_ref, wd_ref, o_ref, xn_ref):
    f = pl.program_id(1)

    @pl.when(f == 0)
    def _():
        xf = x_ref[...]
        ms = jnp.mean(xf * xf, axis=-1, keepdims=True)
        xn_ref[...] = (xf * lax.rsqrt(ms + EPS) * g_ref[...]).astype(BF16)
        o_ref[...] = xf

    xn = xn_ref[...]
    h = _silu(_dot(xn, wg_ref[...])) * _dot(xn, wu_ref[...])
    o_ref[...] += _dot(h.astype(BF16), wd_ref[...])


def _ffn(x, norm_g, wg, wu, wd, *, tm, tf, name):
    M, K = x.shape
    F = wg.shape[1]
    return pl.pallas_call(
        _ffn_kernel,
        out_shape=jax.ShapeDtypeStruct((M, K), F32),
        grid=(M // tm, F // tf),
        in_specs=[
            pl.BlockSpec((tm, K), lambda i, f: (i, 0)),
            pl.BlockSpec((1, K), lambda i, f: (0, 0)),
            pl.BlockSpec((K, tf), lambda i, f: (0, f)),
            pl.BlockSpec((K, tf), lambda i, f: (0, f)),
            pl.BlockSpec((tf, K), lambda i, f: (f, 0)),
        ],
        out_specs=pl.BlockSpec((tm, K), lambda i, f: (i, 0)),
        scratch_shapes=[pltpu.VMEM((tm, K), BF16)],
        compiler_params=_cparams(("arbitrary", "arbitrary")),
        name=name,
    )(x, norm_g, wg, wu, wd)


def _moe_kernel(x_ref, g_ref, rw_ref, rb_ref, wg_ref, wu_ref, wd_ref, o_ref, xn_ref, gate_ref):
    e = pl.program_id(1)
    f = pl.program_id(2)

    @pl.when((e == 0) & (f == 0))
    def _():
        xf = x_ref[...]
        ms = jnp.mean(xf * xf, axis=-1, keepdims=True)
        xn = (xf * lax.rsqrt(ms + EPS) * g_ref[...]).astype(BF16)
        xn_ref[...] = xn
        o_ref[...] = xf
        lane = lax.broadcasted_iota(jnp.int32, (xf.shape[0], LANES), 1)
        logits = jnp.where(lane < N_EXPERTS, _dot(xn, rw_ref[...]) + rb_ref[...], NEG)
        m1 = jnp.max(logits, axis=-1, keepdims=True)
        i1 = jnp.min(jnp.where(logits == m1, lane, LANES), axis=-1, keepdims=True)
        l2 = jnp.where(lane == i1, NEG, logits)
        m2 = jnp.max(l2, axis=-1, keepdims=True)
        i2 = jnp.min(jnp.where(l2 == m2, lane, LANES), axis=-1, keepdims=True)
        e2 = jnp.exp(m2 - m1)
        w1 = 1.0 / (1.0 + e2)
        gate_ref[...] = jnp.where(lane == i1, w1, 0.0) + jnp.where(lane == i2, e2 * w1, 0.0)

    xn = xn_ref[...]
    h = _silu(_dot(xn, wg_ref[0])) * _dot(xn, wu_ref[0])
    lane = lax.broadcasted_iota(jnp.int32, gate_ref.shape, 1)
    ge = jnp.sum(jnp.where(lane == e, gate_ref[...], 0.0), axis=-1, keepdims=True)
    o_ref[...] += ge * _dot(h.astype(BF16), wd_ref[0])


def _moe(x, norm_g, rw, rb, wg, wu, wd, *, tm, tf, name):
    M, K = x.shape
    E, _, F = wg.shape
    return pl.pallas_call(
        _moe_kernel,
        out_shape=jax.ShapeDtypeStruct((M, K), F32),
        grid=(M // tm, E, F // tf),
        in_specs=[
            pl.BlockSpec((tm, K), lambda i, e, f: (i, 0)),
            pl.BlockSpec((1, K), lambda i, e, f: (0, 0)),
            pl.BlockSpec((K, LANES), lambda i, e, f: (0, 0)),
            pl.BlockSpec((1, LANES), lambda i, e, f: (0, 0)),
            pl.BlockSpec((1, K, tf), lambda i, e, f: (e, 0, f)),
            pl.BlockSpec((1, K, tf), lambda i, e, f: (e, 0, f)),
            pl.BlockSpec((1, tf, K), lambda i, e, f: (e, f, 0)),
        ],
        out_specs=pl.BlockSpec((tm, K), lambda i, e, f: (i, 0)),
        scratch_shapes=[pltpu.VMEM((tm, K), BF16), pltpu.VMEM((tm, LANES), F32)],
        compiler_params=_cparams(("arbitrary", "arbitrary", "arbitrary")),
        name=name,
    )(x, norm_g, rw, rb, wg, wu, wd)


def _hgrn_kernel(T, c, aq_ref, af_ref, ai_ref, ag_ref, lb_ref, gn_ref, o_ref, st_ref, st_scr):
    t = pl.program_id(2)

    @pl.when(t == 0)
    def _():
        st_scr[...] = jnp.zeros_like(st_scr)

    lb = lb_ref[...]
    tril = _tril_bf16(c)
    row = lax.broadcasted_iota(jnp.int32, (c, c), 0)
    col = lax.broadcasted_iota(jnp.int32, (c, c), 1)
    for u in range(T // c):
        rows = pl.ds(u * c, c)
        gate = lb + (1.0 - lb) * _sigmoid(af_ref[rows, :])
        k = 1.0 - gate
        b = _dot3(tril, jnp.log(gate))
        qe = (_silu(aq_ref[rows, :]) * jnp.exp(b)).astype(BF16)
        ke = (k * jnp.exp(-b)).astype(BF16)
        sc = jnp.where(col <= row, _dot_nt(qe, ke), 0.0)
        st = st_scr[...]
        vb = ai_ref[rows, :].astype(BF16)
        o = _dot(sc.astype(BF16), vb) + _dot_nt(qe, st.astype(BF16))
        bend = b[c - 1:c, :]
        kend = (k * jnp.exp(bend - b)).astype(BF16)
        st_scr[...] = st * jnp.exp(bend) + _dot_tn(vb, kend)
        o_ref[rows, :] = _head_rms(o, gn_ref[...]) * _silu(ag_ref[rows, :])

    @pl.when(t == pl.num_programs(2) - 1)
    def _():
        st_ref[0, 0] = st_scr[...].T


def _hgrn_prompt(hg, lb, gn, B, S, *, T, c):
    nt = S // T
    H = H_HGRN
    spec = lambda off: pl.BlockSpec((T, HEAD_DIM), lambda b, h, t, off=off: (b * nt + t, off + h))
    return pl.pallas_call(
        functools.partial(_hgrn_kernel, T, c),
        out_shape=[jax.ShapeDtypeStruct((B * S, H * HEAD_DIM), F32),
                   jax.ShapeDtypeStruct((B, H, HEAD_DIM, HEAD_DIM), F32)],
        grid=(B, H, nt),
        in_specs=[spec(0), spec(H), spec(2 * H), spec(3 * H),
                  pl.BlockSpec((1, HEAD_DIM), lambda b, h, t: (0, h)),
                  pl.BlockSpec((1, HEAD_DIM), lambda b, h, t: (0, 0))],
        out_specs=[pl.BlockSpec((T, HEAD_DIM), lambda b, h, t: (b * nt + t, h)),
                   pl.BlockSpec((1, 1, HEAD_DIM, HEAD_DIM), lambda b, h, t: (b, h, 0, 0))],
        scratch_shapes=[pltpu.VMEM((HEAD_DIM, HEAD_DIM), F32)],
        compiler_params=_cparams(("arbitrary", "arbitrary", "arbitrary")),
        name="hgrn_prompt",
    )(hg, hg, hg, hg, lb, gn)


def _cumsum_kernel(lf_ref, c_ref, carry):
    @pl.when(pl.program_id(1) == 0)
    def _():
        carry[...] = jnp.zeros_like(carry)

    tc = lf_ref.shape[0]
    c = _dot3(_tril_bf16(tc), lf_ref[...]) + carry[...]
    c_ref[...] = c
    carry[...] = c[tc - 1:tc, :]


def _seq_cumsum(lf, B, S, *, tc):
    n = S // tc
    return pl.pallas_call(
        _cumsum_kernel,
        out_shape=jax.ShapeDtypeStruct((B * S, LANES), F32),
        grid=(B, n),
        in_specs=[pl.BlockSpec((tc, LANES), lambda b, t: (b * n + t, 0))],
        out_specs=pl.BlockSpec((tc, LANES), lambda b, t: (b * n + t, 0)),
        scratch_shapes=[pltpu.VMEM((1, LANES), F32)],
        compiler_params=_cparams(("arbitrary", "arbitrary")),
        name="seq_cumsum",
    )(lf)


def _online_init(m_scr, l_scr, acc_scr):
    m_scr[...] = jnp.full_like(m_scr, M_INIT)
    l_scr[...] = jnp.zeros_like(l_scr)
    acc_scr[...] = jnp.zeros_like(acc_scr)


def _online_update(s, vb, m_scr, l_scr, acc_scr):
    m_prev = m_scr[...]
    m_new = jnp.maximum(m_prev, jnp.max(s, axis=-1, keepdims=True))
    alpha = jnp.exp(m_prev - m_new)
    p = jnp.exp(s - m_new)
    l_scr[...] = alpha * l_scr[...] + jnp.sum(p, axis=-1, keepdims=True)
    acc_scr[...] = alpha * acc_scr[...] + _dot(p.astype(BF16), vb)
    m_scr[...] = m_new


def _online_result(l_scr, acc_scr):
    l = l_scr[...]
    return acc_scr[...] / jnp.where(l > 0.0, l, 1.0)


def _bias_columns(c_blk, h, q_side):
    lane = lax.broadcasted_iota(jnp.int32, c_blk.shape, 1)
    col = jnp.sum(jnp.where(lane == h, c_blk, 0.0), axis=-1, keepdims=True)
    hi, mid, lo = [t.astype(F32) for t in _split3(col)]
    if q_side:
        a = jnp.where(lane == 0, hi, jnp.where(lane == 1, mid, jnp.where(lane == 2, lo,
                      jnp.where(lane < 6, 1.0, 0.0))))
    else:
        a = jnp.where(lane < 3, 1.0, jnp.where(lane == 3, -hi, jnp.where(lane == 4, -mid,
                      jnp.where(lane == 5, -lo, 0.0))))
    return a.astype(BF16)


def _fox_kernel(tq, tk, q_ref, k_ref, v_ref, cq_ref, ck_ref, o_ref, qa_scr, m_scr, l_scr, acc_scr):
    h = pl.program_id(1)
    qi = pl.program_id(2)
    kj = pl.program_id(3)
    last = (qi * tq + tq - 1) // tk

    @pl.when(kj == 0)
    def _():
        qs = (q_ref[...] * (HEAD_DIM ** -0.5)).astype(BF16)
        qa_scr[...] = jnp.concatenate([qs, _bias_columns(cq_ref[...], h, True)], axis=1)
        _online_init(m_scr, l_scr, acc_scr)

    @pl.when(kj <= last)
    def _():
        ka = jnp.concatenate([k_ref[...].astype(BF16), _bias_columns(ck_ref[...], h, False)], axis=1)
        s = _dot_nt(qa_scr[...], ka)
        row = qi * tq + lax.broadcasted_iota(jnp.int32, s.shape, 0)
        col = kj * tk + lax.broadcasted_iota(jnp.int32, s.shape, 1)
        s = jnp.where(col <= row, s, NEG)
        _online_update(s, v_ref[...].astype(BF16), m_scr, l_scr, acc_scr)

    @pl.when(kj == pl.num_programs(3) - 1)
    def _():
        o_ref[...] = _online_result(l_scr, acc_scr)


def _fox_prompt(fq, fk, fv, c, B, S, *, tq, tk):
    nq, nk = S // tq, S // tk
    kidx = lambda qi, kj: jnp.minimum(kj, (qi * tq + tq - 1) // tk)
    return pl.pallas_call(
        functools.partial(_fox_kernel, tq, tk),
        out_shape=jax.ShapeDtypeStruct(fq.shape, F32),
        grid=(B, H_FOX, nq, nk),
        in_specs=[
            pl.BlockSpec((tq, HEAD_DIM), lambda b, h, qi, kj: (b * nq + qi, h)),
            pl.BlockSpec((tk, HEAD_DIM), lambda b, h, qi, kj: (b * nk + kidx(qi, kj), h)),
            pl.BlockSpec((tk, HEAD_DIM), lambda b, h, qi, kj: (b * nk + kidx(qi, kj), h)),
            pl.BlockSpec((tq, LANES), lambda b, h, qi, kj: (b * nq + qi, 0)),
            pl.BlockSpec((tk, LANES), lambda b, h, qi, kj: (b * nk + kidx(qi, kj), 0)),
        ],
        out_specs=pl.BlockSpec((tq, HEAD_DIM), lambda b, h, qi, kj: (b * nq + qi, h)),
        scratch_shapes=[pltpu.VMEM((tq, 2 * HEAD_DIM), BF16), pltpu.VMEM((tq, 1), F32),
                        pltpu.VMEM((tq, 1), F32), pltpu.VMEM((tq, HEAD_DIM), F32)],
        compiler_params=_cparams(("arbitrary",) * 4),
        name="fox_prompt",
    )(fq, fk, fv, c, c)


def _compress_kernel(nc, x_ref, pea_ref, peb_ref, w1a_ref, w1b_ref, w2_ref, o_ref):
    R = jnp.concatenate([x_ref[0, pl.ds(r, nc, stride=CMP_STRIDE), :] for r in range(CMP_STRIDE)], axis=1)
    p1 = _dot((R + pea_ref[...]).astype(BF16), w1a_ref[...])
    p2 = _dot((R + peb_ref[...]).astype(BF16), w1b_ref[...])
    pre = p1 + pltpu.roll(p2, nc - 1, 0)
    kc = _dot(_silu(pre).astype(BF16), w2_ref[...])
    row = lax.broadcasted_iota(jnp.int32, kc.shape, 0)
    o_ref[0, 0] = jnp.where(row < nc - 1, kc, 0.0)


def _cmp_weights(pe, w1, w2):
    half = CMP_STRIDE * HEAD_DIM
    return (pe[:CMP_STRIDE].reshape(1, half), pe[CMP_STRIDE:].reshape(1, half),
            w1[:half].astype(BF16), w1[half:].astype(BF16), w2.astype(BF16))


def _compress_prompt(x, cw, B, S):
    nc = S // CMP_STRIDE
    half = CMP_STRIDE * HEAD_DIM
    const = lambda shape: pl.BlockSpec(shape, lambda b, g: (0, 0))
    return pl.pallas_call(
        functools.partial(_compress_kernel, nc),
        out_shape=jax.ShapeDtypeStruct((B, NSA_GROUPS, nc, HEAD_DIM), F32),
        grid=(B, NSA_GROUPS),
        in_specs=[pl.BlockSpec((1, S, HEAD_DIM), lambda b, g: (b, 0, g)),
                  const((1, half)), const((1, half)), const((half, HEAD_DIM)), const((half, HEAD_DIM)),
                  const((HEAD_DIM, HEAD_DIM))],
        out_specs=pl.BlockSpec((1, 1, nc, HEAD_DIM), lambda b, g: (b, g, 0, 0)),
        compiler_params=_cparams(("arbitrary", "arbitrary")),
        name="nsa_compress",
    )(x, *cw)


def _overlap_t(n_cmp, n_sel):
    cs = np.arange(n_cmp)[None, :] * CMP_STRIDE
    ss = np.arange(n_sel)[:, None] * SEL_LEN
    ov = np.minimum(cs + CMP_LEN, ss + SEL_LEN) - np.maximum(cs, ss)
    return np.maximum(ov, 0).astype(np.float32) / CMP_LEN


def _group_rows(ref, Tq):
    return jnp.concatenate([ref[0, :, r * HEAD_DIM:(r + 1) * HEAD_DIM] for r in range(NSA_R)], axis=0)


def _store_gated(o_ref, o, gates, Tq, branch):
    for r in range(NSA_R):
        gcol = gates[:, 3 * r + branch:3 * r + branch + 1]
        o_ref[0, :, r * HEAD_DIM:(r + 1) * HEAD_DIM] = o[r * Tq:(r + 1) * Tq] * gcol


def _cmpsel_kernel(Tq, nsel, q_ref, kc_ref, vc_ref, ovt_ref, gate_ref, o_ref, sel_ref):
    qi = pl.program_id(2)
    q = (_group_rows(q_ref, Tq) * (HEAD_DIM ** -0.5)).astype(BF16)
    s = _dot_nt(q, kc_ref[0, 0].astype(BF16))
    qpos = qi * Tq + (lax.broadcasted_iota(jnp.int32, s.shape, 0) & (Tq - 1))
    cend = lax.broadcasted_iota(jnp.int32, s.shape, 1) * CMP_STRIDE + (CMP_LEN - 1)
    mask = cend <= qpos
    sm = jnp.where(mask, s, NEG)
    e = jnp.where(mask, jnp.exp(sm - jnp.max(sm, axis=-1, keepdims=True)), 0.0)
    den = jnp.sum(e, axis=-1, keepdims=True)
    p = e / jnp.where(den > 0.0, den, 1.0)
    o = _dot(p.astype(BF16), vc_ref[0, 0].astype(BF16))
    _store_gated(o_ref, o, gate_ref[0, 0], Tq, 0)

    psum = p[0:Tq]
    for r in range(1, NSA_R):
        psum = psum + p[r * Tq:(r + 1) * Tq]
    ovt = ovt_ref[...]
    sc = sum(_dot_nt(ovt, t) for t in _split3(psum))
    n = lax.broadcasted_iota(jnp.int32, sc.shape, 0)
    tpos = qi * Tq + lax.broadcasted_iota(jnp.int32, sc.shape, 1)
    cur = tpos >> SEL_SHIFT
    forced = (n == 0) | (n == cur) | (n == cur - 1)
    sc = jnp.where(n * SEL_LEN <= tpos, jnp.where(forced, -NEG, sc), NEG)
    rank = jnp.zeros(sc.shape, F32)
    for m in range(nsel):
        rowm = sc[m:m + 1, :]
        rank = rank + jnp.where(n > m, jnp.where(rowm >= sc, 1.0, 0.0), jnp.where(rowm > sc, 1.0, 0.0))
    selneg = jnp.where(rank < TOPN, 0.0, NEG)
    if nsel < LANES:
        selneg = jnp.concatenate([selneg, jnp.full((LANES - nsel, Tq), NEG, F32)], axis=0)
    sel_ref[0, 0] = selneg.T.astype(BF16)


def _cmpsel_prompt(qn, kc, vc, gates_g, B, S, *, Tq):
    nq = S // Tq
    ncp = kc.shape[2]
    nsel = S // SEL_LEN
    ovt = jnp.asarray(_overlap_t(ncp, nsel), BF16)
    return pl.pallas_call(
        functools.partial(_cmpsel_kernel, Tq, nsel),
        out_shape=[jax.ShapeDtypeStruct((B, S, H_NSA * HEAD_DIM), F32),
                   jax.ShapeDtypeStruct((B, NSA_GROUPS, S, LANES), BF16)],
        grid=(B, NSA_GROUPS, nq),
        in_specs=[pl.BlockSpec((1, Tq, NSA_R * HEAD_DIM), lambda b, g, qi: (b, qi, g)),
                  pl.BlockSpec((1, 1, ncp, HEAD_DIM), lambda b, g, qi: (b, g, 0, 0)),
                  pl.BlockSpec((1, 1, ncp, HEAD_DIM), lambda b, g, qi: (b, g, 0, 0)),
                  pl.BlockSpec((nsel, ncp), lambda b, g, qi: (0, 0)),
                  pl.BlockSpec((1, 1, Tq, 16), lambda b, g, qi: (b, g, qi, 0))],
        out_specs=[pl.BlockSpec((1, Tq, NSA_R * HEAD_DIM), lambda b, g, qi: (b, qi, g)),
                   pl.BlockSpec((1, 1, Tq, LANES), lambda b, g, qi: (b, g, qi, 0))],
        compiler_params=_cparams(("arbitrary",) * 3),
        name="nsa_cmpsel",
    )(qn, kc, vc, ovt, gates_g)


def _sel_kernel(Tq, tk, q_ref, sn_ref, k_ref, v_ref, e_ref, gate_ref, o_ref, qa_scr, m_scr, l_scr, acc_scr):
    qi = pl.program_id(2)
    kj = pl.program_id(3)
    last = (qi * Tq + Tq - 1) // tk

    @pl.when(kj == 0)
    def _():
        sn = sn_ref[0, 0]
        for r in range(NSA_R):
            qs = (q_ref[0, :, r * HEAD_DIM:(r + 1) * HEAD_DIM] * (HEAD_DIM ** -0.5)).astype(BF16)
            qa_scr[r * Tq:(r + 1) * Tq, :] = jnp.concatenate([qs, sn], axis=1)
        _online_init(m_scr, l_scr, acc_scr)

    @pl.when(kj <= last)
    def _():
        ka = jnp.concatenate([k_ref[0].astype(BF16), e_ref[...]], axis=1)
        s = _dot_nt(qa_scr[...], ka)
        qpos = qi * Tq + (lax.broadcasted_iota(jnp.int32, s.shape, 0) & (Tq - 1))
        kpos = kj * tk + lax.broadcasted_iota(jnp.int32, s.shape, 1)
        s = jnp.where(kpos <= qpos, s, NEG)
        _online_update(s, v_ref[0].astype(BF16), m_scr, l_scr, acc_scr)

    @pl.when(kj == pl.num_programs(3) - 1)
    def _():
        _store_gated(o_ref, _online_result(l_scr, acc_scr), gate_ref[0, 0], Tq, 1)


def _sel_prompt(qr, selneg, sk, sv, gates_g, B, S, *, Tq, tk):
    nq, nk = S // Tq, S // tk
    key = np.arange(S)[:, None] // SEL_LEN
    e_all = jnp.asarray((key == np.arange(LANES)[None, :]).astype(np.float32), BF16)
    kidx = lambda qi, kj: jnp.minimum(kj, (qi * Tq + Tq - 1) // tk)
    rows = NSA_R * Tq
    return pl.pallas_call(
        functools.partial(_sel_kernel, Tq, tk),
        out_shape=jax.ShapeDtypeStruct((B, S, H_NSA * HEAD_DIM), F32),
        grid=(B, NSA_GROUPS, nq, nk),
        in_specs=[pl.BlockSpec((1, Tq, NSA_R * HEAD_DIM), lambda b, g, qi, kj: (b, qi, g)),
                  pl.BlockSpec((1, 1, Tq, LANES), lambda b, g, qi, kj: (b, g, qi, 0)),
                  pl.BlockSpec((1, tk, HEAD_DIM), lambda b, g, qi, kj: (b, kidx(qi, kj), g)),
                  pl.BlockSpec((1, tk, HEAD_DIM), lambda b, g, qi, kj: (b, kidx(qi, kj), g)),
                  pl.BlockSpec((tk, LANES), lambda b, g, qi, kj: (kidx(qi, kj), 0)),
                  pl.BlockSpec((1, 1, Tq, 16), lambda b, g, qi, kj: (b, g, qi, 0))],
        out_specs=pl.BlockSpec((1, Tq, NSA_R * HEAD_DIM), lambda b, g, qi, kj: (b, qi, g)),
        scratch_shapes=[pltpu.VMEM((rows, 2 * HEAD_DIM), BF16), pltpu.VMEM((rows, 1), F32),
                        pltpu.VMEM((rows, 1), F32), pltpu.VMEM((rows, HEAD_DIM), F32)],
        compiler_params=_cparams(("arbitrary",) * 4),
        name="nsa_sel",
    )(qr, selneg, sk, sv, e_all, gates_g)


def _win_kernel(Tq, nwb, q_ref, k_ref, v_ref, gate_ref, o_ref, qs_scr, m_scr, l_scr, acc_scr):
    qi = pl.program_id(2)
    kj = pl.program_id(3)
    kb = qi - (nwb - 1) + kj

    @pl.when(kj == 0)
    def _():
        qs_scr[...] = (_group_rows(q_ref, Tq) * (HEAD_DIM ** -0.5)).astype(BF16)
        _online_init(m_scr, l_scr, acc_scr)

    @pl.when(kb >= 0)
    def _():
        s = _dot_nt(qs_scr[...], k_ref[0].astype(BF16))
        qpos = qi * Tq + (lax.broadcasted_iota(jnp.int32, s.shape, 0) & (Tq - 1))
        kpos = kb * Tq + lax.broadcasted_iota(jnp.int32, s.shape, 1)
        d = qpos - kpos
        s = jnp.where((d >= 0) & (d < WINDOW), s, NEG)
        _online_update(s, v_ref[0].astype(BF16), m_scr, l_scr, acc_scr)

    @pl.when(kj == nwb - 1)
    def _():
        _store_gated(o_ref, _online_result(l_scr, acc_scr), gate_ref[0, 0], Tq, 2)


def _win_prompt(qr, wk, wv, gates_g, B, S, *, Tq):
    nq = S // Tq
    nwb = WINDOW // Tq + 1
    kidx = lambda qi, kj: jnp.maximum(qi - (nwb - 1) + kj, 0)
    rows = NSA_R * Tq
    return pl.pallas_call(
        functools.partial(_win_kernel, Tq, nwb),
        out_shape=jax.ShapeDtypeStruct((B, S, H_NSA * HEAD_DIM), F32),
        grid=(B, NSA_GROUPS, nq, nwb),
        in_specs=[pl.BlockSpec((1, Tq, NSA_R * HEAD_DIM), lambda b, g, qi, kj: (b, qi, g)),
                  pl.BlockSpec((1, Tq, HEAD_DIM), lambda b, g, qi, kj: (b, kidx(qi, kj), g)),
                  pl.BlockSpec((1, Tq, HEAD_DIM), lambda b, g, qi, kj: (b, kidx(qi, kj), g)),
                  pl.BlockSpec((1, 1, Tq, 16), lambda b, g, qi, kj: (b, g, qi, 0))],
        out_specs=pl.BlockSpec((1, Tq, NSA_R * HEAD_DIM), lambda b, g, qi, kj: (b, qi, g)),
        scratch_shapes=[pltpu.VMEM((rows, HEAD_DIM), BF16), pltpu.VMEM((rows, 1), F32),
                        pltpu.VMEM((rows, 1), F32), pltpu.VMEM((rows, HEAD_DIM), F32)],
        compiler_params=_cparams(("arbitrary",) * 4),
        name="nsa_win",
    )(qr, wk, wv, gates_g)


def _hgrn_step_kernel(aq_ref, af_ref, ai_ref, ag_ref, lb_ref, gn_ref, s_ref, o_ref, so_ref):
    lb = lb_ref[0]
    gate = lb + (1.0 - lb) * _sigmoid(af_ref[0, 0])
    s_new = s_ref[0, 0] * gate + (1.0 - gate) * ai_ref[0, 0]
    so_ref[0, 0] = s_new
    o = jnp.sum(_silu(aq_ref[0, 0]) * s_new, axis=0, keepdims=True)
    o_ref[0, 0] = _head_rms(o, gn_ref[...]) * _silu(ag_ref[0, 0])


def _hgrn_step(hg, lb, gn, state):
    D, H = state.shape[:2]
    W = H * HEAD_DIM
    colv = lambda a: a.reshape(D, H, HEAD_DIM, 1)
    rowv = lambda a: a.reshape(D, H, 1, HEAD_DIM)
    cspec = pl.BlockSpec((1, 1, HEAD_DIM, 1), lambda d, h: (d, h, 0, 0))
    rspec = pl.BlockSpec((1, 1, 1, HEAD_DIM), lambda d, h: (d, h, 0, 0))
    sspec = pl.BlockSpec((1, 1, HEAD_DIM, HEAD_DIM), lambda d, h: (d, h, 0, 0))
    o, s_new = pl.pallas_call(
        _hgrn_step_kernel,
        out_shape=[jax.ShapeDtypeStruct((D, H, 1, HEAD_DIM), F32), jax.ShapeDtypeStruct(state.shape, F32)],
        grid=(D, H),
        in_specs=[cspec, cspec, rspec, rspec,
                  pl.BlockSpec((1, HEAD_DIM, 1), lambda d, h: (h, 0, 0)),
                  pl.BlockSpec((1, HEAD_DIM), lambda d, h: (0, 0)), sspec],
        out_specs=[rspec, sspec],
        compiler_params=_cparams(("arbitrary", "arbitrary")),
        name="hgrn_step",
    )(colv(hg[:, :W]), colv(hg[:, W:2 * W]), rowv(hg[:, 2 * W:3 * W]), rowv(hg[:, 3 * W:]),
      lb.reshape(H, HEAD_DIM, 1), gn, state)
    return o.reshape(D, W), s_new


def _fox_step_kernel(PG, pt_ref, q_ref, kn_ref, vn_ref, lfn_ref, *refs):
    k_refs, v_refs, lf_refs = refs[:PG], refs[PG:2 * PG], refs[2 * PG:3 * PG]
    o_ref, m_scr, l_scr, acc_scr, carry = refs[3 * PG:]
    W = H_FOX * PAGE
    q = q_ref[0] * (HEAD_DIM ** -0.5)

    @pl.when(pl.program_id(1) == 0)
    def _():
        m_scr[...] = jnp.sum(q * kn_ref[0], axis=-1, keepdims=True)
        l_scr[...] = jnp.ones_like(l_scr)
        acc_scr[...] = vn_ref[0]
        carry[...] = jnp.broadcast_to(lfn_ref[0], carry.shape)

    qb = q.astype(BF16)
    lane = lax.broadcasted_iota(jnp.int32, (H_FOX, W), 1)
    head = lax.broadcasted_iota(jnp.int32, (H_FOX, W), 0)
    for i in range(PG):
        lf = jnp.broadcast_to(lf_refs[i][0], (H_FOX, W))
        suf, tot = lf, lf
        step = H_FOX
        while step < W:
            suf = suf + jnp.where(lane + step < W, pltpu.roll(suf, W - step, 1), 0.0)
            tot = tot + pltpu.roll(tot, step, 1)
            step *= 2
        s = _dot_nt(qb, k_refs[i][0].astype(BF16)) + (carry[...] + suf - lf)
        s = jnp.where((lane & (H_FOX - 1)) == head, s, NEG)
        _online_update(s, v_refs[i][0].astype(BF16), m_scr, l_scr, acc_scr)
        carry[...] = carry[...] + tot

    @pl.when(pl.program_id(1) == pl.num_programs(1) - 1)
    def _():
        o_ref[0] = _online_result(l_scr, acc_scr)


def _fox_step(page_table, q, k_new, v_new, lf_new, cache_k, cache_v, cache_lf, *, PG):
    D, NP = page_table.shape
    n_pool = cache_k.shape[0]
    W = H_FOX * PAGE
    k2 = cache_k.reshape(n_pool, W, HEAD_DIM)
    v2 = cache_v.reshape(n_pool, W, HEAD_DIM)
    lf2 = cache_lf.astype(F32).reshape(n_pool, 1, W)
    lfn = jnp.tile(lf_new, (1, PAGE)).reshape(D, 1, W)
    page = lambda i: (lambda d, j, pt: (pt[d, NP - 1 - (j * PG + i)], 0, 0))
    hspec = pl.BlockSpec((1, H_FOX, HEAD_DIM), lambda d, j, pt: (d, 0, 0))
    in_specs = [hspec, hspec, hspec, pl.BlockSpec((1, 1, W), lambda d, j, pt: (d, 0, 0))]
    in_specs += [pl.BlockSpec((1, W, HEAD_DIM), page(i)) for i in range(PG)]
    in_specs += [pl.BlockSpec((1, W, HEAD_DIM), page(i)) for i in range(PG)]
    in_specs += [pl.BlockSpec((1, 1, W), page(i)) for i in range(PG)]
    return pl.pallas_call(
        functools.partial(_fox_step_kernel, PG),
        out_shape=jax.ShapeDtypeStruct((D, H_FOX, HEAD_DIM), F32),
        grid_spec=pltpu.PrefetchScalarGridSpec(
            num_scalar_prefetch=1, grid=(D, NP // PG), in_specs=in_specs, out_specs=hspec,
            scratch_shapes=[pltpu.VMEM((H_FOX, 1), F32), pltpu.VMEM((H_FOX, 1), F32),
                            pltpu.VMEM((H_FOX, HEAD_DIM), F32), pltpu.VMEM((H_FOX, W), F32)]),
        compiler_params=_cparams(("arbitrary", "arbitrary")),
        name="fox_step",
    )(page_table, q, k_new, v_new, lfn, *([k2] * PG), *([v2] * PG), *([lf2] * PG))


def _cmp_pages_kernel(PG, pt_ref, *refs):
    pages = refs[:PG]
    pea_ref, peb_ref, w1a_ref, w1b_ref, p1_ref, p2_ref = refs[PG:]
    per_page = PAGE // CMP_STRIDE
    parts = []
    for g in range(NSA_GROUPS):
        for i in range(PG):
            parts.append(jnp.concatenate(
                [pages[i][0, pl.ds(r * NSA_GROUPS + g, per_page, stride=CMP_STRIDE * NSA_GROUPS), :]
                 for r in range(CMP_STRIDE)], axis=1))
    R = jnp.concatenate(parts, axis=0)
    shape = (NSA_GROUPS, PG * per_page, HEAD_DIM)
    p1_ref[0] = _dot((R + pea_ref[...]).astype(BF16), w1a_ref[...]).reshape(shape)
    p2_ref[0] = _dot((R + peb_ref[...]).astype(BF16), w1b_ref[...]).reshape(shape)


def _cmp_pages(page_table, cache, cw, *, PG):
    D, NP = page_table.shape
    n_pool = cache.shape[0]
    c2 = cache.reshape(n_pool, PAGE * NSA_GROUPS, HEAD_DIM)
    per_page = PAGE // CMP_STRIDE
    half = CMP_STRIDE * HEAD_DIM
    page = lambda i: (lambda d, j, pt: (pt[d, j * PG + i], 0, 0))
    const = lambda shape: pl.BlockSpec(shape, lambda d, j, pt: (0, 0))
    in_specs = [pl.BlockSpec((1, PAGE * NSA_GROUPS, HEAD_DIM), page(i)) for i in range(PG)]
    in_specs += [const((1, half)), const((1, half)), const((half, HEAD_DIM)), const((half, HEAD_DIM))]
    ospec = pl.BlockSpec((1, NSA_GROUPS, PG * per_page, HEAD_DIM), lambda d, j, pt: (d, 0, j, 0))
    shp = jax.ShapeDtypeStruct((D, NSA_GROUPS, NP * per_page, HEAD_DIM), F32)
    return pl.pallas_call(
        functools.partial(_cmp_pages_kernel, PG),
        out_shape=[shp, shp],
        grid_spec=pltpu.PrefetchScalarGridSpec(
            num_scalar_prefetch=1, grid=(D, NP // PG), in_specs=in_specs, out_specs=[ospec, ospec]),
        compiler_params=_cparams(("arbitrary", "arbitrary")),
        name="nsa_cmp_pages",
    )(page_table, *([c2] * PG), *cw[:4])


def _cmpsel_step_kernel(qpos, nsp, q_ref, p1k_ref, p2k_ref, p1v_ref, p2v_ref, w2k_ref, w2v_ref,
                        ov_ref, g_ref, o_ref, idx_ref):
    nc = p1k_ref.shape[2]

    def finish(p1_ref, p2_ref, w2_ref):
        pre = p1_ref[0, 0] + pltpu.roll(p2_ref[0, 0], nc - 1, 0)
        return _dot(_silu(pre).astype(BF16), w2_ref[...]).astype(BF16)

    kc = finish(p1k_ref, p2k_ref, w2k_ref)
    vc = finish(p1v_ref, p2v_ref, w2v_ref)
    q = (q_ref[0, 0] * (HEAD_DIM ** -0.5)).astype(BF16)
    s = _dot_nt(q, kc)
    cend = lax.broadcasted_iota(jnp.int32, s.shape, 1) * CMP_STRIDE + (CMP_LEN - 1)
    mask = cend <= qpos
    sm = jnp.where(mask, s, NEG)
    e = jnp.where(mask, jnp.exp(sm - jnp.max(sm, axis=-1, keepdims=True)), 0.0)
    den = jnp.sum(e, axis=-1, keepdims=True)
    p = e / jnp.where(den > 0.0, den, 1.0)
    o_ref[0, 0] = _dot(p.astype(BF16), vc) * g_ref[0, 0][:, 0:1]

    psum = jnp.broadcast_to(jnp.sum(p[0:NSA_R], axis=0, keepdims=True), (8, nc))
    ov = ov_ref[...]
    sc_row = sum(_dot(t, ov) for t in _split3(psum))[0:1]
    n_lane = lax.broadcasted_iota(jnp.int32, (1, nsp), 1)
    cur = qpos // SEL_LEN
    forced = (n_lane == 0) | (n_lane == cur) | (n_lane == cur - 1)
    sc_row = jnp.where(n_lane * SEL_LEN <= qpos, jnp.where(forced, -NEG, sc_row), NEG)
    mi = lax.broadcasted_iota(jnp.int32, (nsp, nsp), 0)
    ni = lax.broadcasted_iota(jnp.int32, (nsp, nsp), 1)
    sc_col = jnp.sum(jnp.where(mi == ni, sc_row, 0.0), axis=-1, keepdims=True)
    beats = jnp.where(mi < ni, jnp.where(sc_col >= sc_row, 1.0, 0.0), jnp.where(sc_col > sc_row, 1.0, 0.0))
    rank = jnp.sum(beats, axis=0, keepdims=True)
    lane = lax.broadcasted_iota(jnp.int32, (1, LANES), 1)
    out = jnp.zeros((1, LANES), F32)
    for k in range(TOPN):
        nk = jnp.sum(jnp.where(rank == float(k), n_lane.astype(F32), 0.0), axis=-1, keepdims=True)
        out = jnp.where(lane == k, nk, out)
    idx_ref[0, 0] = jnp.broadcast_to(out, (8, LANES)).astype(jnp.int32)


def _cmpsel_step(q16, p1k, p2k, p1v, p2v, cwk, cwv, gate_rows, qpos):
    D, G, nc = p1k.shape[:3]
    n_sel = -(-(qpos + 1) // SEL_LEN)
    nsp = -(-n_sel // LANES) * LANES
    ov = jnp.asarray(np.pad(_overlap_t(nc, n_sel).T, ((0, 0), (0, nsp - n_sel))), BF16)
    big = pl.BlockSpec((1, 1, nc, HEAD_DIM), lambda d, g: (d, g, 0, 0))
    qspec = pl.BlockSpec((1, 1, 16, HEAD_DIM), lambda d, g: (d, g, 0, 0))
    w2spec = pl.BlockSpec((HEAD_DIM, HEAD_DIM), lambda d, g: (0, 0))
    return pl.pallas_call(
        functools.partial(_cmpsel_step_kernel, qpos, nsp),
        out_shape=[jax.ShapeDtypeStruct((D, G, 16, HEAD_DIM), F32),
                   jax.ShapeDtypeStruct((D, G, 8, LANES), jnp.int32)],
        grid=(D, G),
        in_specs=[qspec, big, big, big, big, w2spec, w2spec,
                  pl.BlockSpec((nc, nsp), lambda d, g: (0, 0)),
                  pl.BlockSpec((1, 1, 16, 3), lambda d, g: (d, g, 0, 0))],
        out_specs=[qspec, pl.BlockSpec((1, 1, 8, LANES), lambda d, g: (d, g, 0, 0))],
        compiler_params=_cparams(("arbitrary", "arbitrary")),
        name="nsa_cmpsel_step",
    )(q16, p1k, p2k, p1v, p2v, cwk[4], cwv[4], ov, gate_rows)


def _sel_step_kernel(n_past, pt_ref, idx_ref, q_ref, kn_ref, vn_ref, k_ref, v_ref, g_ref, o_ref,
                     m_scr, l_scr, acc_scr):
    d, g, k = pl.program_id(0), pl.program_id(1), pl.program_id(2)
    q = q_ref[0, 0] * (HEAD_DIM ** -0.5)

    @pl.when(k == 0)
    def _():
        m_scr[...] = jnp.sum(q * kn_ref[0, 0], axis=-1, keepdims=True)
        l_scr[...] = jnp.ones_like(l_scr)
        acc_scr[...] = jnp.broadcast_to(vn_ref[0, 0], acc_scr.shape)

    @pl.when(idx_ref[d, g, k] < n_past)
    def _():
        s = _dot_nt(q.astype(BF16), k_ref[...].astype(BF16))
        _online_update(s, v_ref[...].astype(BF16), m_scr, l_scr, acc_scr)

    @pl.when(k == pl.num_programs(2) - 1)
    def _():
        o_ref[0, 0] = _online_result(l_scr, acc_scr) * g_ref[0, 0][:, 1:2]


def _sel_step(page_table, idx, q16, k_new, v_new, cache_k, cache_v, gate_rows):
    D, NP = page_table.shape
    G = NSA_GROUPS
    n_pool = cache_k.shape[0]
    per_page = PAGE // SEL_LEN
    n_past = NP * per_page
    k2 = cache_k.reshape(n_pool * PAGE, G * HEAD_DIM)
    v2 = cache_v.reshape(n_pool * PAGE, G * HEAD_DIM)

    def blk(d, g, k, pt, ix):
        n = jnp.minimum(ix[d, g, k], n_past - 1)
        return (pt[d, n // per_page] * per_page + n % per_page, g)

    qspec = pl.BlockSpec((1, 1, 16, HEAD_DIM), lambda d, g, k, pt, ix: (d, g, 0, 0))
    nspec = pl.BlockSpec((1, 1, 1, HEAD_DIM), lambda d, g, k, pt, ix: (d, g, 0, 0))
    return pl.pallas_call(
        functools.partial(_sel_step_kernel, n_past),
        out_shape=jax.ShapeDtypeStruct((D, G, 16, HEAD_DIM), F32),
        grid_spec=pltpu.PrefetchScalarGridSpec(
            num_scalar_prefetch=2, grid=(D, G, TOPN),
            in_specs=[qspec, nspec, nspec,
                      pl.BlockSpec((SEL_LEN, HEAD_DIM), blk), pl.BlockSpec((SEL_LEN, HEAD_DIM), blk),
                      pl.BlockSpec((1, 1, 16, 3), lambda d, g, k, pt, ix: (d, g, 0, 0))],
            out_specs=qspec,
            scratch_shapes=[pltpu.VMEM((16, 1), F32), pltpu.VMEM((16, 1), F32), pltpu.VMEM((16, HEAD_DIM), F32)]),
        compiler_params=_cparams(("arbitrary",) * 3),
        name="nsa_sel_step",
    )(page_table, idx, q16, k_new, v_new, k2, v2, gate_rows)


def _win_step_kernel(q_ref, k_ref, v_ref, g_ref, o_ref):
    q = (q_ref[0, 0] * (HEAD_DIM ** -0.5)).astype(BF16)
    s = _dot_nt(q, k_ref[0].astype(BF16))
    e = jnp.exp(s - jnp.max(s, axis=-1, keepdims=True))
    p = e / jnp.sum(e, axis=-1, keepdims=True)
    o_ref[0, 0] = _dot(p.astype(BF16), v_ref[0].astype(BF16)) * g_ref[0, 0][:, 2:3]


def _win_step(q16, kw, vw, gate_rows):
    D, G = q16.shape[:2]
    L = kw.shape[1]
    qspec = pl.BlockSpec((1, 1, 16, HEAD_DIM), lambda d, g: (d, g, 0, 0))
    kspec = pl.BlockSpec((1, L, HEAD_DIM), lambda d, g: (d, 0, g))
    return pl.pallas_call(
        _win_step_kernel,
        out_shape=jax.ShapeDtypeStruct((D, G, 16, HEAD_DIM), F32),
        grid=(D, G),
        in_specs=[qspec, kspec, kspec, pl.BlockSpec((1, 1, 16, 3), lambda d, g: (d, g, 0, 0))],
        out_specs=qspec,
        compiler_params=_cparams(("arbitrary", "arbitrary")),
        name="nsa_win_step",
    )(q16, kw, vw, gate_rows)


TN = 512
TM = 512
TM_STEP = 8


def _pad_cols(a, n):
    return jnp.pad(a, ((0, 0), (0, n - a.shape[1])))


def _even_params(e_norm_mix, e_w_in, lb, out_norm, f_bias, q_norm, k_norm, e_w_out, e_norm_ffn, wg, wu, wd):
    d = e_w_in.shape[0]
    aw, bw = H_HGRN * HEAD_DIM, H_FOX * HEAD_DIM
    n_real = 4 * aw + 3 * bw + H_FOX
    n_pad = -(-n_real // TN) * TN
    ones = lambda n: jnp.ones((n,), F32)
    gain = jnp.concatenate([ones(4 * aw), jnp.tile(q_norm, H_FOX), jnp.tile(k_norm, H_FOX), ones(n_pad - 4 * aw - 2 * bw)])
    bias = jnp.concatenate([jnp.zeros((4 * aw + 3 * bw,), F32), f_bias.astype(F32),
                            jnp.zeros((n_pad - n_real,), F32)])
    nt = lambda w: w // TN
    segs, t0 = [], 0
    for width, mode in ((4 * aw, 'raw'), (bw, 'hnorm'), (bw, 'hnorm'), (bw, 'raw'), (TN, 'logsig')):
        segs.append((t0, nt(width), mode))
        t0 += nt(width)
    return dict(norm=e_norm_mix.reshape(1, d), w_in=_pad_cols(e_w_in.astype(BF16), n_pad), gain=gain.reshape(1, -1),
                bias=bias.reshape(1, -1), segs=segs, lb=lb.reshape(1, aw), out_norm=out_norm.reshape(1, HEAD_DIM),
                w_out=e_w_out.astype(BF16), norm_ffn=e_norm_ffn.reshape(1, d),
                wg=wg.astype(BF16), wu=wu.astype(BF16), wd=wd.astype(BF16))


def _even_proj(x2, p, tm):
    zeros = jnp.zeros((tm, HEAD_DIM), F32)
    return _proj(x2, p['norm'], p['w_in'], p['gain'], p['bias'], zeros, zeros, p['segs'], tm=tm, tn=TN,
                 name="even_proj")


def _even_finish(x2, oa, of, p, tm):
    x2 = _outproj([jnp.concatenate([oa, of], axis=1)], p['w_out'], x2, tm=tm, tn=TN, name="even_out")
    return _ffn(x2, p['norm_ffn'], p['wg'], p['wu'], p['wd'], tm=tm, tf=TN, name="ffn")


def _odd_params(o_norm_mix, o_w_in, q_norm, ck_norm, sk_norm, wk_norm, pe_k, w1_k, w2_k, pe_v, w1_v, w2_v,
                o_w_out, o_norm_ffn, router_w, router_b, mg, mu, md):
    d = o_w_in.shape[0]
    qw, kvw = H_NSA * HEAD_DIM, NSA_GROUPS * HEAD_DIM
    n_real = qw + 6 * kvw + 3 * H_NSA
    n_pad = -(-n_real // TN) * TN
    ones = jnp.ones((kvw,), F32)
    gain = jnp.concatenate([jnp.tile(q_norm, H_NSA), jnp.tile(ck_norm, NSA_GROUPS), ones,
                            jnp.tile(sk_norm, NSA_GROUPS), ones, jnp.tile(wk_norm, NSA_GROUPS), ones,
                            jnp.ones((n_pad - qw - 6 * kvw,), F32)])
    segs, t0 = [], 0
    for width, mode in ((qw, 'hnorm_both'), (kvw, 'hnorm'), (kvw, 'raw'), (kvw, 'hnorm_rope'), (kvw, 'raw'),
                        (kvw, 'hnorm_rope'), (kvw, 'raw'), (TN, 'sigmoid')):
        segs.append((t0, width // TN, mode))
        t0 += width // TN
    return dict(norm=o_norm_mix.reshape(1, d), w_in=_pad_cols(o_w_in.astype(BF16), n_pad), gain=gain.reshape(1, -1),
                bias=jnp.zeros((1, n_pad), F32), segs=segs,
                cwk=_cmp_weights(pe_k, w1_k, w2_k), cwv=_cmp_weights(pe_v, w1_v, w2_v),
                w_out=o_w_out.astype(BF16), norm_ffn=o_norm_ffn.reshape(1, d),
                rw=_pad_cols(router_w.astype(BF16), LANES), rb=_pad_cols(router_b.reshape(1, -1).astype(F32), LANES),
                mg=mg.astype(BF16), mu=mu.astype(BF16), md=md.astype(BF16))


def _odd_proj(x2, p, pos, tm):
    cosf, sinf = _rope_tables(pos)
    return _proj(x2, p['norm'], p['w_in'], p['gain'], p['bias'], cosf, sinf, p['segs'], tm=tm, tn=TN,
                 name="odd_proj")


def _odd_finish(x2, branches, p, tm):
    x2 = _outproj(branches, p['w_out'], x2, tm=tm, tn=TN, name="odd_out")
    return _moe(x2, p['norm_ffn'], p['rw'], p['rb'], p['mg'], p['mu'], p['md'], tm=tm, tf=256, name="moe")


def kernel(x_prompt, x_sample, state_hgrn, cache_fox_k, cache_fox_v, cache_fox_logf, cache_nsa_cmp_k, cache_nsa_cmp_v, cache_nsa_sel_k, cache_nsa_sel_v, cache_nsa_win_k, cache_nsa_win_v, page_table, e_norm_mix, e_w_in, hgrn_lb_logits, hgrn_out_norm, fox_f_bias, fox_q_norm, fox_k_norm, e_w_out, e_norm_ffn, ffn_w_gate, ffn_w_up, ffn_w_down, o_norm_mix, o_w_in, nsa_q_norm, nsa_cmp_k_norm, nsa_sel_k_norm, nsa_win_k_norm, cmp_pe_k, cmp_w1_k, cmp_w2_k, cmp_pe_v, cmp_w1_v, cmp_w2_v, o_w_out, o_norm_ffn, router_w, router_b, moe_w_gate, moe_w_up, moe_w_down):
    B, S, d = x_prompt.shape
    D, T, _ = x_sample.shape
    NP = page_table.shape[1]
    past = NP * PAGE
    w_buf = cache_nsa_win_k.shape[2]
    assert T == 1 and w_buf == WINDOW and S >= WINDOW and cache_fox_k.shape[2] == PAGE
    G = NSA_GROUPS
    lbs = jnp.cumsum(jax.nn.softmax(hgrn_lb_logits.astype(F32), axis=0), axis=0)
    xp, xd = x_prompt.reshape(B * S, d), x_sample.reshape(D, d)
    tm_p = min(TM, S)

    li = 0
    p = _even_params(e_norm_mix[li], e_w_in[li], lbs[li], hgrn_out_norm[li], fox_f_bias[li], fox_q_norm[li],
                     fox_k_norm[li], e_w_out[li], e_norm_ffn[li], ffn_w_gate[li], ffn_w_up[li], ffn_w_down[li])
    hg, fq, fk, fv, fl = _even_proj(xp, p, tm_p)
    c = _seq_cumsum(fl, B, S, tc=tm_p)
    oa, st_p = _hgrn_prompt(hg, p['lb'], p['out_norm'], B, S, T=min(256, S), c=32)
    of = _fox_prompt(fq, fk, fv, c, B, S, tq=tm_p, tk=tm_p)
    xp = _even_finish(xp, oa, of, p, tm_p)
    fox_p = (fk.reshape(1, B, S, H_FOX, HEAD_DIM), fv.reshape(1, B, S, H_FOX, HEAD_DIM),
             fl[:, :H_FOX].reshape(1, B, S, H_FOX))

    hg, fq, fk, fv, fl = _even_proj(xd, p, TM_STEP)
    oa, st_d = _hgrn_step(hg, p['lb'], p['out_norm'], state_hgrn[li].astype(F32))
    heads = lambda a: a.reshape(D, H_FOX, HEAD_DIM)
    of = _fox_step(page_table, heads(fq), heads(fk), heads(fv), fl[:, :H_FOX],
                   cache_fox_k[li], cache_fox_v[li], cache_fox_logf[li], PG=8)
    xd = _even_finish(xd, oa, of.reshape(D, H_FOX * HEAD_DIM), p, TM_STEP)
    fox_d = (fk.reshape(1, D, 1, H_FOX, HEAD_DIM), fv.reshape(1, D, 1, H_FOX, HEAD_DIM),
             fl[:, :H_FOX].reshape(1, D, 1, H_FOX))

    p = _odd_params(o_norm_mix[li], o_w_in[li], nsa_q_norm[li], nsa_cmp_k_norm[li], nsa_sel_k_norm[li],
                    nsa_win_k_norm[li], cmp_pe_k[li], cmp_w1_k[li], cmp_w2_k[li], cmp_pe_v[li], cmp_w1_v[li],
                    cmp_w2_v[li], o_w_out[li], o_norm_ffn[li], router_w[li], router_b[li],
                    moe_w_gate[li], moe_w_up[li], moe_w_down[li])
    qn, qr, ck, cv, sk, sv, wk, wv, gt = _odd_proj(xp, p, jnp.arange(S), tm_p)
    seq = lambda a: a.reshape(B, S, -1)
    gates_g = jnp.pad(gt[:, :3 * H_NSA].reshape(B, S, G, 3 * NSA_R).transpose(0, 2, 1, 3),
                      ((0, 0), (0, 0), (0, 0), (0, 16 - 3 * NSA_R)))
    kc = _compress_prompt(seq(ck), p['cwk'], B, S)
    vc = _compress_prompt(seq(cv), p['cwv'], B, S)
    tq = min(256, S)
    o_cmp, selneg = _cmpsel_prompt(seq(qn), kc, vc, gates_g, B, S, Tq=tq)
    o_sel = _sel_prompt(seq(qr), selneg, seq(sk), seq(sv), gates_g, B, S, Tq=tq, tk=min(512, S))
    o_win = _win_prompt(seq(qr), seq(wk), seq(wv), gates_g, B, S, Tq=tq)
    flat = lambda a: a.reshape(B * S, -1)
    xp = _odd_finish(xp, [flat(o_cmp), flat(o_sel), flat(o_win)], p, tm_p)
    kv = lambda a: a.reshape(1, B, S, G, HEAD_DIM)
    nsa_p = (kv(ck), kv(cv), kv(sk), kv(sv), kv(wk)[:, :, S - WINDOW:], kv(wv)[:, :, S - WINDOW:])

    qn, qr, ck, cv, sk, sv, wk, wv, gt = _odd_proj(xd, p, jnp.full((TM_STEP,), past), TM_STEP)
    rows16 = lambda a, w: jnp.pad(a.reshape(D, G, NSA_R, w), ((0, 0), (0, 0), (0, 16 - NSA_R), (0, 0)))
    gate_rows = rows16(gt[:, :3 * H_NSA], 3)
    q16n, q16r = rows16(qn, HEAD_DIM), rows16(qr, HEAD_DIM)
    p1k, p2k = _cmp_pages(page_table, cache_nsa_cmp_k[li], p['cwk'], PG=16)
    p1v, p2v = _cmp_pages(page_table, cache_nsa_cmp_v[li], p['cwv'], PG=16)
    o_cmp, idx = _cmpsel_step(q16n, p1k, p2k, p1v, p2v, p['cwk'], p['cwv'], gate_rows, past)
    new = lambda a: a.reshape(D, G, 1, HEAD_DIM)
    o_sel = _sel_step(page_table, idx[:, :, 0, :TOPN], q16r, new(sk), new(sv),
                      cache_nsa_sel_k[li], cache_nsa_sel_v[li], gate_rows)
    kvd = lambda a: a.reshape(D, 1, G, HEAD_DIM)
    win_k = jnp.concatenate([cache_nsa_win_k[li], kvd(wk)], axis=1)[:, -w_buf:]
    win_v = jnp.concatenate([cache_nsa_win_v[li], kvd(wv)], axis=1)[:, -w_buf:]
    o_win = _win_step(q16r, win_k.reshape(D, w_buf, G * HEAD_DIM), win_v.reshape(D, w_buf, G * HEAD_DIM), gate_rows)
    unrow = lambda a: a[:, :, :NSA_R].reshape(D, H_NSA * HEAD_DIM)
    xd = _odd_finish(xd, [unrow(o_cmp), unrow(o_sel), unrow(o_win)], p, TM_STEP)
    kv1 = lambda a: a.reshape(1, D, 1, G, HEAD_DIM)
    nsa_d = (kv1(ck), kv1(cv), kv1(sk), kv1(sv), win_k[None], win_v[None])

    return (xp.reshape(B, S, d), xd.reshape(D, T, d), st_p[None].astype(state_hgrn.dtype),
            st_d[None].astype(state_hgrn.dtype), *fox_p, *fox_d, *nsa_p, *nsa_d)
```

```python
import functools

import numpy as np
import jax
import jax.numpy as jnp
from jax import lax
from jax.experimental import pallas as pl
from jax.experimental.pallas import tpu as pltpu

F32 = jnp.float32
BF16 = jnp.bfloat16

LANES = 128
HEAD_DIM = 128
EPS = 1e-6
ROPE_THETA = 10000.0
LOG2E = 1.4426950408889634
QSCALE = HEAD_DIM ** -0.5 * LOG2E
NEG = -1e30
M_INIT = -1e20
VMEM_LIMIT = 56 * 1024 * 1024

H_HGRN = 8
H_FOX = 8
H_NSA = 16
NSA_GROUPS = 4
NSA_R = H_NSA // NSA_GROUPS
CMP_STRIDE = 16
CMP_LEN = 32
SEL_LEN = 64
SEL_SHIFT = 6
TOPN = 16
WINDOW = 512
N_EXPERTS = 8
PAGE = 128


def _cparams(sem):
    return pltpu.CompilerParams(dimension_semantics=sem, vmem_limit_bytes=VMEM_LIMIT)


def _dot(a, b):
    return jnp.dot(a, b, preferred_element_type=F32)


def _dot_nt(a, b):
    return lax.dot_general(a, b, (((1,), (1,)), ((), ())), preferred_element_type=F32)


def _dot_tn(a, b):
    return lax.dot_general(a, b, (((0,), (0,)), ((), ())), preferred_element_type=F32)


def _split3(x):
    hi = x.astype(BF16)
    r1 = x - hi.astype(F32)
    mid = r1.astype(BF16)
    lo = (r1 - mid.astype(F32)).astype(BF16)
    return hi, mid, lo


def _dot3(w_bf16, x):
    hi, mid, lo = _split3(x)
    return _dot(w_bf16, hi) + _dot(w_bf16, mid) + _dot(w_bf16, lo)


def _sigmoid(x):
    return 1.0 / (1.0 + jnp.exp(-x))


def _silu(x):
    return x * _sigmoid(x)


def _log_sigmoid(x):
    return jnp.minimum(x, 0.0) - jnp.log(1.0 + jnp.exp(-jnp.abs(x)))


def _tril_bf16(n):
    r = lax.broadcasted_iota(jnp.int32, (n, n), 0)
    c = lax.broadcasted_iota(jnp.int32, (n, n), 1)
    return jnp.where(c <= r, 1.0, 0.0).astype(BF16)


def _head_rms(a, gain):
    ms = jnp.mean(a * a, axis=-1, keepdims=True)
    return a * lax.rsqrt(ms + EPS) * gain


def _rope(y, cosf, sinf):
    return y * cosf + pltpu.roll(y, HEAD_DIM // 2, 1) * sinf


def _rope_tables(pos):
    half = HEAD_DIM // 2
    inv = ROPE_THETA ** (-jnp.arange(half, dtype=F32) / half)
    ang = pos.astype(F32)[:, None] * inv[None, :]
    cos, sin = jnp.cos(ang), jnp.sin(ang)
    return jnp.concatenate([cos, cos], axis=-1), jnp.concatenate([-sin, sin], axis=-1)


def _proj_kernel(segs, tn, *refs):
    n_out = sum(2 if s[2] == 'hnorm_both' else 1 for s in segs)
    x_ref, g_ref, w_ref, gain_ref, bias_ref, cos_ref, sin_ref = refs[:7]
    out_refs = refs[7:7 + n_out]
    xn_ref = refs[7 + n_out]
    j = pl.program_id(1)

    @pl.when(j == 0)
    def _():
        xf = x_ref[...]
        ms = jnp.mean(xf * xf, axis=-1, keepdims=True)
        xn_ref[...] = (xf * lax.rsqrt(ms + EPS) * g_ref[...]).astype(BF16)

    acc = _dot(xn_ref[...], w_ref[...])
    oi = 0
    for (t0, nt, mode) in segs:
        outs = out_refs[oi:oi + (2 if mode == 'hnorm_both' else 1)]
        oi += len(outs)

        @pl.when((j >= t0) & (j < t0 + nt))
        def _(mode=mode, outs=outs):
            if mode == 'raw':
                outs[0][...] = acc
            elif mode == 'sigmoid':
                outs[0][...] = _sigmoid(acc)
            elif mode == 'logsig':
                outs[0][...] = _log_sigmoid(acc + bias_ref[...])
            else:
                for h in range(tn // HEAD_DIM):
                    sl = slice(h * HEAD_DIM, (h + 1) * HEAD_DIM)
                    y = _head_rms(acc[:, sl], gain_ref[:, sl])
                    if mode == 'hnorm':
                        outs[0][:, sl] = y
                    elif mode == 'hnorm_rope':
                        outs[0][:, sl] = _rope(y, cos_ref[...], sin_ref[...])
                    else:
                        outs[0][:, sl] = y
                        outs[1][:, sl] = _rope(y, cos_ref[...], sin_ref[...])


def _proj(x, norm_g, w, gain_all, bias_all, cosf, sinf, segs, *, tm, tn, name):
    M, K = x.shape
    ntiles = w.shape[1] // tn
    assert M % tm == 0 and w.shape[1] == ntiles * tn and segs[-1][0] + segs[-1][1] == ntiles
    nseq = cosf.shape[0] // tm
    out_shape, out_specs = [], []
    for (t0, nt, mode) in segs:
        for _ in range(2 if mode == 'hnorm_both' else 1):
            out_shape.append(jax.ShapeDtypeStruct((M, nt * tn), F32))
            out_specs.append(pl.BlockSpec(
                (tm, tn), lambda i, j, t0=t0, nt=nt: (i, jnp.clip(j - t0, 0, nt - 1))))
    return pl.pallas_call(
        functools.partial(_proj_kernel, tuple(segs), tn),
        out_shape=out_shape,
        grid=(M // tm, ntiles),
        in_specs=[
            pl.BlockSpec((tm, K), lambda i, j: (i, 0)),
            pl.BlockSpec((1, K), lambda i, j: (0, 0)),
            pl.BlockSpec((K, tn), lambda i, j: (0, j)),
            pl.BlockSpec((1, tn), lambda i, j: (0, j)),
            pl.BlockSpec((1, tn), lambda i, j: (0, j)),
            pl.BlockSpec((tm, HEAD_DIM), lambda i, j: (i % nseq, 0)),
            pl.BlockSpec((tm, HEAD_DIM), lambda i, j: (i % nseq, 0)),
        ],
        out_specs=out_specs,
        scratch_shapes=[pltpu.VMEM((tm, K), BF16)],
        compiler_params=_cparams(("arbitrary", "arbitrary")),
        name=name,
    )(x, norm_g, w, gain_all, bias_all, cosf, sinf)


def _outproj_kernel(n_lhs, *refs):
    lhs = refs[:n_lhs]
    w_ref, res_ref, o_ref, xs_ref = refs[n_lhs:n_lhs + 4]

    @pl.when(pl.program_id(1) == 0)
    def _():
        x = lhs[0][...]
        for r in lhs[1:]:
            x = x + r[...]
        xs_ref[...] = x.astype(BF16)

    o_ref[...] = res_ref[...] + _dot(xs_ref[...], w_ref[...])


def _outproj(lhs_list, w, res, *, tm, tn, name):
    M, K = lhs_list[0].shape
    N = w.shape[1]
    n = len(lhs_list)
    return pl.pallas_call(
        functools.partial(_outproj_kernel, n),
        out_shape=jax.ShapeDtypeStruct((M, N), F32),
        grid=(M // tm, N // tn),
        in_specs=[pl.BlockSpec((tm, K), lambda i, j: (i, 0)) for _ in range(n)] + [
            pl.BlockSpec((K, tn), lambda i, j: (0, j)),
            pl.BlockSpec((tm, tn), lambda i, j: (i, j)),
        ],
        out_specs=pl.BlockSpec((tm, tn), lambda i, j: (i, j)),
        scratch_shapes=[pltpu.VMEM((tm, K), BF16)],
        compiler_params=_cparams(("arbitrary", "arbitrary")),
        name=name,
    )(*lhs_list, w, res)


def _ffn_kernel(x_ref, g_ref, wg_ref, wu_ref, wd_ref, o_ref, xn_ref):
    f = pl.program_id(1)

    @pl.when(f == 0)
    def _():
        xf = x_ref[...]
        ms = jnp.mean(xf * xf, axis=-1, keepdims=True)
        xn_ref[...] = (xf * lax.rsqrt(ms + EPS) * g_ref[...]).astype(BF16)
        o_ref[...] = xf

    xn = xn_ref[...]
    h = _silu(_dot(xn, wg_ref[...])) * _dot(xn, wu_ref[...])
    o_ref[...] += _dot(h.astype(BF16), wd_ref[...])


def _ffn(x, norm_g, wg, wu, wd, *, tm, tf, name):
    M, K = x.shape
    F = wg.shape[1]
    return pl.pallas_call(
        _ffn_kernel,
        out_shape=jax.ShapeDtypeStruct((M, K), F32),
        grid=(M // tm, F // tf),
        in_specs=[
            pl.BlockSpec((tm, K), lambda i, f: (i, 0)),
            pl.BlockSpec((1, K), lambda i, f: (0, 0)),
            pl.BlockSpec((K, tf), lambda i, f: (0, f)),
            pl.BlockSpec((K, tf), lambda i, f: (0, f)),
            pl.BlockSpec((tf, K), lambda i, f: (f, 0)),
        ],
        out_specs=pl.BlockSpec((tm, K), lambda i, f: (i, 0)),
        scratch_shapes=[pltpu.VMEM((tm, K), BF16)],
        compiler_params=_cparams(("arbitrary", "arbitrary")),
        name=name,
    )(x, norm_g, wg, wu, wd)


def _moe_kernel(x_ref, g_ref, rw_ref, rb_ref, wg_ref, wu_ref, wd_ref, o_ref, xn_ref, gate_ref):
    e = pl.program_id(1)
    f = pl.program_id(2)

    @pl.when((e == 0) & (f == 0))
    def _():
        xf = x_ref[...]
        ms = jnp.mean(xf * xf, axis=-1, keepdims=True)
        xn = (xf * lax.rsqrt(ms + EPS) * g_ref[...]).astype(BF16)
        xn_ref[...] = xn
        o_ref[...] = xf
        lane = lax.broadcasted_iota(jnp.int32, (xf.shape[0], LANES), 1)
        logits = jnp.where(lane < N_EXPERTS, _dot(xn, rw_ref[...]) + rb_ref[...], NEG)
        m1 = jnp.max(logits, axis=-1, keepdims=True)
        i1 = jnp.min(jnp.where(logits == m1, lane, LANES), axis=-1, keepdims=True)
        l2 = jnp.where(lane == i1, NEG, logits)
        m2 = jnp.max(l2, axis=-1, keepdims=True)
        i2 = jnp.min(jnp.where(l2 == m2, lane, LANES), axis=-1, keepdims=True)
        e2 = jnp.exp(m2 - m1)
        w1 = 1.0 / (1.0 + e2)
        gate_ref[...] = jnp.where(lane == i1, w1, 0.0) + jnp.where(lane == i2, e2 * w1, 0.0)

    xn = xn_ref[...]
    h = _silu(_dot(xn, wg_ref[0])) * _dot(xn, wu_ref[0])
    lane = lax.broadcasted_iota(jnp.int32, gate_ref.shape, 1)
    ge = jnp.sum(jnp.where(lane == e, gate_ref[...], 0.0), axis=-1, keepdims=True)
    o_ref[...] += ge * _dot(h.astype(BF16), wd_ref[0])


def _moe(x, norm_g, rw, rb, wg, wu, wd, *, tm, tf, name):
    M, K = x.shape
    E, _, F = wg.shape
    return pl.pallas_call(
        _moe_kernel,
        out_shape=jax.ShapeDtypeStruct((M, K), F32),
        grid=(M // tm, E, F // tf),
        in_specs=[
            pl.BlockSpec((tm, K), lambda i, e, f: (i, 0)),
            pl.BlockSpec((1, K), lambda i, e, f: (0, 0)),
            pl.BlockSpec((K, LANES), lambda i, e, f: (0, 0)),
            pl.BlockSpec((1, LANES), lambda i, e, f: (0, 0)),
            pl.BlockSpec((1, K, tf), lambda i, e, f: (e, 0, f)),
            pl.BlockSpec((1, K, tf), lambda i, e, f: (e, 0, f)),
            pl.BlockSpec((1, tf, K), lambda i, e, f: (e, f, 0)),
        ],
        out_specs=pl.BlockSpec((tm, K), lambda i, e, f: (i, 0)),
        scratch_shapes=[pltpu.VMEM((tm, K), BF16), pltpu.VMEM((tm, LANES), F32)],
        compiler_params=_cparams(("arbitrary", "arbitrary", "arbitrary")),
        name=name,
    )(x, norm_g, rw, rb, wg, wu, wd)


def _hgrn_kernel(T, c, aq_ref, af_ref, ai_ref, ag_ref, lb_ref, gn_ref, o_ref, st_ref, st_scr):
    t = pl.program_id(2)

    @pl.when(t == 0)
    def _():
        st_scr[...] = jnp.zeros_like(st_scr)

    lb = lb_ref[...]
    tril = _tril_bf16(c)
    row = lax.broadcasted_iota(jnp.int32, (c, c), 0)
    col = lax.broadcasted_iota(jnp.int32, (c, c), 1)
    for u in range(T // c):
        rows = pl.ds(u * c, c)
        gate = lb + (1.0 - lb) * _sigmoid(af_ref[rows, :])
        k = 1.0 - gate
        b = _dot3(tril, jnp.log(gate))
        qe = (_silu(aq_ref[rows, :]) * jnp.exp(b)).astype(BF16)
        ke = (k * jnp.exp(-b)).astype(BF16)
        sc = jnp.where(col <= row, _dot_nt(qe, ke), 0.0)
        st = st_scr[...]
        vb = ai_ref[rows, :].astype(BF16)
        o = _dot(sc.astype(BF16), vb) + _dot_nt(qe, st.astype(BF16))
        bend = b[c - 1:c, :]
        kend = (k * jnp.exp(bend - b)).astype(BF16)
        st_scr[...] = st * jnp.exp(bend) + _dot_tn(vb, kend)
        o_ref[rows, :] = _head_rms(o, gn_ref[...]) * _silu(ag_ref[rows, :])

    @pl.when(t == pl.num_programs(2) - 1)
    def _():
        st_ref[0, 0] = st_scr[...].T


def _hgrn_prompt(hg, lb, gn, B, S, *, T, c):
    nt = S // T
    H = H_HGRN
    spec = lambda off: pl.BlockSpec((T, HEAD_DIM), lambda b, h, t, off=off: (b * nt + t, off + h))
    return pl.pallas_call(
        functools.partial(_hgrn_kernel, T, c),
        out_shape=[jax.ShapeDtypeStruct((B * S, H * HEAD_DIM), F32),
                   jax.ShapeDtypeStruct((B, H, HEAD_DIM, HEAD_DIM), F32)],
        grid=(B, H, nt),
        in_specs=[spec(0), spec(H), spec(2 * H), spec(3 * H),
                  pl.BlockSpec((1, HEAD_DIM), lambda b, h, t: (0, h)),
                  pl.BlockSpec((1, HEAD_DIM), lambda b, h, t: (0, 0))],
        out_specs=[pl.BlockSpec((T, HEAD_DIM), lambda b, h, t: (b * nt + t, h)),
                   pl.BlockSpec((1, 1, HEAD_DIM, HEAD_DIM), lambda b, h, t: (b, h, 0, 0))],
        scratch_shapes=[pltpu.VMEM((HEAD_DIM, HEAD_DIM), F32)],
        compiler_params=_cparams(("arbitrary", "arbitrary", "arbitrary")),
        name="hgrn_prompt",
    )(hg, hg, hg, hg, lb, gn)


BIAS_SHIFT = 4
BIAS_SLOT = 1 << BIAS_SHIFT


def _cumsum_kernel(lf_ref, pq_ref, pk_ref, oq_ref, ok_ref, aq_ref, ak_ref, carry):
    @pl.when(pl.program_id(1) == 0)
    def _():
        carry[...] = jnp.zeros_like(carry)

    tc = lf_ref.shape[0]
    c = _dot3(_tril_bf16(tc), lf_ref[...]) + carry[...]
    carry[...] = c[tc - 1:tc, :]
    parts = _split3(c * LOG2E)
    aq_ref[...] = sum(_dot(t, pq_ref[i]) for i, t in enumerate(parts)) + oq_ref[...]
    ak_ref[...] = (ok_ref[...] - sum(_dot(t, pk_ref[i]) for i, t in enumerate(parts))).astype(BF16)


def _bias_layout():
    pq, pk = np.zeros((3, LANES, LANES), np.float32), np.zeros((3, LANES, LANES), np.float32)
    oq, ok = np.zeros((1, LANES), np.float32), np.zeros((1, LANES), np.float32)
    for h in range(H_FOX):
        for i in range(3):
            pq[i, h, BIAS_SLOT * h + i] = 1.0
            pk[i, h, BIAS_SLOT * h + 3 + i] = 1.0
            oq[0, BIAS_SLOT * h + 3 + i] = 1.0
            ok[0, BIAS_SLOT * h + i] = 1.0
    return jnp.asarray(pq, BF16), jnp.asarray(pk, BF16), jnp.asarray(oq), jnp.asarray(ok)


def _seq_cumsum(lf, B, S, *, tc):
    n = S // tc
    const3 = pl.BlockSpec((3, LANES, LANES), lambda b, t: (0, 0, 0))
    const1 = pl.BlockSpec((1, LANES), lambda b, t: (0, 0))
    blk = pl.BlockSpec((tc, LANES), lambda b, t: (b * n + t, 0))
    return pl.pallas_call(
        _cumsum_kernel,
        out_shape=[jax.ShapeDtypeStruct((B * S, LANES), F32), jax.ShapeDtypeStruct((B * S, LANES), BF16)],
        grid=(B, n),
        in_specs=[blk, const3, const3, const1, const1],
        out_specs=[blk, blk],
        scratch_shapes=[pltpu.VMEM((1, LANES), F32)],
        compiler_params=_cparams(("arbitrary", "arbitrary")),
        name="seq_cumsum",
    )(lf, *_bias_layout())


def _online_init(m_scr, l_scr, acc_scr):
    m_scr[...] = jnp.full_like(m_scr, M_INIT)
    l_scr[...] = jnp.zeros_like(l_scr)
    acc_scr[...] = jnp.zeros_like(acc_scr)


def _online_update(s, vb, m_scr, l_scr, acc_scr):
    m_prev = m_scr[...]
    m_new = jnp.maximum(m_prev, jnp.max(s, axis=-1, keepdims=True))
    alpha = jnp.exp(m_prev - m_new)
    p = jnp.exp(s - m_new)
    l_scr[...] = alpha * l_scr[...] + jnp.sum(p, axis=-1, keepdims=True)
    acc_scr[...] = alpha * acc_scr[...] + _dot(p.astype(BF16), vb)
    m_scr[...] = m_new


def _online_result(l_scr, acc_scr):
    l = l_scr[...]
    return acc_scr[...] / jnp.where(l > 0.0, l, 1.0)


def _flash_t_step(sT, vT, m, l, acc_scr):
    m_new = jnp.maximum(m, jnp.max(sT, axis=0, keepdims=True))
    alpha = jnp.exp2(m - m_new)
    p = jnp.exp2(sT - m_new)
    acc_scr[...] = alpha * acc_scr[...] + _dot(vT, p.astype(BF16))
    return m_new, alpha * l + jnp.sum(p, axis=0, keepdims=True)


def _flash_t_causal(ka_scr, vt_scr, qa, acc_scr, n_full, n_diag, tk, qpos):
    R = qa.shape[1]
    acc_scr[...] = jnp.zeros_like(acc_scr)

    def body(kj, carry):
        return _flash_t_step(_dot(ka_scr[kj], qa), vt_scr[kj], *carry, acc_scr)

    m, l = lax.fori_loop(0, n_full, body, (jnp.full((1, R), M_INIT, F32), jnp.zeros((1, R), F32)))
    for i in range(n_diag):
        kpos = (n_full + i) * tk + lax.broadcasted_iota(jnp.int32, (tk, R), 0)
        sT = jnp.where(kpos <= qpos, _dot(ka_scr[n_full + i], qa), NEG)
        m, l = _flash_t_step(sT, vt_scr[n_full + i], m, l, acc_scr)
    return acc_scr[...] / jnp.where(l > 0.0, l, 1.0)


def _fox_kernel(tq, tk, q_ref, k_ref, vt_ref, aq_ref, ak_ref, o_ref, ka_scr, vt_scr, acc_scr):
    h = pl.program_id(1)
    qi = pl.program_id(2)

    @pl.when(qi == 0)
    def _():
        for c in range(ka_scr.shape[0]):
            rows = slice(c * tk, (c + 1) * tk)
            ka_scr[c] = jnp.concatenate([k_ref[rows, :].astype(BF16), ak_ref[rows, :]], axis=1)
            vt_scr[c] = vt_ref[0, 0, :, rows].astype(BF16)

    lane = lax.broadcasted_iota(jnp.int32, (tq, LANES), 1)
    aq = jnp.where((lane >> BIAS_SHIFT) == h, aq_ref[...], 0.0)
    qa = jnp.concatenate([(q_ref[...] * QSCALE).T, aq.T], axis=0).astype(BF16)
    qpos = qi * tq + lax.broadcasted_iota(jnp.int32, (tk, tq), 1)
    o_ref[...] = _flash_t_causal(ka_scr, vt_scr, qa, acc_scr, qi * (tq // tk), tq // tk, tk, qpos).T


def _fox_prompt(fq, fk, fvt, aug_q, aug_k, B, S, *, tq, tk):
    nq = S // tq
    return pl.pallas_call(
        functools.partial(_fox_kernel, tq, tk),
        out_shape=jax.ShapeDtypeStruct(fq.shape, F32),
        grid=(B, H_FOX, nq),
        in_specs=[
            pl.BlockSpec((tq, HEAD_DIM), lambda b, h, qi: (b * nq + qi, h)),
            pl.BlockSpec((S, HEAD_DIM), lambda b, h, qi: (b, h)),
            pl.BlockSpec((1, 1, HEAD_DIM, S), lambda b, h, qi: (b, h, 0, 0)),
            pl.BlockSpec((tq, LANES), lambda b, h, qi: (b * nq + qi, 0)),
            pl.BlockSpec((S, LANES), lambda b, h, qi: (b, 0)),
        ],
        out_specs=pl.BlockSpec((tq, HEAD_DIM), lambda b, h, qi: (b * nq + qi, h)),
        scratch_shapes=[pltpu.VMEM((S // tk, tk, 2 * HEAD_DIM), BF16), pltpu.VMEM((S // tk, HEAD_DIM, tk), BF16),
                        pltpu.VMEM((HEAD_DIM, tq), F32)],
        compiler_params=_cparams(("arbitrary",) * 3),
        name="fox_prompt",
    )(fq, fk, fvt, aug_q, aug_k)


def _compress_kernel(nc, x_ref, pea_ref, peb_ref, w1a_ref, w1b_ref, w2_ref, o_ref):
    R = jnp.concatenate([x_ref[0, pl.ds(r, nc, stride=CMP_STRIDE), :] for r in range(CMP_STRIDE)], axis=1)
    p1 = _dot((R + pea_ref[...]).astype(BF16), w1a_ref[...])
    p2 = _dot((R + peb_ref[...]).astype(BF16), w1b_ref[...])
    pre = p1 + pltpu.roll(p2, nc - 1, 0)
    kc = _dot(_silu(pre).astype(BF16), w2_ref[...])
    row = lax.broadcasted_iota(jnp.int32, kc.shape, 0)
    o_ref[0, 0] = jnp.where(row < nc - 1, kc, 0.0)


def _cmp_weights(pe, w1, w2):
    half = CMP_STRIDE * HEAD_DIM
    return (pe[:CMP_STRIDE].reshape(1, half), pe[CMP_STRIDE:].reshape(1, half),
            w1[:half].astype(BF16), w1[half:].astype(BF16), w2.astype(BF16))


def _compress_prompt(x, cw, B, S):
    nc = S // CMP_STRIDE
    half = CMP_STRIDE * HEAD_DIM
    const = lambda shape: pl.BlockSpec(shape, lambda b, g: (0, 0))
    return pl.pallas_call(
        functools.partial(_compress_kernel, nc),
        out_shape=jax.ShapeDtypeStruct((B, NSA_GROUPS, nc, HEAD_DIM), F32),
        grid=(B, NSA_GROUPS),
        in_specs=[pl.BlockSpec((1, S, HEAD_DIM), lambda b, g: (b, 0, g)),
                  const((1, half)), const((1, half)), const((half, HEAD_DIM)), const((half, HEAD_DIM)),
                  const((HEAD_DIM, HEAD_DIM))],
        out_specs=pl.BlockSpec((1, 1, nc, HEAD_DIM), lambda b, g: (b, g, 0, 0)),
        compiler_params=_cparams(("arbitrary", "arbitrary")),
        name="nsa_compress",
    )(x, *cw)


def _overlap_t(n_cmp, n_sel):
    cs = np.arange(n_cmp)[None, :] * CMP_STRIDE
    ss = np.arange(n_sel)[:, None] * SEL_LEN
    ov = np.minimum(cs + CMP_LEN, ss + SEL_LEN) - np.maximum(cs, ss)
    return np.maximum(ov, 0).astype(np.float32) / CMP_LEN


def _group_rows(ref, Tq):
    return jnp.concatenate([ref[0, :, r * HEAD_DIM:(r + 1) * HEAD_DIM] for r in range(NSA_R)], axis=0)


def _store_gated(o_ref, o, gates, Tq, branch):
    for r in range(NSA_R):
        gcol = gates[:, 3 * r + branch:3 * r + branch + 1]
        o_ref[0, :, r * HEAD_DIM:(r + 1) * HEAD_DIM] = o[r * Tq:(r + 1) * Tq] * gcol


def _cmpsel_kernel(Tq, nsel, q_ref, kc_ref, vc_ref, ovt_ref, gate_ref, o_ref, sel_ref):
    qi = pl.program_id(2)
    q = (_group_rows(q_ref, Tq) * (HEAD_DIM ** -0.5)).astype(BF16)
    s = _dot_nt(q, kc_ref[0, 0].astype(BF16))
    qpos = qi * Tq + (lax.broadcasted_iota(jnp.int32, s.shape, 0) & (Tq - 1))
    cend = lax.broadcasted_iota(jnp.int32, s.shape, 1) * CMP_STRIDE + (CMP_LEN - 1)
    mask = cend <= qpos
    sm = jnp.where(mask, s, NEG)
    e = jnp.where(mask, jnp.exp(sm - jnp.max(sm, axis=-1, keepdims=True)), 0.0)
    den = jnp.sum(e, axis=-1, keepdims=True)
    p = e / jnp.where(den > 0.0, den, 1.0)
    o = _dot(p.astype(BF16), vc_ref[0, 0].astype(BF16))
    _store_gated(o_ref, o, gate_ref[0, 0], Tq, 0)

    psum = p[0:Tq]
    for r in range(1, NSA_R):
        psum = psum + p[r * Tq:(r + 1) * Tq]
    ovt = ovt_ref[...]
    sc = sum(_dot_nt(ovt, t) for t in _split3(psum))
    n = lax.broadcasted_iota(jnp.int32, sc.shape, 0)
    tpos = qi * Tq + lax.broadcasted_iota(jnp.int32, sc.shape, 1)
    cur = tpos >> SEL_SHIFT
    forced = (n == 0) | (n == cur) | (n == cur - 1)
    sc = jnp.where(n * SEL_LEN <= tpos, jnp.where(forced, -NEG, sc), NEG)
    rank = jnp.zeros(sc.shape, F32)
    for m in range(nsel):
        rowm = sc[m:m + 1, :]
        rank = rank + jnp.where(n > m, jnp.where(rowm >= sc, 1.0, 0.0), jnp.where(rowm > sc, 1.0, 0.0))
    selneg = jnp.where(rank < TOPN, 0.0, NEG)
    if nsel < LANES:
        selneg = jnp.concatenate([selneg, jnp.full((LANES - nsel, Tq), NEG, F32)], axis=0)
    sel_ref[0, 0, 0] = selneg.astype(BF16)


def _cmpsel_prompt(qn, kc, vc, gates_g, B, S, *, Tq):
    nq = S // Tq
    ncp = kc.shape[2]
    nsel = S // SEL_LEN
    ovt = jnp.asarray(_overlap_t(ncp, nsel), BF16)
    return pl.pallas_call(
        functools.partial(_cmpsel_kernel, Tq, nsel),
        out_shape=[jax.ShapeDtypeStruct((B, S, H_NSA * HEAD_DIM), F32),
                   jax.ShapeDtypeStruct((B, NSA_GROUPS, nq, LANES, Tq), BF16)],
        grid=(B, NSA_GROUPS, nq),
        in_specs=[pl.BlockSpec((1, Tq, NSA_R * HEAD_DIM), lambda b, g, qi: (b, qi, g)),
                  pl.BlockSpec((1, 1, ncp, HEAD_DIM), lambda b, g, qi: (b, g, 0, 0)),
                  pl.BlockSpec((1, 1, ncp, HEAD_DIM), lambda b, g, qi: (b, g, 0, 0)),
                  pl.BlockSpec((nsel, ncp), lambda b, g, qi: (0, 0)),
                  pl.BlockSpec((1, 1, Tq, 16), lambda b, g, qi: (b, g, qi, 0))],
        out_specs=[pl.BlockSpec((1, Tq, NSA_R * HEAD_DIM), lambda b, g, qi: (b, qi, g)),
                   pl.BlockSpec((1, 1, 1, LANES, Tq), lambda b, g, qi: (b, g, qi, 0, 0))],
        compiler_params=_cparams(("arbitrary",) * 3),
        name="nsa_cmpsel",
    )(qn, kc, vc, ovt, gates_g)


def _queries_t(q_ref, Tq):
    return jnp.concatenate([(q_ref[0, :, r * HEAD_DIM:(r + 1) * HEAD_DIM] * QSCALE).T for r in range(NSA_R)],
                           axis=1).astype(BF16)


def _store_gated_t(o_ref, oT, gates, Tq, branch):
    for r in range(NSA_R):
        gcol = gates[:, 3 * r + branch:3 * r + branch + 1]
        o_ref[0, :, r * HEAD_DIM:(r + 1) * HEAD_DIM] = oT[:, r * Tq:(r + 1) * Tq].T * gcol


def _sel_kernel(Tq, tk, q_ref, sn_ref, k_ref, vt_ref, e_ref, gate_ref, o_ref, ka_scr, vt_scr, acc_scr):
    qi = pl.program_id(2)

    @pl.when(qi == 0)
    def _():
        for c in range(ka_scr.shape[0]):
            rows = slice(c * tk, (c + 1) * tk)
            ka_scr[c] = jnp.concatenate([k_ref[0, rows, :].astype(BF16), e_ref[rows, :]], axis=1)
            vt_scr[c] = vt_ref[0, 0, :, rows].astype(BF16)

    sn = sn_ref[0, 0, 0]
    qa = jnp.concatenate([_queries_t(q_ref, Tq), jnp.concatenate([sn] * NSA_R, axis=1)], axis=0)
    R = NSA_R * Tq
    qpos = qi * Tq + (lax.broadcasted_iota(jnp.int32, (tk, R), 1) & (Tq - 1))
    oT = _flash_t_causal(ka_scr, vt_scr, qa, acc_scr, (qi * Tq) // tk, 1, tk, qpos)
    _store_gated_t(o_ref, oT, gate_ref[0, 0], Tq, 1)


def _sel_prompt(qr, selneg, sk, svt, gates_g, B, S, *, Tq, tk):
    nq = S // Tq
    key = np.arange(S)[:, None] // SEL_LEN
    e_all = jnp.asarray((key == np.arange(LANES)[None, :]).astype(np.float32), BF16)
    R = NSA_R * Tq
    return pl.pallas_call(
        functools.partial(_sel_kernel, Tq, tk),
        out_shape=jax.ShapeDtypeStruct((B, S, H_NSA * HEAD_DIM), F32),
        grid=(B, NSA_GROUPS, nq),
        in_specs=[pl.BlockSpec((1, Tq, NSA_R * HEAD_DIM), lambda b, g, qi: (b, qi, g)),
                  pl.BlockSpec((1, 1, 1, LANES, Tq), lambda b, g, qi: (b, g, qi, 0, 0)),
                  pl.BlockSpec((1, S, HEAD_DIM), lambda b, g, qi: (b, 0, g)),
                  pl.BlockSpec((1, 1, HEAD_DIM, S), lambda b, g, qi: (b, g, 0, 0)),
                  pl.BlockSpec((S, LANES), lambda b, g, qi: (0, 0)),
                  pl.BlockSpec((1, 1, Tq, 16), lambda b, g, qi: (b, g, qi, 0))],
        out_specs=pl.BlockSpec((1, Tq, NSA_R * HEAD_DIM), lambda b, g, qi: (b, qi, g)),
        scratch_shapes=[pltpu.VMEM((S // tk, tk, 2 * HEAD_DIM), BF16), pltpu.VMEM((S // tk, HEAD_DIM, tk), BF16),
                        pltpu.VMEM((HEAD_DIM, R), F32)],
        compiler_params=_cparams(("arbitrary",) * 3),
        name="nsa_sel",
    )(qr, selneg, sk, svt, e_all, gates_g)


def _win_kernel(Tq, nwb, q_ref, *refs):
    k_refs, vt_refs = refs[:nwb], refs[nwb:2 * nwb]
    gate_ref, o_ref = refs[2 * nwb:]
    qi = pl.program_id(2)
    R = NSA_R * Tq
    qT = _queries_t(q_ref, Tq)
    qpos = qi * Tq + (lax.broadcasted_iota(jnp.int32, (Tq, R), 1) & (Tq - 1))
    krow = lax.broadcasted_iota(jnp.int32, (Tq, R), 0)
    s = []
    for i in range(nwb):
        kb = qi - (nwb - 1) + i
        si = _dot(k_refs[i][0].astype(BF16), qT)
        if i == 0:
            si = jnp.where(qpos - (kb * Tq + krow) < WINDOW, si, NEG)
        if i == nwb - 1:
            si = jnp.where(kb * Tq + krow <= qpos, si, NEG)
        else:
            si = si + jnp.where(kb >= 0, 0.0, NEG)
        s.append(si)
    m = functools.reduce(jnp.maximum, [jnp.max(si, axis=0, keepdims=True) for si in s])
    p = [jnp.exp2(si - m) for si in s]
    l = sum(jnp.sum(pi, axis=0, keepdims=True) for pi in p)
    oT = sum(_dot(vt_refs[i][0, 0].astype(BF16), p[i].astype(BF16)) for i in range(nwb)) / l
    _store_gated_t(o_ref, oT, gate_ref[0, 0], Tq, 2)


def _win_prompt(qr, wk, wvt, gates_g, B, S, *, Tq):
    nq = S // Tq
    nwb = WINDOW // Tq + 1
    blk = lambda i: (lambda qi: jnp.maximum(qi - (nwb - 1) + i, 0))
    kspec = lambda i: pl.BlockSpec((1, Tq, HEAD_DIM), lambda b, g, qi, f=blk(i): (b, f(qi), g))
    vspec = lambda i: pl.BlockSpec((1, 1, HEAD_DIM, Tq), lambda b, g, qi, f=blk(i): (b, g, 0, f(qi)))
    return pl.pallas_call(
        functools.partial(_win_kernel, Tq, nwb),
        out_shape=jax.ShapeDtypeStruct((B, S, H_NSA * HEAD_DIM), F32),
        grid=(B, NSA_GROUPS, nq),
        in_specs=[pl.BlockSpec((1, Tq, NSA_R * HEAD_DIM), lambda b, g, qi: (b, qi, g))]
        + [kspec(i) for i in range(nwb)] + [vspec(i) for i in range(nwb)]
        + [pl.BlockSpec((1, 1, Tq, 16), lambda b, g, qi: (b, g, qi, 0))],
        out_specs=pl.BlockSpec((1, Tq, NSA_R * HEAD_DIM), lambda b, g, qi: (b, qi, g)),
        compiler_params=_cparams(("arbitrary",) * 3),
        name="nsa_win",
    )(qr, *([wk] * nwb), *([wvt] * nwb), gates_g)


def _hgrn_step_kernel(aq_ref, af_ref, ai_ref, ag_ref, lb_ref, gn_ref, s_ref, o_ref, so_ref):
    lb = lb_ref[0]
    gate = lb + (1.0 - lb) * _sigmoid(af_ref[0, 0])
    s_new = s_ref[0, 0] * gate + (1.0 - gate) * ai_ref[0, 0]
    so_ref[0, 0] = s_new
    o = jnp.sum(_silu(aq_ref[0, 0]) * s_new, axis=0, keepdims=True)
    o_ref[0, 0] = _head_rms(o, gn_ref[...]) * _silu(ag_ref[0, 0])


def _hgrn_step(hg, lb, gn, state):
    D, H = state.shape[:2]
    W = H * HEAD_DIM
    colv = lambda a: a.reshape(D, H, HEAD_DIM, 1)
    rowv = lambda a: a.reshape(D, H, 1, HEAD_DIM)
    cspec = pl.BlockSpec((1, 1, HEAD_DIM, 1), lambda d, h: (d, h, 0, 0))
    rspec = pl.BlockSpec((1, 1, 1, HEAD_DIM), lambda d, h: (d, h, 0, 0))
    sspec = pl.BlockSpec((1, 1, HEAD_DIM, HEAD_DIM), lambda d, h: (d, h, 0, 0))
    o, s_new = pl.pallas_call(
        _hgrn_step_kernel,
        out_shape=[jax.ShapeDtypeStruct((D, H, 1, HEAD_DIM), F32), jax.ShapeDtypeStruct(state.shape, F32)],
        grid=(D, H),
        in_specs=[cspec, cspec, rspec, rspec,
                  pl.BlockSpec((1, HEAD_DIM, 1), lambda d, h: (h, 0, 0)),
                  pl.BlockSpec((1, HEAD_DIM), lambda d, h: (0, 0)), sspec],
        out_specs=[rspec, sspec],
        compiler_params=_cparams(("arbitrary", "arbitrary")),
        name="hgrn_step",
    )(colv(hg[:, :W]), colv(hg[:, W:2 * W]), rowv(hg[:, 2 * W:3 * W]), rowv(hg[:, 3 * W:]),
      lb.reshape(H, HEAD_DIM, 1), gn, state)
    return o.reshape(D, W), s_new


def _fox_step_kernel(PG, pt_ref, q_ref, kn_ref, vn_ref, lfn_ref, *refs):
    k_refs, v_refs, lf_refs = refs[:PG], refs[PG:2 * PG], refs[2 * PG:3 * PG]
    o_ref, m_scr, l_scr, acc_scr, carry = refs[3 * PG:]
    W = H_FOX * PAGE
    q = q_ref[0] * (HEAD_DIM ** -0.5)

    @pl.when(pl.program_id(1) == 0)
    def _():
        m_scr[...] = jnp.sum(q * kn_ref[0], axis=-1, keepdims=True)
        l_scr[...] = jnp.ones_like(l_scr)
        acc_scr[...] = vn_ref[0]
        carry[...] = jnp.broadcast_to(lfn_ref[0], carry.shape)

    qb = q.astype(BF16)
    lane = lax.broadcasted_iota(jnp.int32, (H_FOX, W), 1)
    head = lax.broadcasted_iota(jnp.int32, (H_FOX, W), 0)
    plane = lax.broadcasted_iota(jnp.int32, (PG, W), 1)
    lf = jnp.concatenate([lf_refs[i][0] for i in range(PG)], axis=0)
    suf, tot = lf, lf
    step = H_FOX
    while step < W:
        suf = suf + jnp.where(plane + step < W, pltpu.roll(suf, W - step, 1), 0.0)
        tot = tot + pltpu.roll(tot, step, 1)
        step *= 2
    c = carry[...]
    s = []
    for i in range(PG):
        si = _dot_nt(qb, k_refs[i][0].astype(BF16)) + (c + (suf[i:i + 1] - lf[i:i + 1]))
        s.append(jnp.where((lane & (H_FOX - 1)) == head, si, NEG))
        c = c + tot[i:i + 1]
    carry[...] = c
    m_prev = m_scr[...]
    m_new = functools.reduce(jnp.maximum, [jnp.max(si, axis=-1, keepdims=True) for si in s] + [m_prev])
    alpha = jnp.exp(m_prev - m_new)
    p = [jnp.exp(si - m_new) for si in s]
    l_scr[...] = alpha * l_scr[...] + sum(jnp.sum(pi, axis=-1, keepdims=True) for pi in p)
    acc_scr[...] = alpha * acc_scr[...] + sum(_dot(p[i].astype(BF16), v_refs[i][0].astype(BF16)) for i in range(PG))
    m_scr[...] = m_new

    @pl.when(pl.program_id(1) == pl.num_programs(1) - 1)
    def _():
        o_ref[0] = _online_result(l_scr, acc_scr)


def _fox_step(page_table, q, k_new, v_new, lf_new, cache_k, cache_v, cache_lf, *, PG):
    D, NP = page_table.shape
    n_pool = cache_k.shape[0]
    W = H_FOX * PAGE
    k2 = cache_k.reshape(n_pool, W, HEAD_DIM)
    v2 = cache_v.reshape(n_pool, W, HEAD_DIM)
    lf2 = cache_lf.astype(F32).reshape(n_pool, 1, W)
    lfn = jnp.tile(lf_new, (1, PAGE)).reshape(D, 1, W)
    page = lambda i: (lambda d, j, pt: (pt[d, NP - 1 - (j * PG + i)], 0, 0))
    hspec = pl.BlockSpec((1, H_FOX, HEAD_DIM), lambda d, j, pt: (d, 0, 0))
    in_specs = [hspec, hspec, hspec, pl.BlockSpec((1, 1, W), lambda d, j, pt: (d, 0, 0))]
    in_specs += [pl.BlockSpec((1, W, HEAD_DIM), page(i)) for i in range(PG)]
    in_specs += [pl.BlockSpec((1, W, HEAD_DIM), page(i)) for i in range(PG)]
    in_specs += [pl.BlockSpec((1, 1, W), page(i)) for i in range(PG)]
    return pl.pallas_call(
        functools.partial(_fox_step_kernel, PG),
        out_shape=jax.ShapeDtypeStruct((D, H_FOX, HEAD_DIM), F32),
        grid_spec=pltpu.PrefetchScalarGridSpec(
            num_scalar_prefetch=1, grid=(D, NP // PG), in_specs=in_specs, out_specs=hspec,
            scratch_shapes=[pltpu.VMEM((H_FOX, 1), F32), pltpu.VMEM((H_FOX, 1), F32),
                            pltpu.VMEM((H_FOX, HEAD_DIM), F32), pltpu.VMEM((H_FOX, W), F32)]),
        compiler_params=_cparams(("arbitrary", "arbitrary")),
        name="fox_step",
    )(page_table, q, k_new, v_new, lfn, *([k2] * PG), *([v2] * PG), *([lf2] * PG))


def _cmp_pages_kernel(PG, pt_ref, *refs):
    pages = refs[:PG]
    pea_ref, peb_ref, w1a_ref, w1b_ref, p1_ref, p2_ref = refs[PG:]
    per_page = PAGE // CMP_STRIDE
    parts = []
    for g in range(NSA_GROUPS):
        for i in range(PG):
            parts.append(jnp.concatenate(
                [pages[i][0, pl.ds(r * NSA_GROUPS + g, per_page, stride=CMP_STRIDE * NSA_GROUPS), :]
                 for r in range(CMP_STRIDE)], axis=1))
    R = jnp.concatenate(parts, axis=0)
    shape = (NSA_GROUPS, PG * per_page, HEAD_DIM)
    p1_ref[0] = _dot((R + pea_ref[...]).astype(BF16), w1a_ref[...]).reshape(shape)
    p2_ref[0] = _dot((R + peb_ref[...]).astype(BF16), w1b_ref[...]).reshape(shape)


def _cmp_pages(page_table, cache, cw, *, PG):
    D, NP = page_table.shape
    n_pool = cache.shape[0]
    c2 = cache.reshape(n_pool, PAGE * NSA_GROUPS, HEAD_DIM)
    per_page = PAGE // CMP_STRIDE
    half = CMP_STRIDE * HEAD_DIM
    page = lambda i: (lambda d, j, pt: (pt[d, j * PG + i], 0, 0))
    const = lambda shape: pl.BlockSpec(shape, lambda d, j, pt: (0, 0))
    in_specs = [pl.BlockSpec((1, PAGE * NSA_GROUPS, HEAD_DIM), page(i)) for i in range(PG)]
    in_specs += [const((1, half)), const((1, half)), const((half, HEAD_DIM)), const((half, HEAD_DIM))]
    ospec = pl.BlockSpec((1, NSA_GROUPS, PG * per_page, HEAD_DIM), lambda d, j, pt: (d, 0, j, 0))
    shp = jax.ShapeDtypeStruct((D, NSA_GROUPS, NP * per_page, HEAD_DIM), F32)
    return pl.pallas_call(
        functools.partial(_cmp_pages_kernel, PG),
        out_shape=[shp, shp],
        grid_spec=pltpu.PrefetchScalarGridSpec(
            num_scalar_prefetch=1, grid=(D, NP // PG), in_specs=in_specs, out_specs=[ospec, ospec]),
        compiler_params=_cparams(("arbitrary", "arbitrary")),
        name="nsa_cmp_pages",
    )(page_table, *([c2] * PG), *cw[:4])


def _cmpsel_step_kernel(qpos, nsp, q_ref, p1k_ref, p2k_ref, p1v_ref, p2v_ref, w2k_ref, w2v_ref,
                        ov_ref, g_ref, o_ref, idx_ref):
    nc = p1k_ref.shape[2]

    def finish(p1_ref, p2_ref, w2_ref):
        pre = p1_ref[0, 0] + pltpu.roll(p2_ref[0, 0], nc - 1, 0)
        return _dot(_silu(pre).astype(BF16), w2_ref[...]).astype(BF16)

    kc = finish(p1k_ref, p2k_ref, w2k_ref)
    vc = finish(p1v_ref, p2v_ref, w2v_ref)
    q = (q_ref[0, 0] * (HEAD_DIM ** -0.5)).astype(BF16)
    s = _dot_nt(q, kc)
    cend = lax.broadcasted_iota(jnp.int32, s.shape, 1) * CMP_STRIDE + (CMP_LEN - 1)
    mask = cend <= qpos
    sm = jnp.where(mask, s, NEG)
    e = jnp.where(mask, jnp.exp(sm - jnp.max(sm, axis=-1, keepdims=True)), 0.0)
    den = jnp.sum(e, axis=-1, keepdims=True)
    p = e / jnp.where(den > 0.0, den, 1.0)
    o_ref[0, 0] = _dot(p.astype(BF16), vc) * g_ref[0, 0][:, 0:1]

    psum = jnp.broadcast_to(jnp.sum(p[0:NSA_R], axis=0, keepdims=True), (8, nc))
    ov = ov_ref[...]
    sc_row = sum(_dot(t, ov) for t in _split3(psum))[0:1]
    n_lane = lax.broadcasted_iota(jnp.int32, (1, nsp), 1)
    cur = qpos // SEL_LEN
    forced = (n_lane == 0) | (n_lane == cur) | (n_lane == cur - 1)
    sc_row = jnp.where(n_lane * SEL_LEN <= qpos, jnp.where(forced, -NEG, sc_row), NEG)
    mi = lax.broadcasted_iota(jnp.int32, (nsp, nsp), 0)
    ni = lax.broadcasted_iota(jnp.int32, (nsp, nsp), 1)
    sc_col = jnp.sum(jnp.where(mi == ni, sc_row, 0.0), axis=-1, keepdims=True)
    beats = jnp.where(mi < ni, jnp.where(sc_col >= sc_row, 1.0, 0.0), jnp.where(sc_col > sc_row, 1.0, 0.0))
    rank = jnp.sum(beats, axis=0, keepdims=True)
    lane = lax.broadcasted_iota(jnp.int32, (1, LANES), 1)
    out = jnp.zeros((1, LANES), F32)
    for k in range(TOPN):
        nk = jnp.sum(jnp.where(rank == float(k), n_lane.astype(F32), 0.0), axis=-1, keepdims=True)
        out = jnp.where(lane == k, nk, out)
    idx_ref[0, 0] = jnp.broadcast_to(out, (8, LANES)).astype(jnp.int32)


def _cmpsel_step(q16, p1k, p2k, p1v, p2v, cwk, cwv, gate_rows, qpos):
    D, G, nc = p1k.shape[:3]
    n_sel = -(-(qpos + 1) // SEL_LEN)
    nsp = -(-n_sel // LANES) * LANES
    ov = jnp.asarray(np.pad(_overlap_t(nc, n_sel).T, ((0, 0), (0, nsp - n_sel))), BF16)
    big = pl.BlockSpec((1, 1, nc, HEAD_DIM), lambda d, g: (d, g, 0, 0))
    qspec = pl.BlockSpec((1, 1, 16, HEAD_DIM), lambda d, g: (d, g, 0, 0))
    w2spec = pl.BlockSpec((HEAD_DIM, HEAD_DIM), lambda d, g: (0, 0))
    return pl.pallas_call(
        functools.partial(_cmpsel_step_kernel, qpos, nsp),
        out_shape=[jax.ShapeDtypeStruct((D, G, 16, HEAD_DIM), F32),
                   jax.ShapeDtypeStruct((D, G, 8, LANES), jnp.int32)],
        grid=(D, G),
        in_specs=[qspec, big, big, big, big, w2spec, w2spec,
                  pl.BlockSpec((nc, nsp), lambda d, g: (0, 0)),
                  pl.BlockSpec((1, 1, 16, 3), lambda d, g: (d, g, 0, 0))],
        out_specs=[qspec, pl.BlockSpec((1, 1, 8, LANES), lambda d, g: (d, g, 0, 0))],
        compiler_params=_cparams(("arbitrary", "arbitrary")),
        name="nsa_cmpsel_step",
    )(q16, p1k, p2k, p1v, p2v, cwk[4], cwv[4], ov, gate_rows)


def _sel_step_kernel(n_past, pt_ref, idx_ref, q_ref, kn_ref, vn_ref, *refs):
    k_refs, v_refs = refs[:TOPN], refs[TOPN:2 * TOPN]
    g_ref, o_ref = refs[2 * TOPN:]
    d, g = pl.program_id(0), pl.program_id(1)
    q = q_ref[0, 0] * (HEAD_DIM ** -0.5)
    qb = q.astype(BF16)
    rows = pl.ds(g, SEL_LEN, stride=NSA_GROUPS)
    s_new = jnp.sum(q * kn_ref[0, 0], axis=-1, keepdims=True)
    s = [_dot_nt(qb, k_refs[k][rows, :].astype(BF16)) + jnp.where(idx_ref[d, g, k] < n_past, 0.0, NEG)
         for k in range(TOPN)]
    m = functools.reduce(jnp.maximum, [jnp.max(sk, axis=-1, keepdims=True) for sk in s] + [s_new])
    p = [jnp.exp(sk - m) for sk in s]
    p_new = jnp.exp(s_new - m)
    l = sum(jnp.sum(pk, axis=-1, keepdims=True) for pk in p) + p_new
    o = sum(_dot(p[k].astype(BF16), v_refs[k][rows, :].astype(BF16)) for k in range(TOPN)) + p_new * vn_ref[0, 0]
    o_ref[0, 0] = o / l * g_ref[0, 0][:, 1:2]


def _sel_step(page_table, idx, q16, k_new, v_new, cache_k, cache_v, gate_rows):
    D, NP = page_table.shape
    G = NSA_GROUPS
    n_pool = cache_k.shape[0]
    per_page = PAGE // SEL_LEN
    n_past = NP * per_page
    blk_rows = SEL_LEN * G
    k2 = cache_k.reshape(n_pool * PAGE * G, HEAD_DIM)
    v2 = cache_v.reshape(n_pool * PAGE * G, HEAD_DIM)

    def blk(k):
        def index(d, g, pt, ix):
            n = jnp.minimum(ix[d, g, k], n_past - 1)
            return (pt[d, n // per_page] * per_page + n % per_page, 0)
        return pl.BlockSpec((blk_rows, HEAD_DIM), index)

    qspec = pl.BlockSpec((1, 1, 16, HEAD_DIM), lambda d, g, pt, ix: (d, g, 0, 0))
    nspec = pl.BlockSpec((1, 1, 1, HEAD_DIM), lambda d, g, pt, ix: (d, g, 0, 0))
    return pl.pallas_call(
        functools.partial(_sel_step_kernel, n_past),
        out_shape=jax.ShapeDtypeStruct((D, G, 16, HEAD_DIM), F32),
        grid_spec=pltpu.PrefetchScalarGridSpec(
            num_scalar_prefetch=2, grid=(D, G),
            in_specs=[qspec, nspec, nspec] + [blk(k) for k in range(TOPN)] * 2
            + [pl.BlockSpec((1, 1, 16, 3), lambda d, g, pt, ix: (d, g, 0, 0))],
            out_specs=qspec),
        compiler_params=_cparams(("arbitrary",) * 2),
        name="nsa_sel_step",
    )(page_table, idx, q16, k_new, v_new, *([k2] * TOPN), *([v2] * TOPN), gate_rows)


def _win_step_kernel(q_ref, k_ref, v_ref, g_ref, o_ref):
    q = (q_ref[0, 0] * (HEAD_DIM ** -0.5)).astype(BF16)
    s = _dot_nt(q, k_ref[0].astype(BF16))
    e = jnp.exp(s - jnp.max(s, axis=-1, keepdims=True))
    p = e / jnp.sum(e, axis=-1, keepdims=True)
    o_ref[0, 0] = _dot(p.astype(BF16), v_ref[0].astype(BF16)) * g_ref[0, 0][:, 2:3]


def _win_step(q16, kw, vw, gate_rows):
    D, G = q16.shape[:2]
    L = kw.shape[1]
    qspec = pl.BlockSpec((1, 1, 16, HEAD_DIM), lambda d, g: (d, g, 0, 0))
    kspec = pl.BlockSpec((1, L, HEAD_DIM), lambda d, g: (d, 0, g))
    return pl.pallas_call(
        _win_step_kernel,
        out_shape=jax.ShapeDtypeStruct((D, G, 16, HEAD_DIM), F32),
        grid=(D, G),
        in_specs=[qspec, kspec, kspec, pl.BlockSpec((1, 1, 16, 3), lambda d, g: (d, g, 0, 0))],
        out_specs=qspec,
        compiler_params=_cparams(("arbitrary", "arbitrary")),
        name="nsa_win_step",
    )(q16, kw, vw, gate_rows)


TN = 512
TM = 512
TM_STEP = 8


def _pad_cols(a, n):
    return jnp.pad(a, ((0, 0), (0, n - a.shape[1])))


def _even_params(e_norm_mix, e_w_in, lb, out_norm, f_bias, q_norm, k_norm, e_w_out, e_norm_ffn, wg, wu, wd):
    d = e_w_in.shape[0]
    aw, bw = H_HGRN * HEAD_DIM, H_FOX * HEAD_DIM
    n_real = 4 * aw + 3 * bw + H_FOX
    n_pad = -(-n_real // TN) * TN
    ones = lambda n: jnp.ones((n,), F32)
    gain = jnp.concatenate([ones(4 * aw), jnp.tile(q_norm, H_FOX), jnp.tile(k_norm, H_FOX), ones(n_pad - 4 * aw - 2 * bw)])
    bias = jnp.concatenate([jnp.zeros((4 * aw + 3 * bw,), F32), f_bias.astype(F32),
                            jnp.zeros((n_pad - n_real,), F32)])
    nt = lambda w: w // TN
    segs, t0 = [], 0
    for width, mode in ((4 * aw, 'raw'), (bw, 'hnorm'), (bw, 'hnorm'), (bw, 'raw'), (TN, 'logsig')):
        segs.append((t0, nt(width), mode))
        t0 += nt(width)
    return dict(norm=e_norm_mix.reshape(1, d), w_in=_pad_cols(e_w_in.astype(BF16), n_pad), gain=gain.reshape(1, -1),
                bias=bias.reshape(1, -1), segs=segs, lb=lb.reshape(1, aw), out_norm=out_norm.reshape(1, HEAD_DIM),
                w_out=e_w_out.astype(BF16), norm_ffn=e_norm_ffn.reshape(1, d),
                wg=wg.astype(BF16), wu=wu.astype(BF16), wd=wd.astype(BF16))


def _even_proj(x2, p, tm):
    zeros = jnp.zeros((tm, HEAD_DIM), F32)
    return _proj(x2, p['norm'], p['w_in'], p['gain'], p['bias'], zeros, zeros, p['segs'], tm=tm, tn=TN,
                 name="even_proj")


def _even_finish(x2, oa, of, p, tm):
    x2 = _outproj([jnp.concatenate([oa, of], axis=1)], p['w_out'], x2, tm=tm, tn=TN, name="even_out")
    return _ffn(x2, p['norm_ffn'], p['wg'], p['wu'], p['wd'], tm=tm, tf=TN, name="ffn")


def _odd_params(o_norm_mix, o_w_in, q_norm, ck_norm, sk_norm, wk_norm, pe_k, w1_k, w2_k, pe_v, w1_v, w2_v,
                o_w_out, o_norm_ffn, router_w, router_b, mg, mu, md):
    d = o_w_in.shape[0]
    qw, kvw = H_NSA * HEAD_DIM, NSA_GROUPS * HEAD_DIM
    n_real = qw + 6 * kvw + 3 * H_NSA
    n_pad = -(-n_real // TN) * TN
    ones = jnp.ones((kvw,), F32)
    gain = jnp.concatenate([jnp.tile(q_norm, H_NSA), jnp.tile(ck_norm, NSA_GROUPS), ones,
                            jnp.tile(sk_norm, NSA_GROUPS), ones, jnp.tile(wk_norm, NSA_GROUPS), ones,
                            jnp.ones((n_pad - qw - 6 * kvw,), F32)])
    segs, t0 = [], 0
    for width, mode in ((qw, 'hnorm_both'), (kvw, 'hnorm'), (kvw, 'raw'), (kvw, 'hnorm_rope'), (kvw, 'raw'),
                        (kvw, 'hnorm_rope'), (kvw, 'raw'), (TN, 'sigmoid')):
        segs.append((t0, width // TN, mode))
        t0 += width // TN
    return dict(norm=o_norm_mix.reshape(1, d), w_in=_pad_cols(o_w_in.astype(BF16), n_pad), gain=gain.reshape(1, -1),
                bias=jnp.zeros((1, n_pad), F32), segs=segs,
                cwk=_cmp_weights(pe_k, w1_k, w2_k), cwv=_cmp_weights(pe_v, w1_v, w2_v),
                w_out=o_w_out.astype(BF16), norm_ffn=o_norm_ffn.reshape(1, d),
                rw=_pad_cols(router_w.astype(BF16), LANES), rb=_pad_cols(router_b.reshape(1, -1).astype(F32), LANES),
                mg=mg.astype(BF16), mu=mu.astype(BF16), md=md.astype(BF16))


def _odd_proj(x2, p, pos, tm):
    cosf, sinf = _rope_tables(pos)
    return _proj(x2, p['norm'], p['w_in'], p['gain'], p['bias'], cosf, sinf, p['segs'], tm=tm, tn=TN,
                 name="odd_proj")


def _odd_finish(x2, branches, p, tm):
    x2 = _outproj(branches, p['w_out'], x2, tm=tm, tn=TN, name="odd_out")
    return _moe(x2, p['norm_ffn'], p['rw'], p['rb'], p['mg'], p['mu'], p['md'], tm=tm, tf=256, name="moe")


def kernel(x_prompt, x_sample, state_hgrn, cache_fox_k, cache_fox_v, cache_fox_logf, cache_nsa_cmp_k, cache_nsa_cmp_v, cache_nsa_sel_k, cache_nsa_sel_v, cache_nsa_win_k, cache_nsa_win_v, page_table, e_norm_mix, e_w_in, hgrn_lb_logits, hgrn_out_norm, fox_f_bias, fox_q_norm, fox_k_norm, e_w_out, e_norm_ffn, ffn_w_gate, ffn_w_up, ffn_w_down, o_norm_mix, o_w_in, nsa_q_norm, nsa_cmp_k_norm, nsa_sel_k_norm, nsa_win_k_norm, cmp_pe_k, cmp_w1_k, cmp_w2_k, cmp_pe_v, cmp_w1_v, cmp_w2_v, o_w_out, o_norm_ffn, router_w, router_b, moe_w_gate, moe_w_up, moe_w_down):
    B, S, d = x_prompt.shape
    D, T, _ = x_sample.shape
    NP = page_table.shape[1]
    past = NP * PAGE
    w_buf = cache_nsa_win_k.shape[2]
    assert T == 1 and w_buf == WINDOW and S >= WINDOW and cache_fox_k.shape[2] == PAGE
    G = NSA_GROUPS
    lbs = jnp.cumsum(jax.nn.softmax(hgrn_lb_logits.astype(F32), axis=0), axis=0)
    xp, xd = x_prompt.reshape(B * S, d), x_sample.reshape(D, d)
    tm_p = min(TM, S)

    li = 0
    p = _even_params(e_norm_mix[li], e_w_in[li], lbs[li], hgrn_out_norm[li], fox_f_bias[li], fox_q_norm[li],
                     fox_k_norm[li], e_w_out[li], e_norm_ffn[li], ffn_w_gate[li], ffn_w_up[li], ffn_w_down[li])
    hg, fq, fk, fv, fl = _even_proj(xp, p, tm_p)
    aug_q, aug_k = _seq_cumsum(fl, B, S, tc=tm_p)
    oa, st_p = _hgrn_prompt(hg, p['lb'], p['out_norm'], B, S, T=min(256, S), c=32)
    fvt = fv.reshape(B, S, H_FOX, HEAD_DIM).transpose(0, 2, 3, 1)
    of = _fox_prompt(fq, fk, fvt, aug_q, aug_k, B, S, tq=min(1024, S), tk=tm_p)
    xp = _even_finish(xp, oa, of, p, tm_p)
    fox_p = (fk.reshape(1, B, S, H_FOX, HEAD_DIM), fv.reshape(1, B, S, H_FOX, HEAD_DIM),
             fl[:, :H_FOX].reshape(1, B, S, H_FOX))

    hg, fq, fk, fv, fl = _even_proj(xd, p, TM_STEP)
    oa, st_d = _hgrn_step(hg, p['lb'], p['out_norm'], state_hgrn[li].astype(F32))
    heads = lambda a: a.reshape(D, H_FOX, HEAD_DIM)
    of = _fox_step(page_table, heads(fq), heads(fk), heads(fv), fl[:, :H_FOX],
                   cache_fox_k[li], cache_fox_v[li], cache_fox_logf[li], PG=8)
    xd = _even_finish(xd, oa, of.reshape(D, H_FOX * HEAD_DIM), p, TM_STEP)
    fox_d = (fk.reshape(1, D, 1, H_FOX, HEAD_DIM), fv.reshape(1, D, 1, H_FOX, HEAD_DIM),
             fl[:, :H_FOX].reshape(1, D, 1, H_FOX))

    p = _odd_params(o_norm_mix[li], o_w_in[li], nsa_q_norm[li], nsa_cmp_k_norm[li], nsa_sel_k_norm[li],
                    nsa_win_k_norm[li], cmp_pe_k[li], cmp_w1_k[li], cmp_w2_k[li], cmp_pe_v[li], cmp_w1_v[li],
                    cmp_w2_v[li], o_w_out[li], o_norm_ffn[li], router_w[li], router_b[li],
                    moe_w_gate[li], moe_w_up[li], moe_w_down[li])
    qn, qr, ck, cv, sk, sv, wk, wv, gt = _odd_proj(xp, p, jnp.arange(S), tm_p)
    seq = lambda a: a.reshape(B, S, -1)
    gates_g = jnp.pad(gt[:, :3 * H_NSA].reshape(B, S, G, 3 * NSA_R).transpose(0, 2, 1, 3),
                      ((0, 0), (0, 0), (0, 0), (0, 16 - 3 * NSA_R)))
    kc = _compress_prompt(seq(ck), p['cwk'], B, S)
    vc = _compress_prompt(seq(cv), p['cwv'], B, S)
    tq = min(256, S)
    o_cmp, selneg = _cmpsel_prompt(seq(qn), kc, vc, gates_g, B, S, Tq=tq)
    keys_last = lambda a: a.reshape(B, S, G, HEAD_DIM).transpose(0, 2, 3, 1)
    o_sel = _sel_prompt(seq(qr), selneg, seq(sk), keys_last(sv), gates_g, B, S, Tq=tq, tk=min(512, S))
    o_win = _win_prompt(seq(qr), seq(wk), keys_last(wv), gates_g, B, S, Tq=tq)
    flat = lambda a: a.reshape(B * S, -1)
    xp = _odd_finish(xp, [flat(o_cmp), flat(o_sel), flat(o_win)], p, tm_p)
    kv = lambda a: a.reshape(1, B, S, G, HEAD_DIM)
    nsa_p = (kv(ck), kv(cv), kv(sk), kv(sv), kv(wk)[:, :, S - WINDOW:], kv(wv)[:, :, S - WINDOW:])

    qn, qr, ck, cv, sk, sv, wk, wv, gt = _odd_proj(xd, p, jnp.full((TM_STEP,), past), TM_STEP)
    rows16 = lambda a, w: jnp.pad(a.reshape(D, G, NSA_R, w), ((0, 0), (0, 0), (0, 16 - NSA_R), (0, 0)))
    gate_rows = rows16(gt[:, :3 * H_NSA], 3)
    q16n, q16r = rows16(qn, HEAD_DIM), rows16(qr, HEAD_DIM)
    p1k, p2k = _cmp_pages(page_table, cache_nsa_cmp_k[li], p['cwk'], PG=16)
    p1v, p2v = _cmp_pages(page_table, cache_nsa_cmp_v[li], p['cwv'], PG=16)
    o_cmp, idx = _cmpsel_step(q16n, p1k, p2k, p1v, p2v, p['cwk'], p['cwv'], gate_rows, past)
    new = lambda a: a.reshape(D, G, 1, HEAD_DIM)
    o_sel = _sel_step(page_table, idx[:, :, 0, :TOPN], q16r, new(sk), new(sv),
                      cache_nsa_sel_k[li], cache_nsa_sel_v[li], gate_rows)
    kvd = lambda a: a.reshape(D, 1, G, HEAD_DIM)
    win_k = jnp.concatenate([cache_nsa_win_k[li], kvd(wk)], axis=1)[:, -w_buf:]
    win_v = jnp.concatenate([cache_nsa_win_v[li], kvd(wv)], axis=1)[:, -w_buf:]
    o_win = _win_step(q16r, win_k.reshape(D, w_buf, G * HEAD_DIM), win_v.reshape(D, w_buf, G * HEAD_DIM), gate_rows)
    unrow = lambda a: a[:, :, :NSA_R].reshape(D, H_NSA * HEAD_DIM)
    xd = _odd_finish(xd, [unrow(o_cmp), unrow(o_sel), unrow(o_win)], p, TM_STEP)
    kv1 = lambda a: a.reshape(1, D, 1, G, HEAD_DIM)
    nsa_d = (kv1(ck), kv1(cv), kv1(sk), kv1(sv), win_k[None], win_v[None])

    return (xp.reshape(B, S, d), xd.reshape(D, T, d), st_p[None].astype(state_hgrn.dtype),
            st_d[None].astype(state_hgrn.dtype), *fox_p, *fox_d, *nsa_p, *nsa_d)
```

```python
import functools

import numpy as np
import jax
import jax.numpy as jnp
from jax import lax
from jax.experimental import pallas as pl
from jax.experimental.pallas import tpu as pltpu

F32 = jnp.float32
BF16 = jnp.bfloat16

LANES = 128
HEAD_DIM = 128
EPS = 1e-6
ROPE_THETA = 10000.0
LOG2E = 1.4426950408889634
QSCALE = HEAD_DIM ** -0.5 * LOG2E
NEG = -1e30
M_INIT = -1e20
VMEM_LIMIT = 56 * 1024 * 1024

H_HGRN = 8
H_FOX = 8
H_NSA = 16
NSA_GROUPS = 4
NSA_R = H_NSA // NSA_GROUPS
CMP_STRIDE = 16
CMP_LEN = 32
SEL_LEN = 64
SEL_SHIFT = 6
TOPN = 16
WINDOW = 512
N_EXPERTS = 8
PAGE = 128


def _cparams(sem):
    return pltpu.CompilerParams(dimension_semantics=sem, vmem_limit_bytes=VMEM_LIMIT)


def _dot(a, b):
    return jnp.dot(a, b, preferred_element_type=F32)


def _dot_nt(a, b):
    return lax.dot_general(a, b, (((1,), (1,)), ((), ())), preferred_element_type=F32)


def _dot_tn(a, b):
    return lax.dot_general(a, b, (((0,), (0,)), ((), ())), preferred_element_type=F32)


def _split3(x):
    hi = x.astype(BF16)
    r1 = x - hi.astype(F32)
    mid = r1.astype(BF16)
    lo = (r1 - mid.astype(F32)).astype(BF16)
    return hi, mid, lo


def _dot3(w_bf16, x):
    hi, mid, lo = _split3(x)
    return _dot(w_bf16, hi) + _dot(w_bf16, mid) + _dot(w_bf16, lo)


def _sigmoid(x):
    return 1.0 / (1.0 + jnp.exp(-x))


def _silu(x):
    return x * _sigmoid(x)


def _log_sigmoid(x):
    return jnp.minimum(x, 0.0) - jnp.log(1.0 + jnp.exp(-jnp.abs(x)))


def _tril_bf16(n):
    r = lax.broadcasted_iota(jnp.int32, (n, n), 0)
    c = lax.broadcasted_iota(jnp.int32, (n, n), 1)
    return jnp.where(c <= r, 1.0, 0.0).astype(BF16)


def _head_rms(a, gain):
    ms = jnp.mean(a * a, axis=-1, keepdims=True)
    return a * lax.rsqrt(ms + EPS) * gain


def _rope(y, cosf, sinf):
    return y * cosf + pltpu.roll(y, HEAD_DIM // 2, 1) * sinf


def _rope_tables(pos):
    half = HEAD_DIM // 2
    inv = ROPE_THETA ** (-jnp.arange(half, dtype=F32) / half)
    ang = pos.astype(F32)[:, None] * inv[None, :]
    cos, sin = jnp.cos(ang), jnp.sin(ang)
    return jnp.concatenate([cos, cos], axis=-1), jnp.concatenate([-sin, sin], axis=-1)


def _proj_kernel(segs, tn, *refs):
    n_out = sum(2 if s[2] == 'hnorm_both' else 1 for s in segs)
    x_ref, g_ref, w_ref, gain_ref, bias_ref, cos_ref, sin_ref = refs[:7]
    out_refs = refs[7:7 + n_out]
    xn_ref = refs[7 + n_out]
    j = pl.program_id(1)

    @pl.when(j == 0)
    def _():
        xf = x_ref[...]
        ms = jnp.mean(xf * xf, axis=-1, keepdims=True)
        xn_ref[...] = (xf * lax.rsqrt(ms + EPS) * g_ref[...]).astype(BF16)

    acc = _dot(xn_ref[...], w_ref[...])
    oi = 0
    for (t0, nt, mode) in segs:
        outs = out_refs[oi:oi + (2 if mode == 'hnorm_both' else 1)]
        oi += len(outs)

        @pl.when((j >= t0) & (j < t0 + nt))
        def _(mode=mode, outs=outs):
            if mode == 'raw':
                outs[0][...] = acc
            elif mode == 'sigmoid':
                outs[0][...] = _sigmoid(acc)
            elif mode == 'logsig':
                outs[0][...] = _log_sigmoid(acc + bias_ref[...])
            else:
                for h in range(tn // HEAD_DIM):
                    sl = slice(h * HEAD_DIM, (h + 1) * HEAD_DIM)
                    y = _head_rms(acc[:, sl], gain_ref[:, sl])
                    if mode == 'hnorm':
                        outs[0][:, sl] = y
                    elif mode == 'hnorm_rope':
                        outs[0][:, sl] = _rope(y, cos_ref[...], sin_ref[...])
                    else:
                        outs[0][:, sl] = y
                        outs[1][:, sl] = _rope(y, cos_ref[...], sin_ref[...])


def _proj(x, norm_g, w, gain_all, bias_all, cosf, sinf, segs, *, tm, tn, name):
    M, K = x.shape
    ntiles = w.shape[1] // tn
    assert M % tm == 0 and w.shape[1] == ntiles * tn and segs[-1][0] + segs[-1][1] == ntiles
    nseq = cosf.shape[0] // tm
    out_shape, out_specs = [], []
    for (t0, nt, mode) in segs:
        for _ in range(2 if mode == 'hnorm_both' else 1):
            out_shape.append(jax.ShapeDtypeStruct((M, nt * tn), F32))
            out_specs.append(pl.BlockSpec(
                (tm, tn), lambda i, j, t0=t0, nt=nt: (i, jnp.clip(j - t0, 0, nt - 1))))
    return pl.pallas_call(
        functools.partial(_proj_kernel, tuple(segs), tn),
        out_shape=out_shape,
        grid=(M // tm, ntiles),
        in_specs=[
            pl.BlockSpec((tm, K), lambda i, j: (i, 0)),
            pl.BlockSpec((1, K), lambda i, j: (0, 0)),
            pl.BlockSpec((K, tn), lambda i, j: (0, j)),
            pl.BlockSpec((1, tn), lambda i, j: (0, j)),
            pl.BlockSpec((1, tn), lambda i, j: (0, j)),
            pl.BlockSpec((tm, HEAD_DIM), lambda i, j: (i % nseq, 0)),
            pl.BlockSpec((tm, HEAD_DIM), lambda i, j: (i % nseq, 0)),
        ],
        out_specs=out_specs,
        scratch_shapes=[pltpu.VMEM((tm, K), BF16)],
        compiler_params=_cparams(("arbitrary", "arbitrary")),
        name=name,
    )(x, norm_g, w, gain_all, bias_all, cosf, sinf)


def _outproj_kernel(n_lhs, *refs):
    lhs = refs[:n_lhs]
    w_ref, res_ref, o_ref, xs_ref = refs[n_lhs:n_lhs + 4]

    @pl.when(pl.program_id(1) == 0)
    def _():
        x = lhs[0][...]
        for r in lhs[1:]:
            x = x + r[...]
        xs_ref[...] = x.astype(BF16)

    o_ref[...] = res_ref[...] + _dot(xs_ref[...], w_ref[...])


def _outproj(lhs_list, w, res, *, tm, tn, name):
    M, K = lhs_list[0].shape
    N = w.shape[1]
    n = len(lhs_list)
    return pl.pallas_call(
        functools.partial(_outproj_kernel, n),
        out_shape=jax.ShapeDtypeStruct((M, N), F32),
        grid=(M // tm, N // tn),
        in_specs=[pl.BlockSpec((tm, K), lambda i, j: (i, 0)) for _ in range(n)] + [
            pl.BlockSpec((K, tn), lambda i, j: (0, j)),
            pl.BlockSpec((tm, tn), lambda i, j: (i, j)),
        ],
        out_specs=pl.BlockSpec((tm, tn), lambda i, j: (i, j)),
        scratch_shapes=[pltpu.VMEM((tm, K), BF16)],
        compiler_params=_cparams(("arbitrary", "arbitrary")),
        name=name,
    )(*lhs_list, w, res)


def _ffn_kernel(x_ref, g_ref, wg_ref, wu_ref, wd_ref, o_ref, xn_ref):
    f = pl.program_id(1)

    @pl.when(f == 0)
    def _():
        xf = x_ref[...]
        ms = jnp.mean(xf * xf, axis=-1, keepdims=True)
        xn_ref[...] = (xf * lax.rsqrt(ms + EPS) * g_ref[...]).astype(BF16)
        o_ref[...] = xf

    xn = xn_ref[...]
    h = _silu(_dot(xn, wg_ref[...])) * _dot(xn, wu_ref[...])
    o_ref[...] += _dot(h.astype(BF16), wd_ref[...])


def _ffn(x, norm_g, wg, wu, wd, *, tm, tf, name):
    M, K = x.shape
    F = wg.shape[1]
    return pl.pallas_call(
        _ffn_kernel,
        out_shape=jax.ShapeDtypeStruct((M, K), F32),
        grid=(M // tm, F // tf),
        in_specs=[
            pl.BlockSpec((tm, K), lambda i, f: (i, 0)),
            pl.BlockSpec((1, K), lambda i, f: (0, 0)),
            pl.BlockSpec((K, tf), lambda i, f: (0, f)),
            pl.BlockSpec((K, tf), lambda i, f: (0, f)),
            pl.BlockSpec((tf, K), lambda i, f: (f, 0)),
        ],
        out_specs=pl.BlockSpec((tm, K), lambda i, f: (i, 0)),
        scratch_shapes=[pltpu.VMEM((tm, K), BF16)],
        compiler_params=_cparams(("arbitrary", "arbitrary")),
        name=name,
    )(x, norm_g, wg, wu, wd)


def _moe_kernel(x_ref, g_ref, rw_ref, rb_ref, wg_ref, wu_ref, wd_ref, o_ref, xn_ref, gate_ref):
    e = pl.program_id(1)
    f = pl.program_id(2)

    @pl.when((e == 0) & (f == 0))
    def _():
        xf = x_ref[...]
        ms = jnp.mean(xf * xf, axis=-1, keepdims=True)
        xn = (xf * lax.rsqrt(ms + EPS) * g_ref[...]).astype(BF16)
        xn_ref[...] = xn
        o_ref[...] = xf
        lane = lax.broadcasted_iota(jnp.int32, (xf.shape[0], LANES), 1)
        logits = jnp.where(lane < N_EXPERTS, _dot(xn, rw_ref[...]) + rb_ref[...], NEG)
        m1 = jnp.max(logits, axis=-1, keepdims=True)
        i1 = jnp.min(jnp.where(logits == m1, lane, LANES), axis=-1, keepdims=True)
        l2 = jnp.where(lane == i1, NEG, logits)
        m2 = jnp.max(l2, axis=-1, keepdims=True)
        i2 = jnp.min(jnp.where(l2 == m2, lane, LANES), axis=-1, keepdims=True)
        e2 = jnp.exp(m2 - m1)
        w1 = 1.0 / (1.0 + e2)
        gate_ref[...] = jnp.where(lane == i1, w1, 0.0) + jnp.where(lane == i2, e2 * w1, 0.0)

    xn = xn_ref[...]
    h = _silu(_dot(xn, wg_ref[0])) * _dot(xn, wu_ref[0])
    lane = lax.broadcasted_iota(jnp.int32, gate_ref.shape, 1)
    ge = jnp.sum(jnp.where(lane == e, gate_ref[...], 0.0), axis=-1, keepdims=True)
    o_ref[...] += ge * _dot(h.astype(BF16), wd_ref[0])


def _moe(x, norm_g, rw, rb, wg, wu, wd, *, tm, tf, name):
    M, K = x.shape
    E, _, F = wg.shape
    return pl.pallas_call(
        _moe_kernel,
        out_shape=jax.ShapeDtypeStruct((M, K), F32),
        grid=(M // tm, E, F // tf),
        in_specs=[
            pl.BlockSpec((tm, K), lambda i, e, f: (i, 0)),
            pl.BlockSpec((1, K), lambda i, e, f: (0, 0)),
            pl.BlockSpec((K, LANES), lambda i, e, f: (0, 0)),
            pl.BlockSpec((1, LANES), lambda i, e, f: (0, 0)),
            pl.BlockSpec((1, K, tf), lambda i, e, f: (e, 0, f)),
            pl.BlockSpec((1, K, tf), lambda i, e, f: (e, 0, f)),
            pl.BlockSpec((1, tf, K), lambda i, e, f: (e, f, 0)),
        ],
        out_specs=pl.BlockSpec((tm, K), lambda i, e, f: (i, 0)),
        scratch_shapes=[pltpu.VMEM((tm, K), BF16), pltpu.VMEM((tm, LANES), F32)],
        compiler_params=_cparams(("arbitrary", "arbitrary", "arbitrary")),
        name=name,
    )(x, norm_g, rw, rb, wg, wu, wd)


TM_E = 512


def _top2(logits, lane):
    m1 = jnp.max(logits, axis=-1, keepdims=True)
    i1 = jnp.min(jnp.where(logits == m1, lane, LANES), axis=-1, keepdims=True)
    l2 = jnp.where(lane == i1, NEG, logits)
    m2 = jnp.max(l2, axis=-1, keepdims=True)
    i2 = jnp.min(jnp.where(l2 == m2, lane, LANES), axis=-1, keepdims=True)
    e2 = jnp.exp(m2 - m1)
    w1 = 1.0 / (1.0 + e2)
    return i1, i2, w1, e2 * w1


def _route_kernel(x_ref, g_ref, rw_ref, rb_ref, xn_ref, route_ref, cnt_ref, carry):
    @pl.when(pl.program_id(0) == 0)
    def _():
        carry[...] = jnp.zeros_like(carry)

    xf = x_ref[...]
    tm = xf.shape[0]
    xn = xf * lax.rsqrt(jnp.mean(xf * xf, axis=-1, keepdims=True) + EPS) * g_ref[...]
    xn_ref[...] = xn
    lane = lax.broadcasted_iota(jnp.int32, (tm, LANES), 1)
    logits = jnp.where(lane < N_EXPERTS, _dot(xn.astype(BF16), rw_ref[...]) + rb_ref[...], NEG)
    i1, i2, w1, w2 = _top2(logits, lane)
    hit = jnp.where(lane == i1, 1.0, 0.0) + jnp.where(lane == i2, 1.0, 0.0)
    incl = _dot(_tril_bf16(tm), hit.astype(BF16)) + carry[...]
    before = incl - hit
    carry[...] = incl[tm - 1:tm, :]
    cnt_ref[...] = incl[tm - 1:tm, :]
    r1 = jnp.sum(jnp.where(lane == i1, before, 0.0), axis=-1, keepdims=True)
    r2 = jnp.sum(jnp.where(lane == i2, before, 0.0), axis=-1, keepdims=True)
    cols = (i1.astype(F32), i2.astype(F32), r1, r2, w1, w2)
    rec = jnp.zeros((tm, LANES), F32)
    for c, v in enumerate(cols):
        rec = jnp.where(lane == c, v, rec)
    route_ref[...] = rec


def _route(x, norm_g, rw, rb, *, tm):
    M, K = x.shape
    return pl.pallas_call(
        _route_kernel,
        out_shape=[jax.ShapeDtypeStruct((M, K), F32), jax.ShapeDtypeStruct((M, LANES), F32),
                   jax.ShapeDtypeStruct((1, LANES), F32)],
        grid=(M // tm,),
        in_specs=[pl.BlockSpec((tm, K), lambda i: (i, 0)), pl.BlockSpec((1, K), lambda i: (0, 0)),
                  pl.BlockSpec((K, LANES), lambda i: (0, 0)), pl.BlockSpec((1, LANES), lambda i: (0, 0))],
        out_specs=[pl.BlockSpec((tm, K), lambda i: (i, 0)), pl.BlockSpec((tm, LANES), lambda i: (i, 0)),
                   pl.BlockSpec((1, LANES), lambda i: (0, 0))],
        scratch_shapes=[pltpu.VMEM((1, LANES), F32)],
        compiler_params=_cparams(("arbitrary",)),
        name="moe_route",
    )(x, norm_g, rw, rb)


def _gather_start(src_hbm, idx_ref, first, n, dst, sem, unroll):
    def body(r, carry):
        pltpu.make_async_copy(src_hbm.at[pl.ds(idx_ref[first + r], 1)], dst.at[pl.ds(r, 1)], sem).start()
        return carry
    lax.fori_loop(0, n, body, 0, unroll=unroll)


def _gather_wait(dst, sem):
    pltpu.make_async_copy(dst, dst, sem).wait()


def _experts_kernel(te_ref, nu_ref, src_ref, x_hbm, wg_ref, wu_ref, wd_ref, o_ref, xbuf, xb_scr, acc_scr, sems):
    i, f = pl.program_id(0), pl.program_id(1)
    last = pl.num_programs(1) - 1
    live = i < nu_ref[0]
    slot = i % 2

    @pl.when(live & (f == 0))
    def _():
        @pl.when(i == 0)
        def _():
            _gather_start(x_hbm, src_ref, 0, TM_E, xbuf.at[0], sems.at[0], 8)

        _gather_wait(xbuf.at[slot], sems.at[slot])
        xb_scr[...] = xbuf[slot].astype(BF16)
        acc_scr[...] = jnp.zeros_like(acc_scr)

        @pl.when(i + 1 < nu_ref[0])
        def _():
            _gather_start(x_hbm, src_ref, (i + 1) * TM_E, TM_E, xbuf.at[1 - slot], sems.at[1 - slot], 8)

    @pl.when(live)
    def _():
        xb = xb_scr[...]
        h = _silu(_dot(xb, wg_ref[0])) * _dot(xb, wu_ref[0])
        acc_scr[...] += _dot(h.astype(BF16), wd_ref[0])

    @pl.when(f == last)
    def _():
        o_ref[...] = jnp.where(live, acc_scr[...], 0.0)


def _experts(tile_expert, n_used, src, xn, wg, wu, wd, *, tf):
    P = src.shape[0]
    E, K, F = wg.shape
    nf = F // tf
    col = lambda i, f, nu: jnp.where(i < nu[0], f, nf - 1)
    return pl.pallas_call(
        _experts_kernel,
        out_shape=jax.ShapeDtypeStruct((P, K), F32),
        grid_spec=pltpu.PrefetchScalarGridSpec(
            num_scalar_prefetch=3, grid=(P // TM_E, nf),
            in_specs=[pl.BlockSpec(memory_space=pl.ANY),
                      pl.BlockSpec((1, K, tf), lambda i, f, te, nu, sr: (te[i], 0, col(i, f, nu))),
                      pl.BlockSpec((1, K, tf), lambda i, f, te, nu, sr: (te[i], 0, col(i, f, nu))),
                      pl.BlockSpec((1, tf, K), lambda i, f, te, nu, sr: (te[i], col(i, f, nu), 0))],
            out_specs=pl.BlockSpec((TM_E, K), lambda i, f, te, nu, sr: (i, 0)),
            scratch_shapes=[pltpu.VMEM((2, TM_E, K), F32), pltpu.VMEM((TM_E, K), BF16), pltpu.VMEM((TM_E, K), F32),
                            pltpu.SemaphoreType.DMA((2,))]),
        compiler_params=_cparams(("arbitrary", "arbitrary")),
        name="moe_experts",
    )(tile_expert, n_used, src, xn, wg, wu, wd)


def _combine_kernel(tc, dest_ref, os_hbm, x_ref, route_ref, o_ref, buf, sems):
    i = pl.program_id(0)
    slot = i % 2

    @pl.when(i == 0)
    def _():
        _gather_start(os_hbm, dest_ref, 0, 2 * tc, buf.at[0], sems.at[0], 8)

    @pl.when(i + 1 < pl.num_programs(0))
    def _():
        _gather_start(os_hbm, dest_ref, (i + 1) * 2 * tc, 2 * tc, buf.at[1 - slot], sems.at[1 - slot], 8)

    _gather_wait(buf.at[slot], sems.at[slot])
    w1, w2 = route_ref[:, 4:5], route_ref[:, 5:6]
    o_ref[...] = x_ref[...] + (w1 * buf[slot, 0:tc, :] + w2 * buf[slot, tc:2 * tc, :])


def _combine(dest, os, x, route, *, tc):
    M, K = x.shape
    return pl.pallas_call(
        functools.partial(_combine_kernel, tc),
        out_shape=jax.ShapeDtypeStruct((M, K), F32),
        grid_spec=pltpu.PrefetchScalarGridSpec(
            num_scalar_prefetch=1, grid=(M // tc,),
            in_specs=[pl.BlockSpec(memory_space=pl.ANY),
                      pl.BlockSpec((tc, K), lambda i, d: (i, 0)), pl.BlockSpec((tc, LANES), lambda i, d: (i, 0))],
            out_specs=pl.BlockSpec((tc, K), lambda i, d: (i, 0)),
            scratch_shapes=[pltpu.VMEM((2, 2 * tc, K), F32), pltpu.SemaphoreType.DMA((2,))]),
        compiler_params=_cparams(("arbitrary",)),
        name="moe_combine",
    )(dest, os, x, route)


def _moe_routed(x, norm_g, rw, rb, wg, wu, wd, *, tm):
    M = x.shape[0]
    xn, route, counts = _route(x, norm_g, rw, rb, tm=tm)
    e = route[:, 0:2].astype(jnp.int32)
    rank = route[:, 2:4].astype(jnp.int32)
    cnt = counts[0, :N_EXPERTS].astype(jnp.int32)
    tiles = (cnt + TM_E - 1) // TM_E
    tile_end = jnp.cumsum(tiles)
    dest = ((tile_end - tiles) * TM_E)[e] + rank
    n_tiles = (2 * M) // TM_E + N_EXPERTS
    P = n_tiles * TM_E
    token = jnp.repeat(jnp.arange(M, dtype=jnp.int32), 2)
    src = jnp.zeros((P,), jnp.int32).at[dest.reshape(-1)].set(token)
    tile_expert = jnp.minimum(jnp.searchsorted(tile_end, jnp.arange(n_tiles, dtype=jnp.int32), side='right'),
                              N_EXPERTS - 1).astype(jnp.int32)
    os = _experts(tile_expert, tile_end[-1:].astype(jnp.int32), src, xn, wg, wu, wd, tf=256)
    tc = 256
    dest_tiles = dest.reshape(M // tc, tc, 2).transpose(0, 2, 1).reshape(-1).astype(jnp.int32)
    return _combine(dest_tiles, os, x, route, tc=tc)


HGRN_HEADS_PER_STEP = 4


def _hgrn_kernel(T, c, aq_ref, af_ref, ai_ref, ag_ref, lb_ref, gn_ref, o_ref, st_ref, st_scr):
    t = pl.program_id(2)

    @pl.when(t == 0)
    def _():
        st_scr[...] = jnp.zeros_like(st_scr)

    row = lax.broadcasted_iota(jnp.int32, (T, T), 0)
    col = lax.broadcasted_iota(jnp.int32, (T, T), 1)
    shift = c.bit_length() - 1
    same = (row >> shift) == (col >> shift)
    causal = same & (col <= row)
    tri = jnp.where(causal, 1.0, 0.0).astype(BF16)
    blk = jnp.where(same, 1.0, 0.0).astype(BF16)
    gn = gn_ref[...]
    for hh in range(HGRN_HEADS_PER_STEP):
        cols = slice(hh * HEAD_DIM, (hh + 1) * HEAD_DIM)
        lb = lb_ref[:, cols]
        gate = lb + (1.0 - lb) * _sigmoid(af_ref[:, cols])
        k = 1.0 - gate
        parts = _split3(jnp.log(gate))
        b = sum(_dot(tri, p) for p in parts)
        bend = sum(_dot(blk, p) for p in parts)
        qe = (_silu(aq_ref[:, cols]) * jnp.exp(b)).astype(BF16)
        ke = (k * jnp.exp(-b)).astype(BF16)
        kend = (k * jnp.exp(bend - b)).astype(BF16)
        vb = ai_ref[:, cols].astype(BF16)
        o_intra = _dot(jnp.where(causal, _dot_nt(qe, ke), 0.0).astype(BF16), vb)
        st = st_scr[hh]
        for u in range(T // c):
            rows = slice(u * c, (u + 1) * c)
            o = o_intra[rows] + _dot_nt(qe[rows], st.astype(BF16))
            st = st * jnp.exp(bend[u * c:u * c + 1]) + _dot_tn(vb[rows], kend[rows])
            o_ref[rows, cols] = _head_rms(o, gn) * _silu(ag_ref[rows, cols])
        st_scr[hh] = st

    @pl.when(t == pl.num_programs(2) - 1)
    def _():
        for hh in range(HGRN_HEADS_PER_STEP):
            st_ref[0, hh] = st_scr[hh].T


def _hgrn_prompt(hg, lb, gn, B, S, *, T, c):
    nt = S // T
    H = H_HGRN
    hp = HGRN_HEADS_PER_STEP
    W = hp * HEAD_DIM
    spec = lambda off: pl.BlockSpec((T, W), lambda b, h, t, off=off: (b * nt + t, off + h))
    return pl.pallas_call(
        functools.partial(_hgrn_kernel, T, c),
        out_shape=[jax.ShapeDtypeStruct((B * S, H * HEAD_DIM), F32),
                   jax.ShapeDtypeStruct((B, H, HEAD_DIM, HEAD_DIM), F32)],
        grid=(B, H // hp, nt),
        in_specs=[spec(0), spec(H // hp), spec(2 * H // hp), spec(3 * H // hp),
                  pl.BlockSpec((1, W), lambda b, h, t: (0, h)),
                  pl.BlockSpec((1, HEAD_DIM), lambda b, h, t: (0, 0))],
        out_specs=[pl.BlockSpec((T, W), lambda b, h, t: (b * nt + t, h)),
                   pl.BlockSpec((1, hp, HEAD_DIM, HEAD_DIM), lambda b, h, t: (b, h, 0, 0))],
        scratch_shapes=[pltpu.VMEM((hp, HEAD_DIM, HEAD_DIM), F32)],
        compiler_params=_cparams(("arbitrary", "arbitrary", "arbitrary")),
        name="hgrn_prompt",
    )(hg, hg, hg, hg, lb, gn)


BIAS_SHIFT = 4
BIAS_SLOT = 1 << BIAS_SHIFT


def _cumsum_kernel(lf_ref, pq_ref, pk_ref, oq_ref, ok_ref, aq_ref, ak_ref, carry):
    @pl.when(pl.program_id(1) == 0)
    def _():
        carry[...] = jnp.zeros_like(carry)

    tc = lf_ref.shape[0]
    c = _dot3(_tril_bf16(tc), lf_ref[...]) + carry[...]
    carry[...] = c[tc - 1:tc, :]
    parts = _split3(c * LOG2E)
    aq_ref[...] = sum(_dot(t, pq_ref[i]) for i, t in enumerate(parts)) + oq_ref[...]
    ak_ref[...] = (ok_ref[...] - sum(_dot(t, pk_ref[i]) for i, t in enumerate(parts))).astype(BF16)


def _bias_layout():
    pq, pk = np.zeros((3, LANES, LANES), np.float32), np.zeros((3, LANES, LANES), np.float32)
    oq, ok = np.zeros((1, LANES), np.float32), np.zeros((1, LANES), np.float32)
    for h in range(H_FOX):
        for i in range(3):
            pq[i, h, BIAS_SLOT * h + i] = 1.0
            pk[i, h, BIAS_SLOT * h + 3 + i] = 1.0
            oq[0, BIAS_SLOT * h + 3 + i] = 1.0
            ok[0, BIAS_SLOT * h + i] = 1.0
    return jnp.asarray(pq, BF16), jnp.asarray(pk, BF16), jnp.asarray(oq), jnp.asarray(ok)


def _seq_cumsum(lf, B, S, *, tc):
    n = S // tc
    const3 = pl.BlockSpec((3, LANES, LANES), lambda b, t: (0, 0, 0))
    const1 = pl.BlockSpec((1, LANES), lambda b, t: (0, 0))
    blk = pl.BlockSpec((tc, LANES), lambda b, t: (b * n + t, 0))
    return pl.pallas_call(
        _cumsum_kernel,
        out_shape=[jax.ShapeDtypeStruct((B * S, LANES), F32), jax.ShapeDtypeStruct((B * S, LANES), BF16)],
        grid=(B, n),
        in_specs=[blk, const3, const3, const1, const1],
        out_specs=[blk, blk],
        scratch_shapes=[pltpu.VMEM((1, LANES), F32)],
        compiler_params=_cparams(("arbitrary", "arbitrary")),
        name="seq_cumsum",
    )(lf, *_bias_layout())


def _online_init(m_scr, l_scr, acc_scr):
    m_scr[...] = jnp.full_like(m_scr, M_INIT)
    l_scr[...] = jnp.zeros_like(l_scr)
    acc_scr[...] = jnp.zeros_like(acc_scr)


def _online_update(s, vb, m_scr, l_scr, acc_scr):
    m_prev = m_scr[...]
    m_new = jnp.maximum(m_prev, jnp.max(s, axis=-1, keepdims=True))
    alpha = jnp.exp(m_prev - m_new)
    p = jnp.exp(s - m_new)
    l_scr[...] = alpha * l_scr[...] + jnp.sum(p, axis=-1, keepdims=True)
    acc_scr[...] = alpha * acc_scr[...] + _dot(p.astype(BF16), vb)
    m_scr[...] = m_new


def _online_result(l_scr, acc_scr):
    l = l_scr[...]
    return acc_scr[...] / jnp.where(l > 0.0, l, 1.0)


def _flash_t_step(sT, vT, m, l, acc_scr):
    m_new = jnp.maximum(m, jnp.max(sT, axis=0, keepdims=True))
    alpha = jnp.exp2(m - m_new)
    p = jnp.exp2(sT - m_new)
    acc_scr[...] = alpha * acc_scr[...] + _dot(vT, p.astype(BF16))
    return m_new, alpha * l + jnp.sum(p, axis=0, keepdims=True)


def _flash_t_causal(ka_scr, vt_scr, qa, acc_scr, n_full, n_diag, tk, qpos):
    R = qa.shape[1]
    acc_scr[...] = jnp.zeros_like(acc_scr)

    def body(kj, carry):
        return _flash_t_step(_dot(ka_scr[kj], qa), vt_scr[kj], *carry, acc_scr)

    m, l = lax.fori_loop(0, n_full, body, (jnp.full((1, R), M_INIT, F32), jnp.zeros((1, R), F32)))
    for i in range(n_diag):
        kpos = (n_full + i) * tk + lax.broadcasted_iota(jnp.int32, (tk, R), 0)
        sT = jnp.where(kpos <= qpos, _dot(ka_scr[n_full + i], qa), NEG)
        m, l = _flash_t_step(sT, vt_scr[n_full + i], m, l, acc_scr)
    return acc_scr[...] / jnp.where(l > 0.0, l, 1.0)


def _fox_kernel(tq, tk, q_ref, k_ref, vt_ref, aq_ref, ak_ref, o_ref, ka_scr, vt_scr, acc_scr):
    h = pl.program_id(1)
    qi = pl.program_id(2)

    @pl.when(qi == 0)
    def _():
        for c in range(ka_scr.shape[0]):
            rows = slice(c * tk, (c + 1) * tk)
            ka_scr[c] = jnp.concatenate([k_ref[rows, :].astype(BF16), ak_ref[rows, :]], axis=1)
            vt_scr[c] = vt_ref[0, 0, :, rows].astype(BF16)

    lane = lax.broadcasted_iota(jnp.int32, (tq, LANES), 1)
    aq = jnp.where((lane >> BIAS_SHIFT) == h, aq_ref[...], 0.0)
    qa = jnp.concatenate([(q_ref[...] * QSCALE).T, aq.T], axis=0).astype(BF16)
    qpos = qi * tq + lax.broadcasted_iota(jnp.int32, (tk, tq), 1)
    o_ref[...] = _flash_t_causal(ka_scr, vt_scr, qa, acc_scr, qi * (tq // tk), tq // tk, tk, qpos).T


def _fox_prompt(fq, fk, fvt, aug_q, aug_k, B, S, *, tq, tk):
    nq = S // tq
    return pl.pallas_call(
        functools.partial(_fox_kernel, tq, tk),
        out_shape=jax.ShapeDtypeStruct(fq.shape, F32),
        grid=(B, H_FOX, nq),
        in_specs=[
            pl.BlockSpec((tq, HEAD_DIM), lambda b, h, qi: (b * nq + qi, h)),
            pl.BlockSpec((S, HEAD_DIM), lambda b, h, qi: (b, h)),
            pl.BlockSpec((1, 1, HEAD_DIM, S), lambda b, h, qi: (b, h, 0, 0)),
            pl.BlockSpec((tq, LANES), lambda b, h, qi: (b * nq + qi, 0)),
            pl.BlockSpec((S, LANES), lambda b, h, qi: (b, 0)),
        ],
        out_specs=pl.BlockSpec((tq, HEAD_DIM), lambda b, h, qi: (b * nq + qi, h)),
        scratch_shapes=[pltpu.VMEM((S // tk, tk, 2 * HEAD_DIM), BF16), pltpu.VMEM((S // tk, HEAD_DIM, tk), BF16),
                        pltpu.VMEM((HEAD_DIM, tq), F32)],
        compiler_params=_cparams(("arbitrary",) * 3),
        name="fox_prompt",
    )(fq, fk, fvt, aug_q, aug_k)


def _compress_kernel(nc, x_ref, pea_ref, peb_ref, w1a_ref, w1b_ref, w2_ref, o_ref):
    R = jnp.concatenate([x_ref[0, pl.ds(r, nc, stride=CMP_STRIDE), :] for r in range(CMP_STRIDE)], axis=1)
    p1 = _dot((R + pea_ref[...]).astype(BF16), w1a_ref[...])
    p2 = _dot((R + peb_ref[...]).astype(BF16), w1b_ref[...])
    pre = p1 + pltpu.roll(p2, nc - 1, 0)
    kc = _dot(_silu(pre).astype(BF16), w2_ref[...])
    row = lax.broadcasted_iota(jnp.int32, kc.shape, 0)
    o_ref[0, 0] = jnp.where(row < nc - 1, kc, 0.0)


def _cmp_weights(pe, w1, w2):
    half = CMP_STRIDE * HEAD_DIM
    return (pe[:CMP_STRIDE].reshape(1, half), pe[CMP_STRIDE:].reshape(1, half),
            w1[:half].astype(BF16), w1[half:].astype(BF16), w2.astype(BF16))


def _compress_prompt(x, cw, B, S):
    nc = S // CMP_STRIDE
    half = CMP_STRIDE * HEAD_DIM
    const = lambda shape: pl.BlockSpec(shape, lambda b, g: (0, 0))
    return pl.pallas_call(
        functools.partial(_compress_kernel, nc),
        out_shape=jax.ShapeDtypeStruct((B, NSA_GROUPS, nc, HEAD_DIM), F32),
        grid=(B, NSA_GROUPS),
        in_specs=[pl.BlockSpec((1, S, HEAD_DIM), lambda b, g: (b, 0, g)),
                  const((1, half)), const((1, half)), const((half, HEAD_DIM)), const((half, HEAD_DIM)),
                  const((HEAD_DIM, HEAD_DIM))],
        out_specs=pl.BlockSpec((1, 1, nc, HEAD_DIM), lambda b, g: (b, g, 0, 0)),
        compiler_params=_cparams(("arbitrary", "arbitrary")),
        name="nsa_compress",
    )(x, *cw)


def _overlap_t(n_cmp, n_sel):
    cs = np.arange(n_cmp)[None, :] * CMP_STRIDE
    ss = np.arange(n_sel)[:, None] * SEL_LEN
    ov = np.minimum(cs + CMP_LEN, ss + SEL_LEN) - np.maximum(cs, ss)
    return np.maximum(ov, 0).astype(np.float32) / CMP_LEN


def _group_rows(ref, Tq):
    return jnp.concatenate([ref[0, :, r * HEAD_DIM:(r + 1) * HEAD_DIM] for r in range(NSA_R)], axis=0)


def _store_gated(o_ref, o, gates, Tq, branch):
    for r in range(NSA_R):
        gcol = gates[:, 3 * r + branch:3 * r + branch + 1]
        o_ref[0, :, r * HEAD_DIM:(r + 1) * HEAD_DIM] = o[r * Tq:(r + 1) * Tq] * gcol


def _cmpsel_kernel(Tq, nsel, q_ref, kc_ref, vc_ref, ovt_ref, gate_ref, o_ref, sel_ref):
    qi = pl.program_id(2)
    q = (_group_rows(q_ref, Tq) * (HEAD_DIM ** -0.5)).astype(BF16)
    s = _dot_nt(q, kc_ref[0, 0].astype(BF16))
    qpos = qi * Tq + (lax.broadcasted_iota(jnp.int32, s.shape, 0) & (Tq - 1))
    cend = lax.broadcasted_iota(jnp.int32, s.shape, 1) * CMP_STRIDE + (CMP_LEN - 1)
    mask = cend <= qpos
    sm = jnp.where(mask, s, NEG)
    e = jnp.where(mask, jnp.exp(sm - jnp.max(sm, axis=-1, keepdims=True)), 0.0)
    den = jnp.sum(e, axis=-1, keepdims=True)
    p = e / jnp.where(den > 0.0, den, 1.0)
    o = _dot(p.astype(BF16), vc_ref[0, 0].astype(BF16))
    _store_gated(o_ref, o, gate_ref[0, 0], Tq, 0)

    psum = p[0:Tq]
    for r in range(1, NSA_R):
        psum = psum + p[r * Tq:(r + 1) * Tq]
    ovt = ovt_ref[...]
    sc = sum(_dot_nt(ovt, t) for t in _split3(psum))
    n = lax.broadcasted_iota(jnp.int32, sc.shape, 0)
    tpos = qi * Tq + lax.broadcasted_iota(jnp.int32, sc.shape, 1)
    cur = tpos >> SEL_SHIFT
    forced = (n == 0) | (n == cur) | (n == cur - 1)
    sc = jnp.where(n * SEL_LEN <= tpos, jnp.where(forced, -NEG, sc), NEG)
    rank = jnp.zeros(sc.shape, F32)
    for m in range(nsel):
        rowm = sc[m:m + 1, :]
        rank = rank + jnp.where(n > m, jnp.where(rowm >= sc, 1.0, 0.0), jnp.where(rowm > sc, 1.0, 0.0))
    selneg = jnp.where(rank < TOPN, 0.0, NEG)
    if nsel < LANES:
        selneg = jnp.concatenate([selneg, jnp.full((LANES - nsel, Tq), NEG, F32)], axis=0)
    sel_ref[0, 0, 0] = selneg.astype(BF16)


def _cmpsel_prompt(qn, kc, vc, gates_g, B, S, *, Tq):
    nq = S // Tq
    ncp = kc.shape[2]
    nsel = S // SEL_LEN
    ovt = jnp.asarray(_overlap_t(ncp, nsel), BF16)
    return pl.pallas_call(
        functools.partial(_cmpsel_kernel, Tq, nsel),
        out_shape=[jax.ShapeDtypeStruct((B, S, H_NSA * HEAD_DIM), F32),
                   jax.ShapeDtypeStruct((B, NSA_GROUPS, nq, LANES, Tq), BF16)],
        grid=(B, NSA_GROUPS, nq),
        in_specs=[pl.BlockSpec((1, Tq, NSA_R * HEAD_DIM), lambda b, g, qi: (b, qi, g)),
                  pl.BlockSpec((1, 1, ncp, HEAD_DIM), lambda b, g, qi: (b, g, 0, 0)),
                  pl.BlockSpec((1, 1, ncp, HEAD_DIM), lambda b, g, qi: (b, g, 0, 0)),
                  pl.BlockSpec((nsel, ncp), lambda b, g, qi: (0, 0)),
                  pl.BlockSpec((1, 1, Tq, 16), lambda b, g, qi: (b, g, qi, 0))],
        out_specs=[pl.BlockSpec((1, Tq, NSA_R * HEAD_DIM), lambda b, g, qi: (b, qi, g)),
                   pl.BlockSpec((1, 1, 1, LANES, Tq), lambda b, g, qi: (b, g, qi, 0, 0))],
        compiler_params=_cparams(("arbitrary",) * 3),
        name="nsa_cmpsel",
    )(qn, kc, vc, ovt, gates_g)


def _queries_t(q_ref, Tq):
    return jnp.concatenate([(q_ref[0, :, r * HEAD_DIM:(r + 1) * HEAD_DIM] * QSCALE).T for r in range(NSA_R)],
                           axis=1).astype(BF16)


def _store_gated_t(o_ref, oT, gates, Tq, branch):
    for r in range(NSA_R):
        gcol = gates[:, 3 * r + branch:3 * r + branch + 1]
        o_ref[0, :, r * HEAD_DIM:(r + 1) * HEAD_DIM] = oT[:, r * Tq:(r + 1) * Tq].T * gcol


def _sel_kernel(Tq, tk, q_ref, sn_ref, k_ref, vt_ref, e_ref, gate_ref, o_ref, ka_scr, vt_scr, acc_scr):
    qi = pl.program_id(2)

    @pl.when(qi == 0)
    def _():
        for c in range(ka_scr.shape[0]):
            rows = slice(c * tk, (c + 1) * tk)
            ka_scr[c] = jnp.concatenate([k_ref[0, rows, :].astype(BF16), e_ref[rows, :]], axis=1)
            vt_scr[c] = vt_ref[0, 0, :, rows].astype(BF16)

    sn = sn_ref[0, 0, 0]
    qa = jnp.concatenate([_queries_t(q_ref, Tq), jnp.concatenate([sn] * NSA_R, axis=1)], axis=0)
    R = NSA_R * Tq
    qpos = qi * Tq + (lax.broadcasted_iota(jnp.int32, (tk, R), 1) & (Tq - 1))
    oT = _flash_t_causal(ka_scr, vt_scr, qa, acc_scr, (qi * Tq) // tk, 1, tk, qpos)
    _store_gated_t(o_ref, oT, gate_ref[0, 0], Tq, 1)


def _sel_prompt(qr, selneg, sk, svt, gates_g, B, S, *, Tq, tk):
    nq = S // Tq
    key = np.arange(S)[:, None] // SEL_LEN
    e_all = jnp.asarray((key == np.arange(LANES)[None, :]).astype(np.float32), BF16)
    R = NSA_R * Tq
    return pl.pallas_call(
        functools.partial(_sel_kernel, Tq, tk),
        out_shape=jax.ShapeDtypeStruct((B, S, H_NSA * HEAD_DIM), F32),
        grid=(B, NSA_GROUPS, nq),
        in_specs=[pl.BlockSpec((1, Tq, NSA_R * HEAD_DIM), lambda b, g, qi: (b, qi, g)),
                  pl.BlockSpec((1, 1, 1, LANES, Tq), lambda b, g, qi: (b, g, qi, 0, 0)),
                  pl.BlockSpec((1, S, HEAD_DIM), lambda b, g, qi: (b, 0, g)),
                  pl.BlockSpec((1, 1, HEAD_DIM, S), lambda b, g, qi: (b, g, 0, 0)),
                  pl.BlockSpec((S, LANES), lambda b, g, qi: (0, 0)),
                  pl.BlockSpec((1, 1, Tq, 16), lambda b, g, qi: (b, g, qi, 0))],
        out_specs=pl.BlockSpec((1, Tq, NSA_R * HEAD_DIM), lambda b, g, qi: (b, qi, g)),
        scratch_shapes=[pltpu.VMEM((S // tk, tk, 2 * HEAD_DIM), BF16), pltpu.VMEM((S // tk, HEAD_DIM, tk), BF16),
                        pltpu.VMEM((HEAD_DIM, R), F32)],
        compiler_params=_cparams(("arbitrary",) * 3),
        name="nsa_sel",
    )(qr, selneg, sk, svt, e_all, gates_g)


def _win_kernel(Tq, nwb, q_ref, *refs):
    k_refs, vt_refs = refs[:nwb], refs[nwb:2 * nwb]
    gate_ref, o_ref = refs[2 * nwb:]
    qi = pl.program_id(2)
    R = NSA_R * Tq
    qT = _queries_t(q_ref, Tq)
    qpos = qi * Tq + (lax.broadcasted_iota(jnp.int32, (Tq, R), 1) & (Tq - 1))
    krow = lax.broadcasted_iota(jnp.int32, (Tq, R), 0)
    s = []
    for i in range(nwb):
        kb = qi - (nwb - 1) + i
        si = _dot(k_refs[i][0].astype(BF16), qT)
        if i == 0:
            si = jnp.where(qpos - (kb * Tq + krow) < WINDOW, si, NEG)
        if i == nwb - 1:
            si = jnp.where(kb * Tq + krow <= qpos, si, NEG)
        else:
            si = si + jnp.where(kb >= 0, 0.0, NEG)
        s.append(si)
    m = functools.reduce(jnp.maximum, [jnp.max(si, axis=0, keepdims=True) for si in s])
    p = [jnp.exp2(si - m) for si in s]
    l = sum(jnp.sum(pi, axis=0, keepdims=True) for pi in p)
    oT = sum(_dot(vt_refs[i][0, 0].astype(BF16), p[i].astype(BF16)) for i in range(nwb)) / l
    _store_gated_t(o_ref, oT, gate_ref[0, 0], Tq, 2)


def _win_prompt(qr, wk, wvt, gates_g, B, S, *, Tq):
    nq = S // Tq
    nwb = WINDOW // Tq + 1
    blk = lambda i: (lambda qi: jnp.maximum(qi - (nwb - 1) + i, 0))
    kspec = lambda i: pl.BlockSpec((1, Tq, HEAD_DIM), lambda b, g, qi, f=blk(i): (b, f(qi), g))
    vspec = lambda i: pl.BlockSpec((1, 1, HEAD_DIM, Tq), lambda b, g, qi, f=blk(i): (b, g, 0, f(qi)))
    return pl.pallas_call(
        functools.partial(_win_kernel, Tq, nwb),
        out_shape=jax.ShapeDtypeStruct((B, S, H_NSA * HEAD_DIM), F32),
        grid=(B, NSA_GROUPS, nq),
        in_specs=[pl.BlockSpec((1, Tq, NSA_R * HEAD_DIM), lambda b, g, qi: (b, qi, g))]
        + [kspec(i) for i in range(nwb)] + [vspec(i) for i in range(nwb)]
        + [pl.BlockSpec((1, 1, Tq, 16), lambda b, g, qi: (b, g, qi, 0))],
        out_specs=pl.BlockSpec((1, Tq, NSA_R * HEAD_DIM), lambda b, g, qi: (b, qi, g)),
        compiler_params=_cparams(("arbitrary",) * 3),
        name="nsa_win",
    )(qr, *([wk] * nwb), *([wvt] * nwb), gates_g)


def _hgrn_step_kernel(aq_ref, af_ref, ai_ref, ag_ref, lb_ref, gn_ref, s_ref, o_ref, so_ref):
    lb = lb_ref[0]
    gate = lb + (1.0 - lb) * _sigmoid(af_ref[0, 0])
    s_new = s_ref[0, 0] * gate + (1.0 - gate) * ai_ref[0, 0]
    so_ref[0, 0] = s_new
    o = jnp.sum(_silu(aq_ref[0, 0]) * s_new, axis=0, keepdims=True)
    o_ref[0, 0] = _head_rms(o, gn_ref[...]) * _silu(ag_ref[0, 0])


def _hgrn_step(hg, lb, gn, state):
    D, H = state.shape[:2]
    W = H * HEAD_DIM
    colv = lambda a: a.reshape(D, H, HEAD_DIM, 1)
    rowv = lambda a: a.reshape(D, H, 1, HEAD_DIM)
    cspec = pl.BlockSpec((1, 1, HEAD_DIM, 1), lambda d, h: (d, h, 0, 0))
    rspec = pl.BlockSpec((1, 1, 1, HEAD_DIM), lambda d, h: (d, h, 0, 0))
    sspec = pl.BlockSpec((1, 1, HEAD_DIM, HEAD_DIM), lambda d, h: (d, h, 0, 0))
    o, s_new = pl.pallas_call(
        _hgrn_step_kernel,
        out_shape=[jax.ShapeDtypeStruct((D, H, 1, HEAD_DIM), F32), jax.ShapeDtypeStruct(state.shape, F32)],
        grid=(D, H),
        in_specs=[cspec, cspec, rspec, rspec,
                  pl.BlockSpec((1, HEAD_DIM, 1), lambda d, h: (h, 0, 0)),
                  pl.BlockSpec((1, HEAD_DIM), lambda d, h: (0, 0)), sspec],
        out_specs=[rspec, sspec],
        compiler_params=_cparams(("arbitrary", "arbitrary")),
        name="hgrn_step",
    )(colv(hg[:, :W]), colv(hg[:, W:2 * W]), rowv(hg[:, 2 * W:3 * W]), rowv(hg[:, 3 * W:]),
      lb.reshape(H, HEAD_DIM, 1), gn, state)
    return o.reshape(D, W), s_new


def _fox_step_kernel(PG, pt_ref, q_ref, kn_ref, vn_ref, lfn_ref, *refs):
    k_refs, v_refs, lf_refs = refs[:PG], refs[PG:2 * PG], refs[2 * PG:3 * PG]
    o_ref, m_scr, l_scr, acc_scr, carry = refs[3 * PG:]
    W = H_FOX * PAGE
    q = q_ref[0] * (HEAD_DIM ** -0.5)

    @pl.when(pl.program_id(1) == 0)
    def _():
        m_scr[...] = jnp.sum(q * kn_ref[0], axis=-1, keepdims=True)
        l_scr[...] = jnp.ones_like(l_scr)
        acc_scr[...] = vn_ref[0]
        carry[...] = jnp.broadcast_to(lfn_ref[0], carry.shape)

    qb = q.astype(BF16)
    lane = lax.broadcasted_iota(jnp.int32, (H_FOX, W), 1)
    head = lax.broadcasted_iota(jnp.int32, (H_FOX, W), 0)
    plane = lax.broadcasted_iota(jnp.int32, (PG, W), 1)
    lf = jnp.concatenate([lf_refs[i][0] for i in range(PG)], axis=0)
    suf, tot = lf, lf
    step = H_FOX
    while step < W:
        suf = suf + jnp.where(plane + step < W, pltpu.roll(suf, W - step, 1), 0.0)
        tot = tot + pltpu.roll(tot, step, 1)
        step *= 2
    c = carry[...]
    s = []
    for i in range(PG):
        si = _dot_nt(qb, k_refs[i][0].astype(BF16)) + (c + (suf[i:i + 1] - lf[i:i + 1]))
        s.append(jnp.where((lane & (H_FOX - 1)) == head, si, NEG))
        c = c + tot[i:i + 1]
    carry[...] = c
    m_prev = m_scr[...]
    m_new = functools.reduce(jnp.maximum, [jnp.max(si, axis=-1, keepdims=True) for si in s] + [m_prev])
    alpha = jnp.exp(m_prev - m_new)
    p = [jnp.exp(si - m_new) for si in s]
    l_scr[...] = alpha * l_scr[...] + sum(jnp.sum(pi, axis=-1, keepdims=True) for pi in p)
    acc_scr[...] = alpha * acc_scr[...] + sum(_dot(p[i].astype(BF16), v_refs[i][0].astype(BF16)) for i in range(PG))
    m_scr[...] = m_new

    @pl.when(pl.program_id(1) == pl.num_programs(1) - 1)
    def _():
        o_ref[0] = _online_result(l_scr, acc_scr)


def _fox_step(page_table, q, k_new, v_new, lf_new, cache_k, cache_v, cache_lf, *, PG):
    D, NP = page_table.shape
    n_pool = cache_k.shape[0]
    W = H_FOX * PAGE
    k2 = cache_k.reshape(n_pool, W, HEAD_DIM)
    v2 = cache_v.reshape(n_pool, W, HEAD_DIM)
    lf2 = cache_lf.astype(F32).reshape(n_pool, 1, W)
    lfn = jnp.tile(lf_new, (1, PAGE)).reshape(D, 1, W)
    page = lambda i: (lambda d, j, pt: (pt[d, NP - 1 - (j * PG + i)], 0, 0))
    hspec = pl.BlockSpec((1, H_FOX, HEAD_DIM), lambda d, j, pt: (d, 0, 0))
    in_specs = [hspec, hspec, hspec, pl.BlockSpec((1, 1, W), lambda d, j, pt: (d, 0, 0))]
    in_specs += [pl.BlockSpec((1, W, HEAD_DIM), page(i)) for i in range(PG)]
    in_specs += [pl.BlockSpec((1, W, HEAD_DIM), page(i)) for i in range(PG)]
    in_specs += [pl.BlockSpec((1, 1, W), page(i)) for i in range(PG)]
    return pl.pallas_call(
        functools.partial(_fox_step_kernel, PG),
        out_shape=jax.ShapeDtypeStruct((D, H_FOX, HEAD_DIM), F32),
        grid_spec=pltpu.PrefetchScalarGridSpec(
            num_scalar_prefetch=1, grid=(D, NP // PG), in_specs=in_specs, out_specs=hspec,
            scratch_shapes=[pltpu.VMEM((H_FOX, 1), F32), pltpu.VMEM((H_FOX, 1), F32),
                            pltpu.VMEM((H_FOX, HEAD_DIM), F32), pltpu.VMEM((H_FOX, W), F32)]),
        compiler_params=_cparams(("arbitrary", "arbitrary")),
        name="fox_step",
    )(page_table, q, k_new, v_new, lfn, *([k2] * PG), *([v2] * PG), *([lf2] * PG))


def _cmp_pages_kernel(PG, pt_ref, *refs):
    pages = refs[:PG]
    pea_ref, peb_ref, w1a_ref, w1b_ref, p1_ref, p2_ref = refs[PG:]
    per_page = PAGE // CMP_STRIDE
    parts = []
    for g in range(NSA_GROUPS):
        for i in range(PG):
            parts.append(jnp.concatenate(
                [pages[i][0, pl.ds(r * NSA_GROUPS + g, per_page, stride=CMP_STRIDE * NSA_GROUPS), :]
                 for r in range(CMP_STRIDE)], axis=1))
    R = jnp.concatenate(parts, axis=0)
    shape = (NSA_GROUPS, PG * per_page, HEAD_DIM)
    p1_ref[0] = _dot((R + pea_ref[...]).astype(BF16), w1a_ref[...]).reshape(shape)
    p2_ref[0] = _dot((R + peb_ref[...]).astype(BF16), w1b_ref[...]).reshape(shape)


def _cmp_pages(page_table, cache, cw, *, PG):
    D, NP = page_table.shape
    n_pool = cache.shape[0]
    c2 = cache.reshape(n_pool, PAGE * NSA_GROUPS, HEAD_DIM)
    per_page = PAGE // CMP_STRIDE
    half = CMP_STRIDE * HEAD_DIM
    page = lambda i: (lambda d, j, pt: (pt[d, j * PG + i], 0, 0))
    const = lambda shape: pl.BlockSpec(shape, lambda d, j, pt: (0, 0))
    in_specs = [pl.BlockSpec((1, PAGE * NSA_GROUPS, HEAD_DIM), page(i)) for i in range(PG)]
    in_specs += [const((1, half)), const((1, half)), const((half, HEAD_DIM)), const((half, HEAD_DIM))]
    ospec = pl.BlockSpec((1, NSA_GROUPS, PG * per_page, HEAD_DIM), lambda d, j, pt: (d, 0, j, 0))
    shp = jax.ShapeDtypeStruct((D, NSA_GROUPS, NP * per_page, HEAD_DIM), F32)
    return pl.pallas_call(
        functools.partial(_cmp_pages_kernel, PG),
        out_shape=[shp, shp],
        grid_spec=pltpu.PrefetchScalarGridSpec(
            num_scalar_prefetch=1, grid=(D, NP // PG), in_specs=in_specs, out_specs=[ospec, ospec]),
        compiler_params=_cparams(("arbitrary", "arbitrary")),
        name="nsa_cmp_pages",
    )(page_table, *([c2] * PG), *cw[:4])


def _cmpsel_step_kernel(qpos, nsp, q_ref, p1k_ref, p2k_ref, p1v_ref, p2v_ref, w2k_ref, w2v_ref,
                        ov_ref, g_ref, o_ref, idx_ref):
    nc = p1k_ref.shape[2]

    def finish(p1_ref, p2_ref, w2_ref):
        pre = p1_ref[0, 0] + pltpu.roll(p2_ref[0, 0], nc - 1, 0)
        return _dot(_silu(pre).astype(BF16), w2_ref[...]).astype(BF16)

    kc = finish(p1k_ref, p2k_ref, w2k_ref)
    vc = finish(p1v_ref, p2v_ref, w2v_ref)
    q = (q_ref[0, 0] * (HEAD_DIM ** -0.5)).astype(BF16)
    s = _dot_nt(q, kc)
    cend = lax.broadcasted_iota(jnp.int32, s.shape, 1) * CMP_STRIDE + (CMP_LEN - 1)
    mask = cend <= qpos
    sm = jnp.where(mask, s, NEG)
    e = jnp.where(mask, jnp.exp(sm - jnp.max(sm, axis=-1, keepdims=True)), 0.0)
    den = jnp.sum(e, axis=-1, keepdims=True)
    p = e / jnp.where(den > 0.0, den, 1.0)
    o_ref[0, 0] = _dot(p.astype(BF16), vc) * g_ref[0, 0][:, 0:1]

    psum = jnp.broadcast_to(jnp.sum(p[0:NSA_R], axis=0, keepdims=True), (8, nc))
    ov = ov_ref[...]
    sc_row = sum(_dot(t, ov) for t in _split3(psum))[0:1]
    n_lane = lax.broadcasted_iota(jnp.int32, (1, nsp), 1)
    cur = qpos // SEL_LEN
    forced = (n_lane == 0) | (n_lane == cur) | (n_lane == cur - 1)
    sc_row = jnp.where(n_lane * SEL_LEN <= qpos, jnp.where(forced, -NEG, sc_row), NEG)
    mi = lax.broadcasted_iota(jnp.int32, (nsp, nsp), 0)
    ni = lax.broadcasted_iota(jnp.int32, (nsp, nsp), 1)
    sc_col = jnp.sum(jnp.where(mi == ni, sc_row, 0.0), axis=-1, keepdims=True)
    beats = jnp.where(mi < ni, jnp.where(sc_col >= sc_row, 1.0, 0.0), jnp.where(sc_col > sc_row, 1.0, 0.0))
    rank = jnp.sum(beats, axis=0, keepdims=True)
    lane = lax.broadcasted_iota(jnp.int32, (1, LANES), 1)
    out = jnp.zeros((1, LANES), F32)
    for k in range(TOPN):
        nk = jnp.sum(jnp.where(rank == float(k), n_lane.astype(F32), 0.0), axis=-1, keepdims=True)
        out = jnp.where(lane == k, nk, out)
    idx_ref[0, 0] = jnp.broadcast_to(out, (8, LANES)).astype(jnp.int32)


def _cmpsel_step(q16, p1k, p2k, p1v, p2v, cwk, cwv, gate_rows, qpos):
    D, G, nc = p1k.shape[:3]
    n_sel = -(-(qpos + 1) // SEL_LEN)
    nsp = -(-n_sel // LANES) * LANES
    ov = jnp.asarray(np.pad(_overlap_t(nc, n_sel).T, ((0, 0), (0, nsp - n_sel))), BF16)
    big = pl.BlockSpec((1, 1, nc, HEAD_DIM), lambda d, g: (d, g, 0, 0))
    qspec = pl.BlockSpec((1, 1, 16, HEAD_DIM), lambda d, g: (d, g, 0, 0))
    w2spec = pl.BlockSpec((HEAD_DIM, HEAD_DIM), lambda d, g: (0, 0))
    return pl.pallas_call(
        functools.partial(_cmpsel_step_kernel, qpos, nsp),
        out_shape=[jax.ShapeDtypeStruct((D, G, 16, HEAD_DIM), F32),
                   jax.ShapeDtypeStruct((D, G, 8, LANES), jnp.int32)],
        grid=(D, G),
        in_specs=[qspec, big, big, big, big, w2spec, w2spec,
                  pl.BlockSpec((nc, nsp), lambda d, g: (0, 0)),
                  pl.BlockSpec((1, 1, 16, 3), lambda d, g: (d, g, 0, 0))],
        out_specs=[qspec, pl.BlockSpec((1, 1, 8, LANES), lambda d, g: (d, g, 0, 0))],
        compiler_params=_cparams(("arbitrary", "arbitrary")),
        name="nsa_cmpsel_step",
    )(q16, p1k, p2k, p1v, p2v, cwk[4], cwv[4], ov, gate_rows)


def _sel_step_kernel(n_past, pt_ref, idx_ref, q_ref, kn_ref, vn_ref, *refs):
    k_refs, v_refs = refs[:TOPN], refs[TOPN:2 * TOPN]
    g_ref, o_ref = refs[2 * TOPN:]
    d, g = pl.program_id(0), pl.program_id(1)
    q = q_ref[0, 0] * (HEAD_DIM ** -0.5)
    qb = q.astype(BF16)
    rows = pl.ds(g, SEL_LEN, stride=NSA_GROUPS)
    s_new = jnp.sum(q * kn_ref[0, 0], axis=-1, keepdims=True)
    s = [_dot_nt(qb, k_refs[k][rows, :].astype(BF16)) + jnp.where(idx_ref[d, g, k] < n_past, 0.0, NEG)
         for k in range(TOPN)]
    m = functools.reduce(jnp.maximum, [jnp.max(sk, axis=-1, keepdims=True) for sk in s] + [s_new])
    p = [jnp.exp(sk - m) for sk in s]
    p_new = jnp.exp(s_new - m)
    l = sum(jnp.sum(pk, axis=-1, keepdims=True) for pk in p) + p_new
    o = sum(_dot(p[k].astype(BF16), v_refs[k][rows, :].astype(BF16)) for k in range(TOPN)) + p_new * vn_ref[0, 0]
    o_ref[0, 0] = o / l * g_ref[0, 0][:, 1:2]


def _sel_step(page_table, idx, q16, k_new, v_new, cache_k, cache_v, gate_rows):
    D, NP = page_table.shape
    G = NSA_GROUPS
    n_pool = cache_k.shape[0]
    per_page = PAGE // SEL_LEN
    n_past = NP * per_page
    blk_rows = SEL_LEN * G
    k2 = cache_k.reshape(n_pool * PAGE * G, HEAD_DIM)
    v2 = cache_v.reshape(n_pool * PAGE * G, HEAD_DIM)

    def blk(k):
        def index(d, g, pt, ix):
            n = jnp.minimum(ix[d, g, k], n_past - 1)
            return (pt[d, n // per_page] * per_page + n % per_page, 0)
        return pl.BlockSpec((blk_rows, HEAD_DIM), index)

    qspec = pl.BlockSpec((1, 1, 16, HEAD_DIM), lambda d, g, pt, ix: (d, g, 0, 0))
    nspec = pl.BlockSpec((1, 1, 1, HEAD_DIM), lambda d, g, pt, ix: (d, g, 0, 0))
    return pl.pallas_call(
        functools.partial(_sel_step_kernel, n_past),
        out_shape=jax.ShapeDtypeStruct((D, G, 16, HEAD_DIM), F32),
        grid_spec=pltpu.PrefetchScalarGridSpec(
            num_scalar_prefetch=2, grid=(D, G),
            in_specs=[qspec, nspec, nspec] + [blk(k) for k in range(TOPN)] * 2
            + [pl.BlockSpec((1, 1, 16, 3), lambda d, g, pt, ix: (d, g, 0, 0))],
            out_specs=qspec),
        compiler_params=_cparams(("arbitrary",) * 2),
        name="nsa_sel_step",
    )(page_table, idx, q16, k_new, v_new, *([k2] * TOPN), *([v2] * TOPN), gate_rows)


def _win_step_kernel(q_ref, k_ref, v_ref, g_ref, o_ref):
    q = (q_ref[0, 0] * (HEAD_DIM ** -0.5)).astype(BF16)
    s = _dot_nt(q, k_ref[0].astype(BF16))
    e = jnp.exp(s - jnp.max(s, axis=-1, keepdims=True))
    p = e / jnp.sum(e, axis=-1, keepdims=True)
    o_ref[0, 0] = _dot(p.astype(BF16), v_ref[0].astype(BF16)) * g_ref[0, 0][:, 2:3]


def _win_step(q16, kw, vw, gate_rows):
    D, G = q16.shape[:2]
    L = kw.shape[1]
    qspec = pl.BlockSpec((1, 1, 16, HEAD_DIM), lambda d, g: (d, g, 0, 0))
    kspec = pl.BlockSpec((1, L, HEAD_DIM), lambda d, g: (d, 0, g))
    return pl.pallas_call(
        _win_step_kernel,
        out_shape=jax.ShapeDtypeStruct((D, G, 16, HEAD_DIM), F32),
        grid=(D, G),
        in_specs=[qspec, kspec, kspec, pl.BlockSpec((1, 1, 16, 3), lambda d, g: (d, g, 0, 0))],
        out_specs=qspec,
        compiler_params=_cparams(("arbitrary", "arbitrary")),
        name="nsa_win_step",
    )(q16, kw, vw, gate_rows)


TN = 512
TM = 512
TM_STEP = 8


def _pad_cols(a, n):
    return jnp.pad(a, ((0, 0), (0, n - a.shape[1])))


def _even_params(e_norm_mix, e_w_in, lb, out_norm, f_bias, q_norm, k_norm, e_w_out, e_norm_ffn, wg, wu, wd):
    d = e_w_in.shape[0]
    aw, bw = H_HGRN * HEAD_DIM, H_FOX * HEAD_DIM
    n_real = 4 * aw + 3 * bw + H_FOX
    n_pad = -(-n_real // TN) * TN
    ones = lambda n: jnp.ones((n,), F32)
    gain = jnp.concatenate([ones(4 * aw), jnp.tile(q_norm, H_FOX), jnp.tile(k_norm, H_FOX), ones(n_pad - 4 * aw - 2 * bw)])
    bias = jnp.concatenate([jnp.zeros((4 * aw + 3 * bw,), F32), f_bias.astype(F32),
                            jnp.zeros((n_pad - n_real,), F32)])
    nt = lambda w: w // TN
    segs, t0 = [], 0
    for width, mode in ((4 * aw, 'raw'), (bw, 'hnorm'), (bw, 'hnorm'), (bw, 'raw'), (TN, 'logsig')):
        segs.append((t0, nt(width), mode))
        t0 += nt(width)
    return dict(norm=e_norm_mix.reshape(1, d), w_in=_pad_cols(e_w_in.astype(BF16), n_pad), gain=gain.reshape(1, -1),
                bias=bias.reshape(1, -1), segs=segs, lb=lb.reshape(1, aw), out_norm=out_norm.reshape(1, HEAD_DIM),
                w_out=e_w_out.astype(BF16), norm_ffn=e_norm_ffn.reshape(1, d),
                wg=wg.astype(BF16), wu=wu.astype(BF16), wd=wd.astype(BF16))


def _even_proj(x2, p, tm):
    zeros = jnp.zeros((tm, HEAD_DIM), F32)
    return _proj(x2, p['norm'], p['w_in'], p['gain'], p['bias'], zeros, zeros, p['segs'], tm=tm, tn=TN,
                 name="even_proj")


def _even_finish(x2, oa, of, p, tm):
    x2 = _outproj([jnp.concatenate([oa, of], axis=1)], p['w_out'], x2, tm=tm, tn=TN, name="even_out")
    return _ffn(x2, p['norm_ffn'], p['wg'], p['wu'], p['wd'], tm=tm, tf=TN, name="ffn")


def _odd_params(o_norm_mix, o_w_in, q_norm, ck_norm, sk_norm, wk_norm, pe_k, w1_k, w2_k, pe_v, w1_v, w2_v,
                o_w_out, o_norm_ffn, router_w, router_b, mg, mu, md):
    d = o_w_in.shape[0]
    qw, kvw = H_NSA * HEAD_DIM, NSA_GROUPS * HEAD_DIM
    n_real = qw + 6 * kvw + 3 * H_NSA
    n_pad = -(-n_real // TN) * TN
    ones = jnp.ones((kvw,), F32)
    gain = jnp.concatenate([jnp.tile(q_norm, H_NSA), jnp.tile(ck_norm, NSA_GROUPS), ones,
                            jnp.tile(sk_norm, NSA_GROUPS), ones, jnp.tile(wk_norm, NSA_GROUPS), ones,
                            jnp.ones((n_pad - qw - 6 * kvw,), F32)])
    segs, t0 = [], 0
    for width, mode in ((qw, 'hnorm_both'), (kvw, 'hnorm'), (kvw, 'raw'), (kvw, 'hnorm_rope'), (kvw, 'raw'),
                        (kvw, 'hnorm_rope'), (kvw, 'raw'), (TN, 'sigmoid')):
        segs.append((t0, width // TN, mode))
        t0 += width // TN
    return dict(norm=o_norm_mix.reshape(1, d), w_in=_pad_cols(o_w_in.astype(BF16), n_pad), gain=gain.reshape(1, -1),
                bias=jnp.zeros((1, n_pad), F32), segs=segs,
                cwk=_cmp_weights(pe_k, w1_k, w2_k), cwv=_cmp_weights(pe_v, w1_v, w2_v),
                w_out=o_w_out.astype(BF16), norm_ffn=o_norm_ffn.reshape(1, d),
                rw=_pad_cols(router_w.astype(BF16), LANES), rb=_pad_cols(router_b.reshape(1, -1).astype(F32), LANES),
                mg=mg.astype(BF16), mu=mu.astype(BF16), md=md.astype(BF16))


def _odd_proj(x2, p, pos, tm):
    cosf, sinf = _rope_tables(pos)
    return _proj(x2, p['norm'], p['w_in'], p['gain'], p['bias'], cosf, sinf, p['segs'], tm=tm, tn=TN,
                 name="odd_proj")


def _odd_finish(x2, branches, p, tm):
    x2 = _outproj(branches, p['w_out'], x2, tm=tm, tn=TN, name="odd_out")
    args = (x2, p['norm_ffn'], p['rw'], p['rb'], p['mg'], p['mu'], p['md'])
    if x2.shape[0] >= 2 * TM_E:
        return _moe_routed(*args, tm=tm)
    return _moe(*args, tm=tm, tf=256, name="moe")


def kernel(x_prompt, x_sample, state_hgrn, cache_fox_k, cache_fox_v, cache_fox_logf, cache_nsa_cmp_k, cache_nsa_cmp_v, cache_nsa_sel_k, cache_nsa_sel_v, cache_nsa_win_k, cache_nsa_win_v, page_table, e_norm_mix, e_w_in, hgrn_lb_logits, hgrn_out_norm, fox_f_bias, fox_q_norm, fox_k_norm, e_w_out, e_norm_ffn, ffn_w_gate, ffn_w_up, ffn_w_down, o_norm_mix, o_w_in, nsa_q_norm, nsa_cmp_k_norm, nsa_sel_k_norm, nsa_win_k_norm, cmp_pe_k, cmp_w1_k, cmp_w2_k, cmp_pe_v, cmp_w1_v, cmp_w2_v, o_w_out, o_norm_ffn, router_w, router_b, moe_w_gate, moe_w_up, moe_w_down):
    B, S, d = x_prompt.shape
    D, T, _ = x_sample.shape
    NP = page_table.shape[1]
    past = NP * PAGE
    w_buf = cache_nsa_win_k.shape[2]
    assert T == 1 and w_buf == WINDOW and S >= WINDOW and cache_fox_k.shape[2] == PAGE
    G = NSA_GROUPS
    lbs = jnp.cumsum(jax.nn.softmax(hgrn_lb_logits.astype(F32), axis=0), axis=0)
    xp, xd = x_prompt.reshape(B * S, d), x_sample.reshape(D, d)
    tm_p = min(TM, S)

    li = 0
    p = _even_params(e_norm_mix[li], e_w_in[li], lbs[li], hgrn_out_norm[li], fox_f_bias[li], fox_q_norm[li],
                     fox_k_norm[li], e_w_out[li], e_norm_ffn[li], ffn_w_gate[li], ffn_w_up[li], ffn_w_down[li])
    hg, fq, fk, fv, fl = _even_proj(xp, p, tm_p)
    aug_q, aug_k = _seq_cumsum(fl, B, S, tc=tm_p)
    oa, st_p = _hgrn_prompt(hg, p['lb'], p['out_norm'], B, S, T=min(256, S), c=32)
    fvt = fv.reshape(B, S, H_FOX, HEAD_DIM).transpose(0, 2, 3, 1)
    of = _fox_prompt(fq, fk, fvt, aug_q, aug_k, B, S, tq=min(1024, S), tk=tm_p)
    xp = _even_finish(xp, oa, of, p, tm_p)
    fox_p = (fk.reshape(1, B, S, H_FOX, HEAD_DIM), fv.reshape(1, B, S, H_FOX, HEAD_DIM),
             fl[:, :H_FOX].reshape(1, B, S, H_FOX))

    hg, fq, fk, fv, fl = _even_proj(xd, p, TM_STEP)
    oa, st_d = _hgrn_step(hg, p['lb'], p['out_norm'], state_hgrn[li].astype(F32))
    heads = lambda a: a.reshape(D, H_FOX, HEAD_DIM)
    of = _fox_step(page_table, heads(fq), heads(fk), heads(fv), fl[:, :H_FOX],
                   cache_fox_k[li], cache_fox_v[li], cache_fox_logf[li], PG=8)
    xd = _even_finish(xd, oa, of.reshape(D, H_FOX * HEAD_DIM), p, TM_STEP)
    fox_d = (fk.reshape(1, D, 1, H_FOX, HEAD_DIM), fv.reshape(1, D, 1, H_FOX, HEAD_DIM),
             fl[:, :H_FOX].reshape(1, D, 1, H_FOX))

    p = _odd_params(o_norm_mix[li], o_w_in[li], nsa_q_norm[li], nsa_cmp_k_norm[li], nsa_sel_k_norm[li],
                    nsa_win_k_norm[li], cmp_pe_k[li], cmp_w1_k[li], cmp_w2_k[li], cmp_pe_v[li], cmp_w1_v[li],
                    cmp_w2_v[li], o_w_out[li], o_norm_ffn[li], router_w[li], router_b[li],
                    moe_w_gate[li], moe_w_up[li], moe_w_down[li])
    qn, qr, ck, cv, sk, sv, wk, wv, gt = _odd_proj(xp, p, jnp.arange(S), tm_p)
    seq = lambda a: a.reshape(B, S, -1)
    gates_g = jnp.pad(gt[:, :3 * H_NSA].reshape(B, S, G, 3 * NSA_R).transpose(0, 2, 1, 3),
                      ((0, 0), (0, 0), (0, 0), (0, 16 - 3 * NSA_R)))
    kc = _compress_prompt(seq(ck), p['cwk'], B, S)
    vc = _compress_prompt(seq(cv), p['cwv'], B, S)
    tq = min(256, S)
    o_cmp, selneg = _cmpsel_prompt(seq(qn), kc, vc, gates_g, B, S, Tq=tq)
    keys_last = lambda a: a.reshape(B, S, G, HEAD_DIM).transpose(0, 2, 3, 1)
    o_sel = _sel_prompt(seq(qr), selneg, seq(sk), keys_last(sv), gates_g, B, S, Tq=tq, tk=min(512, S))
    o_win = _win_prompt(seq(qr), seq(wk), keys_last(wv), gates_g, B, S, Tq=tq)
    flat = lambda a: a.reshape(B * S, -1)
    xp = _odd_finish(xp, [flat(o_cmp), flat(o_sel), flat(o_win)], p, tm_p)
    kv = lambda a: a.reshape(1, B, S, G, HEAD_DIM)
    nsa_p = (kv(ck), kv(cv), kv(sk), kv(sv), kv(wk)[:, :, S - WINDOW:], kv(wv)[:, :, S - WINDOW:])

    qn, qr, ck, cv, sk, sv, wk, wv, gt = _odd_proj(xd, p, jnp.full((TM_STEP,), past), TM_STEP)
    rows16 = lambda a, w: jnp.pad(a.reshape(D, G, NSA_R, w), ((0, 0), (0, 0), (0, 16 - NSA_R), (0, 0)))
    gate_rows = rows16(gt[:, :3 * H_NSA], 3)
    q16n, q16r = rows16(qn, HEAD_DIM), rows16(qr, HEAD_DIM)
    p1k, p2k = _cmp_pages(page_table, cache_nsa_cmp_k[li], p['cwk'], PG=16)
    p1v, p2v = _cmp_pages(page_table, cache_nsa_cmp_v[li], p['cwv'], PG=16)
    o_cmp, idx = _cmpsel_step(q16n, p1k, p2k, p1v, p2v, p['cwk'], p['cwv'], gate_rows, past)
    new = lambda a: a.reshape(D, G, 1, HEAD_DIM)
    o_sel = _sel_step(page_table, idx[:, :, 0, :TOPN], q16r, new(sk), new(sv),
                      cache_nsa_sel_k[li], cache_nsa_sel_v[li], gate_rows)
    kvd = lambda a: a.reshape(D, 1, G, HEAD_DIM)
    win_k = jnp.concatenate([cache_nsa_win_k[li], kvd(wk)], axis=1)[:, -w_buf:]
    win_v = jnp.concatenate([cache_nsa_win_v[li], kvd(wv)], axis=1)[:, -w_buf:]
    o_win = _win_step(q16r, win_k.reshape(D, w_buf, G * HEAD_DIM), win_v.reshape(D, w_buf, G * HEAD_DIM), gate_rows)
    unrow = lambda a: a[:, :, :NSA_R].reshape(D, H_NSA * HEAD_DIM)
    xd = _odd_finish(xd, [unrow(o_cmp), unrow(o_sel), unrow(o_win)], p, TM_STEP)
    kv1 = lambda a: a.reshape(1, D, 1, G, HEAD_DIM)
    nsa_d = (kv1(ck), kv1(cv), kv1(sk), kv1(sv), win_k[None], win_v[None])

    return (xp.reshape(B, S, d), xd.reshape(D, T, d), st_p[None].astype(state_hgrn.dtype),
            st_d[None].astype(state_hgrn.dtype), *fox_p, *fox_d, *nsa_p, *nsa_d)
```

```python
import functools

import numpy as np
import jax
import jax.numpy as jnp
from jax import lax
from jax.experimental import pallas as pl
from jax.experimental.pallas import tpu as pltpu

F32 = jnp.float32
BF16 = jnp.bfloat16

LANES = 128
HEAD_DIM = 128
EPS = 1e-6
ROPE_THETA = 10000.0
LOG2E = 1.4426950408889634
QSCALE = HEAD_DIM ** -0.5 * LOG2E
NEG = -1e30
M_INIT = -1e20
VMEM_LIMIT = 56 * 1024 * 1024

H_HGRN = 8
H_FOX = 8
H_NSA = 16
NSA_GROUPS = 4
NSA_R = H_NSA // NSA_GROUPS
CMP_STRIDE = 16
CMP_LEN = 32
SEL_LEN = 64
SEL_SHIFT = 6
TOPN = 16
WINDOW = 512
N_EXPERTS = 8
PAGE = 128


def _cparams(sem):
    return pltpu.CompilerParams(dimension_semantics=sem, vmem_limit_bytes=VMEM_LIMIT)


def _dot(a, b):
    return jnp.dot(a, b, preferred_element_type=F32)


def _dot_nt(a, b):
    return lax.dot_general(a, b, (((1,), (1,)), ((), ())), preferred_element_type=F32)


def _dot_tn(a, b):
    return lax.dot_general(a, b, (((0,), (0,)), ((), ())), preferred_element_type=F32)


def _split3(x):
    hi = x.astype(BF16)
    r1 = x - hi.astype(F32)
    mid = r1.astype(BF16)
    lo = (r1 - mid.astype(F32)).astype(BF16)
    return hi, mid, lo


def _dot3(w_bf16, x):
    hi, mid, lo = _split3(x)
    return _dot(w_bf16, hi) + _dot(w_bf16, mid) + _dot(w_bf16, lo)


def _sigmoid(x):
    return 1.0 / (1.0 + jnp.exp(-x))


def _silu(x):
    return x * _sigmoid(x)


def _log_sigmoid(x):
    return jnp.minimum(x, 0.0) - jnp.log(1.0 + jnp.exp(-jnp.abs(x)))


def _tril_bf16(n):
    r = lax.broadcasted_iota(jnp.int32, (n, n), 0)
    c = lax.broadcasted_iota(jnp.int32, (n, n), 1)
    return jnp.where(c <= r, 1.0, 0.0).astype(BF16)


def _head_rms(a, gain):
    ms = jnp.mean(a * a, axis=-1, keepdims=True)
    return a * lax.rsqrt(ms + EPS) * gain


def _rope(y, cosf, sinf):
    return y * cosf + pltpu.roll(y, HEAD_DIM // 2, 1) * sinf


def _rope_tables(pos):
    half = HEAD_DIM // 2
    inv = ROPE_THETA ** (-jnp.arange(half, dtype=F32) / half)
    ang = pos.astype(F32)[:, None] * inv[None, :]
    cos, sin = jnp.cos(ang), jnp.sin(ang)
    return jnp.concatenate([cos, cos], axis=-1), jnp.concatenate([-sin, sin], axis=-1)


def _n_outs(mode):
    return 2 if mode == 'hnorm_both' else 1


def _seg_block(width, tn):
    return min(width, tn)


def _proj_kernel(segs, tail_mode, tn, *refs):
    n_out = sum(_n_outs(s[2]) for s in segs)
    x_ref, g_ref, w_ref, wt_ref, gain_ref, bias_ref, cos_ref, sin_ref = refs[:8]
    out_refs = refs[8:8 + n_out]
    tail_ref = refs[8 + n_out]
    xn_ref = refs[9 + n_out]
    j = pl.program_id(1)

    @pl.when(j == 0)
    def _():
        xf = x_ref[...]
        ms = jnp.mean(xf * xf, axis=-1, keepdims=True)
        xn_ref[...] = (xf * lax.rsqrt(ms + EPS) * g_ref[...]).astype(BF16)

    acc = _dot(xn_ref[...], w_ref[...])

    def emit(mode, outs, a, b, dst):
        for h in range((b - a) // HEAD_DIM):
            src = slice(a + h * HEAD_DIM, a + (h + 1) * HEAD_DIM)
            to = slice(dst + h * HEAD_DIM, dst + (h + 1) * HEAD_DIM)
            if mode == 'raw':
                outs[0][:, to] = acc[:, src]
                continue
            y = _head_rms(acc[:, src], gain_ref[:, src])
            if mode in ('hnorm', 'hnorm_both'):
                outs[0][:, to] = y
            if mode in ('hnorm_rope', 'hnorm_both'):
                outs[-1][:, to] = _rope(y, cos_ref[...], sin_ref[...])

    ntiles = pl.num_programs(1)
    for tile in range((segs[-1][0] + segs[-1][1]) // tn):
        @pl.when(j == tile)
        def _(tile=tile):
            oi = 0
            for (start, width, mode) in segs:
                outs = out_refs[oi:oi + _n_outs(mode)]
                oi += _n_outs(mode)
                a, b = max(start, tile * tn), min(start + width, (tile + 1) * tn)
                if a < b:
                    emit(mode, outs, a - tile * tn, b - tile * tn, (a - start) % _seg_block(width, tn))

    @pl.when(j == ntiles - 1)
    def _():
        t = _dot(xn_ref[...], wt_ref[...]) + bias_ref[...]
        tail_ref[...] = _sigmoid(t) if tail_mode == 'sigmoid' else _log_sigmoid(t)


def _proj(x, norm_g, w, w_tail, gain_all, bias_tail, cosf, sinf, segs, tail_mode, *, tm, tn, name):
    M, K = x.shape
    n_main = segs[-1][0] + segs[-1][1]
    assert M % tm == 0 and n_main % tn == 0
    nseq = cosf.shape[0] // tm
    out_shape, out_specs = [], []
    for (start, width, mode) in segs:
        bw = _seg_block(width, tn)
        assert width % bw == 0 and start % bw == 0
        for _ in range(_n_outs(mode)):
            out_shape.append(jax.ShapeDtypeStruct((M, width), F32))
            out_specs.append(pl.BlockSpec(
                (tm, bw), lambda i, j, s=start, bw=bw, n=width // bw: (i, jnp.clip((j * tn - s) // bw, 0, n - 1))))
    out_shape.append(jax.ShapeDtypeStruct((M, LANES), F32))
    out_specs.append(pl.BlockSpec((tm, LANES), lambda i, j: (i, 0)))
    return pl.pallas_call(
        functools.partial(_proj_kernel, tuple(segs), tail_mode, tn),
        out_shape=out_shape,
        grid=(M // tm, n_main // tn),
        in_specs=[
            pl.BlockSpec((tm, K), lambda i, j: (i, 0)),
            pl.BlockSpec((1, K), lambda i, j: (0, 0)),
            pl.BlockSpec((K, tn), lambda i, j: (0, j)),
            pl.BlockSpec((K, LANES), lambda i, j: (0, 0)),
            pl.BlockSpec((1, tn), lambda i, j: (0, j)),
            pl.BlockSpec((1, LANES), lambda i, j: (0, 0)),
            pl.BlockSpec((tm, HEAD_DIM), lambda i, j: (i % nseq, 0)),
            pl.BlockSpec((tm, HEAD_DIM), lambda i, j: (i % nseq, 0)),
        ],
        out_specs=out_specs,
        scratch_shapes=[pltpu.VMEM((tm, K), BF16)],
        compiler_params=_cparams(("arbitrary", "arbitrary")),
        name=name,
    )(x, norm_g, w, w_tail, gain_all, bias_tail, cosf, sinf)


def _outproj_kernel(n_lhs, *refs):
    lhs = refs[:n_lhs]
    w_ref, res_ref, o_ref, xs_ref = refs[n_lhs:n_lhs + 4]

    @pl.when(pl.program_id(1) == 0)
    def _():
        x = lhs[0][...]
        for r in lhs[1:]:
            x = x + r[...]
        xs_ref[...] = x.astype(BF16)

    o_ref[...] = res_ref[...] + _dot(xs_ref[...], w_ref[...])


def _outproj(lhs_list, w, res, *, tm, tn, name):
    M, K = lhs_list[0].shape
    N = w.shape[1]
    n = len(lhs_list)
    return pl.pallas_call(
        functools.partial(_outproj_kernel, n),
        out_shape=jax.ShapeDtypeStruct((M, N), F32),
        grid=(M // tm, N // tn),
        in_specs=[pl.BlockSpec((tm, K), lambda i, j: (i, 0)) for _ in range(n)] + [
            pl.BlockSpec((K, tn), lambda i, j: (0, j)),
            pl.BlockSpec((tm, tn), lambda i, j: (i, j)),
        ],
        out_specs=pl.BlockSpec((tm, tn), lambda i, j: (i, j)),
        scratch_shapes=[pltpu.VMEM((tm, K), BF16)],
        compiler_params=_cparams(("arbitrary", "arbitrary")),
        name=name,
    )(*lhs_list, w, res)


def _ffn_kernel(emit, x_ref, g_ref, wg_ref, wu_ref, wd_ref, o_ref, *rest):
    xn_ref = rest[-1]
    f = pl.program_id(1)

    @pl.when(f == 0)
    def _():
        xf = x_ref[...]
        ms = jnp.mean(xf * xf, axis=-1, keepdims=True)
        xn_ref[...] = (xf * lax.rsqrt(ms + EPS) * g_ref[...]).astype(BF16)
        o_ref[...] = xf

    xn = xn_ref[...]
    wg, wu, wd = wg_ref[...].astype(BF16), wu_ref[...].astype(BF16), wd_ref[...].astype(BF16)
    if emit:
        rest[0][...], rest[1][...], rest[2][...] = wg, wu, wd
    h = _silu(_dot(xn, wg)) * _dot(xn, wu)
    o_ref[...] += _dot(h.astype(BF16), wd)


def _ffn(x, norm_g, wg, wu, wd, *, tm, tf, emit=False, name):
    M, K = x.shape
    F = wg.shape[1]
    assert not emit or M == tm
    w_specs = [pl.BlockSpec((K, tf), lambda i, f: (0, f)), pl.BlockSpec((K, tf), lambda i, f: (0, f)),
               pl.BlockSpec((tf, K), lambda i, f: (f, 0))]
    out_shape = [jax.ShapeDtypeStruct((M, K), F32)]
    out_specs = [pl.BlockSpec((tm, K), lambda i, f: (i, 0))]
    if emit:
        out_shape += [jax.ShapeDtypeStruct(w.shape, BF16) for w in (wg, wu, wd)]
        out_specs += w_specs
    out = pl.pallas_call(
        functools.partial(_ffn_kernel, emit),
        out_shape=out_shape,
        grid=(M // tm, F // tf),
        in_specs=[pl.BlockSpec((tm, K), lambda i, f: (i, 0)), pl.BlockSpec((1, K), lambda i, f: (0, 0))] + w_specs,
        out_specs=out_specs,
        scratch_shapes=[pltpu.VMEM((tm, K), BF16)],
        compiler_params=_cparams(("arbitrary", "arbitrary")),
        name=name,
    )(x, norm_g, wg, wu, wd)
    return (out[0], tuple(out[1:])) if emit else out[0]


def _moe_kernel(emit, x_ref, g_ref, rw_ref, rb_ref, wg_ref, wu_ref, wd_ref, o_ref, *rest):
    xn_ref, gate_ref = rest[-2:]
    e = pl.program_id(1)
    f = pl.program_id(2)

    @pl.when((e == 0) & (f == 0))
    def _():
        xf = x_ref[...]
        ms = jnp.mean(xf * xf, axis=-1, keepdims=True)
        xn = (xf * lax.rsqrt(ms + EPS) * g_ref[...]).astype(BF16)
        xn_ref[...] = xn
        o_ref[...] = xf
        lane = lax.broadcasted_iota(jnp.int32, (xf.shape[0], LANES), 1)
        logits = jnp.where(lane < N_EXPERTS, _dot(xn, rw_ref[...]) + rb_ref[...], NEG)
        m1 = jnp.max(logits, axis=-1, keepdims=True)
        i1 = jnp.min(jnp.where(logits == m1, lane, LANES), axis=-1, keepdims=True)
        l2 = jnp.where(lane == i1, NEG, logits)
        m2 = jnp.max(l2, axis=-1, keepdims=True)
        i2 = jnp.min(jnp.where(l2 == m2, lane, LANES), axis=-1, keepdims=True)
        e2 = jnp.exp(m2 - m1)
        w1 = 1.0 / (1.0 + e2)
        gate_ref[...] = jnp.where(lane == i1, w1, 0.0) + jnp.where(lane == i2, e2 * w1, 0.0)

    xn = xn_ref[...]
    wg, wu, wd = wg_ref[0].astype(BF16), wu_ref[0].astype(BF16), wd_ref[0].astype(BF16)
    if emit:
        rest[0][0], rest[1][0], rest[2][0] = wg, wu, wd
    h = _silu(_dot(xn, wg)) * _dot(xn, wu)
    lane = lax.broadcasted_iota(jnp.int32, gate_ref.shape, 1)
    ge = jnp.sum(jnp.where(lane == e, gate_ref[...], 0.0), axis=-1, keepdims=True)
    o_ref[...] += ge * _dot(h.astype(BF16), wd)


def _moe(x, norm_g, rw, rb, wg, wu, wd, *, tm, tf, emit=False, name):
    M, K = x.shape
    E, _, F = wg.shape
    assert not emit or M == tm
    w_specs = [pl.BlockSpec((1, K, tf), lambda i, e, f: (e, 0, f)), pl.BlockSpec((1, K, tf), lambda i, e, f: (e, 0, f)),
               pl.BlockSpec((1, tf, K), lambda i, e, f: (e, f, 0))]
    out_shape = [jax.ShapeDtypeStruct((M, K), F32)]
    out_specs = [pl.BlockSpec((tm, K), lambda i, e, f: (i, 0))]
    if emit:
        out_shape += [jax.ShapeDtypeStruct(w.shape, BF16) for w in (wg, wu, wd)]
        out_specs += w_specs
    out = pl.pallas_call(
        functools.partial(_moe_kernel, emit),
        out_shape=out_shape,
        grid=(M // tm, E, F // tf),
        in_specs=[
            pl.BlockSpec((tm, K), lambda i, e, f: (i, 0)),
            pl.BlockSpec((1, K), lambda i, e, f: (0, 0)),
            pl.BlockSpec((K, LANES), lambda i, e, f: (0, 0)),
            pl.BlockSpec((1, LANES), lambda i, e, f: (0, 0)),
        ] + w_specs,
        out_specs=out_specs,
        scratch_shapes=[pltpu.VMEM((tm, K), BF16), pltpu.VMEM((tm, LANES), F32)],
        compiler_params=_cparams(("arbitrary", "arbitrary", "arbitrary")),
        name=name,
    )(x, norm_g, rw, rb, wg, wu, wd)
    return (out[0], tuple(out[1:])) if emit else out[0]


TM_E = 512
EXPERT_ISSUE_STEPS = 8


def _top2(logits, lane):
    m1 = jnp.max(logits, axis=-1, keepdims=True)
    i1 = jnp.min(jnp.where(logits == m1, lane, LANES), axis=-1, keepdims=True)
    l2 = jnp.where(lane == i1, NEG, logits)
    m2 = jnp.max(l2, axis=-1, keepdims=True)
    i2 = jnp.min(jnp.where(l2 == m2, lane, LANES), axis=-1, keepdims=True)
    e2 = jnp.exp(m2 - m1)
    w1 = 1.0 / (1.0 + e2)
    return i1, i2, w1, e2 * w1


def _route_kernel(x_ref, g_ref, rw_ref, rb_ref, xn_ref, route_ref, cnt_ref, carry):
    @pl.when(pl.program_id(0) == 0)
    def _():
        carry[...] = jnp.zeros_like(carry)

    xf = x_ref[...]
    tm = xf.shape[0]
    xn = xf * lax.rsqrt(jnp.mean(xf * xf, axis=-1, keepdims=True) + EPS) * g_ref[...]
    xn_ref[...] = xn
    lane = lax.broadcasted_iota(jnp.int32, (tm, LANES), 1)
    logits = jnp.where(lane < N_EXPERTS, _dot(xn.astype(BF16), rw_ref[...]) + rb_ref[...], NEG)
    i1, i2, w1, w2 = _top2(logits, lane)
    hit = jnp.where(lane == i1, 1.0, 0.0) + jnp.where(lane == i2, 1.0, 0.0)
    incl = _dot(_tril_bf16(tm), hit.astype(BF16)) + carry[...]
    before = incl - hit
    carry[...] = incl[tm - 1:tm, :]
    cnt_ref[...] = incl[tm - 1:tm, :]
    r1 = jnp.sum(jnp.where(lane == i1, before, 0.0), axis=-1, keepdims=True)
    r2 = jnp.sum(jnp.where(lane == i2, before, 0.0), axis=-1, keepdims=True)
    cols = (i1.astype(F32), i2.astype(F32), r1, r2, w1, w2)
    rec = jnp.zeros((tm, LANES), F32)
    for c, v in enumerate(cols):
        rec = jnp.where(lane == c, v, rec)
    route_ref[...] = rec


def _route(x, norm_g, rw, rb, *, tm):
    M, K = x.shape
    return pl.pallas_call(
        _route_kernel,
        out_shape=[jax.ShapeDtypeStruct((M, K), F32), jax.ShapeDtypeStruct((M, LANES), F32),
                   jax.ShapeDtypeStruct((1, LANES), F32)],
        grid=(M // tm,),
        in_specs=[pl.BlockSpec((tm, K), lambda i: (i, 0)), pl.BlockSpec((1, K), lambda i: (0, 0)),
                  pl.BlockSpec((K, LANES), lambda i: (0, 0)), pl.BlockSpec((1, LANES), lambda i: (0, 0))],
        out_specs=[pl.BlockSpec((tm, K), lambda i: (i, 0)), pl.BlockSpec((tm, LANES), lambda i: (i, 0)),
                   pl.BlockSpec((1, LANES), lambda i: (0, 0))],
        scratch_shapes=[pltpu.VMEM((1, LANES), F32)],
        compiler_params=_cparams(("arbitrary",)),
        name="moe_route",
    )(x, norm_g, rw, rb)


def _gather_start(src_hbm, idx_ref, first, n, dst, sem, unroll):
    def body(r, carry):
        pltpu.make_async_copy(src_hbm.at[pl.ds(idx_ref[first + r], 1)], dst.at[pl.ds(r, 1)], sem).start()
        return carry
    lax.fori_loop(0, n, body, 0, unroll=unroll)


def _gather_wait(dst, sem):
    pltpu.make_async_copy(dst, dst, sem).wait()


def _experts_kernel(te_ref, nu_ref, src_ref, x_hbm, wg_ref, wu_ref, wd_ref, o_ref, xbuf, xb_scr, acc_scr, sems):
    i, f = pl.program_id(0), pl.program_id(1)
    last = pl.num_programs(1) - 1
    live = i < nu_ref[0]
    slot = i % 2

    @pl.when(live & (f == 0))
    def _():
        @pl.when(i == 0)
        def _():
            _gather_start(x_hbm, src_ref, 0, TM_E, xbuf.at[0], sems.at[0], 8)

        _gather_wait(xbuf.at[slot], sems.at[slot])
        xb_scr[...] = xbuf[slot].astype(BF16)
        acc_scr[...] = jnp.zeros_like(acc_scr)

    @pl.when(live)
    def _():
        per_step = TM_E // EXPERT_ISSUE_STEPS
        first = (f - 1) * per_step

        @pl.when((i + 1 < nu_ref[0]) & (f >= 1) & (f <= EXPERT_ISSUE_STEPS))
        def _():
            for r in range(per_step):
                pltpu.make_async_copy(x_hbm.at[pl.ds(src_ref[(i + 1) * TM_E + first + r], 1)],
                                      xbuf.at[1 - slot, pl.ds(first + r, 1)], sems.at[1 - slot]).start()

        xb = xb_scr[...]
        h = _silu(_dot(xb, wg_ref[0])) * _dot(xb, wu_ref[0])
        acc_scr[...] += _dot(h.astype(BF16), wd_ref[0])

    @pl.when(f == last)
    def _():
        o_ref[...] = jnp.where(live, acc_scr[...], 0.0)


def _experts(tile_expert, n_used, src, xn, wg, wu, wd, *, tf):
    P = src.shape[0]
    E, K, F = wg.shape
    nf = F // tf
    col = lambda i, f, nu: jnp.where(i < nu[0], f, nf - 1)
    return pl.pallas_call(
        _experts_kernel,
        out_shape=jax.ShapeDtypeStruct((P, K), F32),
        grid_spec=pltpu.PrefetchScalarGridSpec(
            num_scalar_prefetch=3, grid=(P // TM_E, nf),
            in_specs=[pl.BlockSpec(memory_space=pl.ANY),
                      pl.BlockSpec((1, K, tf), lambda i, f, te, nu, sr: (te[i], 0, col(i, f, nu))),
                      pl.BlockSpec((1, K, tf), lambda i, f, te, nu, sr: (te[i], 0, col(i, f, nu))),
                      pl.BlockSpec((1, tf, K), lambda i, f, te, nu, sr: (te[i], col(i, f, nu), 0))],
            out_specs=pl.BlockSpec((TM_E, K), lambda i, f, te, nu, sr: (i, 0)),
            scratch_shapes=[pltpu.VMEM((2, TM_E, K), F32), pltpu.VMEM((TM_E, K), BF16), pltpu.VMEM((TM_E, K), F32),
                            pltpu.SemaphoreType.DMA((2,))]),
        compiler_params=_cparams(("arbitrary", "arbitrary")),
        name="moe_experts",
    )(tile_expert, n_used, src, xn, wg, wu, wd)


def _combine_kernel(tc, dest_ref, os_hbm, x_ref, route_ref, o_ref, buf, sems):
    i = pl.program_id(0)
    slot = i % 2

    @pl.when(i == 0)
    def _():
        _gather_start(os_hbm, dest_ref, 0, 2 * tc, buf.at[0], sems.at[0], 8)

    @pl.when(i + 1 < pl.num_programs(0))
    def _():
        _gather_start(os_hbm, dest_ref, (i + 1) * 2 * tc, 2 * tc, buf.at[1 - slot], sems.at[1 - slot], 8)

    _gather_wait(buf.at[slot], sems.at[slot])
    w1, w2 = route_ref[:, 4:5], route_ref[:, 5:6]
    o_ref[...] = x_ref[...] + (w1 * buf[slot, 0:tc, :] + w2 * buf[slot, tc:2 * tc, :])


def _combine(dest, os, x, route, *, tc):
    M, K = x.shape
    return pl.pallas_call(
        functools.partial(_combine_kernel, tc),
        out_shape=jax.ShapeDtypeStruct((M, K), F32),
        grid_spec=pltpu.PrefetchScalarGridSpec(
            num_scalar_prefetch=1, grid=(M // tc,),
            in_specs=[pl.BlockSpec(memory_space=pl.ANY),
                      pl.BlockSpec((tc, K), lambda i, d: (i, 0)), pl.BlockSpec((tc, LANES), lambda i, d: (i, 0))],
            out_specs=pl.BlockSpec((tc, K), lambda i, d: (i, 0)),
            scratch_shapes=[pltpu.VMEM((2, 2 * tc, K), F32), pltpu.SemaphoreType.DMA((2,))]),
        compiler_params=_cparams(("arbitrary",)),
        name="moe_combine",
    )(dest, os, x, route)


def _moe_routed(x, norm_g, rw, rb, wg, wu, wd, *, tm):
    M = x.shape[0]
    xn, route, counts = _route(x, norm_g, rw, rb, tm=tm)
    e = route[:, 0:2].astype(jnp.int32)
    rank = route[:, 2:4].astype(jnp.int32)
    cnt = counts[0, :N_EXPERTS].astype(jnp.int32)
    tiles = (cnt + TM_E - 1) // TM_E
    tile_end = jnp.cumsum(tiles)
    dest = ((tile_end - tiles) * TM_E)[e] + rank
    n_tiles = (2 * M) // TM_E + N_EXPERTS
    P = n_tiles * TM_E
    token = jnp.repeat(jnp.arange(M, dtype=jnp.int32), 2)
    src = jnp.zeros((P,), jnp.int32).at[dest.reshape(-1)].set(token)
    tile_expert = jnp.minimum(jnp.sum(tile_end[None, :] <= jnp.arange(n_tiles, dtype=jnp.int32)[:, None], axis=1),
                              N_EXPERTS - 1).astype(jnp.int32)
    os = _experts(tile_expert, tile_end[-1:].astype(jnp.int32), src, xn, wg, wu, wd, tf=256)
    tc = 256
    dest_tiles = dest.reshape(M // tc, tc, 2).transpose(0, 2, 1).reshape(-1).astype(jnp.int32)
    return _combine(dest_tiles, os, x, route, tc=tc)


HGRN_HEADS_PER_STEP = 4


def _hgrn_kernel(T, c, aq_ref, af_ref, ai_ref, ag_ref, lb_ref, gn_ref, o_ref, st_ref, st_scr):
    t = pl.program_id(2)

    @pl.when(t == 0)
    def _():
        st_scr[...] = jnp.zeros_like(st_scr)

    row = lax.broadcasted_iota(jnp.int32, (T, T), 0)
    col = lax.broadcasted_iota(jnp.int32, (T, T), 1)
    shift = c.bit_length() - 1
    same = (row >> shift) == (col >> shift)
    causal = same & (col <= row)
    tri = jnp.where(causal, 1.0, 0.0).astype(BF16)
    blk = jnp.where(same, 1.0, 0.0).astype(BF16)
    gn = gn_ref[...]
    for hh in range(HGRN_HEADS_PER_STEP):
        cols = slice(hh * HEAD_DIM, (hh + 1) * HEAD_DIM)
        lb = lb_ref[:, cols]
        gate = lb + (1.0 - lb) * _sigmoid(af_ref[:, cols])
        k = 1.0 - gate
        parts = _split3(jnp.log(gate))
        b = sum(_dot(tri, p) for p in parts)
        bend = sum(_dot(blk, p) for p in parts)
        qe = (_silu(aq_ref[:, cols]) * jnp.exp(b)).astype(BF16)
        ke = (k * jnp.exp(-b)).astype(BF16)
        kend = (k * jnp.exp(bend - b)).astype(BF16)
        vb = ai_ref[:, cols].astype(BF16)
        o_intra = _dot(jnp.where(causal, _dot_nt(qe, ke), 0.0).astype(BF16), vb)
        st = st_scr[hh]
        for u in range(T // c):
            rows = slice(u * c, (u + 1) * c)
            o = o_intra[rows] + _dot_nt(qe[rows], st.astype(BF16))
            st = st * jnp.exp(bend[u * c:u * c + 1]) + _dot_tn(vb[rows], kend[rows])
            o_ref[rows, cols] = _head_rms(o, gn) * _silu(ag_ref[rows, cols])
        st_scr[hh] = st

    @pl.when(t == pl.num_programs(2) - 1)
    def _():
        for hh in range(HGRN_HEADS_PER_STEP):
            st_ref[0, hh] = st_scr[hh].T


def _hgrn_prompt(hg, lb, gn, B, S, *, T, c):
    nt = S // T
    H = H_HGRN
    hp = HGRN_HEADS_PER_STEP
    W = hp * HEAD_DIM
    spec = lambda off: pl.BlockSpec((T, W), lambda b, h, t, off=off: (b * nt + t, off + h))
    return pl.pallas_call(
        functools.partial(_hgrn_kernel, T, c),
        out_shape=[jax.ShapeDtypeStruct((B * S, H * HEAD_DIM), F32),
                   jax.ShapeDtypeStruct((B, H, HEAD_DIM, HEAD_DIM), F32)],
        grid=(B, H // hp, nt),
        in_specs=[spec(0), spec(H // hp), spec(2 * H // hp), spec(3 * H // hp),
                  pl.BlockSpec((1, W), lambda b, h, t: (0, h)),
                  pl.BlockSpec((1, HEAD_DIM), lambda b, h, t: (0, 0))],
        out_specs=[pl.BlockSpec((T, W), lambda b, h, t: (b * nt + t, h)),
                   pl.BlockSpec((1, hp, HEAD_DIM, HEAD_DIM), lambda b, h, t: (b, h, 0, 0))],
        scratch_shapes=[pltpu.VMEM((hp, HEAD_DIM, HEAD_DIM), F32)],
        compiler_params=_cparams(("arbitrary", "arbitrary", "arbitrary")),
        name="hgrn_prompt",
    )(hg, hg, hg, hg, lb, gn)


BIAS_SHIFT = 4
BIAS_SLOT = 1 << BIAS_SHIFT


def _cumsum_kernel(lf_ref, pq_ref, pk_ref, oq_ref, ok_ref, aq_ref, ak_ref, carry):
    @pl.when(pl.program_id(1) == 0)
    def _():
        carry[...] = jnp.zeros_like(carry)

    tc = lf_ref.shape[0]
    c = _dot3(_tril_bf16(tc), lf_ref[...]) + carry[...]
    carry[...] = c[tc - 1:tc, :]
    parts = _split3(c * LOG2E)
    aq_ref[...] = sum(_dot(t, pq_ref[i]) for i, t in enumerate(parts)) + oq_ref[...]
    ak_ref[...] = (ok_ref[...] - sum(_dot(t, pk_ref[i]) for i, t in enumerate(parts))).astype(BF16)


def _bias_layout():
    pq, pk = np.zeros((3, LANES, LANES), np.float32), np.zeros((3, LANES, LANES), np.float32)
    oq, ok = np.zeros((1, LANES), np.float32), np.zeros((1, LANES), np.float32)
    for h in range(H_FOX):
        for i in range(3):
            pq[i, h, BIAS_SLOT * h + i] = 1.0
            pk[i, h, BIAS_SLOT * h + 3 + i] = 1.0
            oq[0, BIAS_SLOT * h + 3 + i] = 1.0
            ok[0, BIAS_SLOT * h + i] = 1.0
    return jnp.asarray(pq, BF16), jnp.asarray(pk, BF16), jnp.asarray(oq), jnp.asarray(ok)


def _seq_cumsum(lf, B, S, *, tc):
    n = S // tc
    const3 = pl.BlockSpec((3, LANES, LANES), lambda b, t: (0, 0, 0))
    const1 = pl.BlockSpec((1, LANES), lambda b, t: (0, 0))
    blk = pl.BlockSpec((tc, LANES), lambda b, t: (b * n + t, 0))
    return pl.pallas_call(
        _cumsum_kernel,
        out_shape=[jax.ShapeDtypeStruct((B * S, LANES), F32), jax.ShapeDtypeStruct((B * S, LANES), BF16)],
        grid=(B, n),
        in_specs=[blk, const3, const3, const1, const1],
        out_specs=[blk, blk],
        scratch_shapes=[pltpu.VMEM((1, LANES), F32)],
        compiler_params=_cparams(("arbitrary", "arbitrary")),
        name="seq_cumsum",
    )(lf, *_bias_layout())


def _online_init(m_scr, l_scr, acc_scr):
    m_scr[...] = jnp.full_like(m_scr, M_INIT)
    l_scr[...] = jnp.zeros_like(l_scr)
    acc_scr[...] = jnp.zeros_like(acc_scr)


def _online_update(s, vb, m_scr, l_scr, acc_scr):
    m_prev = m_scr[...]
    m_new = jnp.maximum(m_prev, jnp.max(s, axis=-1, keepdims=True))
    alpha = jnp.exp(m_prev - m_new)
    p = jnp.exp(s - m_new)
    l_scr[...] = alpha * l_scr[...] + jnp.sum(p, axis=-1, keepdims=True)
    acc_scr[...] = alpha * acc_scr[...] + _dot(p.astype(BF16), vb)
    m_scr[...] = m_new


def _online_result(l_scr, acc_scr):
    l = l_scr[...]
    return acc_scr[...] / jnp.where(l > 0.0, l, 1.0)


def _flash_t_step(sT, vT, m, l, acc_scr):
    m_new = jnp.maximum(m, jnp.max(sT, axis=0, keepdims=True))
    alpha = jnp.exp2(m - m_new)
    p = jnp.exp2(sT - m_new)
    acc_scr[...] = alpha * acc_scr[...] + _dot(vT, p.astype(BF16))
    return m_new, alpha * l + jnp.sum(p, axis=0, keepdims=True)


def _flash_t_causal(ka_scr, vt_scr, qa, acc_scr, n_full, n_diag, tk, qpos):
    R = qa.shape[1]
    acc_scr[...] = jnp.zeros_like(acc_scr)

    def body(kj, carry):
        return _flash_t_step(_dot(ka_scr[kj], qa), vt_scr[kj], *carry, acc_scr)

    m, l = lax.fori_loop(0, n_full, body, (jnp.full((1, R), M_INIT, F32), jnp.zeros((1, R), F32)))
    for i in range(n_diag):
        kpos = (n_full + i) * tk + lax.broadcasted_iota(jnp.int32, (tk, R), 0)
        sT = jnp.where(kpos <= qpos, _dot(ka_scr[n_full + i], qa), NEG)
        m, l = _flash_t_step(sT, vt_scr[n_full + i], m, l, acc_scr)
    return acc_scr[...] / jnp.where(l > 0.0, l, 1.0)


def _fox_kernel(tq, tk, q_ref, k_ref, vt_ref, aq_ref, ak_ref, o_ref, ka_scr, vt_scr, acc_scr):
    h = pl.program_id(1)
    qi = pl.program_id(2)

    @pl.when(qi == 0)
    def _():
        for c in range(ka_scr.shape[0]):
            rows = slice(c * tk, (c + 1) * tk)
            ka_scr[c] = jnp.concatenate([k_ref[rows, :].astype(BF16), ak_ref[rows, :]], axis=1)
            vt_scr[c] = vt_ref[0, 0, :, rows].astype(BF16)

    lane = lax.broadcasted_iota(jnp.int32, (tq, LANES), 1)
    aq = jnp.where((lane >> BIAS_SHIFT) == h, aq_ref[...], 0.0)
    qa = jnp.concatenate([(q_ref[...] * QSCALE).T, aq.T], axis=0).astype(BF16)
    qpos = qi * tq + lax.broadcasted_iota(jnp.int32, (tk, tq), 1)
    o_ref[...] = _flash_t_causal(ka_scr, vt_scr, qa, acc_scr, qi * (tq // tk), tq // tk, tk, qpos).T


def _fox_prompt(fq, fk, fvt, aug_q, aug_k, B, S, *, tq, tk):
    nq = S // tq
    return pl.pallas_call(
        functools.partial(_fox_kernel, tq, tk),
        out_shape=jax.ShapeDtypeStruct(fq.shape, F32),
        grid=(B, H_FOX, nq),
        in_specs=[
            pl.BlockSpec((tq, HEAD_DIM), lambda b, h, qi: (b * nq + qi, h)),
            pl.BlockSpec((S, HEAD_DIM), lambda b, h, qi: (b, h)),
            pl.BlockSpec((1, 1, HEAD_DIM, S), lambda b, h, qi: (b, h, 0, 0)),
            pl.BlockSpec((tq, LANES), lambda b, h, qi: (b * nq + qi, 0)),
            pl.BlockSpec((S, LANES), lambda b, h, qi: (b, 0)),
        ],
        out_specs=pl.BlockSpec((tq, HEAD_DIM), lambda b, h, qi: (b * nq + qi, h)),
        scratch_shapes=[pltpu.VMEM((S // tk, tk, 2 * HEAD_DIM), BF16), pltpu.VMEM((S // tk, HEAD_DIM, tk), BF16),
                        pltpu.VMEM((HEAD_DIM, tq), F32)],
        compiler_params=_cparams(("arbitrary",) * 3),
        name="fox_prompt",
    )(fq, fk, fvt, aug_q, aug_k)


def _compress_kernel(nc, x_ref, pea_ref, peb_ref, w1a_ref, w1b_ref, w2_ref, o_ref):
    R = jnp.concatenate([x_ref[0, pl.ds(r, nc, stride=CMP_STRIDE), :] for r in range(CMP_STRIDE)], axis=1)
    p1 = _dot((R + pea_ref[...]).astype(BF16), w1a_ref[...])
    p2 = _dot((R + peb_ref[...]).astype(BF16), w1b_ref[...])
    pre = p1 + pltpu.roll(p2, nc - 1, 0)
    kc = _dot(_silu(pre).astype(BF16), w2_ref[...])
    row = lax.broadcasted_iota(jnp.int32, kc.shape, 0)
    o_ref[0, 0] = jnp.where(row < nc - 1, kc, 0.0)


def _cmp_weights(pe, w1, w2):
    half = CMP_STRIDE * HEAD_DIM
    return (pe[:CMP_STRIDE].reshape(1, half), pe[CMP_STRIDE:].reshape(1, half),
            w1[:half].astype(BF16), w1[half:].astype(BF16), w2.astype(BF16))


def _compress_prompt(x, cw, B, S):
    nc = S // CMP_STRIDE
    half = CMP_STRIDE * HEAD_DIM
    const = lambda shape: pl.BlockSpec(shape, lambda b, g: (0, 0))
    return pl.pallas_call(
        functools.partial(_compress_kernel, nc),
        out_shape=jax.ShapeDtypeStruct((B, NSA_GROUPS, nc, HEAD_DIM), F32),
        grid=(B, NSA_GROUPS),
        in_specs=[pl.BlockSpec((1, S, HEAD_DIM), lambda b, g: (b, 0, g)),
                  const((1, half)), const((1, half)), const((half, HEAD_DIM)), const((half, HEAD_DIM)),
                  const((HEAD_DIM, HEAD_DIM))],
        out_specs=pl.BlockSpec((1, 1, nc, HEAD_DIM), lambda b, g: (b, g, 0, 0)),
        compiler_params=_cparams(("arbitrary", "arbitrary")),
        name="nsa_compress",
    )(x, *cw)


def _overlap_t(n_cmp, n_sel):
    cs = np.arange(n_cmp)[None, :] * CMP_STRIDE
    ss = np.arange(n_sel)[:, None] * SEL_LEN
    ov = np.minimum(cs + CMP_LEN, ss + SEL_LEN) - np.maximum(cs, ss)
    return np.maximum(ov, 0).astype(np.float32) / CMP_LEN


def _group_rows(ref, Tq):
    return jnp.concatenate([ref[0, :, r * HEAD_DIM:(r + 1) * HEAD_DIM] for r in range(NSA_R)], axis=0)


def _store_gated(o_ref, o, gates, Tq, branch):
    for r in range(NSA_R):
        gcol = gates[:, 3 * r + branch:3 * r + branch + 1]
        o_ref[0, :, r * HEAD_DIM:(r + 1) * HEAD_DIM] = o[r * Tq:(r + 1) * Tq] * gcol


def _cmpsel_kernel(Tq, nsel, q_ref, kc_ref, vc_ref, ovt_ref, gate_ref, o_ref, sel_ref):
    qi = pl.program_id(2)
    q = (_group_rows(q_ref, Tq) * (HEAD_DIM ** -0.5)).astype(BF16)
    s = _dot_nt(q, kc_ref[0, 0].astype(BF16))
    qpos = qi * Tq + (lax.broadcasted_iota(jnp.int32, s.shape, 0) & (Tq - 1))
    cend = lax.broadcasted_iota(jnp.int32, s.shape, 1) * CMP_STRIDE + (CMP_LEN - 1)
    mask = cend <= qpos
    sm = jnp.where(mask, s, NEG)
    e = jnp.where(mask, jnp.exp(sm - jnp.max(sm, axis=-1, keepdims=True)), 0.0)
    den = jnp.sum(e, axis=-1, keepdims=True)
    p = e / jnp.where(den > 0.0, den, 1.0)
    o = _dot(p.astype(BF16), vc_ref[0, 0].astype(BF16))
    _store_gated(o_ref, o, gate_ref[0, 0], Tq, 0)

    psum = p[0:Tq]
    for r in range(1, NSA_R):
        psum = psum + p[r * Tq:(r + 1) * Tq]
    ovt = ovt_ref[...]
    sc = sum(_dot_nt(ovt, t) for t in _split3(psum))
    n = lax.broadcasted_iota(jnp.int32, sc.shape, 0)
    tpos = qi * Tq + lax.broadcasted_iota(jnp.int32, sc.shape, 1)
    cur = tpos >> SEL_SHIFT
    forced = (n == 0) | (n == cur) | (n == cur - 1)
    sc = jnp.where(n * SEL_LEN <= tpos, jnp.where(forced, -NEG, sc), NEG)
    rank = jnp.zeros(sc.shape, F32)
    for m in range(nsel):
        rowm = sc[m:m + 1, :]
        rank = rank + jnp.where(n > m, jnp.where(rowm >= sc, 1.0, 0.0), jnp.where(rowm > sc, 1.0, 0.0))
    selneg = jnp.where(rank < TOPN, 0.0, NEG)
    if nsel < LANES:
        selneg = jnp.concatenate([selneg, jnp.full((LANES - nsel, Tq), NEG, F32)], axis=0)
    sel_ref[0, 0, 0] = selneg.astype(BF16)


def _cmpsel_prompt(qn, kc, vc, gates_g, B, S, *, Tq):
    nq = S // Tq
    ncp = kc.shape[2]
    nsel = S // SEL_LEN
    ovt = jnp.asarray(_overlap_t(ncp, nsel), BF16)
    return pl.pallas_call(
        functools.partial(_cmpsel_kernel, Tq, nsel),
        out_shape=[jax.ShapeDtypeStruct((B, S, H_NSA * HEAD_DIM), F32),
                   jax.ShapeDtypeStruct((B, NSA_GROUPS, nq, LANES, Tq), BF16)],
        grid=(B, NSA_GROUPS, nq),
        in_specs=[pl.BlockSpec((1, Tq, NSA_R * HEAD_DIM), lambda b, g, qi: (b, qi, g)),
                  pl.BlockSpec((1, 1, ncp, HEAD_DIM), lambda b, g, qi: (b, g, 0, 0)),
                  pl.BlockSpec((1, 1, ncp, HEAD_DIM), lambda b, g, qi: (b, g, 0, 0)),
                  pl.BlockSpec((nsel, ncp), lambda b, g, qi: (0, 0)),
                  pl.BlockSpec((1, 1, Tq, 16), lambda b, g, qi: (b, g, qi, 0))],
        out_specs=[pl.BlockSpec((1, Tq, NSA_R * HEAD_DIM), lambda b, g, qi: (b, qi, g)),
                   pl.BlockSpec((1, 1, 1, LANES, Tq), lambda b, g, qi: (b, g, qi, 0, 0))],
        compiler_params=_cparams(("arbitrary",) * 3),
        name="nsa_cmpsel",
    )(qn, kc, vc, ovt, gates_g)


def _queries_t(q_ref, Tq):
    return jnp.concatenate([(q_ref[0, :, r * HEAD_DIM:(r + 1) * HEAD_DIM] * QSCALE).T for r in range(NSA_R)],
                           axis=1).astype(BF16)


def _store_gated_t(o_ref, oT, gates, Tq, branch):
    for r in range(NSA_R):
        gcol = gates[:, 3 * r + branch:3 * r + branch + 1]
        o_ref[0, :, r * HEAD_DIM:(r + 1) * HEAD_DIM] = oT[:, r * Tq:(r + 1) * Tq].T * gcol


def _sel_kernel(Tq, tk, q_ref, sn_ref, k_ref, vt_ref, e_ref, gate_ref, o_ref, ka_scr, vt_scr, acc_scr):
    qi = pl.program_id(2)

    @pl.when(qi == 0)
    def _():
        for c in range(ka_scr.shape[0]):
            rows = slice(c * tk, (c + 1) * tk)
            ka_scr[c] = jnp.concatenate([k_ref[0, rows, :].astype(BF16), e_ref[rows, :]], axis=1)
            vt_scr[c] = vt_ref[0, 0, :, rows].astype(BF16)

    sn = sn_ref[0, 0, 0]
    qa = jnp.concatenate([_queries_t(q_ref, Tq), jnp.concatenate([sn] * NSA_R, axis=1)], axis=0)
    R = NSA_R * Tq
    qpos = qi * Tq + (lax.broadcasted_iota(jnp.int32, (tk, R), 1) & (Tq - 1))
    oT = _flash_t_causal(ka_scr, vt_scr, qa, acc_scr, (qi * Tq) // tk, 1, tk, qpos)
    _store_gated_t(o_ref, oT, gate_ref[0, 0], Tq, 1)


def _sel_prompt(qr, selneg, sk, svt, gates_g, B, S, *, Tq, tk):
    nq = S // Tq
    key = np.arange(S)[:, None] // SEL_LEN
    e_all = jnp.asarray((key == np.arange(LANES)[None, :]).astype(np.float32), BF16)
    R = NSA_R * Tq
    return pl.pallas_call(
        functools.partial(_sel_kernel, Tq, tk),
        out_shape=jax.ShapeDtypeStruct((B, S, H_NSA * HEAD_DIM), F32),
        grid=(B, NSA_GROUPS, nq),
        in_specs=[pl.BlockSpec((1, Tq, NSA_R * HEAD_DIM), lambda b, g, qi: (b, qi, g)),
                  pl.BlockSpec((1, 1, 1, LANES, Tq), lambda b, g, qi: (b, g, qi, 0, 0)),
                  pl.BlockSpec((1, S, HEAD_DIM), lambda b, g, qi: (b, 0, g)),
                  pl.BlockSpec((1, 1, HEAD_DIM, S), lambda b, g, qi: (b, g, 0, 0)),
                  pl.BlockSpec((S, LANES), lambda b, g, qi: (0, 0)),
                  pl.BlockSpec((1, 1, Tq, 16), lambda b, g, qi: (b, g, qi, 0))],
        out_specs=pl.BlockSpec((1, Tq, NSA_R * HEAD_DIM), lambda b, g, qi: (b, qi, g)),
        scratch_shapes=[pltpu.VMEM((S // tk, tk, 2 * HEAD_DIM), BF16), pltpu.VMEM((S // tk, HEAD_DIM, tk), BF16),
                        pltpu.VMEM((HEAD_DIM, R), F32)],
        compiler_params=_cparams(("arbitrary",) * 3),
        name="nsa_sel",
    )(qr, selneg, sk, svt, e_all, gates_g)


def _win_kernel(Tq, nwb, q_ref, *refs):
    k_refs, vt_refs = refs[:nwb], refs[nwb:2 * nwb]
    gate_ref, o_ref = refs[2 * nwb:]
    qi = pl.program_id(2)
    R = NSA_R * Tq
    qT = _queries_t(q_ref, Tq)
    qpos = qi * Tq + (lax.broadcasted_iota(jnp.int32, (Tq, R), 1) & (Tq - 1))
    krow = lax.broadcasted_iota(jnp.int32, (Tq, R), 0)
    s = []
    for i in range(nwb):
        kb = qi - (nwb - 1) + i
        si = _dot(k_refs[i][0].astype(BF16), qT)
        if i == 0:
            si = jnp.where(qpos - (kb * Tq + krow) < WINDOW, si, NEG)
        if i == nwb - 1:
            si = jnp.where(kb * Tq + krow <= qpos, si, NEG)
        else:
            si = si + jnp.where(kb >= 0, 0.0, NEG)
        s.append(si)
    m = functools.reduce(jnp.maximum, [jnp.max(si, axis=0, keepdims=True) for si in s])
    p = [jnp.exp2(si - m) for si in s]
    l = sum(jnp.sum(pi, axis=0, keepdims=True) for pi in p)
    oT = sum(_dot(vt_refs[i][0, 0].astype(BF16), p[i].astype(BF16)) for i in range(nwb)) / l
    _store_gated_t(o_ref, oT, gate_ref[0, 0], Tq, 2)


def _win_prompt(qr, wk, wvt, gates_g, B, S, *, Tq):
    nq = S // Tq
    nwb = WINDOW // Tq + 1
    blk = lambda i: (lambda qi: jnp.maximum(qi - (nwb - 1) + i, 0))
    kspec = lambda i: pl.BlockSpec((1, Tq, HEAD_DIM), lambda b, g, qi, f=blk(i): (b, f(qi), g))
    vspec = lambda i: pl.BlockSpec((1, 1, HEAD_DIM, Tq), lambda b, g, qi, f=blk(i): (b, g, 0, f(qi)))
    return pl.pallas_call(
        functools.partial(_win_kernel, Tq, nwb),
        out_shape=jax.ShapeDtypeStruct((B, S, H_NSA * HEAD_DIM), F32),
        grid=(B, NSA_GROUPS, nq),
        in_specs=[pl.BlockSpec((1, Tq, NSA_R * HEAD_DIM), lambda b, g, qi: (b, qi, g))]
        + [kspec(i) for i in range(nwb)] + [vspec(i) for i in range(nwb)]
        + [pl.BlockSpec((1, 1, Tq, 16), lambda b, g, qi: (b, g, qi, 0))],
        out_specs=pl.BlockSpec((1, Tq, NSA_R * HEAD_DIM), lambda b, g, qi: (b, qi, g)),
        compiler_params=_cparams(("arbitrary",) * 3),
        name="nsa_win",
    )(qr, *([wk] * nwb), *([wvt] * nwb), gates_g)


def _hgrn_step_kernel(aq_ref, af_ref, ai_ref, ag_ref, lb_ref, gn_ref, s_ref, o_ref, so_ref):
    lb = lb_ref[0]
    gate = lb + (1.0 - lb) * _sigmoid(af_ref[0, 0])
    s_new = s_ref[0, 0] * gate + (1.0 - gate) * ai_ref[0, 0]
    so_ref[0, 0] = s_new
    o = jnp.sum(_silu(aq_ref[0, 0]) * s_new, axis=0, keepdims=True)
    o_ref[0, 0] = _head_rms(o, gn_ref[...]) * _silu(ag_ref[0, 0])


def _hgrn_step(hg, lb, gn, state):
    D, H = state.shape[:2]
    W = H * HEAD_DIM
    colv = lambda a: a.reshape(D, H, HEAD_DIM, 1)
    rowv = lambda a: a.reshape(D, H, 1, HEAD_DIM)
    cspec = pl.BlockSpec((1, 1, HEAD_DIM, 1), lambda d, h: (d, h, 0, 0))
    rspec = pl.BlockSpec((1, 1, 1, HEAD_DIM), lambda d, h: (d, h, 0, 0))
    sspec = pl.BlockSpec((1, 1, HEAD_DIM, HEAD_DIM), lambda d, h: (d, h, 0, 0))
    o, s_new = pl.pallas_call(
        _hgrn_step_kernel,
        out_shape=[jax.ShapeDtypeStruct((D, H, 1, HEAD_DIM), F32), jax.ShapeDtypeStruct(state.shape, F32)],
        grid=(D, H),
        in_specs=[cspec, cspec, rspec, rspec,
                  pl.BlockSpec((1, HEAD_DIM, 1), lambda d, h: (h, 0, 0)),
                  pl.BlockSpec((1, HEAD_DIM), lambda d, h: (0, 0)), sspec],
        out_specs=[rspec, sspec],
        compiler_params=_cparams(("arbitrary", "arbitrary")),
        name="hgrn_step",
    )(colv(hg[:, :W]), colv(hg[:, W:2 * W]), rowv(hg[:, 2 * W:3 * W]), rowv(hg[:, 3 * W:]),
      lb.reshape(H, HEAD_DIM, 1), gn, state)
    return o.reshape(D, W), s_new


def _fox_step_kernel(PG, pt_ref, q_ref, kn_ref, vn_ref, lfn_ref, *refs):
    k_refs, v_refs, lf_refs = refs[:PG], refs[PG:2 * PG], refs[2 * PG:3 * PG]
    o_ref, m_scr, l_scr, acc_scr, carry = refs[3 * PG:]
    W = H_FOX * PAGE
    q = q_ref[0] * (HEAD_DIM ** -0.5)

    @pl.when(pl.program_id(1) == 0)
    def _():
        m_scr[...] = jnp.sum(q * kn_ref[0], axis=-1, keepdims=True)
        l_scr[...] = jnp.ones_like(l_scr)
        acc_scr[...] = vn_ref[0]
        carry[...] = jnp.broadcast_to(lfn_ref[0], carry.shape)

    qb = q.astype(BF16)
    lane = lax.broadcasted_iota(jnp.int32, (H_FOX, W), 1)
    head = lax.broadcasted_iota(jnp.int32, (H_FOX, W), 0)
    plane = lax.broadcasted_iota(jnp.int32, (PG, W), 1)
    lf = jnp.concatenate([lf_refs[i][0] for i in range(PG)], axis=0)
    suf, tot = lf, lf
    step = H_FOX
    while step < W:
        suf = suf + jnp.where(plane + step < W, pltpu.roll(suf, W - step, 1), 0.0)
        tot = tot + pltpu.roll(tot, step, 1)
        step *= 2
    c = carry[...]
    s = []
    for i in range(PG):
        si = _dot_nt(qb, k_refs[i][0].astype(BF16)) + (c + (suf[i:i + 1] - lf[i:i + 1]))
        s.append(jnp.where((lane & (H_FOX - 1)) == head, si, NEG))
        c = c + tot[i:i + 1]
    carry[...] = c
    m_prev = m_scr[...]
    m_new = functools.reduce(jnp.maximum, [jnp.max(si, axis=-1, keepdims=True) for si in s] + [m_prev])
    alpha = jnp.exp(m_prev - m_new)
    p = [jnp.exp(si - m_new) for si in s]
    l_scr[...] = alpha * l_scr[...] + sum(jnp.sum(pi, axis=-1, keepdims=True) for pi in p)
    acc_scr[...] = alpha * acc_scr[...] + sum(_dot(p[i].astype(BF16), v_refs[i][0].astype(BF16)) for i in range(PG))
    m_scr[...] = m_new

    @pl.when(pl.program_id(1) == pl.num_programs(1) - 1)
    def _():
        o_ref[0] = _online_result(l_scr, acc_scr)


def _fox_step(page_table, q, k_new, v_new, lf_new, cache_k, cache_v, cache_lf, *, PG):
    D, NP = page_table.shape
    n_pool = cache_k.shape[0]
    W = H_FOX * PAGE
    k2 = cache_k.reshape(n_pool, W, HEAD_DIM)
    v2 = cache_v.reshape(n_pool, W, HEAD_DIM)
    lf2 = cache_lf.astype(F32).reshape(n_pool, 1, W)
    lfn = jnp.tile(lf_new, (1, PAGE)).reshape(D, 1, W)
    page = lambda i: (lambda d, j, pt: (pt[d, NP - 1 - (j * PG + i)], 0, 0))
    hspec = pl.BlockSpec((1, H_FOX, HEAD_DIM), lambda d, j, pt: (d, 0, 0))
    in_specs = [hspec, hspec, hspec, pl.BlockSpec((1, 1, W), lambda d, j, pt: (d, 0, 0))]
    in_specs += [pl.BlockSpec((1, W, HEAD_DIM), page(i)) for i in range(PG)]
    in_specs += [pl.BlockSpec((1, W, HEAD_DIM), page(i)) for i in range(PG)]
    in_specs += [pl.BlockSpec((1, 1, W), page(i)) for i in range(PG)]
    return pl.pallas_call(
        functools.partial(_fox_step_kernel, PG),
        out_shape=jax.ShapeDtypeStruct((D, H_FOX, HEAD_DIM), F32),
        grid_spec=pltpu.PrefetchScalarGridSpec(
            num_scalar_prefetch=1, grid=(D, NP // PG), in_specs=in_specs, out_specs=hspec,
            scratch_shapes=[pltpu.VMEM((H_FOX, 1), F32), pltpu.VMEM((H_FOX, 1), F32),
                            pltpu.VMEM((H_FOX, HEAD_DIM), F32), pltpu.VMEM((H_FOX, W), F32)]),
        compiler_params=_cparams(("arbitrary", "arbitrary")),
        name="fox_step",
    )(page_table, q, k_new, v_new, lfn, *([k2] * PG), *([v2] * PG), *([lf2] * PG))


def _cmp_pages_kernel(PG, pt_ref, *refs):
    pages = refs[:PG]
    pea_ref, peb_ref, w1a_ref, w1b_ref, p1_ref, p2_ref = refs[PG:]
    per_page = PAGE // CMP_STRIDE
    parts = []
    for g in range(NSA_GROUPS):
        for i in range(PG):
            parts.append(jnp.concatenate(
                [pages[i][0, pl.ds(r * NSA_GROUPS + g, per_page, stride=CMP_STRIDE * NSA_GROUPS), :]
                 for r in range(CMP_STRIDE)], axis=1))
    R = jnp.concatenate(parts, axis=0)
    shape = (NSA_GROUPS, PG * per_page, HEAD_DIM)
    p1_ref[0] = _dot((R + pea_ref[...]).astype(BF16), w1a_ref[...]).reshape(shape)
    p2_ref[0] = _dot((R + peb_ref[...]).astype(BF16), w1b_ref[...]).reshape(shape)


def _cmp_pages(page_table, cache, cw, *, PG):
    D, NP = page_table.shape
    n_pool = cache.shape[0]
    c2 = cache.reshape(n_pool, PAGE * NSA_GROUPS, HEAD_DIM)
    per_page = PAGE // CMP_STRIDE
    half = CMP_STRIDE * HEAD_DIM
    page = lambda i: (lambda d, j, pt: (pt[d, j * PG + i], 0, 0))
    const = lambda shape: pl.BlockSpec(shape, lambda d, j, pt: (0, 0))
    in_specs = [pl.BlockSpec((1, PAGE * NSA_GROUPS, HEAD_DIM), page(i)) for i in range(PG)]
    in_specs += [const((1, half)), const((1, half)), const((half, HEAD_DIM)), const((half, HEAD_DIM))]
    ospec = pl.BlockSpec((1, NSA_GROUPS, PG * per_page, HEAD_DIM), lambda d, j, pt: (d, 0, j, 0))
    shp = jax.ShapeDtypeStruct((D, NSA_GROUPS, NP * per_page, HEAD_DIM), F32)
    return pl.pallas_call(
        functools.partial(_cmp_pages_kernel, PG),
        out_shape=[shp, shp],
        grid_spec=pltpu.PrefetchScalarGridSpec(
            num_scalar_prefetch=1, grid=(D, NP // PG), in_specs=in_specs, out_specs=[ospec, ospec]),
        compiler_params=_cparams(("arbitrary", "arbitrary")),
        name="nsa_cmp_pages",
    )(page_table, *([c2] * PG), *cw[:4])


def _cmpsel_step_kernel(qpos, nsp, q_ref, p1k_ref, p2k_ref, p1v_ref, p2v_ref, w2k_ref, w2v_ref,
                        ov_ref, g_ref, o_ref, idx_ref):
    nc = p1k_ref.shape[2]

    def finish(p1_ref, p2_ref, w2_ref):
        pre = p1_ref[0, 0] + pltpu.roll(p2_ref[0, 0], nc - 1, 0)
        return _dot(_silu(pre).astype(BF16), w2_ref[...]).astype(BF16)

    kc = finish(p1k_ref, p2k_ref, w2k_ref)
    vc = finish(p1v_ref, p2v_ref, w2v_ref)
    q = (q_ref[0, 0] * (HEAD_DIM ** -0.5)).astype(BF16)
    s = _dot_nt(q, kc)
    cend = lax.broadcasted_iota(jnp.int32, s.shape, 1) * CMP_STRIDE + (CMP_LEN - 1)
    mask = cend <= qpos
    sm = jnp.where(mask, s, NEG)
    e = jnp.where(mask, jnp.exp(sm - jnp.max(sm, axis=-1, keepdims=True)), 0.0)
    den = jnp.sum(e, axis=-1, keepdims=True)
    p = e / jnp.where(den > 0.0, den, 1.0)
    o_ref[0, 0] = _dot(p.astype(BF16), vc) * g_ref[0, 0][:, 0:1]

    psum = jnp.broadcast_to(jnp.sum(p[0:NSA_R], axis=0, keepdims=True), (8, nc))
    ov = ov_ref[...]
    sc_row = sum(_dot(t, ov) for t in _split3(psum))[0:1]
    n_lane = lax.broadcasted_iota(jnp.int32, (1, nsp), 1)
    cur = qpos // SEL_LEN
    forced = (n_lane == 0) | (n_lane == cur) | (n_lane == cur - 1)
    sc_row = jnp.where(n_lane * SEL_LEN <= qpos, jnp.where(forced, -NEG, sc_row), NEG)
    mi = lax.broadcasted_iota(jnp.int32, (nsp, nsp), 0)
    ni = lax.broadcasted_iota(jnp.int32, (nsp, nsp), 1)
    sc_col = jnp.sum(jnp.where(mi == ni, sc_row, 0.0), axis=-1, keepdims=True)
    beats = jnp.where(mi < ni, jnp.where(sc_col >= sc_row, 1.0, 0.0), jnp.where(sc_col > sc_row, 1.0, 0.0))
    rank = jnp.sum(beats, axis=0, keepdims=True)
    lane = lax.broadcasted_iota(jnp.int32, (1, LANES), 1)
    out = jnp.zeros((1, LANES), F32)
    for k in range(TOPN):
        nk = jnp.sum(jnp.where(rank == float(k), n_lane.astype(F32), 0.0), axis=-1, keepdims=True)
        out = jnp.where(lane == k, nk, out)
    idx_ref[0, 0] = jnp.broadcast_to(out, (8, LANES)).astype(jnp.int32)


def _cmpsel_step(q16, p1k, p2k, p1v, p2v, cwk, cwv, gate_rows, qpos):
    D, G, nc = p1k.shape[:3]
    n_sel = -(-(qpos + 1) // SEL_LEN)
    nsp = -(-n_sel // LANES) * LANES
    ov = jnp.asarray(np.pad(_overlap_t(nc, n_sel).T, ((0, 0), (0, nsp - n_sel))), BF16)
    big = pl.BlockSpec((1, 1, nc, HEAD_DIM), lambda d, g: (d, g, 0, 0))
    qspec = pl.BlockSpec((1, 1, 16, HEAD_DIM), lambda d, g: (d, g, 0, 0))
    w2spec = pl.BlockSpec((HEAD_DIM, HEAD_DIM), lambda d, g: (0, 0))
    return pl.pallas_call(
        functools.partial(_cmpsel_step_kernel, qpos, nsp),
        out_shape=[jax.ShapeDtypeStruct((D, G, 16, HEAD_DIM), F32),
                   jax.ShapeDtypeStruct((D, G, 8, LANES), jnp.int32)],
        grid=(D, G),
        in_specs=[qspec, big, big, big, big, w2spec, w2spec,
                  pl.BlockSpec((nc, nsp), lambda d, g: (0, 0)),
                  pl.BlockSpec((1, 1, 16, 3), lambda d, g: (d, g, 0, 0))],
        out_specs=[qspec, pl.BlockSpec((1, 1, 8, LANES), lambda d, g: (d, g, 0, 0))],
        compiler_params=_cparams(("arbitrary", "arbitrary")),
        name="nsa_cmpsel_step",
    )(q16, p1k, p2k, p1v, p2v, cwk[4], cwv[4], ov, gate_rows)


def _sel_step_kernel(n_past, pt_ref, idx_ref, q_ref, kn_ref, vn_ref, *refs):
    k_refs, v_refs = refs[:TOPN], refs[TOPN:2 * TOPN]
    g_ref, o_ref = refs[2 * TOPN:]
    d, g = pl.program_id(0), pl.program_id(1)
    q = q_ref[0, 0] * (HEAD_DIM ** -0.5)
    qb = q.astype(BF16)
    rows = pl.ds(g, SEL_LEN, stride=NSA_GROUPS)
    s_new = jnp.sum(q * kn_ref[0, 0], axis=-1, keepdims=True)
    s = [_dot_nt(qb, k_refs[k][rows, :].astype(BF16)) + jnp.where(idx_ref[d, g, k] < n_past, 0.0, NEG)
         for k in range(TOPN)]
    m = functools.reduce(jnp.maximum, [jnp.max(sk, axis=-1, keepdims=True) for sk in s] + [s_new])
    p = [jnp.exp(sk - m) for sk in s]
    p_new = jnp.exp(s_new - m)
    l = sum(jnp.sum(pk, axis=-1, keepdims=True) for pk in p) + p_new
    o = sum(_dot(p[k].astype(BF16), v_refs[k][rows, :].astype(BF16)) for k in range(TOPN)) + p_new * vn_ref[0, 0]
    o_ref[0, 0] = o / l * g_ref[0, 0][:, 1:2]


def _sel_step(page_table, idx, q16, k_new, v_new, cache_k, cache_v, gate_rows):
    D, NP = page_table.shape
    G = NSA_GROUPS
    n_pool = cache_k.shape[0]
    per_page = PAGE // SEL_LEN
    n_past = NP * per_page
    blk_rows = SEL_LEN * G
    k2 = cache_k.reshape(n_pool * PAGE * G, HEAD_DIM)
    v2 = cache_v.reshape(n_pool * PAGE * G, HEAD_DIM)

    def blk(k):
        def index(d, g, pt, ix):
            n = jnp.minimum(ix[d, g, k], n_past - 1)
            return (pt[d, n // per_page] * per_page + n % per_page, 0)
        return pl.BlockSpec((blk_rows, HEAD_DIM), index)

    qspec = pl.BlockSpec((1, 1, 16, HEAD_DIM), lambda d, g, pt, ix: (d, g, 0, 0))
    nspec = pl.BlockSpec((1, 1, 1, HEAD_DIM), lambda d, g, pt, ix: (d, g, 0, 0))
    return pl.pallas_call(
        functools.partial(_sel_step_kernel, n_past),
        out_shape=jax.ShapeDtypeStruct((D, G, 16, HEAD_DIM), F32),
        grid_spec=pltpu.PrefetchScalarGridSpec(
            num_scalar_prefetch=2, grid=(D, G),
            in_specs=[qspec, nspec, nspec] + [blk(k) for k in range(TOPN)] * 2
            + [pl.BlockSpec((1, 1, 16, 3), lambda d, g, pt, ix: (d, g, 0, 0))],
            out_specs=qspec),
        compiler_params=_cparams(("arbitrary",) * 2),
        name="nsa_sel_step",
    )(page_table, idx, q16, k_new, v_new, *([k2] * TOPN), *([v2] * TOPN), gate_rows)


def _win_step_kernel(q_ref, k_ref, v_ref, g_ref, o_ref):
    q = (q_ref[0, 0] * (HEAD_DIM ** -0.5)).astype(BF16)
    s = _dot_nt(q, k_ref[0].astype(BF16))
    e = jnp.exp(s - jnp.max(s, axis=-1, keepdims=True))
    p = e / jnp.sum(e, axis=-1, keepdims=True)
    o_ref[0, 0] = _dot(p.astype(BF16), v_ref[0].astype(BF16)) * g_ref[0, 0][:, 2:3]


def _win_step(q16, kw, vw, gate_rows):
    D, G = q16.shape[:2]
    L = kw.shape[1]
    qspec = pl.BlockSpec((1, 1, 16, HEAD_DIM), lambda d, g: (d, g, 0, 0))
    kspec = pl.BlockSpec((1, L, HEAD_DIM), lambda d, g: (d, 0, g))
    return pl.pallas_call(
        _win_step_kernel,
        out_shape=jax.ShapeDtypeStruct((D, G, 16, HEAD_DIM), F32),
        grid=(D, G),
        in_specs=[qspec, kspec, kspec, pl.BlockSpec((1, 1, 16, 3), lambda d, g: (d, g, 0, 0))],
        out_specs=qspec,
        compiler_params=_cparams(("arbitrary", "arbitrary")),
        name="nsa_win_step",
    )(q16, kw, vw, gate_rows)


TN = 512
TM = 512
TM_STEP = 8


def _pad_cols(a, n):
    return jnp.pad(a, ((0, 0), (0, n - a.shape[1])))


TN_PROJ = 1024


def _segments(widths_modes):
    segs, start = [], 0
    for width, mode in widths_modes:
        segs.append((start, width, mode))
        start += width
    return segs, start


def _proj_weights(w_in, n_main):
    wb = w_in.astype(BF16)
    return wb, _pad_cols(wb[:, n_main:], LANES)


def _even_params(e_norm_mix, e_w_in, lb, out_norm, f_bias, q_norm, k_norm, e_w_out, e_norm_ffn):
    d = e_w_in.shape[0]
    aw, bw = H_HGRN * HEAD_DIM, H_FOX * HEAD_DIM
    segs, n_main = _segments(((4 * aw, 'raw'), (bw, 'hnorm'), (bw, 'hnorm'), (bw, 'raw')))
    gain = jnp.concatenate([jnp.ones((4 * aw,), F32), jnp.tile(q_norm, H_FOX), jnp.tile(k_norm, H_FOX),
                            jnp.ones((bw,), F32)])
    w_in, w_tail = _proj_weights(e_w_in, n_main)
    return dict(norm=e_norm_mix.reshape(1, d), w_in=w_in, w_tail=w_tail, gain=gain.reshape(1, -1),
                bias=_pad_cols(f_bias.astype(F32).reshape(1, -1), LANES), segs=segs, lb=lb.reshape(1, aw),
                out_norm=out_norm.reshape(1, HEAD_DIM), w_out=e_w_out.astype(BF16), norm_ffn=e_norm_ffn.reshape(1, d))


def _even_proj(x2, p, tm):
    zeros = jnp.zeros((tm, HEAD_DIM), F32)
    return _proj(x2, p['norm'], p['w_in'], p['w_tail'], p['gain'], p['bias'], zeros, zeros, p['segs'], 'logsig',
                 tm=tm, tn=TN_PROJ, name="even_proj")


def _even_finish(x2, oa, of, p, ffn_w, tm, emit=False):
    x2 = _outproj([jnp.concatenate([oa, of], axis=1)], p['w_out'], x2, tm=tm, tn=TN, name="even_out")
    return _ffn(x2, p['norm_ffn'], *ffn_w, tm=tm, tf=TN, emit=emit, name="ffn")


def _odd_params(o_norm_mix, o_w_in, q_norm, ck_norm, sk_norm, wk_norm, pe_k, w1_k, w2_k, pe_v, w1_v, w2_v,
                o_w_out, o_norm_ffn, router_w, router_b):
    d = o_w_in.shape[0]
    qw, kvw = H_NSA * HEAD_DIM, NSA_GROUPS * HEAD_DIM
    segs, n_main = _segments(((qw, 'hnorm_both'), (kvw, 'hnorm'), (kvw, 'raw'), (kvw, 'hnorm_rope'), (kvw, 'raw'),
                              (kvw, 'hnorm_rope'), (kvw, 'raw')))
    ones = jnp.ones((kvw,), F32)
    gain = jnp.concatenate([jnp.tile(q_norm, H_NSA), jnp.tile(ck_norm, NSA_GROUPS), ones,
                            jnp.tile(sk_norm, NSA_GROUPS), ones, jnp.tile(wk_norm, NSA_GROUPS), ones])
    w_in, w_tail = _proj_weights(o_w_in, n_main)
    return dict(norm=o_norm_mix.reshape(1, d), w_in=w_in, w_tail=w_tail, gain=gain.reshape(1, -1),
                bias=jnp.zeros((1, LANES), F32), segs=segs,
                cwk=_cmp_weights(pe_k, w1_k, w2_k), cwv=_cmp_weights(pe_v, w1_v, w2_v),
                w_out=o_w_out.astype(BF16), norm_ffn=o_norm_ffn.reshape(1, d),
                rw=_pad_cols(router_w.astype(BF16), LANES), rb=_pad_cols(router_b.reshape(1, -1).astype(F32), LANES))


def _odd_proj(x2, p, pos, tm):
    cosf, sinf = _rope_tables(pos)
    return _proj(x2, p['norm'], p['w_in'], p['w_tail'], p['gain'], p['bias'], cosf, sinf, p['segs'], 'sigmoid',
                 tm=tm, tn=TN_PROJ, name="odd_proj")


def _odd_finish(x2, branches, p, moe_w, tm, emit=False):
    x2 = _outproj(branches, p['w_out'], x2, tm=tm, tn=TN, name="odd_out")
    args = (x2, p['norm_ffn'], p['rw'], p['rb'], *moe_w)
    if emit:
        return _moe(*args, tm=tm, tf=256, emit=True, name="moe")
    return _moe_routed(*args, tm=tm)


def kernel(x_prompt, x_sample, state_hgrn, cache_fox_k, cache_fox_v, cache_fox_logf, cache_nsa_cmp_k, cache_nsa_cmp_v, cache_nsa_sel_k, cache_nsa_sel_v, cache_nsa_win_k, cache_nsa_win_v, page_table, e_norm_mix, e_w_in, hgrn_lb_logits, hgrn_out_norm, fox_f_bias, fox_q_norm, fox_k_norm, e_w_out, e_norm_ffn, ffn_w_gate, ffn_w_up, ffn_w_down, o_norm_mix, o_w_in, nsa_q_norm, nsa_cmp_k_norm, nsa_sel_k_norm, nsa_win_k_norm, cmp_pe_k, cmp_w1_k, cmp_w2_k, cmp_pe_v, cmp_w1_v, cmp_w2_v, o_w_out, o_norm_ffn, router_w, router_b, moe_w_gate, moe_w_up, moe_w_down):
    B, S, d = x_prompt.shape
    D, T, _ = x_sample.shape
    NP = page_table.shape[1]
    past = NP * PAGE
    w_buf = cache_nsa_win_k.shape[2]
    assert T == 1 and w_buf == WINDOW and S >= WINDOW and cache_fox_k.shape[2] == PAGE
    G = NSA_GROUPS
    lbs = jnp.cumsum(jax.nn.softmax(hgrn_lb_logits.astype(F32), axis=0), axis=0)
    xp, xd = x_prompt.reshape(B * S, d), x_sample.reshape(D, d)
    tm_p = min(TM, S)

    li = 0
    p = _even_params(e_norm_mix[li], e_w_in[li], lbs[li], hgrn_out_norm[li], fox_f_bias[li], fox_q_norm[li],
                     fox_k_norm[li], e_w_out[li], e_norm_ffn[li])
    hg, fq, fk, fv, fl = _even_proj(xd, p, TM_STEP)
    oa, st_d = _hgrn_step(hg, p['lb'], p['out_norm'], state_hgrn[li].astype(F32))
    heads = lambda a: a.reshape(D, H_FOX, HEAD_DIM)
    of = _fox_step(page_table, heads(fq), heads(fk), heads(fv), fl[:, :H_FOX],
                   cache_fox_k[li], cache_fox_v[li], cache_fox_logf[li], PG=8)
    xd, ffn_bf16 = _even_finish(xd, oa, of.reshape(D, H_FOX * HEAD_DIM), p,
                                (ffn_w_gate[li], ffn_w_up[li], ffn_w_down[li]), TM_STEP, emit=True)
    fox_d = (fk.reshape(1, D, 1, H_FOX, HEAD_DIM), fv.reshape(1, D, 1, H_FOX, HEAD_DIM),
             fl[:, :H_FOX].reshape(1, D, 1, H_FOX))

    hg, fq, fk, fv, fl = _even_proj(xp, p, tm_p)
    aug_q, aug_k = _seq_cumsum(fl, B, S, tc=tm_p)
    oa, st_p = _hgrn_prompt(hg, p['lb'], p['out_norm'], B, S, T=min(256, S), c=32)
    fvt = fv.reshape(B, S, H_FOX, HEAD_DIM).transpose(0, 2, 3, 1)
    of = _fox_prompt(fq, fk, fvt, aug_q, aug_k, B, S, tq=min(1024, S), tk=tm_p)
    xp = _even_finish(xp, oa, of, p, ffn_bf16, tm_p)
    fox_p = (fk.reshape(1, B, S, H_FOX, HEAD_DIM), fv.reshape(1, B, S, H_FOX, HEAD_DIM),
             fl[:, :H_FOX].reshape(1, B, S, H_FOX))

    p = _odd_params(o_norm_mix[li], o_w_in[li], nsa_q_norm[li], nsa_cmp_k_norm[li], nsa_sel_k_norm[li],
                    nsa_win_k_norm[li], cmp_pe_k[li], cmp_w1_k[li], cmp_w2_k[li], cmp_pe_v[li], cmp_w1_v[li],
                    cmp_w2_v[li], o_w_out[li], o_norm_ffn[li], router_w[li], router_b[li])
    qn, qr, ck, cv, sk, sv, wk, wv, gt = _odd_proj(xd, p, jnp.full((TM_STEP,), past), TM_STEP)
    rows16 = lambda a, w: jnp.pad(a.reshape(D, G, NSA_R, w), ((0, 0), (0, 0), (0, 16 - NSA_R), (0, 0)))
    gate_rows = rows16(gt[:, :3 * H_NSA], 3)
    q16n, q16r = rows16(qn, HEAD_DIM), rows16(qr, HEAD_DIM)
    p1k, p2k = _cmp_pages(page_table, cache_nsa_cmp_k[li], p['cwk'], PG=16)
    p1v, p2v = _cmp_pages(page_table, cache_nsa_cmp_v[li], p['cwv'], PG=16)
    o_cmp, idx = _cmpsel_step(q16n, p1k, p2k, p1v, p2v, p['cwk'], p['cwv'], gate_rows, past)
    new = lambda a: a.reshape(D, G, 1, HEAD_DIM)
    o_sel = _sel_step(page_table, idx[:, :, 0, :TOPN], q16r, new(sk), new(sv),
                      cache_nsa_sel_k[li], cache_nsa_sel_v[li], gate_rows)
    kvd = lambda a: a.reshape(D, 1, G, HEAD_DIM)
    win_k = jnp.concatenate([cache_nsa_win_k[li], kvd(wk)], axis=1)[:, -w_buf:]
    win_v = jnp.concatenate([cache_nsa_win_v[li], kvd(wv)], axis=1)[:, -w_buf:]
    o_win = _win_step(q16r, win_k.reshape(D, w_buf, G * HEAD_DIM), win_v.reshape(D, w_buf, G * HEAD_DIM), gate_rows)
    unrow = lambda a: a[:, :, :NSA_R].reshape(D, H_NSA * HEAD_DIM)
    xd, moe_bf16 = _odd_finish(xd, [unrow(o_cmp), unrow(o_sel), unrow(o_win)], p,
                               (moe_w_gate[li], moe_w_up[li], moe_w_down[li]), TM_STEP, emit=True)
    kv1 = lambda a: a.reshape(1, D, 1, G, HEAD_DIM)
    nsa_d = (kv1(ck), kv1(cv), kv1(sk), kv1(sv), win_k[None], win_v[None])

    qn, qr, ck, cv, sk, sv, wk, wv, gt = _odd_proj(xp, p, jnp.arange(S), tm_p)
    seq = lambda a: a.reshape(B, S, -1)
    gates_g = jnp.pad(gt[:, :3 * H_NSA].reshape(B, S, G, 3 * NSA_R).transpose(0, 2, 1, 3),
                      ((0, 0), (0, 0), (0, 0), (0, 16 - 3 * NSA_R)))
    kc = _compress_prompt(seq(ck), p['cwk'], B, S)
    vc = _compress_prompt(seq(cv), p['cwv'], B, S)
    tq = min(256, S)
    o_cmp, selneg = _cmpsel_prompt(seq(qn), kc, vc, gates_g, B, S, Tq=tq)
    keys_last = lambda a: a.reshape(B, S, G, HEAD_DIM).transpose(0, 2, 3, 1)
    o_sel = _sel_prompt(seq(qr), selneg, seq(sk), keys_last(sv), gates_g, B, S, Tq=tq, tk=min(512, S))
    o_win = _win_prompt(seq(qr), seq(wk), keys_last(wv), gates_g, B, S, Tq=tq)
    flat = lambda a: a.reshape(B * S, -1)
    xp = _odd_finish(xp, [flat(o_cmp), flat(o_sel), flat(o_win)], p, moe_bf16, tm_p)
    kv = lambda a: a.reshape(1, B, S, G, HEAD_DIM)
    nsa_p = (kv(ck), kv(cv), kv(sk), kv(sv), kv(wk)[:, :, S - WINDOW:], kv(wv)[:, :, S - WINDOW:])

    return (xp.reshape(B, S, d), xd.reshape(D, T, d), st_p[None].astype(state_hgrn.dtype),
            st_d[None].astype(state_hgrn.dtype), *fox_p, *fox_d, *nsa_p, *nsa_d)
```

```python
import functools

import numpy as np
import jax
import jax.numpy as jnp
from jax import lax
from jax.experimental import pallas as pl
from jax.experimental.pallas import tpu as pltpu

F32 = jnp.float32
BF16 = jnp.bfloat16

LANES = 128
HEAD_DIM = 128
EPS = 1e-6
ROPE_THETA = 10000.0
LOG2E = 1.4426950408889634
QSCALE = HEAD_DIM ** -0.5 * LOG2E
NEG = -1e30
M_INIT = -1e20
VMEM_LIMIT = 56 * 1024 * 1024

H_HGRN = 8
H_FOX = 8
H_NSA = 16
NSA_GROUPS = 4
NSA_R = H_NSA // NSA_GROUPS
CMP_STRIDE = 16
CMP_LEN = 32
SEL_LEN = 64
SEL_SHIFT = 6
TOPN = 16
WINDOW = 512
N_EXPERTS = 8
PAGE = 128


def _cparams(sem):
    return pltpu.CompilerParams(dimension_semantics=sem, vmem_limit_bytes=VMEM_LIMIT)


def _dot(a, b):
    return jnp.dot(a, b, preferred_element_type=F32)


def _dot_nt(a, b):
    return lax.dot_general(a, b, (((1,), (1,)), ((), ())), preferred_element_type=F32)


def _dot_tn(a, b):
    return lax.dot_general(a, b, (((0,), (0,)), ((), ())), preferred_element_type=F32)


def _split3(x):
    hi = x.astype(BF16)
    r1 = x - hi.astype(F32)
    mid = r1.astype(BF16)
    lo = (r1 - mid.astype(F32)).astype(BF16)
    return hi, mid, lo


def _dot3(w_bf16, x):
    hi, mid, lo = _split3(x)
    return _dot(w_bf16, hi) + _dot(w_bf16, mid) + _dot(w_bf16, lo)


def _sigmoid(x):
    return 1.0 / (1.0 + jnp.exp(-x))


def _silu(x):
    return x * _sigmoid(x)


def _log_sigmoid(x):
    return jnp.minimum(x, 0.0) - jnp.log(1.0 + jnp.exp(-jnp.abs(x)))


def _tril_bf16(n):
    r = lax.broadcasted_iota(jnp.int32, (n, n), 0)
    c = lax.broadcasted_iota(jnp.int32, (n, n), 1)
    return jnp.where(c <= r, 1.0, 0.0).astype(BF16)


def _head_rms(a, gain):
    ms = jnp.mean(a * a, axis=-1, keepdims=True)
    return a * lax.rsqrt(ms + EPS) * gain


def _rope(y, cosf, sinf):
    return y * cosf + pltpu.roll(y, HEAD_DIM // 2, 1) * sinf


def _rope_tables(pos):
    half = HEAD_DIM // 2
    inv = ROPE_THETA ** (-jnp.arange(half, dtype=F32) / half)
    ang = pos.astype(F32)[:, None] * inv[None, :]
    cos, sin = jnp.cos(ang), jnp.sin(ang)
    return jnp.concatenate([cos, cos], axis=-1), jnp.concatenate([-sin, sin], axis=-1)


def _n_outs(mode):
    return 2 if mode == 'hnorm_both' else 1


def _seg_block(width, tn):
    return min(width, tn)


def _proj_kernel(segs, tail_mode, tn, *refs):
    n_out = sum(_n_outs(s[2]) for s in segs)
    x_ref, g_ref, w_ref, wt_ref, gain_ref, bias_ref, cos_ref, sin_ref = refs[:8]
    out_refs = refs[8:8 + n_out]
    tail_ref = refs[8 + n_out]
    xn_ref = refs[9 + n_out]
    j = pl.program_id(1)

    @pl.when(j == 0)
    def _():
        xf = x_ref[...]
        ms = jnp.mean(xf * xf, axis=-1, keepdims=True)
        xn_ref[...] = (xf * lax.rsqrt(ms + EPS) * g_ref[...]).astype(BF16)

    acc = _dot(xn_ref[...], w_ref[...])

    def emit(mode, outs, a, b, dst):
        for h in range((b - a) // HEAD_DIM):
            src = slice(a + h * HEAD_DIM, a + (h + 1) * HEAD_DIM)
            to = slice(dst + h * HEAD_DIM, dst + (h + 1) * HEAD_DIM)
            if mode == 'raw':
                outs[0][:, to] = acc[:, src]
                continue
            y = _head_rms(acc[:, src], gain_ref[:, src])
            if mode in ('hnorm', 'hnorm_both'):
                outs[0][:, to] = y
            if mode in ('hnorm_rope', 'hnorm_both'):
                outs[-1][:, to] = _rope(y, cos_ref[...], sin_ref[...])

    ntiles = pl.num_programs(1)
    for tile in range((segs[-1][0] + segs[-1][1]) // tn):
        @pl.when(j == tile)
        def _(tile=tile):
            oi = 0
            for (start, width, mode) in segs:
                outs = out_refs[oi:oi + _n_outs(mode)]
                oi += _n_outs(mode)
                a, b = max(start, tile * tn), min(start + width, (tile + 1) * tn)
                if a < b:
                    emit(mode, outs, a - tile * tn, b - tile * tn, (a - start) % _seg_block(width, tn))

    @pl.when(j == ntiles - 1)
    def _():
        t = _dot(xn_ref[...], wt_ref[...]) + bias_ref[...]
        tail_ref[...] = _sigmoid(t) if tail_mode == 'sigmoid' else _log_sigmoid(t)


def _proj(x, norm_g, w, w_tail, gain_all, bias_tail, cosf, sinf, segs, tail_mode, *, tm, tn, name):
    M, K = x.shape
    n_main = segs[-1][0] + segs[-1][1]
    assert M % tm == 0 and n_main % tn == 0
    nseq = cosf.shape[0] // tm
    out_shape, out_specs = [], []
    for (start, width, mode) in segs:
        bw = _seg_block(width, tn)
        assert width % bw == 0 and start % bw == 0
        for _ in range(_n_outs(mode)):
            out_shape.append(jax.ShapeDtypeStruct((M, width), F32))
            out_specs.append(pl.BlockSpec(
                (tm, bw), lambda i, j, s=start, bw=bw, n=width // bw: (i, jnp.clip((j * tn - s) // bw, 0, n - 1))))
    out_shape.append(jax.ShapeDtypeStruct((M, LANES), F32))
    out_specs.append(pl.BlockSpec((tm, LANES), lambda i, j: (i, 0)))
    return pl.pallas_call(
        functools.partial(_proj_kernel, tuple(segs), tail_mode, tn),
        out_shape=out_shape,
        grid=(M // tm, n_main // tn),
        in_specs=[
            pl.BlockSpec((tm, K), lambda i, j: (i, 0)),
            pl.BlockSpec((1, K), lambda i, j: (0, 0)),
            pl.BlockSpec((K, tn), lambda i, j: (0, j)),
            pl.BlockSpec((K, LANES), lambda i, j: (0, 0)),
            pl.BlockSpec((1, tn), lambda i, j: (0, j)),
            pl.BlockSpec((1, LANES), lambda i, j: (0, 0)),
            pl.BlockSpec((tm, HEAD_DIM), lambda i, j: (i % nseq, 0)),
            pl.BlockSpec((tm, HEAD_DIM), lambda i, j: (i % nseq, 0)),
        ],
        out_specs=out_specs,
        scratch_shapes=[pltpu.VMEM((tm, K), BF16)],
        compiler_params=_cparams(("arbitrary", "arbitrary")),
        name=name,
    )(x, norm_g, w, w_tail, gain_all, bias_tail, cosf, sinf)


def _outproj_kernel(group_sizes, *refs):
    n_lhs = sum(group_sizes)
    lhs = refs[:n_lhs]
    w_ref, res_ref, o_ref, xs_ref = refs[n_lhs:n_lhs + 4]

    @pl.when(pl.program_id(1) == 0)
    def _():
        k0, r0 = 0, 0
        for n in group_sizes:
            x = lhs[r0][...]
            for r in lhs[r0 + 1:r0 + n]:
                x = x + r[...]
            xs_ref[:, k0:k0 + x.shape[1]] = x.astype(BF16)
            k0, r0 = k0 + x.shape[1], r0 + n

    o_ref[...] = res_ref[...] + _dot(xs_ref[...], w_ref[...])


def _outproj(lhs_groups, w, res, *, tm, tn, name):
    M = res.shape[0]
    K, N = w.shape
    flat = [a for g in lhs_groups for a in g]
    assert sum(g[0].shape[1] for g in lhs_groups) == K
    return pl.pallas_call(
        functools.partial(_outproj_kernel, tuple(len(g) for g in lhs_groups)),
        out_shape=jax.ShapeDtypeStruct((M, N), F32),
        grid=(M // tm, N // tn),
        in_specs=[pl.BlockSpec((tm, a.shape[1]), lambda i, j: (i, 0)) for a in flat] + [
            pl.BlockSpec((K, tn), lambda i, j: (0, j)),
            pl.BlockSpec((tm, tn), lambda i, j: (i, j)),
        ],
        out_specs=pl.BlockSpec((tm, tn), lambda i, j: (i, j)),
        scratch_shapes=[pltpu.VMEM((tm, K), BF16)],
        compiler_params=_cparams(("arbitrary", "arbitrary")),
        name=name,
    )(*flat, w, res)


def _ffn_kernel(emit, x_ref, g_ref, wg_ref, wu_ref, wd_ref, o_ref, *rest):
    xn_ref = rest[-1]
    f = pl.program_id(1)

    @pl.when(f == 0)
    def _():
        xf = x_ref[...]
        ms = jnp.mean(xf * xf, axis=-1, keepdims=True)
        xn_ref[...] = (xf * lax.rsqrt(ms + EPS) * g_ref[...]).astype(BF16)
        o_ref[...] = xf

    xn = xn_ref[...]
    wg, wu, wd = wg_ref[...].astype(BF16), wu_ref[...].astype(BF16), wd_ref[...].astype(BF16)
    if emit:
        rest[0][...], rest[1][...], rest[2][...] = wg, wu, wd
    h = _silu(_dot(xn, wg)) * _dot(xn, wu)
    o_ref[...] += _dot(h.astype(BF16), wd)


def _ffn(x, norm_g, wg, wu, wd, *, tm, tf, emit=False, name):
    M, K = x.shape
    F = wg.shape[1]
    assert not emit or M == tm
    w_specs = [pl.BlockSpec((K, tf), lambda i, f: (0, f)), pl.BlockSpec((K, tf), lambda i, f: (0, f)),
               pl.BlockSpec((tf, K), lambda i, f: (f, 0))]
    out_shape = [jax.ShapeDtypeStruct((M, K), F32)]
    out_specs = [pl.BlockSpec((tm, K), lambda i, f: (i, 0))]
    if emit:
        out_shape += [jax.ShapeDtypeStruct(w.shape, BF16) for w in (wg, wu, wd)]
        out_specs += w_specs
    out = pl.pallas_call(
        functools.partial(_ffn_kernel, emit),
        out_shape=out_shape,
        grid=(M // tm, F // tf),
        in_specs=[pl.BlockSpec((tm, K), lambda i, f: (i, 0)), pl.BlockSpec((1, K), lambda i, f: (0, 0))] + w_specs,
        out_specs=out_specs,
        scratch_shapes=[pltpu.VMEM((tm, K), BF16)],
        compiler_params=_cparams(("arbitrary", "arbitrary")),
        name=name,
    )(x, norm_g, wg, wu, wd)
    return (out[0], tuple(out[1:])) if emit else out[0]


def _moe_kernel(emit, x_ref, g_ref, rw_ref, rb_ref, wg_ref, wu_ref, wd_ref, o_ref, *rest):
    xn_ref, gate_ref = rest[-2:]
    e = pl.program_id(1)
    f = pl.program_id(2)

    @pl.when((e == 0) & (f == 0))
    def _():
        xf = x_ref[...]
        ms = jnp.mean(xf * xf, axis=-1, keepdims=True)
        xn = (xf * lax.rsqrt(ms + EPS) * g_ref[...]).astype(BF16)
        xn_ref[...] = xn
        o_ref[...] = xf
        lane = lax.broadcasted_iota(jnp.int32, (xf.shape[0], LANES), 1)
        logits = jnp.where(lane < N_EXPERTS, _dot(xn, rw_ref[...]) + rb_ref[...], NEG)
        m1 = jnp.max(logits, axis=-1, keepdims=True)
        i1 = jnp.min(jnp.where(logits == m1, lane, LANES), axis=-1, keepdims=True)
        l2 = jnp.where(lane == i1, NEG, logits)
        m2 = jnp.max(l2, axis=-1, keepdims=True)
        i2 = jnp.min(jnp.where(l2 == m2, lane, LANES), axis=-1, keepdims=True)
        e2 = jnp.exp(m2 - m1)
        w1 = 1.0 / (1.0 + e2)
        gate_ref[...] = jnp.where(lane == i1, w1, 0.0) + jnp.where(lane == i2, e2 * w1, 0.0)

    xn = xn_ref[...]
    wg, wu, wd = wg_ref[0].astype(BF16), wu_ref[0].astype(BF16), wd_ref[0].astype(BF16)
    if emit:
        rest[0][0], rest[1][0], rest[2][0] = wg, wu, wd
    h = _silu(_dot(xn, wg)) * _dot(xn, wu)
    lane = lax.broadcasted_iota(jnp.int32, gate_ref.shape, 1)
    ge = jnp.sum(jnp.where(lane == e, gate_ref[...], 0.0), axis=-1, keepdims=True)
    o_ref[...] += ge * _dot(h.astype(BF16), wd)


def _moe(x, norm_g, rw, rb, wg, wu, wd, *, tm, tf, emit=False, name):
    M, K = x.shape
    E, _, F = wg.shape
    assert not emit or M == tm
    w_specs = [pl.BlockSpec((1, K, tf), lambda i, e, f: (e, 0, f)), pl.BlockSpec((1, K, tf), lambda i, e, f: (e, 0, f)),
               pl.BlockSpec((1, tf, K), lambda i, e, f: (e, f, 0))]
    out_shape = [jax.ShapeDtypeStruct((M, K), F32)]
    out_specs = [pl.BlockSpec((tm, K), lambda i, e, f: (i, 0))]
    if emit:
        out_shape += [jax.ShapeDtypeStruct(w.shape, BF16) for w in (wg, wu, wd)]
        out_specs += w_specs
    out = pl.pallas_call(
        functools.partial(_moe_kernel, emit),
        out_shape=out_shape,
        grid=(M // tm, E, F // tf),
        in_specs=[
            pl.BlockSpec((tm, K), lambda i, e, f: (i, 0)),
            pl.BlockSpec((1, K), lambda i, e, f: (0, 0)),
            pl.BlockSpec((K, LANES), lambda i, e, f: (0, 0)),
            pl.BlockSpec((1, LANES), lambda i, e, f: (0, 0)),
        ] + w_specs,
        out_specs=out_specs,
        scratch_shapes=[pltpu.VMEM((tm, K), BF16), pltpu.VMEM((tm, LANES), F32)],
        compiler_params=_cparams(("arbitrary", "arbitrary", "arbitrary")),
        name=name,
    )(x, norm_g, rw, rb, wg, wu, wd)
    return (out[0], tuple(out[1:])) if emit else out[0]


TM_E = 512
EXPERT_ISSUE_STEPS = 8


def _top2(logits, lane):
    m1 = jnp.max(logits, axis=-1, keepdims=True)
    i1 = jnp.min(jnp.where(logits == m1, lane, LANES), axis=-1, keepdims=True)
    l2 = jnp.where(lane == i1, NEG, logits)
    m2 = jnp.max(l2, axis=-1, keepdims=True)
    i2 = jnp.min(jnp.where(l2 == m2, lane, LANES), axis=-1, keepdims=True)
    e2 = jnp.exp(m2 - m1)
    w1 = 1.0 / (1.0 + e2)
    return i1, i2, w1, e2 * w1


def _route_kernel(x_ref, g_ref, rw_ref, rb_ref, xn_ref, route_ref, cnt_ref, carry):
    @pl.when(pl.program_id(0) == 0)
    def _():
        carry[...] = jnp.zeros_like(carry)

    xf = x_ref[...]
    tm = xf.shape[0]
    xn = xf * lax.rsqrt(jnp.mean(xf * xf, axis=-1, keepdims=True) + EPS) * g_ref[...]
    xn_ref[...] = xn
    lane = lax.broadcasted_iota(jnp.int32, (tm, LANES), 1)
    logits = jnp.where(lane < N_EXPERTS, _dot(xn.astype(BF16), rw_ref[...]) + rb_ref[...], NEG)
    i1, i2, w1, w2 = _top2(logits, lane)
    hit = jnp.where(lane == i1, 1.0, 0.0) + jnp.where(lane == i2, 1.0, 0.0)
    incl = _dot(_tril_bf16(tm), hit.astype(BF16)) + carry[...]
    before = incl - hit
    carry[...] = incl[tm - 1:tm, :]
    cnt_ref[...] = incl[tm - 1:tm, :]
    r1 = jnp.sum(jnp.where(lane == i1, before, 0.0), axis=-1, keepdims=True)
    r2 = jnp.sum(jnp.where(lane == i2, before, 0.0), axis=-1, keepdims=True)
    cols = (i1.astype(F32), i2.astype(F32), r1, r2, w1, w2)
    rec = jnp.zeros((tm, LANES), F32)
    for c, v in enumerate(cols):
        rec = jnp.where(lane == c, v, rec)
    route_ref[...] = rec


def _route(x, norm_g, rw, rb, *, tm):
    M, K = x.shape
    return pl.pallas_call(
        _route_kernel,
        out_shape=[jax.ShapeDtypeStruct((M, K), F32), jax.ShapeDtypeStruct((M, LANES), F32),
                   jax.ShapeDtypeStruct((1, LANES), F32)],
        grid=(M // tm,),
        in_specs=[pl.BlockSpec((tm, K), lambda i: (i, 0)), pl.BlockSpec((1, K), lambda i: (0, 0)),
                  pl.BlockSpec((K, LANES), lambda i: (0, 0)), pl.BlockSpec((1, LANES), lambda i: (0, 0))],
        out_specs=[pl.BlockSpec((tm, K), lambda i: (i, 0)), pl.BlockSpec((tm, LANES), lambda i: (i, 0)),
                   pl.BlockSpec((1, LANES), lambda i: (0, 0))],
        scratch_shapes=[pltpu.VMEM((1, LANES), F32)],
        compiler_params=_cparams(("arbitrary",)),
        name="moe_route",
    )(x, norm_g, rw, rb)


def _gather_start(src_hbm, idx_ref, first, n, dst, sem, unroll):
    def body(r, carry):
        pltpu.make_async_copy(src_hbm.at[pl.ds(idx_ref[first + r], 1)], dst.at[pl.ds(r, 1)], sem).start()
        return carry
    lax.fori_loop(0, n, body, 0, unroll=unroll)


def _gather_wait(dst, sem):
    pltpu.make_async_copy(dst, dst, sem).wait()


def _experts_kernel(te_ref, nu_ref, src_ref, x_hbm, wg_ref, wu_ref, wd_ref, o_ref, xbuf, xb_scr, acc_scr, sems):
    i, f = pl.program_id(0), pl.program_id(1)
    last = pl.num_programs(1) - 1
    live = i < nu_ref[0]
    slot = i % 2

    @pl.when(live & (f == 0))
    def _():
        @pl.when(i == 0)
        def _():
            _gather_start(x_hbm, src_ref, 0, TM_E, xbuf.at[0], sems.at[0], 8)

        _gather_wait(xbuf.at[slot], sems.at[slot])
        xb_scr[...] = xbuf[slot].astype(BF16)
        acc_scr[...] = jnp.zeros_like(acc_scr)

    @pl.when(live)
    def _():
        per_step = TM_E // EXPERT_ISSUE_STEPS
        first = (f - 1) * per_step

        @pl.when((i + 1 < nu_ref[0]) & (f >= 1) & (f <= EXPERT_ISSUE_STEPS))
        def _():
            for r in range(per_step):
                pltpu.make_async_copy(x_hbm.at[pl.ds(src_ref[(i + 1) * TM_E + first + r], 1)],
                                      xbuf.at[1 - slot, pl.ds(first + r, 1)], sems.at[1 - slot]).start()

        xb = xb_scr[...]
        h = _silu(_dot(xb, wg_ref[0])) * _dot(xb, wu_ref[0])
        acc_scr[...] += _dot(h.astype(BF16), wd_ref[0])

    @pl.when(f == last)
    def _():
        o_ref[...] = jnp.where(live, acc_scr[...], 0.0)


def _experts(tile_expert, n_used, src, xn, wg, wu, wd, *, tf):
    P = src.shape[0]
    E, K, F = wg.shape
    nf = F // tf
    col = lambda i, f, nu: jnp.where(i < nu[0], f, nf - 1)
    return pl.pallas_call(
        _experts_kernel,
        out_shape=jax.ShapeDtypeStruct((P, K), F32),
        grid_spec=pltpu.PrefetchScalarGridSpec(
            num_scalar_prefetch=3, grid=(P // TM_E, nf),
            in_specs=[pl.BlockSpec(memory_space=pl.ANY),
                      pl.BlockSpec((1, K, tf), lambda i, f, te, nu, sr: (te[i], 0, col(i, f, nu))),
                      pl.BlockSpec((1, K, tf), lambda i, f, te, nu, sr: (te[i], 0, col(i, f, nu))),
                      pl.BlockSpec((1, tf, K), lambda i, f, te, nu, sr: (te[i], col(i, f, nu), 0))],
            out_specs=pl.BlockSpec((TM_E, K), lambda i, f, te, nu, sr: (i, 0)),
            scratch_shapes=[pltpu.VMEM((2, TM_E, K), F32), pltpu.VMEM((TM_E, K), BF16), pltpu.VMEM((TM_E, K), F32),
                            pltpu.SemaphoreType.DMA((2,))]),
        compiler_params=_cparams(("arbitrary", "arbitrary")),
        name="moe_experts",
    )(tile_expert, n_used, src, xn, wg, wu, wd)


def _combine_kernel(tc, dest_ref, os_hbm, x_ref, route_ref, o_ref, buf, sems):
    i = pl.program_id(0)
    slot = i % 2

    @pl.when(i == 0)
    def _():
        _gather_start(os_hbm, dest_ref, 0, 2 * tc, buf.at[0], sems.at[0], 8)

    @pl.when(i + 1 < pl.num_programs(0))
    def _():
        _gather_start(os_hbm, dest_ref, (i + 1) * 2 * tc, 2 * tc, buf.at[1 - slot], sems.at[1 - slot], 8)

    _gather_wait(buf.at[slot], sems.at[slot])
    w1, w2 = route_ref[:, 4:5], route_ref[:, 5:6]
    o_ref[...] = x_ref[...] + (w1 * buf[slot, 0:tc, :] + w2 * buf[slot, tc:2 * tc, :])


def _combine(dest, os, x, route, *, tc):
    M, K = x.shape
    return pl.pallas_call(
        functools.partial(_combine_kernel, tc),
        out_shape=jax.ShapeDtypeStruct((M, K), F32),
        grid_spec=pltpu.PrefetchScalarGridSpec(
            num_scalar_prefetch=1, grid=(M // tc,),
            in_specs=[pl.BlockSpec(memory_space=pl.ANY),
                      pl.BlockSpec((tc, K), lambda i, d: (i, 0)), pl.BlockSpec((tc, LANES), lambda i, d: (i, 0))],
            out_specs=pl.BlockSpec((tc, K), lambda i, d: (i, 0)),
            scratch_shapes=[pltpu.VMEM((2, 2 * tc, K), F32), pltpu.SemaphoreType.DMA((2,))]),
        compiler_params=_cparams(("arbitrary",)),
        name="moe_combine",
    )(dest, os, x, route)


def _moe_routed(x, norm_g, rw, rb, wg, wu, wd, *, tm):
    M = x.shape[0]
    xn, route, counts = _route(x, norm_g, rw, rb, tm=tm)
    e = route[:, 0:2].astype(jnp.int32)
    rank = route[:, 2:4].astype(jnp.int32)
    cnt = counts[0, :N_EXPERTS].astype(jnp.int32)
    tiles = (cnt + TM_E - 1) // TM_E
    tile_end = jnp.cumsum(tiles)
    dest = ((tile_end - tiles) * TM_E)[e] + rank
    n_tiles = (2 * M) // TM_E + N_EXPERTS
    P = n_tiles * TM_E
    token = jnp.repeat(jnp.arange(M, dtype=jnp.int32), 2)
    src = jnp.zeros((P,), jnp.int32).at[dest.reshape(-1)].set(token)
    tile_expert = jnp.minimum(jnp.sum(tile_end[None, :] <= jnp.arange(n_tiles, dtype=jnp.int32)[:, None], axis=1),
                              N_EXPERTS - 1).astype(jnp.int32)
    os = _experts(tile_expert, tile_end[-1:].astype(jnp.int32), src, xn, wg, wu, wd, tf=256)
    tc = 256
    dest_tiles = dest.reshape(M // tc, tc, 2).transpose(0, 2, 1).reshape(-1).astype(jnp.int32)
    return _combine(dest_tiles, os, x, route, tc=tc)


HGRN_HEADS_PER_STEP = 4


def _hgrn_kernel(T, c, aq_ref, af_ref, ai_ref, ag_ref, lb_ref, gn_ref, o_ref, st_ref, st_scr):
    t = pl.program_id(2)

    @pl.when(t == 0)
    def _():
        st_scr[...] = jnp.zeros_like(st_scr)

    row = lax.broadcasted_iota(jnp.int32, (T, T), 0)
    col = lax.broadcasted_iota(jnp.int32, (T, T), 1)
    shift = c.bit_length() - 1
    same = (row >> shift) == (col >> shift)
    causal = same & (col <= row)
    tri = jnp.where(causal, 1.0, 0.0).astype(BF16)
    blk = jnp.where(same, 1.0, 0.0).astype(BF16)
    gn = gn_ref[...]
    for hh in range(HGRN_HEADS_PER_STEP):
        cols = slice(hh * HEAD_DIM, (hh + 1) * HEAD_DIM)
        lb = lb_ref[:, cols]
        gate = lb + (1.0 - lb) * _sigmoid(af_ref[:, cols])
        k = 1.0 - gate
        parts = _split3(jnp.log(gate))
        b = sum(_dot(tri, p) for p in parts)
        bend = sum(_dot(blk, p) for p in parts)
        qe = (_silu(aq_ref[:, cols]) * jnp.exp(b)).astype(BF16)
        ke = (k * jnp.exp(-b)).astype(BF16)
        kend = (k * jnp.exp(bend - b)).astype(BF16)
        vb = ai_ref[:, cols].astype(BF16)
        o_intra = _dot(jnp.where(causal, _dot_nt(qe, ke), 0.0).astype(BF16), vb)
        st = st_scr[hh]
        for u in range(T // c):
            rows = slice(u * c, (u + 1) * c)
            o = o_intra[rows] + _dot_nt(qe[rows], st.astype(BF16))
            st = st * jnp.exp(bend[u * c:u * c + 1]) + _dot_tn(vb[rows], kend[rows])
            o_ref[rows, cols] = _head_rms(o, gn) * _silu(ag_ref[rows, cols])
        st_scr[hh] = st

    @pl.when(t == pl.num_programs(2) - 1)
    def _():
        for hh in range(HGRN_HEADS_PER_STEP):
            st_ref[0, hh] = st_scr[hh].T


def _hgrn_prompt(hg, lb, gn, B, S, *, T, c):
    nt = S // T
    H = H_HGRN
    hp = HGRN_HEADS_PER_STEP
    W = hp * HEAD_DIM
    spec = lambda off: pl.BlockSpec((T, W), lambda b, h, t, off=off: (b * nt + t, off + h))
    return pl.pallas_call(
        functools.partial(_hgrn_kernel, T, c),
        out_shape=[jax.ShapeDtypeStruct((B * S, H * HEAD_DIM), F32),
                   jax.ShapeDtypeStruct((B, H, HEAD_DIM, HEAD_DIM), F32)],
        grid=(B, H // hp, nt),
        in_specs=[spec(0), spec(H // hp), spec(2 * H // hp), spec(3 * H // hp),
                  pl.BlockSpec((1, W), lambda b, h, t: (0, h)),
                  pl.BlockSpec((1, HEAD_DIM), lambda b, h, t: (0, 0))],
        out_specs=[pl.BlockSpec((T, W), lambda b, h, t: (b * nt + t, h)),
                   pl.BlockSpec((1, hp, HEAD_DIM, HEAD_DIM), lambda b, h, t: (b, h, 0, 0))],
        scratch_shapes=[pltpu.VMEM((hp, HEAD_DIM, HEAD_DIM), F32)],
        compiler_params=_cparams(("arbitrary", "arbitrary", "arbitrary")),
        name="hgrn_prompt",
    )(hg, hg, hg, hg, lb, gn)


BIAS_SHIFT = 4
BIAS_SLOT = 1 << BIAS_SHIFT


def _cumsum_kernel(lf_ref, pq_ref, pk_ref, oq_ref, ok_ref, aq_ref, ak_ref, carry):
    @pl.when(pl.program_id(1) == 0)
    def _():
        carry[...] = jnp.zeros_like(carry)

    tc = lf_ref.shape[0]
    c = _dot3(_tril_bf16(tc), lf_ref[...]) + carry[...]
    carry[...] = c[tc - 1:tc, :]
    parts = _split3(c * LOG2E)
    aq_ref[...] = sum(_dot(t, pq_ref[i]) for i, t in enumerate(parts)) + oq_ref[...]
    ak_ref[...] = (ok_ref[...] - sum(_dot(t, pk_ref[i]) for i, t in enumerate(parts))).astype(BF16)


def _bias_layout():
    pq, pk = np.zeros((3, LANES, LANES), np.float32), np.zeros((3, LANES, LANES), np.float32)
    oq, ok = np.zeros((1, LANES), np.float32), np.zeros((1, LANES), np.float32)
    for h in range(H_FOX):
        for i in range(3):
            pq[i, h, BIAS_SLOT * h + i] = 1.0
            pk[i, h, BIAS_SLOT * h + 3 + i] = 1.0
            oq[0, BIAS_SLOT * h + 3 + i] = 1.0
            ok[0, BIAS_SLOT * h + i] = 1.0
    return jnp.asarray(pq, BF16), jnp.asarray(pk, BF16), jnp.asarray(oq), jnp.asarray(ok)


def _seq_cumsum(lf, B, S, *, tc):
    n = S // tc
    const3 = pl.BlockSpec((3, LANES, LANES), lambda b, t: (0, 0, 0))
    const1 = pl.BlockSpec((1, LANES), lambda b, t: (0, 0))
    blk = pl.BlockSpec((tc, LANES), lambda b, t: (b * n + t, 0))
    return pl.pallas_call(
        _cumsum_kernel,
        out_shape=[jax.ShapeDtypeStruct((B * S, LANES), F32), jax.ShapeDtypeStruct((B * S, LANES), BF16)],
        grid=(B, n),
        in_specs=[blk, const3, const3, const1, const1],
        out_specs=[blk, blk],
        scratch_shapes=[pltpu.VMEM((1, LANES), F32)],
        compiler_params=_cparams(("arbitrary", "arbitrary")),
        name="seq_cumsum",
    )(lf, *_bias_layout())


def _online_init(m_scr, l_scr, acc_scr):
    m_scr[...] = jnp.full_like(m_scr, M_INIT)
    l_scr[...] = jnp.zeros_like(l_scr)
    acc_scr[...] = jnp.zeros_like(acc_scr)


def _online_update(s, vb, m_scr, l_scr, acc_scr):
    m_prev = m_scr[...]
    m_new = jnp.maximum(m_prev, jnp.max(s, axis=-1, keepdims=True))
    alpha = jnp.exp(m_prev - m_new)
    p = jnp.exp(s - m_new)
    l_scr[...] = alpha * l_scr[...] + jnp.sum(p, axis=-1, keepdims=True)
    acc_scr[...] = alpha * acc_scr[...] + _dot(p.astype(BF16), vb)
    m_scr[...] = m_new


def _online_result(l_scr, acc_scr):
    l = l_scr[...]
    return acc_scr[...] / jnp.where(l > 0.0, l, 1.0)


def _flash_t_step(sT, vT, m, l, acc_scr):
    m_new = jnp.maximum(m, jnp.max(sT, axis=0, keepdims=True))
    alpha = jnp.exp2(m - m_new)
    p = jnp.exp2(sT - m_new)
    acc_scr[...] = alpha * acc_scr[...] + _dot(vT, p.astype(BF16))
    return m_new, alpha * l + jnp.sum(p, axis=0, keepdims=True)


def _flash_t_causal(ka_scr, vt_scr, qa, acc_scr, n_full, n_diag, tk, qpos, unroll=1):
    R = qa.shape[1]
    acc_scr[...] = jnp.zeros_like(acc_scr)

    def body(it, carry):
        for u in range(unroll):
            kj = it * unroll + u
            carry = _flash_t_step(_dot(ka_scr[kj], qa), vt_scr[kj], *carry, acc_scr)
        return carry

    m, l = lax.fori_loop(0, n_full // unroll, body, (jnp.full((1, R), M_INIT, F32), jnp.zeros((1, R), F32)))
    for i in range(n_diag):
        kpos = (n_full + i) * tk + lax.broadcasted_iota(jnp.int32, (tk, R), 0)
        sT = jnp.where(kpos <= qpos, _dot(ka_scr[n_full + i], qa), NEG)
        m, l = _flash_t_step(sT, vt_scr[n_full + i], m, l, acc_scr)
    return acc_scr[...] / jnp.where(l > 0.0, l, 1.0)


def _fox_kernel(tq, tk, q_ref, k_ref, vt_ref, aq_ref, ak_ref, o_ref, ka_scr, vt_scr, acc_scr):
    h = pl.program_id(1)
    qi = pl.program_id(2)

    @pl.when(qi == 0)
    def _():
        for c in range(ka_scr.shape[0]):
            rows = slice(c * tk, (c + 1) * tk)
            ka_scr[c] = jnp.concatenate([k_ref[rows, :].astype(BF16), ak_ref[rows, :]], axis=1)
            vt_scr[c] = vt_ref[0, 0, :, rows].astype(BF16)

    lane = lax.broadcasted_iota(jnp.int32, (tq, LANES), 1)
    aq = jnp.where((lane >> BIAS_SHIFT) == h, aq_ref[...], 0.0)
    qa = jnp.concatenate([(q_ref[...] * QSCALE).T, aq.T], axis=0).astype(BF16)
    qpos = qi * tq + lax.broadcasted_iota(jnp.int32, (tk, tq), 1)
    o_ref[...] = _flash_t_causal(ka_scr, vt_scr, qa, acc_scr, qi * (tq // tk), tq // tk, tk, qpos,
                                 unroll=tq // tk).T


def _fox_prompt(fq, fk, fvt, aug_q, aug_k, B, S, *, tq, tk):
    nq = S // tq
    return pl.pallas_call(
        functools.partial(_fox_kernel, tq, tk),
        out_shape=jax.ShapeDtypeStruct(fq.shape, F32),
        grid=(B, H_FOX, nq),
        in_specs=[
            pl.BlockSpec((tq, HEAD_DIM), lambda b, h, qi: (b * nq + qi, h)),
            pl.BlockSpec((S, HEAD_DIM), lambda b, h, qi: (b, h)),
            pl.BlockSpec((1, 1, HEAD_DIM, S), lambda b, h, qi: (b, h, 0, 0)),
            pl.BlockSpec((tq, LANES), lambda b, h, qi: (b * nq + qi, 0)),
            pl.BlockSpec((S, LANES), lambda b, h, qi: (b, 0)),
        ],
        out_specs=pl.BlockSpec((tq, HEAD_DIM), lambda b, h, qi: (b * nq + qi, h)),
        scratch_shapes=[pltpu.VMEM((S // tk, tk, 2 * HEAD_DIM), BF16), pltpu.VMEM((S // tk, HEAD_DIM, tk), BF16),
                        pltpu.VMEM((HEAD_DIM, tq), F32)],
        compiler_params=_cparams(("arbitrary",) * 3),
        name="fox_prompt",
    )(fq, fk, fvt, aug_q, aug_k)


def _compress_kernel(nc, x_ref, pea_ref, peb_ref, w1a_ref, w1b_ref, w2_ref, o_ref):
    R = jnp.concatenate([x_ref[0, pl.ds(r, nc, stride=CMP_STRIDE), :] for r in range(CMP_STRIDE)], axis=1)
    p1 = _dot((R + pea_ref[...]).astype(BF16), w1a_ref[...])
    p2 = _dot((R + peb_ref[...]).astype(BF16), w1b_ref[...])
    pre = p1 + pltpu.roll(p2, nc - 1, 0)
    kc = _dot(_silu(pre).astype(BF16), w2_ref[...])
    row = lax.broadcasted_iota(jnp.int32, kc.shape, 0)
    o_ref[0, 0] = jnp.where(row < nc - 1, kc, 0.0)


def _cmp_weights(pe, w1, w2):
    half = CMP_STRIDE * HEAD_DIM
    return (pe[:CMP_STRIDE].reshape(1, half), pe[CMP_STRIDE:].reshape(1, half),
            w1[:half].astype(BF16), w1[half:].astype(BF16), w2.astype(BF16))


def _compress_prompt(x, cw, B, S):
    nc = S // CMP_STRIDE
    half = CMP_STRIDE * HEAD_DIM
    const = lambda shape: pl.BlockSpec(shape, lambda b, g: (0, 0))
    return pl.pallas_call(
        functools.partial(_compress_kernel, nc),
        out_shape=jax.ShapeDtypeStruct((B, NSA_GROUPS, nc, HEAD_DIM), F32),
        grid=(B, NSA_GROUPS),
        in_specs=[pl.BlockSpec((1, S, HEAD_DIM), lambda b, g: (b, 0, g)),
                  const((1, half)), const((1, half)), const((half, HEAD_DIM)), const((half, HEAD_DIM)),
                  const((HEAD_DIM, HEAD_DIM))],
        out_specs=pl.BlockSpec((1, 1, nc, HEAD_DIM), lambda b, g: (b, g, 0, 0)),
        compiler_params=_cparams(("arbitrary", "arbitrary")),
        name="nsa_compress",
    )(x, *cw)


def _overlap_t(n_cmp, n_sel):
    cs = np.arange(n_cmp)[None, :] * CMP_STRIDE
    ss = np.arange(n_sel)[:, None] * SEL_LEN
    ov = np.minimum(cs + CMP_LEN, ss + SEL_LEN) - np.maximum(cs, ss)
    return np.maximum(ov, 0).astype(np.float32) / CMP_LEN


def _group_rows(ref, Tq):
    return jnp.concatenate([ref[0, :, r * HEAD_DIM:(r + 1) * HEAD_DIM] for r in range(NSA_R)], axis=0)


def _store_gated(o_ref, o, gates, Tq, branch):
    for r in range(NSA_R):
        gcol = gates[:, 3 * r + branch:3 * r + branch + 1]
        o_ref[0, :, r * HEAD_DIM:(r + 1) * HEAD_DIM] = o[r * Tq:(r + 1) * Tq] * gcol


def _cmpsel_kernel(Tq, nsel, q_ref, kc_ref, vc_ref, ovt_ref, gate_ref, o_ref, sel_ref):
    qi = pl.program_id(2)
    q = (_group_rows(q_ref, Tq) * (HEAD_DIM ** -0.5)).astype(BF16)
    s = _dot_nt(q, kc_ref[0, 0].astype(BF16))
    qpos = qi * Tq + (lax.broadcasted_iota(jnp.int32, s.shape, 0) & (Tq - 1))
    cend = lax.broadcasted_iota(jnp.int32, s.shape, 1) * CMP_STRIDE + (CMP_LEN - 1)
    mask = cend <= qpos
    sm = jnp.where(mask, s, NEG)
    e = jnp.where(mask, jnp.exp(sm - jnp.max(sm, axis=-1, keepdims=True)), 0.0)
    den = jnp.sum(e, axis=-1, keepdims=True)
    p = e / jnp.where(den > 0.0, den, 1.0)
    o = _dot(p.astype(BF16), vc_ref[0, 0].astype(BF16))
    _store_gated(o_ref, o, gate_ref[0, 0], Tq, 0)

    psum = p[0:Tq]
    for r in range(1, NSA_R):
        psum = psum + p[r * Tq:(r + 1) * Tq]
    ovt = ovt_ref[...]
    sc = sum(_dot_nt(ovt, t) for t in _split3(psum))
    n = lax.broadcasted_iota(jnp.int32, sc.shape, 0)
    tpos = qi * Tq + lax.broadcasted_iota(jnp.int32, sc.shape, 1)
    cur = tpos >> SEL_SHIFT
    forced = (n == 0) | (n == cur) | (n == cur - 1)
    sc = jnp.where(n * SEL_LEN <= tpos, jnp.where(forced, -NEG, sc), NEG)
    rank = jnp.zeros(sc.shape, F32)
    for m in range(nsel):
        rowm = sc[m:m + 1, :]
        rank = rank + jnp.where(n > m, jnp.where(rowm >= sc, 1.0, 0.0), jnp.where(rowm > sc, 1.0, 0.0))
    selneg = jnp.where(rank < TOPN, 0.0, NEG)
    if nsel < LANES:
        selneg = jnp.concatenate([selneg, jnp.full((LANES - nsel, Tq), NEG, F32)], axis=0)
    sel_ref[0, 0, 0] = selneg.astype(BF16)


def _cmpsel_prompt(qn, kc, vc, gates_g, B, S, *, Tq):
    nq = S // Tq
    ncp = kc.shape[2]
    nsel = S // SEL_LEN
    ovt = jnp.asarray(_overlap_t(ncp, nsel), BF16)
    return pl.pallas_call(
        functools.partial(_cmpsel_kernel, Tq, nsel),
        out_shape=[jax.ShapeDtypeStruct((B, S, H_NSA * HEAD_DIM), F32),
                   jax.ShapeDtypeStruct((B, NSA_GROUPS, nq, LANES, Tq), BF16)],
        grid=(B, NSA_GROUPS, nq),
        in_specs=[pl.BlockSpec((1, Tq, NSA_R * HEAD_DIM), lambda b, g, qi: (b, qi, g)),
                  pl.BlockSpec((1, 1, ncp, HEAD_DIM), lambda b, g, qi: (b, g, 0, 0)),
                  pl.BlockSpec((1, 1, ncp, HEAD_DIM), lambda b, g, qi: (b, g, 0, 0)),
                  pl.BlockSpec((nsel, ncp), lambda b, g, qi: (0, 0)),
                  pl.BlockSpec((1, 1, Tq, 16), lambda b, g, qi: (b, g, qi, 0))],
        out_specs=[pl.BlockSpec((1, Tq, NSA_R * HEAD_DIM), lambda b, g, qi: (b, qi, g)),
                   pl.BlockSpec((1, 1, 1, LANES, Tq), lambda b, g, qi: (b, g, qi, 0, 0))],
        compiler_params=_cparams(("arbitrary",) * 3),
        name="nsa_cmpsel",
    )(qn, kc, vc, ovt, gates_g)


def _queries_t(q_ref, Tq):
    return jnp.concatenate([(q_ref[0, :, r * HEAD_DIM:(r + 1) * HEAD_DIM] * QSCALE).T for r in range(NSA_R)],
                           axis=1).astype(BF16)


def _store_gated_t(o_ref, oT, gates, Tq, branch):
    for r in range(NSA_R):
        gcol = gates[:, 3 * r + branch:3 * r + branch + 1]
        o_ref[0, :, r * HEAD_DIM:(r + 1) * HEAD_DIM] = oT[:, r * Tq:(r + 1) * Tq].T * gcol


def _sel_kernel(Tq, tk, q_ref, sn_ref, k_ref, vt_ref, e_ref, gate_ref, o_ref, ka_scr, vt_scr, acc_scr):
    qi = pl.program_id(2)

    @pl.when(qi == 0)
    def _():
        for c in range(ka_scr.shape[0]):
            rows = slice(c * tk, (c + 1) * tk)
            ka_scr[c] = jnp.concatenate([k_ref[0, rows, :].astype(BF16), e_ref[rows, :]], axis=1)
            vt_scr[c] = vt_ref[0, 0, :, rows].astype(BF16)

    sn = sn_ref[0, 0, 0]
    qa = jnp.concatenate([_queries_t(q_ref, Tq), jnp.concatenate([sn] * NSA_R, axis=1)], axis=0)
    R = NSA_R * Tq
    qpos = qi * Tq + (lax.broadcasted_iota(jnp.int32, (tk, R), 1) & (Tq - 1))
    oT = _flash_t_causal(ka_scr, vt_scr, qa, acc_scr, (qi * Tq) // tk, 1, tk, qpos)
    _store_gated_t(o_ref, oT, gate_ref[0, 0], Tq, 1)


def _sel_prompt(qr, selneg, sk, svt, gates_g, B, S, *, Tq, tk):
    nq = S // Tq
    key = np.arange(S)[:, None] // SEL_LEN
    e_all = jnp.asarray((key == np.arange(LANES)[None, :]).astype(np.float32), BF16)
    R = NSA_R * Tq
    return pl.pallas_call(
        functools.partial(_sel_kernel, Tq, tk),
        out_shape=jax.ShapeDtypeStruct((B, S, H_NSA * HEAD_DIM), F32),
        grid=(B, NSA_GROUPS, nq),
        in_specs=[pl.BlockSpec((1, Tq, NSA_R * HEAD_DIM), lambda b, g, qi: (b, qi, g)),
                  pl.BlockSpec((1, 1, 1, LANES, Tq), lambda b, g, qi: (b, g, qi, 0, 0)),
                  pl.BlockSpec((1, S, HEAD_DIM), lambda b, g, qi: (b, 0, g)),
                  pl.BlockSpec((1, 1, HEAD_DIM, S), lambda b, g, qi: (b, g, 0, 0)),
                  pl.BlockSpec((S, LANES), lambda b, g, qi: (0, 0)),
                  pl.BlockSpec((1, 1, Tq, 16), lambda b, g, qi: (b, g, qi, 0))],
        out_specs=pl.BlockSpec((1, Tq, NSA_R * HEAD_DIM), lambda b, g, qi: (b, qi, g)),
        scratch_shapes=[pltpu.VMEM((S // tk, tk, 2 * HEAD_DIM), BF16), pltpu.VMEM((S // tk, HEAD_DIM, tk), BF16),
                        pltpu.VMEM((HEAD_DIM, R), F32)],
        compiler_params=_cparams(("arbitrary",) * 3),
        name="nsa_sel",
    )(qr, selneg, sk, svt, e_all, gates_g)


def _win_kernel(Tq, nwb, q_ref, *refs):
    k_refs, vt_refs = refs[:nwb], refs[nwb:2 * nwb]
    gate_ref, o_ref = refs[2 * nwb:]
    qi = pl.program_id(2)
    R = NSA_R * Tq
    qT = _queries_t(q_ref, Tq)
    qpos = qi * Tq + (lax.broadcasted_iota(jnp.int32, (Tq, R), 1) & (Tq - 1))
    krow = lax.broadcasted_iota(jnp.int32, (Tq, R), 0)
    s = []
    for i in range(nwb):
        kb = qi - (nwb - 1) + i
        si = _dot(k_refs[i][0].astype(BF16), qT)
        if i == 0:
            si = jnp.where(qpos - (kb * Tq + krow) < WINDOW, si, NEG)
        if i == nwb - 1:
            si = jnp.where(kb * Tq + krow <= qpos, si, NEG)
        else:
            si = si + jnp.where(kb >= 0, 0.0, NEG)
        s.append(si)
    m = functools.reduce(jnp.maximum, [jnp.max(si, axis=0, keepdims=True) for si in s])
    p = [jnp.exp2(si - m) for si in s]
    l = sum(jnp.sum(pi, axis=0, keepdims=True) for pi in p)
    oT = sum(_dot(vt_refs[i][0, 0].astype(BF16), p[i].astype(BF16)) for i in range(nwb)) / l
    _store_gated_t(o_ref, oT, gate_ref[0, 0], Tq, 2)


def _win_prompt(qr, wk, wvt, gates_g, B, S, *, Tq):
    nq = S // Tq
    nwb = WINDOW // Tq + 1
    blk = lambda i: (lambda qi: jnp.maximum(qi - (nwb - 1) + i, 0))
    kspec = lambda i: pl.BlockSpec((1, Tq, HEAD_DIM), lambda b, g, qi, f=blk(i): (b, f(qi), g))
    vspec = lambda i: pl.BlockSpec((1, 1, HEAD_DIM, Tq), lambda b, g, qi, f=blk(i): (b, g, 0, f(qi)))
    return pl.pallas_call(
        functools.partial(_win_kernel, Tq, nwb),
        out_shape=jax.ShapeDtypeStruct((B, S, H_NSA * HEAD_DIM), F32),
        grid=(B, NSA_GROUPS, nq),
        in_specs=[pl.BlockSpec((1, Tq, NSA_R * HEAD_DIM), lambda b, g, qi: (b, qi, g))]
        + [kspec(i) for i in range(nwb)] + [vspec(i) for i in range(nwb)]
        + [pl.BlockSpec((1, 1, Tq, 16), lambda b, g, qi: (b, g, qi, 0))],
        out_specs=pl.BlockSpec((1, Tq, NSA_R * HEAD_DIM), lambda b, g, qi: (b, qi, g)),
        compiler_params=_cparams(("arbitrary",) * 3),
        name="nsa_win",
    )(qr, *([wk] * nwb), *([wvt] * nwb), gates_g)


def _hgrn_step_kernel(aq_ref, af_ref, ai_ref, ag_ref, lb_ref, gn_ref, s_ref, o_ref, so_ref):
    lb = lb_ref[0]
    gate = lb + (1.0 - lb) * _sigmoid(af_ref[0, 0])
    s_new = s_ref[0, 0] * gate + (1.0 - gate) * ai_ref[0, 0]
    so_ref[0, 0] = s_new
    o = jnp.sum(_silu(aq_ref[0, 0]) * s_new, axis=0, keepdims=True)
    o_ref[0, 0] = _head_rms(o, gn_ref[...]) * _silu(ag_ref[0, 0])


def _hgrn_step(hg, lb, gn, state):
    D, H = state.shape[:2]
    W = H * HEAD_DIM
    colv = lambda a: a.reshape(D, H, HEAD_DIM, 1)
    rowv = lambda a: a.reshape(D, H, 1, HEAD_DIM)
    cspec = pl.BlockSpec((1, 1, HEAD_DIM, 1), lambda d, h: (d, h, 0, 0))
    rspec = pl.BlockSpec((1, 1, 1, HEAD_DIM), lambda d, h: (d, h, 0, 0))
    sspec = pl.BlockSpec((1, 1, HEAD_DIM, HEAD_DIM), lambda d, h: (d, h, 0, 0))
    o, s_new = pl.pallas_call(
        _hgrn_step_kernel,
        out_shape=[jax.ShapeDtypeStruct((D, H, 1, HEAD_DIM), F32), jax.ShapeDtypeStruct(state.shape, F32)],
        grid=(D, H),
        in_specs=[cspec, cspec, rspec, rspec,
                  pl.BlockSpec((1, HEAD_DIM, 1), lambda d, h: (h, 0, 0)),
                  pl.BlockSpec((1, HEAD_DIM), lambda d, h: (0, 0)), sspec],
        out_specs=[rspec, sspec],
        compiler_params=_cparams(("arbitrary", "arbitrary")),
        name="hgrn_step",
    )(colv(hg[:, :W]), colv(hg[:, W:2 * W]), rowv(hg[:, 2 * W:3 * W]), rowv(hg[:, 3 * W:]),
      lb.reshape(H, HEAD_DIM, 1), gn, state)
    return o.reshape(D, W), s_new


def _fox_step_kernel(PG, pt_ref, q_ref, kn_ref, vn_ref, lfn_ref, *refs):
    k_refs, v_refs, lf_refs = refs[:PG], refs[PG:2 * PG], refs[2 * PG:3 * PG]
    o_ref, m_scr, l_scr, acc_scr, carry = refs[3 * PG:]
    W = H_FOX * PAGE
    q = q_ref[0] * (HEAD_DIM ** -0.5)

    @pl.when(pl.program_id(1) == 0)
    def _():
        m_scr[...] = jnp.sum(q * kn_ref[0], axis=-1, keepdims=True)
        l_scr[...] = jnp.ones_like(l_scr)
        acc_scr[...] = vn_ref[0]
        carry[...] = jnp.broadcast_to(lfn_ref[0], carry.shape)

    qb = q.astype(BF16)
    lane = lax.broadcasted_iota(jnp.int32, (H_FOX, W), 1)
    head = lax.broadcasted_iota(jnp.int32, (H_FOX, W), 0)
    plane = lax.broadcasted_iota(jnp.int32, (PG, W), 1)
    lf = jnp.concatenate([lf_refs[i][0] for i in range(PG)], axis=0)
    suf, tot = lf, lf
    step = H_FOX
    while step < W:
        suf = suf + jnp.where(plane + step < W, pltpu.roll(suf, W - step, 1), 0.0)
        tot = tot + pltpu.roll(tot, step, 1)
        step *= 2
    c = carry[...]
    s = []
    for i in range(PG):
        si = _dot_nt(qb, k_refs[i][0].astype(BF16)) + (c + (suf[i:i + 1] - lf[i:i + 1]))
        s.append(jnp.where((lane & (H_FOX - 1)) == head, si, NEG))
        c = c + tot[i:i + 1]
    carry[...] = c
    m_prev = m_scr[...]
    m_new = functools.reduce(jnp.maximum, [jnp.max(si, axis=-1, keepdims=True) for si in s] + [m_prev])
    alpha = jnp.exp(m_prev - m_new)
    p = [jnp.exp(si - m_new) for si in s]
    l_scr[...] = alpha * l_scr[...] + sum(jnp.sum(pi, axis=-1, keepdims=True) for pi in p)
    acc_scr[...] = alpha * acc_scr[...] + sum(_dot(p[i].astype(BF16), v_refs[i][0].astype(BF16)) for i in range(PG))
    m_scr[...] = m_new

    @pl.when(pl.program_id(1) == pl.num_programs(1) - 1)
    def _():
        o_ref[0] = _online_result(l_scr, acc_scr)


def _fox_step(page_table, q, k_new, v_new, lf_new, cache_k, cache_v, cache_lf, *, PG):
    D, NP = page_table.shape
    n_pool = cache_k.shape[0]
    W = H_FOX * PAGE
    k2 = cache_k.reshape(n_pool, W, HEAD_DIM)
    v2 = cache_v.reshape(n_pool, W, HEAD_DIM)
    lf2 = cache_lf.astype(F32).reshape(n_pool, 1, W)
    lfn = jnp.tile(lf_new, (1, PAGE)).reshape(D, 1, W)
    page = lambda i: (lambda d, j, pt: (pt[d, NP - 1 - (j * PG + i)], 0, 0))
    hspec = pl.BlockSpec((1, H_FOX, HEAD_DIM), lambda d, j, pt: (d, 0, 0))
    in_specs = [hspec, hspec, hspec, pl.BlockSpec((1, 1, W), lambda d, j, pt: (d, 0, 0))]
    in_specs += [pl.BlockSpec((1, W, HEAD_DIM), page(i)) for i in range(PG)]
    in_specs += [pl.BlockSpec((1, W, HEAD_DIM), page(i)) for i in range(PG)]
    in_specs += [pl.BlockSpec((1, 1, W), page(i)) for i in range(PG)]
    return pl.pallas_call(
        functools.partial(_fox_step_kernel, PG),
        out_shape=jax.ShapeDtypeStruct((D, H_FOX, HEAD_DIM), F32),
        grid_spec=pltpu.PrefetchScalarGridSpec(
            num_scalar_prefetch=1, grid=(D, NP // PG), in_specs=in_specs, out_specs=hspec,
            scratch_shapes=[pltpu.VMEM((H_FOX, 1), F32), pltpu.VMEM((H_FOX, 1), F32),
                            pltpu.VMEM((H_FOX, HEAD_DIM), F32), pltpu.VMEM((H_FOX, W), F32)]),
        compiler_params=_cparams(("arbitrary", "arbitrary")),
        name="fox_step",
    )(page_table, q, k_new, v_new, lfn, *([k2] * PG), *([v2] * PG), *([lf2] * PG))


def _cmp_pages_kernel(PG, pt_ref, *refs):
    pages = refs[:PG]
    pea_ref, peb_ref, w1a_ref, w1b_ref, p1_ref, p2_ref = refs[PG:]
    per_page = PAGE // CMP_STRIDE
    parts = []
    for g in range(NSA_GROUPS):
        for i in range(PG):
            parts.append(jnp.concatenate(
                [pages[i][0, pl.ds(r * NSA_GROUPS + g, per_page, stride=CMP_STRIDE * NSA_GROUPS), :]
                 for r in range(CMP_STRIDE)], axis=1))
    R = jnp.concatenate(parts, axis=0)
    shape = (NSA_GROUPS, PG * per_page, HEAD_DIM)
    p1_ref[0] = _dot((R + pea_ref[...]).astype(BF16), w1a_ref[...]).reshape(shape)
    p2_ref[0] = _dot((R + peb_ref[...]).astype(BF16), w1b_ref[...]).reshape(shape)


def _cmp_pages(page_table, cache, cw, *, PG):
    D, NP = page_table.shape
    n_pool = cache.shape[0]
    c2 = cache.reshape(n_pool, PAGE * NSA_GROUPS, HEAD_DIM)
    per_page = PAGE // CMP_STRIDE
    half = CMP_STRIDE * HEAD_DIM
    page = lambda i: (lambda d, j, pt: (pt[d, j * PG + i], 0, 0))
    const = lambda shape: pl.BlockSpec(shape, lambda d, j, pt: (0, 0))
    in_specs = [pl.BlockSpec((1, PAGE * NSA_GROUPS, HEAD_DIM), page(i)) for i in range(PG)]
    in_specs += [const((1, half)), const((1, half)), const((half, HEAD_DIM)), const((half, HEAD_DIM))]
    ospec = pl.BlockSpec((1, NSA_GROUPS, PG * per_page, HEAD_DIM), lambda d, j, pt: (d, 0, j, 0))
    shp = jax.ShapeDtypeStruct((D, NSA_GROUPS, NP * per_page, HEAD_DIM), F32)
    return pl.pallas_call(
        functools.partial(_cmp_pages_kernel, PG),
        out_shape=[shp, shp],
        grid_spec=pltpu.PrefetchScalarGridSpec(
            num_scalar_prefetch=1, grid=(D, NP // PG), in_specs=in_specs, out_specs=[ospec, ospec]),
        compiler_params=_cparams(("arbitrary", "arbitrary")),
        name="nsa_cmp_pages",
    )(page_table, *([c2] * PG), *cw[:4])


def _cmpsel_step_kernel(qpos, nsp, q_ref, p1k_ref, p2k_ref, p1v_ref, p2v_ref, w2k_ref, w2v_ref,
                        ov_ref, g_ref, o_ref, idx_ref):
    nc = p1k_ref.shape[2]

    def finish(p1_ref, p2_ref, w2_ref):
        pre = p1_ref[0, 0] + pltpu.roll(p2_ref[0, 0], nc - 1, 0)
        return _dot(_silu(pre).astype(BF16), w2_ref[...]).astype(BF16)

    kc = finish(p1k_ref, p2k_ref, w2k_ref)
    vc = finish(p1v_ref, p2v_ref, w2v_ref)
    q = (q_ref[0, 0] * (HEAD_DIM ** -0.5)).astype(BF16)
    s = _dot_nt(q, kc)
    cend = lax.broadcasted_iota(jnp.int32, s.shape, 1) * CMP_STRIDE + (CMP_LEN - 1)
    mask = cend <= qpos
    sm = jnp.where(mask, s, NEG)
    e = jnp.where(mask, jnp.exp(sm - jnp.max(sm, axis=-1, keepdims=True)), 0.0)
    den = jnp.sum(e, axis=-1, keepdims=True)
    p = e / jnp.where(den > 0.0, den, 1.0)
    o_ref[0, 0] = _dot(p.astype(BF16), vc) * g_ref[0, 0][:, 0:1]

    psum = jnp.broadcast_to(jnp.sum(p[0:NSA_R], axis=0, keepdims=True), (8, nc))
    ov = ov_ref[...]
    sc_row = sum(_dot(t, ov) for t in _split3(psum))[0:1]
    n_lane = lax.broadcasted_iota(jnp.int32, (1, nsp), 1)
    cur = qpos // SEL_LEN
    forced = (n_lane == 0) | (n_lane == cur) | (n_lane == cur - 1)
    sc_row = jnp.where(n_lane * SEL_LEN <= qpos, jnp.where(forced, -NEG, sc_row), NEG)
    mi = lax.broadcasted_iota(jnp.int32, (nsp, nsp), 0)
    ni = lax.broadcasted_iota(jnp.int32, (nsp, nsp), 1)
    sc_col = jnp.sum(jnp.where(mi == ni, sc_row, 0.0), axis=-1, keepdims=True)
    beats = jnp.where(mi < ni, jnp.where(sc_col >= sc_row, 1.0, 0.0), jnp.where(sc_col > sc_row, 1.0, 0.0))
    rank = jnp.sum(beats, axis=0, keepdims=True)
    lane = lax.broadcasted_iota(jnp.int32, (1, LANES), 1)
    out = jnp.zeros((1, LANES), F32)
    for k in range(TOPN):
        nk = jnp.sum(jnp.where(rank == float(k), n_lane.astype(F32), 0.0), axis=-1, keepdims=True)
        out = jnp.where(lane == k, nk, out)
    idx_ref[0, 0] = jnp.broadcast_to(out, (8, LANES)).astype(jnp.int32)


def _cmpsel_step(q16, p1k, p2k, p1v, p2v, cwk, cwv, gate_rows, qpos):
    D, G, nc = p1k.shape[:3]
    n_sel = -(-(qpos + 1) // SEL_LEN)
    nsp = -(-n_sel // LANES) * LANES
    ov = jnp.asarray(np.pad(_overlap_t(nc, n_sel).T, ((0, 0), (0, nsp - n_sel))), BF16)
    big = pl.BlockSpec((1, 1, nc, HEAD_DIM), lambda d, g: (d, g, 0, 0))
    qspec = pl.BlockSpec((1, 1, 16, HEAD_DIM), lambda d, g: (d, g, 0, 0))
    w2spec = pl.BlockSpec((HEAD_DIM, HEAD_DIM), lambda d, g: (0, 0))
    return pl.pallas_call(
        functools.partial(_cmpsel_step_kernel, qpos, nsp),
        out_shape=[jax.ShapeDtypeStruct((D, G, 16, HEAD_DIM), F32),
                   jax.ShapeDtypeStruct((D, G, 8, LANES), jnp.int32)],
        grid=(D, G),
        in_specs=[qspec, big, big, big, big, w2spec, w2spec,
                  pl.BlockSpec((nc, nsp), lambda d, g: (0, 0)),
                  pl.BlockSpec((1, 1, 16, 3), lambda d, g: (d, g, 0, 0))],
        out_specs=[qspec, pl.BlockSpec((1, 1, 8, LANES), lambda d, g: (d, g, 0, 0))],
        compiler_params=_cparams(("arbitrary", "arbitrary")),
        name="nsa_cmpsel_step",
    )(q16, p1k, p2k, p1v, p2v, cwk[4], cwv[4], ov, gate_rows)


def _sel_step_kernel(n_past, pt_ref, idx_ref, q_ref, kn_ref, vn_ref, *refs):
    k_refs, v_refs = refs[:TOPN], refs[TOPN:2 * TOPN]
    g_ref, o_ref = refs[2 * TOPN:]
    d, g = pl.program_id(0), pl.program_id(1)
    q = q_ref[0, 0] * (HEAD_DIM ** -0.5)
    qb = q.astype(BF16)
    rows = pl.ds(g, SEL_LEN, stride=NSA_GROUPS)
    s_new = jnp.sum(q * kn_ref[0, 0], axis=-1, keepdims=True)
    s = [_dot_nt(qb, k_refs[k][rows, :].astype(BF16)) + jnp.where(idx_ref[d, g, k] < n_past, 0.0, NEG)
         for k in range(TOPN)]
    m = functools.reduce(jnp.maximum, [jnp.max(sk, axis=-1, keepdims=True) for sk in s] + [s_new])
    p = [jnp.exp(sk - m) for sk in s]
    p_new = jnp.exp(s_new - m)
    l = sum(jnp.sum(pk, axis=-1, keepdims=True) for pk in p) + p_new
    o = sum(_dot(p[k].astype(BF16), v_refs[k][rows, :].astype(BF16)) for k in range(TOPN)) + p_new * vn_ref[0, 0]
    o_ref[0, 0] = o / l * g_ref[0, 0][:, 1:2]


def _sel_step(page_table, idx, q16, k_new, v_new, cache_k, cache_v, gate_rows):
    D, NP = page_table.shape
    G = NSA_GROUPS
    n_pool = cache_k.shape[0]
    per_page = PAGE // SEL_LEN
    n_past = NP * per_page
    blk_rows = SEL_LEN * G
    k2 = cache_k.reshape(n_pool * PAGE * G, HEAD_DIM)
    v2 = cache_v.reshape(n_pool * PAGE * G, HEAD_DIM)

    def blk(k):
        def index(d, g, pt, ix):
            n = jnp.minimum(ix[d, g, k], n_past - 1)
            return (pt[d, n // per_page] * per_page + n % per_page, 0)
        return pl.BlockSpec((blk_rows, HEAD_DIM), index)

    qspec = pl.BlockSpec((1, 1, 16, HEAD_DIM), lambda d, g, pt, ix: (d, g, 0, 0))
    nspec = pl.BlockSpec((1, 1, 1, HEAD_DIM), lambda d, g, pt, ix: (d, g, 0, 0))
    return pl.pallas_call(
        functools.partial(_sel_step_kernel, n_past),
        out_shape=jax.ShapeDtypeStruct((D, G, 16, HEAD_DIM), F32),
        grid_spec=pltpu.PrefetchScalarGridSpec(
            num_scalar_prefetch=2, grid=(D, G),
            in_specs=[qspec, nspec, nspec] + [blk(k) for k in range(TOPN)] * 2
            + [pl.BlockSpec((1, 1, 16, 3), lambda d, g, pt, ix: (d, g, 0, 0))],
            out_specs=qspec),
        compiler_params=_cparams(("arbitrary",) * 2),
        name="nsa_sel_step",
    )(page_table, idx, q16, k_new, v_new, *([k2] * TOPN), *([v2] * TOPN), gate_rows)


def _win_step_kernel(q_ref, k_ref, v_ref, g_ref, o_ref):
    q = (q_ref[0, 0] * (HEAD_DIM ** -0.5)).astype(BF16)
    s = _dot_nt(q, k_ref[0].astype(BF16))
    e = jnp.exp(s - jnp.max(s, axis=-1, keepdims=True))
    p = e / jnp.sum(e, axis=-1, keepdims=True)
    o_ref[0, 0] = _dot(p.astype(BF16), v_ref[0].astype(BF16)) * g_ref[0, 0][:, 2:3]


def _win_step(q16, kw, vw, gate_rows):
    D, G = q16.shape[:2]
    L = kw.shape[1]
    qspec = pl.BlockSpec((1, 1, 16, HEAD_DIM), lambda d, g: (d, g, 0, 0))
    kspec = pl.BlockSpec((1, L, HEAD_DIM), lambda d, g: (d, 0, g))
    return pl.pallas_call(
        _win_step_kernel,
        out_shape=jax.ShapeDtypeStruct((D, G, 16, HEAD_DIM), F32),
        grid=(D, G),
        in_specs=[qspec, kspec, kspec, pl.BlockSpec((1, 1, 16, 3), lambda d, g: (d, g, 0, 0))],
        out_specs=qspec,
        compiler_params=_cparams(("arbitrary", "arbitrary")),
        name="nsa_win_step",
    )(q16, kw, vw, gate_rows)


TN = 512
TM = 512
TM_STEP = 8


def _pad_cols(a, n):
    return jnp.pad(a, ((0, 0), (0, n - a.shape[1])))


TN_PROJ = 1024


def _segments(widths_modes):
    segs, start = [], 0
    for width, mode in widths_modes:
        segs.append((start, width, mode))
        start += width
    return segs, start


def _proj_weights(w_in, n_main):
    wb = w_in.astype(BF16)
    return wb, _pad_cols(wb[:, n_main:], LANES)


def _even_params(e_norm_mix, e_w_in, lb, out_norm, f_bias, q_norm, k_norm, e_w_out, e_norm_ffn):
    d = e_w_in.shape[0]
    aw, bw = H_HGRN * HEAD_DIM, H_FOX * HEAD_DIM
    segs, n_main = _segments(((4 * aw, 'raw'), (bw, 'hnorm'), (bw, 'hnorm'), (bw, 'raw')))
    gain = jnp.concatenate([jnp.ones((4 * aw,), F32), jnp.tile(q_norm, H_FOX), jnp.tile(k_norm, H_FOX),
                            jnp.ones((bw,), F32)])
    w_in, w_tail = _proj_weights(e_w_in, n_main)
    return dict(norm=e_norm_mix.reshape(1, d), w_in=w_in, w_tail=w_tail, gain=gain.reshape(1, -1),
                bias=_pad_cols(f_bias.astype(F32).reshape(1, -1), LANES), segs=segs, lb=lb.reshape(1, aw),
                out_norm=out_norm.reshape(1, HEAD_DIM), w_out=e_w_out.astype(BF16), norm_ffn=e_norm_ffn.reshape(1, d))


def _even_proj(x2, p, tm):
    zeros = jnp.zeros((tm, HEAD_DIM), F32)
    return _proj(x2, p['norm'], p['w_in'], p['w_tail'], p['gain'], p['bias'], zeros, zeros, p['segs'], 'logsig',
                 tm=tm, tn=TN_PROJ, name="even_proj")


def _even_finish(x2, oa, of, p, ffn_w, tm, emit=False):
    x2 = _outproj([[oa], [of]], p['w_out'], x2, tm=tm, tn=TN_PROJ, name="even_out")
    return _ffn(x2, p['norm_ffn'], *ffn_w, tm=tm, tf=TN, emit=emit, name="ffn")


def _odd_params(o_norm_mix, o_w_in, q_norm, ck_norm, sk_norm, wk_norm, pe_k, w1_k, w2_k, pe_v, w1_v, w2_v,
                o_w_out, o_norm_ffn, router_w, router_b):
    d = o_w_in.shape[0]
    qw, kvw = H_NSA * HEAD_DIM, NSA_GROUPS * HEAD_DIM
    segs, n_main = _segments(((qw, 'hnorm_both'), (kvw, 'hnorm'), (kvw, 'raw'), (kvw, 'hnorm_rope'), (kvw, 'raw'),
                              (kvw, 'hnorm_rope'), (kvw, 'raw')))
    ones = jnp.ones((kvw,), F32)
    gain = jnp.concatenate([jnp.tile(q_norm, H_NSA), jnp.tile(ck_norm, NSA_GROUPS), ones,
                            jnp.tile(sk_norm, NSA_GROUPS), ones, jnp.tile(wk_norm, NSA_GROUPS), ones])
    w_in, w_tail = _proj_weights(o_w_in, n_main)
    return dict(norm=o_norm_mix.reshape(1, d), w_in=w_in, w_tail=w_tail, gain=gain.reshape(1, -1),
                bias=jnp.zeros((1, LANES), F32), segs=segs,
                cwk=_cmp_weights(pe_k, w1_k, w2_k), cwv=_cmp_weights(pe_v, w1_v, w2_v),
                w_out=o_w_out.astype(BF16), norm_ffn=o_norm_ffn.reshape(1, d),
                rw=_pad_cols(router_w.astype(BF16), LANES), rb=_pad_cols(router_b.reshape(1, -1).astype(F32), LANES))


def _odd_proj(x2, p, pos, tm):
    cosf, sinf = _rope_tables(pos)
    return _proj(x2, p['norm'], p['w_in'], p['w_tail'], p['gain'], p['bias'], cosf, sinf, p['segs'], 'sigmoid',
                 tm=tm, tn=TN_PROJ, name="odd_proj")


def _odd_finish(x2, branches, p, moe_w, tm, emit=False):
    x2 = _outproj([branches], p['w_out'], x2, tm=tm, tn=TN_PROJ, name="odd_out")
    args = (x2, p['norm_ffn'], p['rw'], p['rb'], *moe_w)
    if emit:
        return _moe(*args, tm=tm, tf=256, emit=True, name="moe")
    return _moe_routed(*args, tm=tm)


def kernel(x_prompt, x_sample, state_hgrn, cache_fox_k, cache_fox_v, cache_fox_logf, cache_nsa_cmp_k, cache_nsa_cmp_v, cache_nsa_sel_k, cache_nsa_sel_v, cache_nsa_win_k, cache_nsa_win_v, page_table, e_norm_mix, e_w_in, hgrn_lb_logits, hgrn_out_norm, fox_f_bias, fox_q_norm, fox_k_norm, e_w_out, e_norm_ffn, ffn_w_gate, ffn_w_up, ffn_w_down, o_norm_mix, o_w_in, nsa_q_norm, nsa_cmp_k_norm, nsa_sel_k_norm, nsa_win_k_norm, cmp_pe_k, cmp_w1_k, cmp_w2_k, cmp_pe_v, cmp_w1_v, cmp_w2_v, o_w_out, o_norm_ffn, router_w, router_b, moe_w_gate, moe_w_up, moe_w_down):
    B, S, d = x_prompt.shape
    D, T, _ = x_sample.shape
    NP = page_table.shape[1]
    past = NP * PAGE
    w_buf = cache_nsa_win_k.shape[2]
    assert T == 1 and w_buf == WINDOW and S >= WINDOW and cache_fox_k.shape[2] == PAGE
    G = NSA_GROUPS
    lbs = jnp.cumsum(jax.nn.softmax(hgrn_lb_logits.astype(F32), axis=0), axis=0)
    xp, xd = x_prompt.reshape(B * S, d), x_sample.reshape(D, d)
    tm_p = min(TM, S)

    li = 0
    p = _even_params(e_norm_mix[li], e_w_in[li], lbs[li], hgrn_out_norm[li], fox_f_bias[li], fox_q_norm[li],
                     fox_k_norm[li], e_w_out[li], e_norm_ffn[li])
    hg, fq, fk, fv, fl = _even_proj(xd, p, TM_STEP)
    oa, st_d = _hgrn_step(hg, p['lb'], p['out_norm'], state_hgrn[li].astype(F32))
    heads = lambda a: a.reshape(D, H_FOX, HEAD_DIM)
    of = _fox_step(page_table, heads(fq), heads(fk), heads(fv), fl[:, :H_FOX],
                   cache_fox_k[li], cache_fox_v[li], cache_fox_logf[li], PG=8)
    xd, ffn_bf16 = _even_finish(xd, oa, of.reshape(D, H_FOX * HEAD_DIM), p,
                                (ffn_w_gate[li], ffn_w_up[li], ffn_w_down[li]), TM_STEP, emit=True)
    fox_d = (fk.reshape(1, D, 1, H_FOX, HEAD_DIM), fv.reshape(1, D, 1, H_FOX, HEAD_DIM),
             fl[:, :H_FOX].reshape(1, D, 1, H_FOX))

    hg, fq, fk, fv, fl = _even_proj(xp, p, tm_p)
    aug_q, aug_k = _seq_cumsum(fl, B, S, tc=tm_p)
    oa, st_p = _hgrn_prompt(hg, p['lb'], p['out_norm'], B, S, T=min(256, S), c=32)
    fvt = fv.reshape(B, S, H_FOX, HEAD_DIM).transpose(0, 2, 3, 1)
    of = _fox_prompt(fq, fk, fvt, aug_q, aug_k, B, S, tq=min(1024, S), tk=tm_p)
    xp = _even_finish(xp, oa, of, p, ffn_bf16, tm_p)
    fox_p = (fk.reshape(1, B, S, H_FOX, HEAD_DIM), fv.reshape(1, B, S, H_FOX, HEAD_DIM),
             fl[:, :H_FOX].reshape(1, B, S, H_FOX))

    p = _odd_params(o_norm_mix[li], o_w_in[li], nsa_q_norm[li], nsa_cmp_k_norm[li], nsa_sel_k_norm[li],
                    nsa_win_k_norm[li], cmp_pe_k[li], cmp_w1_k[li], cmp_w2_k[li], cmp_pe_v[li], cmp_w1_v[li],
                    cmp_w2_v[li], o_w_out[li], o_norm_ffn[li], router_w[li], router_b[li])
    qn, qr, ck, cv, sk, sv, wk, wv, gt = _odd_proj(xd, p, jnp.full((TM_STEP,), past), TM_STEP)
    rows16 = lambda a, w: jnp.pad(a.reshape(D, G, NSA_R, w), ((0, 0), (0, 0), (0, 16 - NSA_R), (0, 0)))
    gate_rows = rows16(gt[:, :3 * H_NSA], 3)
    q16n, q16r = rows16(qn, HEAD_DIM), rows16(qr, HEAD_DIM)
    p1k, p2k = _cmp_pages(page_table, cache_nsa_cmp_k[li], p['cwk'], PG=16)
    p1v, p2v = _cmp_pages(page_table, cache_nsa_cmp_v[li], p['cwv'], PG=16)
    o_cmp, idx = _cmpsel_step(q16n, p1k, p2k, p1v, p2v, p['cwk'], p['cwv'], gate_rows, past)
    new = lambda a: a.reshape(D, G, 1, HEAD_DIM)
    o_sel = _sel_step(page_table, idx[:, :, 0, :TOPN], q16r, new(sk), new(sv),
                      cache_nsa_sel_k[li], cache_nsa_sel_v[li], gate_rows)
    kvd = lambda a: a.reshape(D, 1, G, HEAD_DIM)
    win_k = jnp.concatenate([cache_nsa_win_k[li], kvd(wk)], axis=1)[:, -w_buf:]
    win_v = jnp.concatenate([cache_nsa_win_v[li], kvd(wv)], axis=1)[:, -w_buf:]
    o_win = _win_step(q16r, win_k.reshape(D, w_buf, G * HEAD_DIM), win_v.reshape(D, w_buf, G * HEAD_DIM), gate_rows)
    unrow = lambda a: a[:, :, :NSA_R].reshape(D, H_NSA * HEAD_DIM)
    xd, moe_bf16 = _odd_finish(xd, [unrow(o_cmp), unrow(o_sel), unrow(o_win)], p,
                               (moe_w_gate[li], moe_w_up[li], moe_w_down[li]), TM_STEP, emit=True)
    kv1 = lambda a: a.reshape(1, D, 1, G, HEAD_DIM)
    nsa_d = (kv1(ck), kv1(cv), kv1(sk), kv1(sv), win_k[None], win_v[None])

    qn, qr, ck, cv, sk, sv, wk, wv, gt = _odd_proj(xp, p, jnp.arange(S), tm_p)
    seq = lambda a: a.reshape(B, S, -1)
    gates_g = jnp.pad(gt[:, :3 * H_NSA].reshape(B, S, G, 3 * NSA_R).transpose(0, 2, 1, 3),
                      ((0, 0), (0, 0), (0, 0), (0, 16 - 3 * NSA_R)))
    kc = _compress_prompt(seq(ck), p['cwk'], B, S)
    vc = _compress_prompt(seq(cv), p['cwv'], B, S)
    tq = min(256, S)
    o_cmp, selneg = _cmpsel_prompt(seq(qn), kc, vc, gates_g, B, S, Tq=tq)
    keys_last = lambda a: a.reshape(B, S, G, HEAD_DIM).transpose(0, 2, 3, 1)
    o_sel = _sel_prompt(seq(qr), selneg, seq(sk), keys_last(sv), gates_g, B, S, Tq=tq, tk=min(512, S))
    o_win = _win_prompt(seq(qr), seq(wk), keys_last(wv), gates_g, B, S, Tq=tq)
    flat = lambda a: a.reshape(B * S, -1)
    xp = _odd_finish(xp, [flat(o_cmp), flat(o_sel), flat(o_win)], p, moe_bf16, tm_p)
    kv = lambda a: a.reshape(1, B, S, G, HEAD_DIM)
    nsa_p = (kv(ck), kv(cv), kv(sk), kv(sv), kv(wk)[:, :, S - WINDOW:], kv(wv)[:, :, S - WINDOW:])

    return (xp.reshape(B, S, d), xd.reshape(D, T, d), st_p[None].astype(state_hgrn.dtype),
            st_d[None].astype(state_hgrn.dtype), *fox_p, *fox_d, *nsa_p, *nsa_d)
```

```python
import functools

import numpy as np
import jax
import jax.numpy as jnp
from jax import lax
from jax.experimental import pallas as pl
from jax.experimental.pallas import tpu as pltpu

F32 = jnp.float32
BF16 = jnp.bfloat16

LANES = 128
HEAD_DIM = 128
EPS = 1e-6
ROPE_THETA = 10000.0
LOG2E = 1.4426950408889634
QSCALE = HEAD_DIM ** -0.5 * LOG2E
NEG = -1e30
M_INIT = -1e20
VMEM_LIMIT = 56 * 1024 * 1024

H_HGRN = 8
H_FOX = 8
H_NSA = 16
NSA_GROUPS = 4
NSA_R = H_NSA // NSA_GROUPS
CMP_STRIDE = 16
CMP_LEN = 32
SEL_LEN = 64
SEL_SHIFT = 6
TOPN = 16
WINDOW = 512
N_EXPERTS = 8
PAGE = 128


def _cparams(sem):
    return pltpu.CompilerParams(dimension_semantics=sem, vmem_limit_bytes=VMEM_LIMIT)


def _dot(a, b):
    return jnp.dot(a, b, preferred_element_type=F32)


def _dot_nt(a, b):
    return lax.dot_general(a, b, (((1,), (1,)), ((), ())), preferred_element_type=F32)


def _dot_tn(a, b):
    return lax.dot_general(a, b, (((0,), (0,)), ((), ())), preferred_element_type=F32)


def _split3(x):
    hi = x.astype(BF16)
    r1 = x - hi.astype(F32)
    mid = r1.astype(BF16)
    lo = (r1 - mid.astype(F32)).astype(BF16)
    return hi, mid, lo


def _dot3(w_bf16, x):
    hi, mid, lo = _split3(x)
    return _dot(w_bf16, hi) + _dot(w_bf16, mid) + _dot(w_bf16, lo)


def _sigmoid(x):
    return 1.0 / (1.0 + jnp.exp(-x))


def _silu(x):
    return x * _sigmoid(x)


def _log_sigmoid(x):
    return jnp.minimum(x, 0.0) - jnp.log(1.0 + jnp.exp(-jnp.abs(x)))


def _tril_bf16(n):
    r = lax.broadcasted_iota(jnp.int32, (n, n), 0)
    c = lax.broadcasted_iota(jnp.int32, (n, n), 1)
    return jnp.where(c <= r, 1.0, 0.0).astype(BF16)


def _head_rms(a, gain):
    ms = jnp.mean(a * a, axis=-1, keepdims=True)
    return a * lax.rsqrt(ms + EPS) * gain


def _rope(y, cosf, sinf):
    return y * cosf + pltpu.roll(y, HEAD_DIM // 2, 1) * sinf


def _rope_tables(pos):
    half = HEAD_DIM // 2
    inv = ROPE_THETA ** (-jnp.arange(half, dtype=F32) / half)
    ang = pos.astype(F32)[:, None] * inv[None, :]
    cos, sin = jnp.cos(ang), jnp.sin(ang)
    return jnp.concatenate([cos, cos], axis=-1), jnp.concatenate([-sin, sin], axis=-1)


PROJ_PIECE = 512


def _n_outs(mode):
    return 2 if mode == 'hnorm_both' else 1


def _seg_block(width, tn):
    return min(width, tn)


def _proj_kernel(segs, tail_mode, tn, *refs):
    n_out = sum(_n_outs(s[2]) for s in segs)
    x_ref, g_ref, w_ref, wt_ref, gain_ref, bias_ref, cos_ref, sin_ref = refs[:8]
    out_refs = refs[8:8 + n_out]
    tail_ref = refs[8 + n_out]
    xn_ref = refs[9 + n_out]
    j = pl.program_id(1)

    @pl.when(j == 0)
    def _():
        xf = x_ref[...]
        ms = jnp.mean(xf * xf, axis=-1, keepdims=True)
        xn_ref[...] = (xf * lax.rsqrt(ms + EPS) * g_ref[...]).astype(BF16)

    def emit(mode, outs, a, b, dst):
        for p0 in range(a, b, PROJ_PIECE):
            p1 = min(p0 + PROJ_PIECE, b)
            acc = _dot(xn_ref[...], w_ref[:, p0:p1])
            for h in range((p1 - p0) // HEAD_DIM):
                sub = slice(h * HEAD_DIM, (h + 1) * HEAD_DIM)
                src = slice(p0 + h * HEAD_DIM, p0 + (h + 1) * HEAD_DIM)
                to = slice(dst + p0 - a + h * HEAD_DIM, dst + p0 - a + (h + 1) * HEAD_DIM)
                if mode == 'raw':
                    outs[0][:, to] = acc[:, sub]
                    continue
                y = _head_rms(acc[:, sub], gain_ref[:, src])
                if mode in ('hnorm', 'hnorm_both'):
                    outs[0][:, to] = y
                if mode in ('hnorm_rope', 'hnorm_both'):
                    outs[-1][:, to] = _rope(y, cos_ref[...], sin_ref[...])

    ntiles = pl.num_programs(1)
    for tile in range((segs[-1][0] + segs[-1][1]) // tn):
        @pl.when(j == tile)
        def _(tile=tile):
            oi = 0
            for (start, width, mode) in segs:
                outs = out_refs[oi:oi + _n_outs(mode)]
                oi += _n_outs(mode)
                a, b = max(start, tile * tn), min(start + width, (tile + 1) * tn)
                if a < b:
                    emit(mode, outs, a - tile * tn, b - tile * tn, (a - start) % _seg_block(width, tn))

    @pl.when(j == ntiles - 1)
    def _():
        t = _dot(xn_ref[...], wt_ref[...]) + bias_ref[...]
        tail_ref[...] = _sigmoid(t) if tail_mode == 'sigmoid' else _log_sigmoid(t)


def _proj(x, norm_g, w, w_tail, gain_all, bias_tail, cosf, sinf, segs, tail_mode, *, tm, tn, name):
    M, K = x.shape
    n_main = segs[-1][0] + segs[-1][1]
    assert M % tm == 0 and n_main % tn == 0
    nseq = cosf.shape[0] // tm
    out_shape, out_specs = [], []
    for (start, width, mode) in segs:
        bw = _seg_block(width, tn)
        assert width % bw == 0 and start % bw == 0
        for _ in range(_n_outs(mode)):
            out_shape.append(jax.ShapeDtypeStruct((M, width), F32))
            out_specs.append(pl.BlockSpec(
                (tm, bw), lambda i, j, s=start, bw=bw, n=width // bw: (i, jnp.clip((j * tn - s) // bw, 0, n - 1))))
    out_shape.append(jax.ShapeDtypeStruct((M, LANES), F32))
    out_specs.append(pl.BlockSpec((tm, LANES), lambda i, j: (i, 0)))
    return pl.pallas_call(
        functools.partial(_proj_kernel, tuple(segs), tail_mode, tn),
        out_shape=out_shape,
        grid=(M // tm, n_main // tn),
        in_specs=[
            pl.BlockSpec((tm, K), lambda i, j: (i, 0)),
            pl.BlockSpec((1, K), lambda i, j: (0, 0)),
            pl.BlockSpec((K, tn), lambda i, j: (0, j)),
            pl.BlockSpec((K, LANES), lambda i, j: (0, 0)),
            pl.BlockSpec((1, tn), lambda i, j: (0, j)),
            pl.BlockSpec((1, LANES), lambda i, j: (0, 0)),
            pl.BlockSpec((tm, HEAD_DIM), lambda i, j: (i % nseq, 0)),
            pl.BlockSpec((tm, HEAD_DIM), lambda i, j: (i % nseq, 0)),
        ],
        out_specs=out_specs,
        scratch_shapes=[pltpu.VMEM((tm, K), BF16)],
        compiler_params=_cparams(("arbitrary", "arbitrary")),
        name=name,
    )(x, norm_g, w, w_tail, gain_all, bias_tail, cosf, sinf)


def _outproj_kernel(group_sizes, *refs):
    n_lhs = sum(group_sizes)
    lhs = refs[:n_lhs]
    w_ref, res_ref, o_ref, xs_ref = refs[n_lhs:n_lhs + 4]

    @pl.when(pl.program_id(1) == 0)
    def _():
        k0, r0 = 0, 0
        for n in group_sizes:
            x = lhs[r0][...]
            for r in lhs[r0 + 1:r0 + n]:
                x = x + r[...]
            xs_ref[:, k0:k0 + x.shape[1]] = x.astype(BF16)
            k0, r0 = k0 + x.shape[1], r0 + n

    o_ref[...] = res_ref[...] + _dot(xs_ref[...], w_ref[...])


def _outproj(lhs_groups, w, res, *, tm, tn, name):
    M = res.shape[0]
    K, N = w.shape
    flat = [a for g in lhs_groups for a in g]
    assert sum(g[0].shape[1] for g in lhs_groups) == K
    return pl.pallas_call(
        functools.partial(_outproj_kernel, tuple(len(g) for g in lhs_groups)),
        out_shape=jax.ShapeDtypeStruct((M, N), F32),
        grid=(M // tm, N // tn),
        in_specs=[pl.BlockSpec((tm, a.shape[1]), lambda i, j: (i, 0)) for a in flat] + [
            pl.BlockSpec((K, tn), lambda i, j: (0, j)),
            pl.BlockSpec((tm, tn), lambda i, j: (i, j)),
        ],
        out_specs=pl.BlockSpec((tm, tn), lambda i, j: (i, j)),
        scratch_shapes=[pltpu.VMEM((tm, K), BF16)],
        compiler_params=_cparams(("arbitrary", "arbitrary")),
        name=name,
    )(*flat, w, res)


def _ffn_kernel(emit, x_ref, g_ref, wg_ref, wu_ref, wd_ref, o_ref, *rest):
    xn_ref = rest[-1]
    f = pl.program_id(1)

    @pl.when(f == 0)
    def _():
        xf = x_ref[...]
        ms = jnp.mean(xf * xf, axis=-1, keepdims=True)
        xn_ref[...] = (xf * lax.rsqrt(ms + EPS) * g_ref[...]).astype(BF16)
        o_ref[...] = xf

    xn = xn_ref[...]
    wg, wu, wd = wg_ref[...].astype(BF16), wu_ref[...].astype(BF16), wd_ref[...].astype(BF16)
    if emit:
        rest[0][...], rest[1][...], rest[2][...] = wg, wu, wd
    h = _silu(_dot(xn, wg)) * _dot(xn, wu)
    o_ref[...] += _dot(h.astype(BF16), wd)


def _ffn(x, norm_g, wg, wu, wd, *, tm, tf, emit=False, name):
    M, K = x.shape
    F = wg.shape[1]
    assert not emit or M == tm
    w_specs = [pl.BlockSpec((K, tf), lambda i, f: (0, f)), pl.BlockSpec((K, tf), lambda i, f: (0, f)),
               pl.BlockSpec((tf, K), lambda i, f: (f, 0))]
    out_shape = [jax.ShapeDtypeStruct((M, K), F32)]
    out_specs = [pl.BlockSpec((tm, K), lambda i, f: (i, 0))]
    if emit:
        out_shape += [jax.ShapeDtypeStruct(w.shape, BF16) for w in (wg, wu, wd)]
        out_specs += w_specs
    out = pl.pallas_call(
        functools.partial(_ffn_kernel, emit),
        out_shape=out_shape,
        grid=(M // tm, F // tf),
        in_specs=[pl.BlockSpec((tm, K), lambda i, f: (i, 0)), pl.BlockSpec((1, K), lambda i, f: (0, 0))] + w_specs,
        out_specs=out_specs,
        scratch_shapes=[pltpu.VMEM((tm, K), BF16)],
        compiler_params=_cparams(("arbitrary", "arbitrary")),
        name=name,
    )(x, norm_g, wg, wu, wd)
    return (out[0], tuple(out[1:])) if emit else out[0]


def _moe_kernel(emit, x_ref, g_ref, rw_ref, rb_ref, wg_ref, wu_ref, wd_ref, o_ref, *rest):
    xn_ref, gate_ref = rest[-2:]
    e = pl.program_id(1)
    f = pl.program_id(2)

    @pl.when((e == 0) & (f == 0))
    def _():
        xf = x_ref[...]
        ms = jnp.mean(xf * xf, axis=-1, keepdims=True)
        xn = (xf * lax.rsqrt(ms + EPS) * g_ref[...]).astype(BF16)
        xn_ref[...] = xn
        o_ref[...] = xf
        lane = lax.broadcasted_iota(jnp.int32, (xf.shape[0], LANES), 1)
        logits = jnp.where(lane < N_EXPERTS, _dot(xn, rw_ref[...]) + rb_ref[...], NEG)
        m1 = jnp.max(logits, axis=-1, keepdims=True)
        i1 = jnp.min(jnp.where(logits == m1, lane, LANES), axis=-1, keepdims=True)
        l2 = jnp.where(lane == i1, NEG, logits)
        m2 = jnp.max(l2, axis=-1, keepdims=True)
        i2 = jnp.min(jnp.where(l2 == m2, lane, LANES), axis=-1, keepdims=True)
        e2 = jnp.exp(m2 - m1)
        w1 = 1.0 / (1.0 + e2)
        gate_ref[...] = jnp.where(lane == i1, w1, 0.0) + jnp.where(lane == i2, e2 * w1, 0.0)

    xn = xn_ref[...]
    wg, wu, wd = wg_ref[0].astype(BF16), wu_ref[0].astype(BF16), wd_ref[0].astype(BF16)
    if emit:
        rest[0][0], rest[1][0], rest[2][0] = wg, wu, wd
    h = _silu(_dot(xn, wg)) * _dot(xn, wu)
    lane = lax.broadcasted_iota(jnp.int32, gate_ref.shape, 1)
    ge = jnp.sum(jnp.where(lane == e, gate_ref[...], 0.0), axis=-1, keepdims=True)
    o_ref[...] += ge * _dot(h.astype(BF16), wd)


def _moe(x, norm_g, rw, rb, wg, wu, wd, *, tm, tf, emit=False, name):
    M, K = x.shape
    E, _, F = wg.shape
    assert not emit or M == tm
    w_specs = [pl.BlockSpec((1, K, tf), lambda i, e, f: (e, 0, f)), pl.BlockSpec((1, K, tf), lambda i, e, f: (e, 0, f)),
               pl.BlockSpec((1, tf, K), lambda i, e, f: (e, f, 0))]
    out_shape = [jax.ShapeDtypeStruct((M, K), F32)]
    out_specs = [pl.BlockSpec((tm, K), lambda i, e, f: (i, 0))]
    if emit:
        out_shape += [jax.ShapeDtypeStruct(w.shape, BF16) for w in (wg, wu, wd)]
        out_specs += w_specs
    out = pl.pallas_call(
        functools.partial(_moe_kernel, emit),
        out_shape=out_shape,
        grid=(M // tm, E, F // tf),
        in_specs=[
            pl.BlockSpec((tm, K), lambda i, e, f: (i, 0)),
            pl.BlockSpec((1, K), lambda i, e, f: (0, 0)),
            pl.BlockSpec((K, LANES), lambda i, e, f: (0, 0)),
            pl.BlockSpec((1, LANES), lambda i, e, f: (0, 0)),
        ] + w_specs,
        out_specs=out_specs,
        scratch_shapes=[pltpu.VMEM((tm, K), BF16), pltpu.VMEM((tm, LANES), F32)],
        compiler_params=_cparams(("arbitrary", "arbitrary", "arbitrary")),
        name=name,
    )(x, norm_g, rw, rb, wg, wu, wd)
    return (out[0], tuple(out[1:])) if emit else out[0]


TM_E = 512
EXPERT_ISSUE_STEPS = 8


def _top2(logits, lane):
    m1 = jnp.max(logits, axis=-1, keepdims=True)
    i1 = jnp.min(jnp.where(logits == m1, lane, LANES), axis=-1, keepdims=True)
    l2 = jnp.where(lane == i1, NEG, logits)
    m2 = jnp.max(l2, axis=-1, keepdims=True)
    i2 = jnp.min(jnp.where(l2 == m2, lane, LANES), axis=-1, keepdims=True)
    e2 = jnp.exp(m2 - m1)
    w1 = 1.0 / (1.0 + e2)
    return i1, i2, w1, e2 * w1


def _route_kernel(x_ref, g_ref, rw_ref, rb_ref, xn_ref, route_ref, cnt_ref, carry):
    @pl.when(pl.program_id(0) == 0)
    def _():
        carry[...] = jnp.zeros_like(carry)

    xf = x_ref[...]
    tm = xf.shape[0]
    xn = xf * lax.rsqrt(jnp.mean(xf * xf, axis=-1, keepdims=True) + EPS) * g_ref[...]
    xn_ref[...] = xn
    lane = lax.broadcasted_iota(jnp.int32, (tm, LANES), 1)
    logits = jnp.where(lane < N_EXPERTS, _dot(xn.astype(BF16), rw_ref[...]) + rb_ref[...], NEG)
    i1, i2, w1, w2 = _top2(logits, lane)
    hit = jnp.where(lane == i1, 1.0, 0.0) + jnp.where(lane == i2, 1.0, 0.0)
    incl = _dot(_tril_bf16(tm), hit.astype(BF16)) + carry[...]
    before = incl - hit
    carry[...] = incl[tm - 1:tm, :]
    cnt_ref[...] = incl[tm - 1:tm, :]
    r1 = jnp.sum(jnp.where(lane == i1, before, 0.0), axis=-1, keepdims=True)
    r2 = jnp.sum(jnp.where(lane == i2, before, 0.0), axis=-1, keepdims=True)
    cols = (i1.astype(F32), i2.astype(F32), r1, r2, w1, w2)
    rec = jnp.zeros((tm, LANES), F32)
    for c, v in enumerate(cols):
        rec = jnp.where(lane == c, v, rec)
    route_ref[...] = rec


def _route(x, norm_g, rw, rb, *, tm):
    M, K = x.shape
    return pl.pallas_call(
        _route_kernel,
        out_shape=[jax.ShapeDtypeStruct((M, K), F32), jax.ShapeDtypeStruct((M, LANES), F32),
                   jax.ShapeDtypeStruct((1, LANES), F32)],
        grid=(M // tm,),
        in_specs=[pl.BlockSpec((tm, K), lambda i: (i, 0)), pl.BlockSpec((1, K), lambda i: (0, 0)),
                  pl.BlockSpec((K, LANES), lambda i: (0, 0)), pl.BlockSpec((1, LANES), lambda i: (0, 0))],
        out_specs=[pl.BlockSpec((tm, K), lambda i: (i, 0)), pl.BlockSpec((tm, LANES), lambda i: (i, 0)),
                   pl.BlockSpec((1, LANES), lambda i: (0, 0))],
        scratch_shapes=[pltpu.VMEM((1, LANES), F32)],
        compiler_params=_cparams(("arbitrary",)),
        name="moe_route",
    )(x, norm_g, rw, rb)


def _gather_start(src_hbm, idx_ref, first, n, dst, sem, unroll):
    def body(r, carry):
        pltpu.make_async_copy(src_hbm.at[pl.ds(idx_ref[first + r], 1)], dst.at[pl.ds(r, 1)], sem).start()
        return carry
    lax.fori_loop(0, n, body, 0, unroll=unroll)


def _gather_wait(dst, sem):
    pltpu.make_async_copy(dst, dst, sem).wait()


def _experts_kernel(te_ref, nu_ref, src_ref, x_hbm, wg_ref, wu_ref, wd_ref, o_ref, xbuf, xb_scr, acc_scr, sems):
    i, f = pl.program_id(0), pl.program_id(1)
    last = pl.num_programs(1) - 1
    live = i < nu_ref[0]
    slot = i % 2

    @pl.when(live & (f == 0))
    def _():
        @pl.when(i == 0)
        def _():
            _gather_start(x_hbm, src_ref, 0, TM_E, xbuf.at[0], sems.at[0], 8)

        _gather_wait(xbuf.at[slot], sems.at[slot])
        xb_scr[...] = xbuf[slot].astype(BF16)
        acc_scr[...] = jnp.zeros_like(acc_scr)

    @pl.when(live)
    def _():
        per_step = TM_E // EXPERT_ISSUE_STEPS
        first = (f - 1) * per_step

        @pl.when((i + 1 < nu_ref[0]) & (f >= 1) & (f <= EXPERT_ISSUE_STEPS))
        def _():
            for r in range(per_step):
                pltpu.make_async_copy(x_hbm.at[pl.ds(src_ref[(i + 1) * TM_E + first + r], 1)],
                                      xbuf.at[1 - slot, pl.ds(first + r, 1)], sems.at[1 - slot]).start()

        xb = xb_scr[...]
        h = _silu(_dot(xb, wg_ref[0])) * _dot(xb, wu_ref[0])
        acc_scr[...] += _dot(h.astype(BF16), wd_ref[0])

    @pl.when(f == last)
    def _():
        o_ref[...] = jnp.where(live, acc_scr[...], 0.0)


def _experts(tile_expert, n_used, src, xn, wg, wu, wd, *, tf):
    P = src.shape[0]
    E, K, F = wg.shape
    nf = F // tf
    col = lambda i, f, nu: jnp.where(i < nu[0], f, nf - 1)
    return pl.pallas_call(
        _experts_kernel,
        out_shape=jax.ShapeDtypeStruct((P, K), F32),
        grid_spec=pltpu.PrefetchScalarGridSpec(
            num_scalar_prefetch=3, grid=(P // TM_E, nf),
            in_specs=[pl.BlockSpec(memory_space=pl.ANY),
                      pl.BlockSpec((1, K, tf), lambda i, f, te, nu, sr: (te[i], 0, col(i, f, nu))),
                      pl.BlockSpec((1, K, tf), lambda i, f, te, nu, sr: (te[i], 0, col(i, f, nu))),
                      pl.BlockSpec((1, tf, K), lambda i, f, te, nu, sr: (te[i], col(i, f, nu), 0))],
            out_specs=pl.BlockSpec((TM_E, K), lambda i, f, te, nu, sr: (i, 0)),
            scratch_shapes=[pltpu.VMEM((2, TM_E, K), F32), pltpu.VMEM((TM_E, K), BF16), pltpu.VMEM((TM_E, K), F32),
                            pltpu.SemaphoreType.DMA((2,))]),
        compiler_params=_cparams(("arbitrary", "arbitrary")),
        name="moe_experts",
    )(tile_expert, n_used, src, xn, wg, wu, wd)


def _combine_kernel(tc, dest_ref, os_hbm, x_ref, route_ref, o_ref, buf, sems):
    i = pl.program_id(0)
    slot = i % 2

    @pl.when(i == 0)
    def _():
        _gather_start(os_hbm, dest_ref, 0, 2 * tc, buf.at[0], sems.at[0], 8)

    @pl.when(i + 1 < pl.num_programs(0))
    def _():
        _gather_start(os_hbm, dest_ref, (i + 1) * 2 * tc, 2 * tc, buf.at[1 - slot], sems.at[1 - slot], 8)

    _gather_wait(buf.at[slot], sems.at[slot])
    w1, w2 = route_ref[:, 4:5], route_ref[:, 5:6]
    o_ref[...] = x_ref[...] + (w1 * buf[slot, 0:tc, :] + w2 * buf[slot, tc:2 * tc, :])


def _combine(dest, os, x, route, *, tc):
    M, K = x.shape
    return pl.pallas_call(
        functools.partial(_combine_kernel, tc),
        out_shape=jax.ShapeDtypeStruct((M, K), F32),
        grid_spec=pltpu.PrefetchScalarGridSpec(
            num_scalar_prefetch=1, grid=(M // tc,),
            in_specs=[pl.BlockSpec(memory_space=pl.ANY),
                      pl.BlockSpec((tc, K), lambda i, d: (i, 0)), pl.BlockSpec((tc, LANES), lambda i, d: (i, 0))],
            out_specs=pl.BlockSpec((tc, K), lambda i, d: (i, 0)),
            scratch_shapes=[pltpu.VMEM((2, 2 * tc, K), F32), pltpu.SemaphoreType.DMA((2,))]),
        compiler_params=_cparams(("arbitrary",)),
        name="moe_combine",
    )(dest, os, x, route)


def _moe_routed(x, norm_g, rw, rb, wg, wu, wd, *, tm):
    M = x.shape[0]
    xn, route, counts = _route(x, norm_g, rw, rb, tm=tm)
    e = route[:, 0:2].astype(jnp.int32)
    rank = route[:, 2:4].astype(jnp.int32)
    cnt = counts[0, :N_EXPERTS].astype(jnp.int32)
    tiles = (cnt + TM_E - 1) // TM_E
    tile_end = jnp.cumsum(tiles)
    dest = ((tile_end - tiles) * TM_E)[e] + rank
    n_tiles = (2 * M) // TM_E + N_EXPERTS
    P = n_tiles * TM_E
    token = jnp.repeat(jnp.arange(M, dtype=jnp.int32), 2)
    src = jnp.zeros((P,), jnp.int32).at[dest.reshape(-1)].set(token)
    tile_expert = jnp.minimum(jnp.sum(tile_end[None, :] <= jnp.arange(n_tiles, dtype=jnp.int32)[:, None], axis=1),
                              N_EXPERTS - 1).astype(jnp.int32)
    os = _experts(tile_expert, tile_end[-1:].astype(jnp.int32), src, xn, wg, wu, wd, tf=256)
    tc = 256
    dest_tiles = dest.reshape(M // tc, tc, 2).transpose(0, 2, 1).reshape(-1).astype(jnp.int32)
    return _combine(dest_tiles, os, x, route, tc=tc)


HGRN_HEADS_PER_STEP = 4


def _hgrn_kernel(T, c, aq_ref, af_ref, ai_ref, ag_ref, lb_ref, gn_ref, o_ref, st_ref, st_scr):
    t = pl.program_id(2)

    @pl.when(t == 0)
    def _():
        st_scr[...] = jnp.zeros_like(st_scr)

    row = lax.broadcasted_iota(jnp.int32, (T, T), 0)
    col = lax.broadcasted_iota(jnp.int32, (T, T), 1)
    shift = c.bit_length() - 1
    same = (row >> shift) == (col >> shift)
    causal = same & (col <= row)
    tri = jnp.where(causal, 1.0, 0.0).astype(BF16)
    blk = jnp.where(same, 1.0, 0.0).astype(BF16)
    gn = gn_ref[...]
    for hh in range(HGRN_HEADS_PER_STEP):
        cols = slice(hh * HEAD_DIM, (hh + 1) * HEAD_DIM)
        lb = lb_ref[:, cols]
        gate = lb + (1.0 - lb) * _sigmoid(af_ref[:, cols])
        k = 1.0 - gate
        parts = _split3(jnp.log(gate))
        b = sum(_dot(tri, p) for p in parts)
        bend = sum(_dot(blk, p) for p in parts)
        qe = (_silu(aq_ref[:, cols]) * jnp.exp(b)).astype(BF16)
        ke = (k * jnp.exp(-b)).astype(BF16)
        kend = (k * jnp.exp(bend - b)).astype(BF16)
        vb = ai_ref[:, cols].astype(BF16)
        o_intra = _dot(jnp.where(causal, _dot_nt(qe, ke), 0.0).astype(BF16), vb)
        st = st_scr[hh]
        for u in range(T // c):
            rows = slice(u * c, (u + 1) * c)
            o = o_intra[rows] + _dot_nt(qe[rows], st.astype(BF16))
            st = st * jnp.exp(bend[u * c:u * c + 1]) + _dot_tn(vb[rows], kend[rows])
            o_ref[rows, cols] = _head_rms(o, gn) * _silu(ag_ref[rows, cols])
        st_scr[hh] = st

    @pl.when(t == pl.num_programs(2) - 1)
    def _():
        for hh in range(HGRN_HEADS_PER_STEP):
            st_ref[0, hh] = st_scr[hh].T


def _hgrn_prompt(hg, lb, gn, B, S, *, T, c):
    nt = S // T
    H = H_HGRN
    hp = HGRN_HEADS_PER_STEP
    W = hp * HEAD_DIM
    spec = lambda off: pl.BlockSpec((T, W), lambda b, h, t, off=off: (b * nt + t, off + h))
    return pl.pallas_call(
        functools.partial(_hgrn_kernel, T, c),
        out_shape=[jax.ShapeDtypeStruct((B * S, H * HEAD_DIM), F32),
                   jax.ShapeDtypeStruct((B, H, HEAD_DIM, HEAD_DIM), F32)],
        grid=(B, H // hp, nt),
        in_specs=[spec(0), spec(H // hp), spec(2 * H // hp), spec(3 * H // hp),
                  pl.BlockSpec((1, W), lambda b, h, t: (0, h)),
                  pl.BlockSpec((1, HEAD_DIM), lambda b, h, t: (0, 0))],
        out_specs=[pl.BlockSpec((T, W), lambda b, h, t: (b * nt + t, h)),
                   pl.BlockSpec((1, hp, HEAD_DIM, HEAD_DIM), lambda b, h, t: (b, h, 0, 0))],
        scratch_shapes=[pltpu.VMEM((hp, HEAD_DIM, HEAD_DIM), F32)],
        compiler_params=_cparams(("arbitrary", "arbitrary", "arbitrary")),
        name="hgrn_prompt",
    )(hg, hg, hg, hg, lb, gn)


BIAS_SHIFT = 4
BIAS_SLOT = 1 << BIAS_SHIFT


def _cumsum_kernel(lf_ref, pq_ref, pk_ref, oq_ref, ok_ref, aq_ref, ak_ref, carry):
    @pl.when(pl.program_id(1) == 0)
    def _():
        carry[...] = jnp.zeros_like(carry)

    tc = lf_ref.shape[0]
    c = _dot3(_tril_bf16(tc), lf_ref[...]) + carry[...]
    carry[...] = c[tc - 1:tc, :]
    parts = _split3(c * LOG2E)
    aq_ref[...] = sum(_dot(t, pq_ref[i]) for i, t in enumerate(parts)) + oq_ref[...]
    ak_ref[...] = (ok_ref[...] - sum(_dot(t, pk_ref[i]) for i, t in enumerate(parts))).astype(BF16)


def _bias_layout():
    pq, pk = np.zeros((3, LANES, LANES), np.float32), np.zeros((3, LANES, LANES), np.float32)
    oq, ok = np.zeros((1, LANES), np.float32), np.zeros((1, LANES), np.float32)
    for h in range(H_FOX):
        for i in range(3):
            pq[i, h, BIAS_SLOT * h + i] = 1.0
            pk[i, h, BIAS_SLOT * h + 3 + i] = 1.0
            oq[0, BIAS_SLOT * h + 3 + i] = 1.0
            ok[0, BIAS_SLOT * h + i] = 1.0
    return jnp.asarray(pq, BF16), jnp.asarray(pk, BF16), jnp.asarray(oq), jnp.asarray(ok)


def _seq_cumsum(lf, B, S, *, tc):
    n = S // tc
    const3 = pl.BlockSpec((3, LANES, LANES), lambda b, t: (0, 0, 0))
    const1 = pl.BlockSpec((1, LANES), lambda b, t: (0, 0))
    blk = pl.BlockSpec((tc, LANES), lambda b, t: (b * n + t, 0))
    return pl.pallas_call(
        _cumsum_kernel,
        out_shape=[jax.ShapeDtypeStruct((B * S, LANES), F32), jax.ShapeDtypeStruct((B * S, LANES), BF16)],
        grid=(B, n),
        in_specs=[blk, const3, const3, const1, const1],
        out_specs=[blk, blk],
        scratch_shapes=[pltpu.VMEM((1, LANES), F32)],
        compiler_params=_cparams(("arbitrary", "arbitrary")),
        name="seq_cumsum",
    )(lf, *_bias_layout())


def _online_init(m_scr, l_scr, acc_scr):
    m_scr[...] = jnp.full_like(m_scr, M_INIT)
    l_scr[...] = jnp.zeros_like(l_scr)
    acc_scr[...] = jnp.zeros_like(acc_scr)


def _online_update(s, vb, m_scr, l_scr, acc_scr):
    m_prev = m_scr[...]
    m_new = jnp.maximum(m_prev, jnp.max(s, axis=-1, keepdims=True))
    alpha = jnp.exp(m_prev - m_new)
    p = jnp.exp(s - m_new)
    l_scr[...] = alpha * l_scr[...] + jnp.sum(p, axis=-1, keepdims=True)
    acc_scr[...] = alpha * acc_scr[...] + _dot(p.astype(BF16), vb)
    m_scr[...] = m_new


def _online_result(l_scr, acc_scr):
    l = l_scr[...]
    return acc_scr[...] / jnp.where(l > 0.0, l, 1.0)


def _flash_t_step(sT, vT, m, l, acc_scr):
    m_new = jnp.maximum(m, jnp.max(sT, axis=0, keepdims=True))
    alpha = jnp.exp2(m - m_new)
    p = jnp.exp2(sT - m_new)
    acc_scr[...] = alpha * acc_scr[...] + _dot(vT, p.astype(BF16))
    return m_new, alpha * l + jnp.sum(p, axis=0, keepdims=True)


def _flash_t_causal(ka_scr, vt_scr, qa, acc_scr, n_full, n_diag, tk, qpos, unroll=1):
    R = qa.shape[1]
    acc_scr[...] = jnp.zeros_like(acc_scr)

    def body(it, carry):
        for u in range(unroll):
            kj = it * unroll + u
            carry = _flash_t_step(_dot(ka_scr[kj], qa), vt_scr[kj], *carry, acc_scr)
        return carry

    m, l = lax.fori_loop(0, n_full // unroll, body, (jnp.full((1, R), M_INIT, F32), jnp.zeros((1, R), F32)))
    for i in range(n_diag):
        kpos = (n_full + i) * tk + lax.broadcasted_iota(jnp.int32, (tk, R), 0)
        sT = jnp.where(kpos <= qpos, _dot(ka_scr[n_full + i], qa), NEG)
        m, l = _flash_t_step(sT, vt_scr[n_full + i], m, l, acc_scr)
    return acc_scr[...] / jnp.where(l > 0.0, l, 1.0)


def _fox_kernel(tq, tk, q_ref, k_ref, vt_ref, aq_ref, ak_ref, o_ref, ka_scr, vt_scr, acc_scr):
    h = pl.program_id(1)
    qi = pl.program_id(2)

    @pl.when(qi == 0)
    def _():
        for c in range(ka_scr.shape[0]):
            rows = slice(c * tk, (c + 1) * tk)
            ka_scr[c] = jnp.concatenate([k_ref[rows, :].astype(BF16), ak_ref[rows, :]], axis=1)
            vt_scr[c] = vt_ref[0, 0, :, rows].astype(BF16)

    lane = lax.broadcasted_iota(jnp.int32, (tq, LANES), 1)
    aq = jnp.where((lane >> BIAS_SHIFT) == h, aq_ref[...], 0.0)
    qa = jnp.concatenate([(q_ref[...] * QSCALE).T, aq.T], axis=0).astype(BF16)
    qpos = qi * tq + lax.broadcasted_iota(jnp.int32, (tk, tq), 1)
    o_ref[...] = _flash_t_causal(ka_scr, vt_scr, qa, acc_scr, qi * (tq // tk), tq // tk, tk, qpos,
                                 unroll=tq // tk).T


def _fox_prompt(fq, fk, fvt, aug_q, aug_k, B, S, *, tq, tk):
    nq = S // tq
    return pl.pallas_call(
        functools.partial(_fox_kernel, tq, tk),
        out_shape=jax.ShapeDtypeStruct(fq.shape, F32),
        grid=(B, H_FOX, nq),
        in_specs=[
            pl.BlockSpec((tq, HEAD_DIM), lambda b, h, qi: (b * nq + qi, h)),
            pl.BlockSpec((S, HEAD_DIM), lambda b, h, qi: (b, h)),
            pl.BlockSpec((1, 1, HEAD_DIM, S), lambda b, h, qi: (b, h, 0, 0)),
            pl.BlockSpec((tq, LANES), lambda b, h, qi: (b * nq + qi, 0)),
            pl.BlockSpec((S, LANES), lambda b, h, qi: (b, 0)),
        ],
        out_specs=pl.BlockSpec((tq, HEAD_DIM), lambda b, h, qi: (b * nq + qi, h)),
        scratch_shapes=[pltpu.VMEM((S // tk, tk, 2 * HEAD_DIM), BF16), pltpu.VMEM((S // tk, HEAD_DIM, tk), BF16),
                        pltpu.VMEM((HEAD_DIM, tq), F32)],
        compiler_params=_cparams(("arbitrary",) * 3),
        name="fox_prompt",
    )(fq, fk, fvt, aug_q, aug_k)


def _compress_kernel(nc, x_ref, pea_ref, peb_ref, w1a_ref, w1b_ref, w2_ref, o_ref):
    R = jnp.concatenate([x_ref[0, pl.ds(r, nc, stride=CMP_STRIDE), :] for r in range(CMP_STRIDE)], axis=1)
    p1 = _dot((R + pea_ref[...]).astype(BF16), w1a_ref[...])
    p2 = _dot((R + peb_ref[...]).astype(BF16), w1b_ref[...])
    pre = p1 + pltpu.roll(p2, nc - 1, 0)
    kc = _dot(_silu(pre).astype(BF16), w2_ref[...])
    row = lax.broadcasted_iota(jnp.int32, kc.shape, 0)
    o_ref[0, 0] = jnp.where(row < nc - 1, kc, 0.0)


def _cmp_weights(pe, w1, w2):
    half = CMP_STRIDE * HEAD_DIM
    return (pe[:CMP_STRIDE].reshape(1, half), pe[CMP_STRIDE:].reshape(1, half),
            w1[:half].astype(BF16), w1[half:].astype(BF16), w2.astype(BF16))


def _compress_prompt(x, cw, B, S):
    nc = S // CMP_STRIDE
    half = CMP_STRIDE * HEAD_DIM
    const = lambda shape: pl.BlockSpec(shape, lambda b, g: (0, 0))
    return pl.pallas_call(
        functools.partial(_compress_kernel, nc),
        out_shape=jax.ShapeDtypeStruct((B, NSA_GROUPS, nc, HEAD_DIM), F32),
        grid=(B, NSA_GROUPS),
        in_specs=[pl.BlockSpec((1, S, HEAD_DIM), lambda b, g: (b, 0, g)),
                  const((1, half)), const((1, half)), const((half, HEAD_DIM)), const((half, HEAD_DIM)),
                  const((HEAD_DIM, HEAD_DIM))],
        out_specs=pl.BlockSpec((1, 1, nc, HEAD_DIM), lambda b, g: (b, g, 0, 0)),
        compiler_params=_cparams(("arbitrary", "arbitrary")),
        name="nsa_compress",
    )(x, *cw)


def _overlap_t(n_cmp, n_sel):
    cs = np.arange(n_cmp)[None, :] * CMP_STRIDE
    ss = np.arange(n_sel)[:, None] * SEL_LEN
    ov = np.minimum(cs + CMP_LEN, ss + SEL_LEN) - np.maximum(cs, ss)
    return np.maximum(ov, 0).astype(np.float32) / CMP_LEN


def _group_rows(ref, Tq):
    return jnp.concatenate([ref[0, :, r * HEAD_DIM:(r + 1) * HEAD_DIM] for r in range(NSA_R)], axis=0)


def _store_gated(o_ref, o, gates, Tq, branch):
    for r in range(NSA_R):
        gcol = gates[:, 3 * r + branch:3 * r + branch + 1]
        o_ref[0, :, r * HEAD_DIM:(r + 1) * HEAD_DIM] = o[r * Tq:(r + 1) * Tq] * gcol


def _cmpsel_kernel(Tq, nsel, q_ref, kc_ref, vc_ref, ovt_ref, gate_ref, o_ref, sel_ref):
    qi = pl.program_id(2)
    q = (_group_rows(q_ref, Tq) * (HEAD_DIM ** -0.5)).astype(BF16)
    s = _dot_nt(q, kc_ref[0, 0].astype(BF16))
    qpos = qi * Tq + (lax.broadcasted_iota(jnp.int32, s.shape, 0) & (Tq - 1))
    cend = lax.broadcasted_iota(jnp.int32, s.shape, 1) * CMP_STRIDE + (CMP_LEN - 1)
    mask = cend <= qpos
    sm = jnp.where(mask, s, NEG)
    e = jnp.where(mask, jnp.exp(sm - jnp.max(sm, axis=-1, keepdims=True)), 0.0)
    den = jnp.sum(e, axis=-1, keepdims=True)
    p = e / jnp.where(den > 0.0, den, 1.0)
    o = _dot(p.astype(BF16), vc_ref[0, 0].astype(BF16))
    _store_gated(o_ref, o, gate_ref[0, 0], Tq, 0)

    psum = p[0:Tq]
    for r in range(1, NSA_R):
        psum = psum + p[r * Tq:(r + 1) * Tq]
    ovt = ovt_ref[...]
    sc = sum(_dot_nt(ovt, t) for t in _split3(psum))
    n = lax.broadcasted_iota(jnp.int32, sc.shape, 0)
    tpos = qi * Tq + lax.broadcasted_iota(jnp.int32, sc.shape, 1)
    cur = tpos >> SEL_SHIFT
    forced = (n == 0) | (n == cur) | (n == cur - 1)
    sc = jnp.where(n * SEL_LEN <= tpos, jnp.where(forced, -NEG, sc), NEG)
    ranks = []
    for v in range(nsel // 8):
        blk = sc[8 * v:8 * v + 8]
        nv = 8 * v + lax.broadcasted_iota(jnp.int32, blk.shape, 0)
        r = jnp.zeros(blk.shape, F32)
        for m in range(nsel):
            rowm = sc[m:m + 1, :]
            ge, gt = jnp.where(rowm >= blk, 1.0, 0.0), jnp.where(rowm > blk, 1.0, 0.0)
            if m < 8 * v:
                r = r + ge
            elif m > 8 * v + 7:
                r = r + gt
            else:
                r = r + jnp.where(nv > m, ge, gt)
        ranks.append(r)
    rank = jnp.concatenate(ranks, axis=0)
    selneg = jnp.where(rank < TOPN, 0.0, NEG)
    if nsel < LANES:
        selneg = jnp.concatenate([selneg, jnp.full((LANES - nsel, Tq), NEG, F32)], axis=0)
    sel_ref[0, 0, 0] = selneg.astype(BF16)


def _cmpsel_prompt(qn, kc, vc, gates_g, B, S, *, Tq):
    nq = S // Tq
    ncp = kc.shape[2]
    nsel = S // SEL_LEN
    ovt = jnp.asarray(_overlap_t(ncp, nsel), BF16)
    return pl.pallas_call(
        functools.partial(_cmpsel_kernel, Tq, nsel),
        out_shape=[jax.ShapeDtypeStruct((B, S, H_NSA * HEAD_DIM), F32),
                   jax.ShapeDtypeStruct((B, NSA_GROUPS, nq, LANES, Tq), BF16)],
        grid=(B, NSA_GROUPS, nq),
        in_specs=[pl.BlockSpec((1, Tq, NSA_R * HEAD_DIM), lambda b, g, qi: (b, qi, g)),
                  pl.BlockSpec((1, 1, ncp, HEAD_DIM), lambda b, g, qi: (b, g, 0, 0)),
                  pl.BlockSpec((1, 1, ncp, HEAD_DIM), lambda b, g, qi: (b, g, 0, 0)),
                  pl.BlockSpec((nsel, ncp), lambda b, g, qi: (0, 0)),
                  pl.BlockSpec((1, 1, Tq, 16), lambda b, g, qi: (b, g, qi, 0))],
        out_specs=[pl.BlockSpec((1, Tq, NSA_R * HEAD_DIM), lambda b, g, qi: (b, qi, g)),
                   pl.BlockSpec((1, 1, 1, LANES, Tq), lambda b, g, qi: (b, g, qi, 0, 0))],
        compiler_params=_cparams(("arbitrary",) * 3),
        name="nsa_cmpsel",
    )(qn, kc, vc, ovt, gates_g)


def _queries_t(q_ref, Tq):
    return jnp.concatenate([(q_ref[0, :, r * HEAD_DIM:(r + 1) * HEAD_DIM] * QSCALE).T for r in range(NSA_R)],
                           axis=1).astype(BF16)


def _store_gated_t(o_ref, oT, gates, Tq, branch):
    for r in range(NSA_R):
        gcol = gates[:, 3 * r + branch:3 * r + branch + 1]
        o_ref[0, :, r * HEAD_DIM:(r + 1) * HEAD_DIM] = oT[:, r * Tq:(r + 1) * Tq].T * gcol


def _sel_kernel(Tq, tk, q_ref, sn_ref, k_ref, vt_ref, e_ref, gate_ref, o_ref, ka_scr, vt_scr, acc_scr):
    qi = pl.program_id(2)

    @pl.when(qi == 0)
    def _():
        for c in range(ka_scr.shape[0]):
            rows = slice(c * tk, (c + 1) * tk)
            ka_scr[c] = jnp.concatenate([k_ref[0, rows, :].astype(BF16), e_ref[rows, :]], axis=1)
            vt_scr[c] = vt_ref[0, 0, :, rows].astype(BF16)

    sn = sn_ref[0, 0, 0]
    qa = jnp.concatenate([_queries_t(q_ref, Tq), jnp.concatenate([sn] * NSA_R, axis=1)], axis=0)
    R = NSA_R * Tq
    qpos = qi * Tq + (lax.broadcasted_iota(jnp.int32, (tk, R), 1) & (Tq - 1))
    oT = _flash_t_causal(ka_scr, vt_scr, qa, acc_scr, (qi * Tq) // tk, 1, tk, qpos)
    _store_gated_t(o_ref, oT, gate_ref[0, 0], Tq, 1)


def _sel_prompt(qr, selneg, sk, svt, gates_g, B, S, *, Tq, tk):
    nq = S // Tq
    key = np.arange(S)[:, None] // SEL_LEN
    e_all = jnp.asarray((key == np.arange(LANES)[None, :]).astype(np.float32), BF16)
    R = NSA_R * Tq
    return pl.pallas_call(
        functools.partial(_sel_kernel, Tq, tk),
        out_shape=jax.ShapeDtypeStruct((B, S, H_NSA * HEAD_DIM), F32),
        grid=(B, NSA_GROUPS, nq),
        in_specs=[pl.BlockSpec((1, Tq, NSA_R * HEAD_DIM), lambda b, g, qi: (b, qi, g)),
                  pl.BlockSpec((1, 1, 1, LANES, Tq), lambda b, g, qi: (b, g, qi, 0, 0)),
                  pl.BlockSpec((1, S, HEAD_DIM), lambda b, g, qi: (b, 0, g)),
                  pl.BlockSpec((1, 1, HEAD_DIM, S), lambda b, g, qi: (b, g, 0, 0)),
                  pl.BlockSpec((S, LANES), lambda b, g, qi: (0, 0)),
                  pl.BlockSpec((1, 1, Tq, 16), lambda b, g, qi: (b, g, qi, 0))],
        out_specs=pl.BlockSpec((1, Tq, NSA_R * HEAD_DIM), lambda b, g, qi: (b, qi, g)),
        scratch_shapes=[pltpu.VMEM((S // tk, tk, 2 * HEAD_DIM), BF16), pltpu.VMEM((S // tk, HEAD_DIM, tk), BF16),
                        pltpu.VMEM((HEAD_DIM, R), F32)],
        compiler_params=_cparams(("arbitrary",) * 3),
        name="nsa_sel",
    )(qr, selneg, sk, svt, e_all, gates_g)


def _win_kernel(Tq, nwb, q_ref, *refs):
    k_refs, vt_refs = refs[:nwb], refs[nwb:2 * nwb]
    gate_ref, o_ref = refs[2 * nwb:]
    qi = pl.program_id(2)
    R = NSA_R * Tq
    qT = _queries_t(q_ref, Tq)
    qpos = qi * Tq + (lax.broadcasted_iota(jnp.int32, (Tq, R), 1) & (Tq - 1))
    krow = lax.broadcasted_iota(jnp.int32, (Tq, R), 0)
    s = []
    for i in range(nwb):
        kb = qi - (nwb - 1) + i
        si = _dot(k_refs[i][0].astype(BF16), qT)
        if i == 0:
            si = jnp.where(qpos - (kb * Tq + krow) < WINDOW, si, NEG)
        if i == nwb - 1:
            si = jnp.where(kb * Tq + krow <= qpos, si, NEG)
        else:
            si = si + jnp.where(kb >= 0, 0.0, NEG)
        s.append(si)
    m = functools.reduce(jnp.maximum, [jnp.max(si, axis=0, keepdims=True) for si in s])
    p = [jnp.exp2(si - m) for si in s]
    l = sum(jnp.sum(pi, axis=0, keepdims=True) for pi in p)
    oT = sum(_dot(vt_refs[i][0, 0].astype(BF16), p[i].astype(BF16)) for i in range(nwb)) / l
    _store_gated_t(o_ref, oT, gate_ref[0, 0], Tq, 2)


def _win_prompt(qr, wk, wvt, gates_g, B, S, *, Tq):
    nq = S // Tq
    nwb = WINDOW // Tq + 1
    blk = lambda i: (lambda qi: jnp.maximum(qi - (nwb - 1) + i, 0))
    kspec = lambda i: pl.BlockSpec((1, Tq, HEAD_DIM), lambda b, g, qi, f=blk(i): (b, f(qi), g))
    vspec = lambda i: pl.BlockSpec((1, 1, HEAD_DIM, Tq), lambda b, g, qi, f=blk(i): (b, g, 0, f(qi)))
    return pl.pallas_call(
        functools.partial(_win_kernel, Tq, nwb),
        out_shape=jax.ShapeDtypeStruct((B, S, H_NSA * HEAD_DIM), F32),
        grid=(B, NSA_GROUPS, nq),
        in_specs=[pl.BlockSpec((1, Tq, NSA_R * HEAD_DIM), lambda b, g, qi: (b, qi, g))]
        + [kspec(i) for i in range(nwb)] + [vspec(i) for i in range(nwb)]
        + [pl.BlockSpec((1, 1, Tq, 16), lambda b, g, qi: (b, g, qi, 0))],
        out_specs=pl.BlockSpec((1, Tq, NSA_R * HEAD_DIM), lambda b, g, qi: (b, qi, g)),
        compiler_params=_cparams(("arbitrary",) * 3),
        name="nsa_win",
    )(qr, *([wk] * nwb), *([wvt] * nwb), gates_g)


def _hgrn_step_kernel(aq_ref, af_ref, ai_ref, ag_ref, lb_ref, gn_ref, s_ref, o_ref, so_ref):
    lb = lb_ref[0]
    gate = lb + (1.0 - lb) * _sigmoid(af_ref[0, 0])
    s_new = s_ref[0, 0] * gate + (1.0 - gate) * ai_ref[0, 0]
    so_ref[0, 0] = s_new
    o = jnp.sum(_silu(aq_ref[0, 0]) * s_new, axis=0, keepdims=True)
    o_ref[0, 0] = _head_rms(o, gn_ref[...]) * _silu(ag_ref[0, 0])


def _hgrn_step(hg, lb, gn, state):
    D, H = state.shape[:2]
    W = H * HEAD_DIM
    colv = lambda a: a.reshape(D, H, HEAD_DIM, 1)
    rowv = lambda a: a.reshape(D, H, 1, HEAD_DIM)
    cspec = pl.BlockSpec((1, 1, HEAD_DIM, 1), lambda d, h: (d, h, 0, 0))
    rspec = pl.BlockSpec((1, 1, 1, HEAD_DIM), lambda d, h: (d, h, 0, 0))
    sspec = pl.BlockSpec((1, 1, HEAD_DIM, HEAD_DIM), lambda d, h: (d, h, 0, 0))
    o, s_new = pl.pallas_call(
        _hgrn_step_kernel,
        out_shape=[jax.ShapeDtypeStruct((D, H, 1, HEAD_DIM), F32), jax.ShapeDtypeStruct(state.shape, F32)],
        grid=(D, H),
        in_specs=[cspec, cspec, rspec, rspec,
                  pl.BlockSpec((1, HEAD_DIM, 1), lambda d, h: (h, 0, 0)),
                  pl.BlockSpec((1, HEAD_DIM), lambda d, h: (0, 0)), sspec],
        out_specs=[rspec, sspec],
        compiler_params=_cparams(("arbitrary", "arbitrary")),
        name="hgrn_step",
    )(colv(hg[:, :W]), colv(hg[:, W:2 * W]), rowv(hg[:, 2 * W:3 * W]), rowv(hg[:, 3 * W:]),
      lb.reshape(H, HEAD_DIM, 1), gn, state)
    return o.reshape(D, W), s_new


def _fox_step_kernel(PG, pt_ref, q_ref, kn_ref, vn_ref, lfn_ref, *refs):
    k_refs, v_refs, lf_refs = refs[:PG], refs[PG:2 * PG], refs[2 * PG:3 * PG]
    o_ref, m_scr, l_scr, acc_scr, carry = refs[3 * PG:]
    W = H_FOX * PAGE
    q = q_ref[0] * (HEAD_DIM ** -0.5)

    @pl.when(pl.program_id(1) == 0)
    def _():
        m_scr[...] = jnp.sum(q * kn_ref[0], axis=-1, keepdims=True)
        l_scr[...] = jnp.ones_like(l_scr)
        acc_scr[...] = vn_ref[0]
        carry[...] = jnp.broadcast_to(lfn_ref[0], carry.shape)

    qb = q.astype(BF16)
    lane = lax.broadcasted_iota(jnp.int32, (H_FOX, W), 1)
    head = lax.broadcasted_iota(jnp.int32, (H_FOX, W), 0)
    plane = lax.broadcasted_iota(jnp.int32, (PG, W), 1)
    lf = jnp.concatenate([lf_refs[i][0] for i in range(PG)], axis=0)
    suf, tot = lf, lf
    step = H_FOX
    while step < W:
        suf = suf + jnp.where(plane + step < W, pltpu.roll(suf, W - step, 1), 0.0)
        tot = tot + pltpu.roll(tot, step, 1)
        step *= 2
    c = carry[...]
    s = []
    for i in range(PG):
        si = _dot_nt(qb, k_refs[i][0].astype(BF16)) + (c + (suf[i:i + 1] - lf[i:i + 1]))
        s.append(jnp.where((lane & (H_FOX - 1)) == head, si, NEG))
        c = c + tot[i:i + 1]
    carry[...] = c
    m_prev = m_scr[...]
    m_new = functools.reduce(jnp.maximum, [jnp.max(si, axis=-1, keepdims=True) for si in s] + [m_prev])
    alpha = jnp.exp(m_prev - m_new)
    p = [jnp.exp(si - m_new) for si in s]
    l_scr[...] = alpha * l_scr[...] + sum(jnp.sum(pi, axis=-1, keepdims=True) for pi in p)
    acc_scr[...] = alpha * acc_scr[...] + sum(_dot(p[i].astype(BF16), v_refs[i][0].astype(BF16)) for i in range(PG))
    m_scr[...] = m_new

    @pl.when(pl.program_id(1) == pl.num_programs(1) - 1)
    def _():
        o_ref[0] = _online_result(l_scr, acc_scr)


def _fox_step(page_table, q, k_new, v_new, lf_new, cache_k, cache_v, cache_lf, *, PG):
    D, NP = page_table.shape
    n_pool = cache_k.shape[0]
    W = H_FOX * PAGE
    k2 = cache_k.reshape(n_pool, W, HEAD_DIM)
    v2 = cache_v.reshape(n_pool, W, HEAD_DIM)
    lf2 = cache_lf.astype(F32).reshape(n_pool, 1, W)
    lfn = jnp.tile(lf_new, (1, PAGE)).reshape(D, 1, W)
    page = lambda i: (lambda d, j, pt: (pt[d, NP - 1 - (j * PG + i)], 0, 0))
    hspec = pl.BlockSpec((1, H_FOX, HEAD_DIM), lambda d, j, pt: (d, 0, 0))
    in_specs = [hspec, hspec, hspec, pl.BlockSpec((1, 1, W), lambda d, j, pt: (d, 0, 0))]
    in_specs += [pl.BlockSpec((1, W, HEAD_DIM), page(i)) for i in range(PG)]
    in_specs += [pl.BlockSpec((1, W, HEAD_DIM), page(i)) for i in range(PG)]
    in_specs += [pl.BlockSpec((1, 1, W), page(i)) for i in range(PG)]
    return pl.pallas_call(
        functools.partial(_fox_step_kernel, PG),
        out_shape=jax.ShapeDtypeStruct((D, H_FOX, HEAD_DIM), F32),
        grid_spec=pltpu.PrefetchScalarGridSpec(
            num_scalar_prefetch=1, grid=(D, NP // PG), in_specs=in_specs, out_specs=hspec,
            scratch_shapes=[pltpu.VMEM((H_FOX, 1), F32), pltpu.VMEM((H_FOX, 1), F32),
                            pltpu.VMEM((H_FOX, HEAD_DIM), F32), pltpu.VMEM((H_FOX, W), F32)]),
        compiler_params=_cparams(("arbitrary", "arbitrary")),
        name="fox_step",
    )(page_table, q, k_new, v_new, lfn, *([k2] * PG), *([v2] * PG), *([lf2] * PG))


def _cmp_pages_kernel(PG, pt_ref, *refs):
    pages = refs[:PG]
    pea_ref, peb_ref, w1a_ref, w1b_ref, p1_ref, p2_ref, r_scr = refs[PG:]
    per_page = PAGE // CMP_STRIDE
    G = NSA_GROUPS
    for i in range(PG):
        for m in range(per_page):
            dst = (i * per_page + m) * G
            for r in range(CMP_STRIDE):
                src = (m * CMP_STRIDE + r) * G
                r_scr[dst:dst + G, r * HEAD_DIM:(r + 1) * HEAD_DIM] = pages[i][0, src:src + G, :]
    R = r_scr[...]
    p1_ref[0] = _dot((R + pea_ref[...]).astype(BF16), w1a_ref[...])
    p2_ref[0] = _dot((R + peb_ref[...]).astype(BF16), w1b_ref[...])


def _cmp_pages(page_table, cache, cw, *, PG):
    D, NP = page_table.shape
    n_pool = cache.shape[0]
    c2 = cache.reshape(n_pool, PAGE * NSA_GROUPS, HEAD_DIM)
    rows = PG * (PAGE // CMP_STRIDE) * NSA_GROUPS
    half = CMP_STRIDE * HEAD_DIM
    page = lambda i: (lambda d, j, pt: (pt[d, j * PG + i], 0, 0))
    const = lambda shape: pl.BlockSpec(shape, lambda d, j, pt: (0, 0))
    in_specs = [pl.BlockSpec((1, PAGE * NSA_GROUPS, HEAD_DIM), page(i)) for i in range(PG)]
    in_specs += [const((1, half)), const((1, half)), const((half, HEAD_DIM)), const((half, HEAD_DIM))]
    ospec = pl.BlockSpec((1, rows, HEAD_DIM), lambda d, j, pt: (d, j, 0))
    shp = jax.ShapeDtypeStruct((D, (NP // PG) * rows, HEAD_DIM), F32)
    return pl.pallas_call(
        functools.partial(_cmp_pages_kernel, PG),
        out_shape=[shp, shp],
        grid_spec=pltpu.PrefetchScalarGridSpec(
            num_scalar_prefetch=1, grid=(D, NP // PG), in_specs=in_specs, out_specs=[ospec, ospec],
            scratch_shapes=[pltpu.VMEM((rows, half), F32)]),
        compiler_params=_cparams(("arbitrary", "arbitrary")),
        name="nsa_cmp_pages",
    )(page_table, *([c2] * PG), *cw[:4])


def _cmpsel_step_kernel(qpos, nsp, q_ref, p1k_ref, p2k_ref, p1v_ref, p2v_ref, w2k_ref, w2v_ref,
                        ov_ref, g_ref, o_ref, idx_ref):
    nc = p1k_ref.shape[1] // NSA_GROUPS
    rows = pl.ds(pl.program_id(1), nc, stride=NSA_GROUPS)

    def finish(p1_ref, p2_ref, w2_ref):
        pre = p1_ref[0, rows, :] + pltpu.roll(p2_ref[0, rows, :], nc - 1, 0)
        return _dot(_silu(pre).astype(BF16), w2_ref[...]).astype(BF16)

    kc = finish(p1k_ref, p2k_ref, w2k_ref)
    vc = finish(p1v_ref, p2v_ref, w2v_ref)
    q = (q_ref[0, 0] * (HEAD_DIM ** -0.5)).astype(BF16)
    s = _dot_nt(q, kc)
    cend = lax.broadcasted_iota(jnp.int32, s.shape, 1) * CMP_STRIDE + (CMP_LEN - 1)
    mask = cend <= qpos
    sm = jnp.where(mask, s, NEG)
    e = jnp.where(mask, jnp.exp(sm - jnp.max(sm, axis=-1, keepdims=True)), 0.0)
    den = jnp.sum(e, axis=-1, keepdims=True)
    p = e / jnp.where(den > 0.0, den, 1.0)
    o_ref[0, 0] = _dot(p.astype(BF16), vc) * g_ref[0, 0][:, 0:1]

    psum = jnp.broadcast_to(jnp.sum(p[0:NSA_R], axis=0, keepdims=True), (8, nc))
    ov = ov_ref[...]
    sc_row = sum(_dot(t, ov) for t in _split3(psum))[0:1]
    n_lane = lax.broadcasted_iota(jnp.int32, (1, nsp), 1)
    cur = qpos // SEL_LEN
    forced = (n_lane == 0) | (n_lane == cur) | (n_lane == cur - 1)
    sc_row = jnp.where(n_lane * SEL_LEN <= qpos, jnp.where(forced, -NEG, sc_row), NEG)
    mi = lax.broadcasted_iota(jnp.int32, (nsp, nsp), 0)
    ni = lax.broadcasted_iota(jnp.int32, (nsp, nsp), 1)
    sc_col = jnp.sum(jnp.where(mi == ni, sc_row, 0.0), axis=-1, keepdims=True)
    beats = jnp.where(mi < ni, jnp.where(sc_col >= sc_row, 1.0, 0.0), jnp.where(sc_col > sc_row, 1.0, 0.0))
    rank = jnp.sum(beats, axis=0, keepdims=True)
    lane = lax.broadcasted_iota(jnp.int32, (1, LANES), 1)
    out = jnp.zeros((1, LANES), F32)
    for k in range(TOPN):
        nk = jnp.sum(jnp.where(rank == float(k), n_lane.astype(F32), 0.0), axis=-1, keepdims=True)
        out = jnp.where(lane == k, nk, out)
    idx_ref[0, 0] = jnp.broadcast_to(out, (8, LANES)).astype(jnp.int32)


def _cmpsel_step(q16, p1k, p2k, p1v, p2v, cwk, cwv, gate_rows, qpos):
    D, G = q16.shape[:2]
    nc = p1k.shape[1] // G
    n_sel = -(-(qpos + 1) // SEL_LEN)
    nsp = -(-n_sel // LANES) * LANES
    ov = jnp.asarray(np.pad(_overlap_t(nc, n_sel).T, ((0, 0), (0, nsp - n_sel))), BF16)
    big = pl.BlockSpec((1, nc * G, HEAD_DIM), lambda d, g: (d, 0, 0))
    qspec = pl.BlockSpec((1, 1, 16, HEAD_DIM), lambda d, g: (d, g, 0, 0))
    w2spec = pl.BlockSpec((HEAD_DIM, HEAD_DIM), lambda d, g: (0, 0))
    return pl.pallas_call(
        functools.partial(_cmpsel_step_kernel, qpos, nsp),
        out_shape=[jax.ShapeDtypeStruct((D, G, 16, HEAD_DIM), F32),
                   jax.ShapeDtypeStruct((D, G, 8, LANES), jnp.int32)],
        grid=(D, G),
        in_specs=[qspec, big, big, big, big, w2spec, w2spec,
                  pl.BlockSpec((nc, nsp), lambda d, g: (0, 0)),
                  pl.BlockSpec((1, 1, 16, 3), lambda d, g: (d, g, 0, 0))],
        out_specs=[qspec, pl.BlockSpec((1, 1, 8, LANES), lambda d, g: (d, g, 0, 0))],
        compiler_params=_cparams(("arbitrary", "arbitrary")),
        name="nsa_cmpsel_step",
    )(q16, p1k, p2k, p1v, p2v, cwk[4], cwv[4], ov, gate_rows)


def _sel_step_kernel(n_past, pt_ref, idx_ref, q_ref, kn_ref, vn_ref, *refs):
    k_refs, v_refs = refs[:TOPN], refs[TOPN:2 * TOPN]
    g_ref, o_ref = refs[2 * TOPN:]
    d, g = pl.program_id(0), pl.program_id(1)
    q = q_ref[0, 0] * (HEAD_DIM ** -0.5)
    qb = q.astype(BF16)
    rows = pl.ds(g, SEL_LEN, stride=NSA_GROUPS)
    s_new = jnp.sum(q * kn_ref[0, 0], axis=-1, keepdims=True)
    s = [_dot_nt(qb, k_refs[k][rows, :].astype(BF16)) + jnp.where(idx_ref[d, g, k] < n_past, 0.0, NEG)
         for k in range(TOPN)]
    m = functools.reduce(jnp.maximum, [jnp.max(sk, axis=-1, keepdims=True) for sk in s] + [s_new])
    p = [jnp.exp(sk - m) for sk in s]
    p_new = jnp.exp(s_new - m)
    l = sum(jnp.sum(pk, axis=-1, keepdims=True) for pk in p) + p_new
    o = sum(_dot(p[k].astype(BF16), v_refs[k][rows, :].astype(BF16)) for k in range(TOPN)) + p_new * vn_ref[0, 0]
    o_ref[0, 0] = o / l * g_ref[0, 0][:, 1:2]


def _sel_step(page_table, idx, q16, k_new, v_new, cache_k, cache_v, gate_rows):
    D, NP = page_table.shape
    G = NSA_GROUPS
    n_pool = cache_k.shape[0]
    per_page = PAGE // SEL_LEN
    n_past = NP * per_page
    blk_rows = SEL_LEN * G
    k2 = cache_k.reshape(n_pool * PAGE * G, HEAD_DIM)
    v2 = cache_v.reshape(n_pool * PAGE * G, HEAD_DIM)

    def blk(k):
        def index(d, g, pt, ix):
            n = jnp.minimum(ix[d, g, k], n_past - 1)
            return (pt[d, n // per_page] * per_page + n % per_page, 0)
        return pl.BlockSpec((blk_rows, HEAD_DIM), index)

    qspec = pl.BlockSpec((1, 1, 16, HEAD_DIM), lambda d, g, pt, ix: (d, g, 0, 0))
    nspec = pl.BlockSpec((1, 1, 1, HEAD_DIM), lambda d, g, pt, ix: (d, g, 0, 0))
    return pl.pallas_call(
        functools.partial(_sel_step_kernel, n_past),
        out_shape=jax.ShapeDtypeStruct((D, G, 16, HEAD_DIM), F32),
        grid_spec=pltpu.PrefetchScalarGridSpec(
            num_scalar_prefetch=2, grid=(D, G),
            in_specs=[qspec, nspec, nspec] + [blk(k) for k in range(TOPN)] * 2
            + [pl.BlockSpec((1, 1, 16, 3), lambda d, g, pt, ix: (d, g, 0, 0))],
            out_specs=qspec),
        compiler_params=_cparams(("arbitrary",) * 2),
        name="nsa_sel_step",
    )(page_table, idx, q16, k_new, v_new, *([k2] * TOPN), *([v2] * TOPN), gate_rows)


def _win_step_kernel(q_ref, k_ref, v_ref, g_ref, o_ref):
    q = (q_ref[0, 0] * (HEAD_DIM ** -0.5)).astype(BF16)
    s = _dot_nt(q, k_ref[0].astype(BF16))
    e = jnp.exp(s - jnp.max(s, axis=-1, keepdims=True))
    p = e / jnp.sum(e, axis=-1, keepdims=True)
    o_ref[0, 0] = _dot(p.astype(BF16), v_ref[0].astype(BF16)) * g_ref[0, 0][:, 2:3]


def _win_step(q16, kw, vw, gate_rows):
    D, G = q16.shape[:2]
    L = kw.shape[1]
    qspec = pl.BlockSpec((1, 1, 16, HEAD_DIM), lambda d, g: (d, g, 0, 0))
    kspec = pl.BlockSpec((1, L, HEAD_DIM), lambda d, g: (d, 0, g))
    return pl.pallas_call(
        _win_step_kernel,
        out_shape=jax.ShapeDtypeStruct((D, G, 16, HEAD_DIM), F32),
        grid=(D, G),
        in_specs=[qspec, kspec, kspec, pl.BlockSpec((1, 1, 16, 3), lambda d, g: (d, g, 0, 0))],
        out_specs=qspec,
        compiler_params=_cparams(("arbitrary", "arbitrary")),
        name="nsa_win_step",
    )(q16, kw, vw, gate_rows)


TN = 512
TM = 512
TM_STEP = 8


def _pad_cols(a, n):
    return jnp.pad(a, ((0, 0), (0, n - a.shape[1])))


TN_PROJ = 1024


def _segments(widths_modes):
    segs, start = [], 0
    for width, mode in widths_modes:
        segs.append((start, width, mode))
        start += width
    return segs, start


def _proj_weights(w_in, n_main):
    wb = w_in.astype(BF16)
    return wb, _pad_cols(wb[:, n_main:], LANES)


def _even_params(e_norm_mix, e_w_in, lb, out_norm, f_bias, q_norm, k_norm, e_w_out, e_norm_ffn):
    d = e_w_in.shape[0]
    aw, bw = H_HGRN * HEAD_DIM, H_FOX * HEAD_DIM
    segs, n_main = _segments(((4 * aw, 'raw'), (bw, 'hnorm'), (bw, 'hnorm'), (bw, 'raw')))
    gain = jnp.concatenate([jnp.ones((4 * aw,), F32), jnp.tile(q_norm, H_FOX), jnp.tile(k_norm, H_FOX),
                            jnp.ones((bw,), F32)])
    w_in, w_tail = _proj_weights(e_w_in, n_main)
    return dict(norm=e_norm_mix.reshape(1, d), w_in=w_in, w_tail=w_tail, gain=gain.reshape(1, -1),
                bias=_pad_cols(f_bias.astype(F32).reshape(1, -1), LANES), segs=segs, lb=lb.reshape(1, aw),
                out_norm=out_norm.reshape(1, HEAD_DIM), w_out=e_w_out.astype(BF16), norm_ffn=e_norm_ffn.reshape(1, d))


def _even_proj(x2, p, tm):
    zeros = jnp.zeros((tm, HEAD_DIM), F32)
    return _proj(x2, p['norm'], p['w_in'], p['w_tail'], p['gain'], p['bias'], zeros, zeros, p['segs'], 'logsig',
                 tm=tm, tn=TN_PROJ, name="even_proj")


def _even_finish(x2, oa, of, p, ffn_w, tm, emit=False):
    x2 = _outproj([[oa], [of]], p['w_out'], x2, tm=tm, tn=TN_PROJ, name="even_out")
    return _ffn(x2, p['norm_ffn'], *ffn_w, tm=tm, tf=TN, emit=emit, name="ffn")


def _odd_params(o_norm_mix, o_w_in, q_norm, ck_norm, sk_norm, wk_norm, pe_k, w1_k, w2_k, pe_v, w1_v, w2_v,
                o_w_out, o_norm_ffn, router_w, router_b):
    d = o_w_in.shape[0]
    qw, kvw = H_NSA * HEAD_DIM, NSA_GROUPS * HEAD_DIM
    segs, n_main = _segments(((qw, 'hnorm_both'), (kvw, 'hnorm'), (kvw, 'raw'), (kvw, 'hnorm_rope'), (kvw, 'raw'),
                              (kvw, 'hnorm_rope'), (kvw, 'raw')))
    ones = jnp.ones((kvw,), F32)
    gain = jnp.concatenate([jnp.tile(q_norm, H_NSA), jnp.tile(ck_norm, NSA_GROUPS), ones,
                            jnp.tile(sk_norm, NSA_GROUPS), ones, jnp.tile(wk_norm, NSA_GROUPS), ones])
    w_in, w_tail = _proj_weights(o_w_in, n_main)
    return dict(norm=o_norm_mix.reshape(1, d), w_in=w_in, w_tail=w_tail, gain=gain.reshape(1, -1),
                bias=jnp.zeros((1, LANES), F32), segs=segs,
                cwk=_cmp_weights(pe_k, w1_k, w2_k), cwv=_cmp_weights(pe_v, w1_v, w2_v),
                w_out=o_w_out.astype(BF16), norm_ffn=o_norm_ffn.reshape(1, d),
                rw=_pad_cols(router_w.astype(BF16), LANES), rb=_pad_cols(router_b.reshape(1, -1).astype(F32), LANES))


def _odd_proj(x2, p, pos, tm):
    cosf, sinf = _rope_tables(pos)
    return _proj(x2, p['norm'], p['w_in'], p['w_tail'], p['gain'], p['bias'], cosf, sinf, p['segs'], 'sigmoid',
                 tm=tm, tn=TN_PROJ, name="odd_proj")


def _odd_finish(x2, branches, p, moe_w, tm, emit=False):
    x2 = _outproj([branches], p['w_out'], x2, tm=tm, tn=TN_PROJ, name="odd_out")
    args = (x2, p['norm_ffn'], p['rw'], p['rb'], *moe_w)
    if emit:
        return _moe(*args, tm=tm, tf=256, emit=True, name="moe")
    return _moe_routed(*args, tm=tm)


def kernel(x_prompt, x_sample, state_hgrn, cache_fox_k, cache_fox_v, cache_fox_logf, cache_nsa_cmp_k, cache_nsa_cmp_v, cache_nsa_sel_k, cache_nsa_sel_v, cache_nsa_win_k, cache_nsa_win_v, page_table, e_norm_mix, e_w_in, hgrn_lb_logits, hgrn_out_norm, fox_f_bias, fox_q_norm, fox_k_norm, e_w_out, e_norm_ffn, ffn_w_gate, ffn_w_up, ffn_w_down, o_norm_mix, o_w_in, nsa_q_norm, nsa_cmp_k_norm, nsa_sel_k_norm, nsa_win_k_norm, cmp_pe_k, cmp_w1_k, cmp_w2_k, cmp_pe_v, cmp_w1_v, cmp_w2_v, o_w_out, o_norm_ffn, router_w, router_b, moe_w_gate, moe_w_up, moe_w_down):
    B, S, d = x_prompt.shape
    D, T, _ = x_sample.shape
    NP = page_table.shape[1]
    past = NP * PAGE
    w_buf = cache_nsa_win_k.shape[2]
    assert T == 1 and w_buf == WINDOW and S >= WINDOW and cache_fox_k.shape[2] == PAGE
    G = NSA_GROUPS
    lbs = jnp.cumsum(jax.nn.softmax(hgrn_lb_logits.astype(F32), axis=0), axis=0)
    xp, xd = x_prompt.reshape(B * S, d), x_sample.reshape(D, d)
    tm_p = min(TM, S)

    li = 0
    p = _even_params(e_norm_mix[li], e_w_in[li], lbs[li], hgrn_out_norm[li], fox_f_bias[li], fox_q_norm[li],
                     fox_k_norm[li], e_w_out[li], e_norm_ffn[li])
    hg, fq, fk, fv, fl = _even_proj(xd, p, TM_STEP)
    oa, st_d = _hgrn_step(hg, p['lb'], p['out_norm'], state_hgrn[li].astype(F32))
    heads = lambda a: a.reshape(D, H_FOX, HEAD_DIM)
    of = _fox_step(page_table, heads(fq), heads(fk), heads(fv), fl[:, :H_FOX],
                   cache_fox_k[li], cache_fox_v[li], cache_fox_logf[li], PG=8)
    xd, ffn_bf16 = _even_finish(xd, oa, of.reshape(D, H_FOX * HEAD_DIM), p,
                                (ffn_w_gate[li], ffn_w_up[li], ffn_w_down[li]), TM_STEP, emit=True)
    fox_d = (fk.reshape(1, D, 1, H_FOX, HEAD_DIM), fv.reshape(1, D, 1, H_FOX, HEAD_DIM),
             fl[:, :H_FOX].reshape(1, D, 1, H_FOX))

    hg, fq, fk, fv, fl = _even_proj(xp, p, tm_p)
    aug_q, aug_k = _seq_cumsum(fl, B, S, tc=tm_p)
    oa, st_p = _hgrn_prompt(hg, p['lb'], p['out_norm'], B, S, T=min(256, S), c=32)
    fvt = fv.reshape(B, S, H_FOX, HEAD_DIM).transpose(0, 2, 3, 1)
    of = _fox_prompt(fq, fk, fvt, aug_q, aug_k, B, S, tq=min(1024, S), tk=tm_p)
    xp = _even_finish(xp, oa, of, p, ffn_bf16, tm_p)
    fox_p = (fk.reshape(1, B, S, H_FOX, HEAD_DIM), fv.reshape(1, B, S, H_FOX, HEAD_DIM),
             fl[:, :H_FOX].reshape(1, B, S, H_FOX))

    p = _odd_params(o_norm_mix[li], o_w_in[li], nsa_q_norm[li], nsa_cmp_k_norm[li], nsa_sel_k_norm[li],
                    nsa_win_k_norm[li], cmp_pe_k[li], cmp_w1_k[li], cmp_w2_k[li], cmp_pe_v[li], cmp_w1_v[li],
                    cmp_w2_v[li], o_w_out[li], o_norm_ffn[li], router_w[li], router_b[li])
    qn, qr, ck, cv, sk, sv, wk, wv, gt = _odd_proj(xd, p, jnp.full((TM_STEP,), past), TM_STEP)
    rows16 = lambda a, w: jnp.pad(a.reshape(D, G, NSA_R, w), ((0, 0), (0, 0), (0, 16 - NSA_R), (0, 0)))
    gate_rows = rows16(gt[:, :3 * H_NSA], 3)
    q16n, q16r = rows16(qn, HEAD_DIM), rows16(qr, HEAD_DIM)
    p1k, p2k = _cmp_pages(page_table, cache_nsa_cmp_k[li], p['cwk'], PG=16)
    p1v, p2v = _cmp_pages(page_table, cache_nsa_cmp_v[li], p['cwv'], PG=16)
    o_cmp, idx = _cmpsel_step(q16n, p1k, p2k, p1v, p2v, p['cwk'], p['cwv'], gate_rows, past)
    new = lambda a: a.reshape(D, G, 1, HEAD_DIM)
    o_sel = _sel_step(page_table, idx[:, :, 0, :TOPN], q16r, new(sk), new(sv),
                      cache_nsa_sel_k[li], cache_nsa_sel_v[li], gate_rows)
    kvd = lambda a: a.reshape(D, 1, G, HEAD_DIM)
    win_k = jnp.concatenate([cache_nsa_win_k[li], kvd(wk)], axis=1)[:, -w_buf:]
    win_v = jnp.concatenate([cache_nsa_win_v[li], kvd(wv)], axis=1)[:, -w_buf:]
    o_win = _win_step(q16r, win_k.reshape(D, w_buf, G * HEAD_DIM), win_v.reshape(D, w_buf, G * HEAD_DIM), gate_rows)
    unrow = lambda a: a[:, :, :NSA_R].reshape(D, H_NSA * HEAD_DIM)
    xd, moe_bf16 = _odd_finish(xd, [unrow(o_cmp), unrow(o_sel), unrow(o_win)], p,
                               (moe_w_gate[li], moe_w_up[li], moe_w_down[li]), TM_STEP, emit=True)
    kv1 = lambda a: a.reshape(1, D, 1, G, HEAD_DIM)
    nsa_d = (kv1(ck), kv1(cv), kv1(sk), kv1(sv), win_k[None], win_v[None])

    qn, qr, ck, cv, sk, sv, wk, wv, gt = _odd_proj(xp, p, jnp.arange(S), tm_p)
    seq = lambda a: a.reshape(B, S, -1)
    gates_g = jnp.pad(gt[:, :3 * H_NSA].reshape(B, S, G, 3 * NSA_R).transpose(0, 2, 1, 3),
                      ((0, 0), (0, 0), (0, 0), (0, 16 - 3 * NSA_R)))
    kc = _compress_prompt(seq(ck), p['cwk'], B, S)
    vc = _compress_prompt(seq(cv), p['cwv'], B, S)
    tq = min(256, S)
    o_cmp, selneg = _cmpsel_prompt(seq(qn), kc, vc, gates_g, B, S, Tq=tq)
    keys_last = lambda a: a.reshape(B, S, G, HEAD_DIM).transpose(0, 2, 3, 1)
    o_sel = _sel_prompt(seq(qr), selneg, seq(sk), keys_last(sv), gates_g, B, S, Tq=tq, tk=min(512, S))
    o_win = _win_prompt(seq(qr), seq(wk), keys_last(wv), gates_g, B, S, Tq=tq)
    flat = lambda a: a.reshape(B * S, -1)
    xp = _odd_finish(xp, [flat(o_cmp), flat(o_sel), flat(o_win)], p, moe_bf16, tm_p)
    kv = lambda a: a.reshape(1, B, S, G, HEAD_DIM)
    nsa_p = (kv(ck), kv(cv), kv(sk), kv(sv), kv(wk)[:, :, S - WINDOW:], kv(wv)[:, :, S - WINDOW:])

    return (xp.reshape(B, S, d), xd.reshape(D, T, d), st_p[None].astype(state_hgrn.dtype),
            st_d[None].astype(state_hgrn.dtype), *fox_p, *fox_d, *nsa_p, *nsa_d)
```

```python
import functools

import numpy as np
import jax
import jax.numpy as jnp
from jax import lax
from jax.experimental import pallas as pl
from jax.experimental.pallas import tpu as pltpu

F32 = jnp.float32
BF16 = jnp.bfloat16

LANES = 128
HEAD_DIM = 128
EPS = 1e-6
ROPE_THETA = 10000.0
LOG2E = 1.4426950408889634
QSCALE = HEAD_DIM ** -0.5 * LOG2E
NEG = -1e30
M_INIT = -1e20
VMEM_LIMIT = 56 * 1024 * 1024

H_HGRN = 8
H_FOX = 8
H_NSA = 16
NSA_GROUPS = 4
NSA_R = H_NSA // NSA_GROUPS
CMP_STRIDE = 16
CMP_LEN = 32
SEL_LEN = 64
SEL_SHIFT = 6
TOPN = 16
WINDOW = 512
N_EXPERTS = 8
PAGE = 128


def _cparams(sem):
    return pltpu.CompilerParams(dimension_semantics=sem, vmem_limit_bytes=VMEM_LIMIT)


def _dot(a, b):
    return jnp.dot(a, b, preferred_element_type=F32)


def _dot_nt(a, b):
    return lax.dot_general(a, b, (((1,), (1,)), ((), ())), preferred_element_type=F32)


def _dot_tn(a, b):
    return lax.dot_general(a, b, (((0,), (0,)), ((), ())), preferred_element_type=F32)


def _split3(x):
    hi = x.astype(BF16)
    r1 = x - hi.astype(F32)
    mid = r1.astype(BF16)
    lo = (r1 - mid.astype(F32)).astype(BF16)
    return hi, mid, lo


def _dot3(w_bf16, x):
    hi, mid, lo = _split3(x)
    return _dot(w_bf16, hi) + _dot(w_bf16, mid) + _dot(w_bf16, lo)


def _sigmoid(x):
    return 1.0 / (1.0 + jnp.exp(-x))


def _silu(x):
    return x * _sigmoid(x)


def _log_sigmoid(x):
    return jnp.minimum(x, 0.0) - jnp.log(1.0 + jnp.exp(-jnp.abs(x)))


def _tril_bf16(n):
    r = lax.broadcasted_iota(jnp.int32, (n, n), 0)
    c = lax.broadcasted_iota(jnp.int32, (n, n), 1)
    return jnp.where(c <= r, 1.0, 0.0).astype(BF16)


def _head_rms(a, gain):
    ms = jnp.mean(a * a, axis=-1, keepdims=True)
    return a * lax.rsqrt(ms + EPS) * gain


def _rope(y, cosf, sinf):
    return y * cosf + pltpu.roll(y, HEAD_DIM // 2, 1) * sinf


def _rope_tables(pos):
    half = HEAD_DIM // 2
    inv = ROPE_THETA ** (-jnp.arange(half, dtype=F32) / half)
    ang = pos.astype(F32)[:, None] * inv[None, :]
    cos, sin = jnp.cos(ang), jnp.sin(ang)
    return jnp.concatenate([cos, cos], axis=-1), jnp.concatenate([-sin, sin], axis=-1)


PROJ_PIECE = 512


def _n_outs(mode):
    return 2 if mode == 'hnorm_both' else 1


def _seg_block(width, tn):
    return min(width, tn)


def _proj_kernel(segs, tail_mode, tn, emit_w, *refs):
    n_out = sum(_n_outs(s[2]) for s in segs)
    x_ref, g_ref, w_ref, wt_ref, gain_ref, bias_ref, cos_ref, sin_ref = refs[:8]
    out_refs = refs[8:8 + n_out]
    tail_ref = refs[8 + n_out]
    wo_ref = refs[9 + n_out] if emit_w else None
    xn_ref = refs[-1]
    j = pl.program_id(1)

    @pl.when(j == 0)
    def _():
        xf = x_ref[...]
        ms = jnp.mean(xf * xf, axis=-1, keepdims=True)
        xn_ref[...] = (xf * lax.rsqrt(ms + EPS) * g_ref[...]).astype(BF16)

    def emit(mode, outs, a, b, dst):
        for p0 in range(a, b, PROJ_PIECE):
            p1 = min(p0 + PROJ_PIECE, b)
            w = w_ref[:, p0:p1].astype(BF16)
            if emit_w:
                wo_ref[:, p0:p1] = w
            acc = _dot(xn_ref[...], w)
            for h in range((p1 - p0) // HEAD_DIM):
                sub = slice(h * HEAD_DIM, (h + 1) * HEAD_DIM)
                src = slice(p0 + h * HEAD_DIM, p0 + (h + 1) * HEAD_DIM)
                to = slice(dst + p0 - a + h * HEAD_DIM, dst + p0 - a + (h + 1) * HEAD_DIM)
                if mode == 'raw':
                    outs[0][:, to] = acc[:, sub]
                    continue
                y = _head_rms(acc[:, sub], gain_ref[:, src])
                if mode in ('hnorm', 'hnorm_both'):
                    outs[0][:, to] = y
                if mode in ('hnorm_rope', 'hnorm_both'):
                    outs[-1][:, to] = _rope(y, cos_ref[...], sin_ref[...])

    ntiles = pl.num_programs(1)
    for tile in range((segs[-1][0] + segs[-1][1]) // tn):
        @pl.when(j == tile)
        def _(tile=tile):
            oi = 0
            for (start, width, mode) in segs:
                outs = out_refs[oi:oi + _n_outs(mode)]
                oi += _n_outs(mode)
                a, b = max(start, tile * tn), min(start + width, (tile + 1) * tn)
                if a < b:
                    emit(mode, outs, a - tile * tn, b - tile * tn, (a - start) % _seg_block(width, tn))

    @pl.when(j == ntiles - 1)
    def _():
        t = _dot(xn_ref[...], wt_ref[...]) + bias_ref[...]
        tail_ref[...] = _sigmoid(t) if tail_mode == 'sigmoid' else _log_sigmoid(t)


def _proj(x, norm_g, w, w_tail, gain_all, bias_tail, cosf, sinf, segs, tail_mode, *, tm, tn, emit_w=False, name):
    M, K = x.shape
    n_main = segs[-1][0] + segs[-1][1]
    assert M % tm == 0 and n_main % tn == 0 and (not emit_w or M == tm)
    nseq = cosf.shape[0] // tm
    out_shape, out_specs = [], []
    for (start, width, mode) in segs:
        bw = _seg_block(width, tn)
        assert width % bw == 0 and start % bw == 0
        for _ in range(_n_outs(mode)):
            out_shape.append(jax.ShapeDtypeStruct((M, width), F32))
            out_specs.append(pl.BlockSpec(
                (tm, bw), lambda i, j, s=start, bw=bw, n=width // bw: (i, jnp.clip((j * tn - s) // bw, 0, n - 1))))
    out_shape.append(jax.ShapeDtypeStruct((M, LANES), F32))
    out_specs.append(pl.BlockSpec((tm, LANES), lambda i, j: (i, 0)))
    if emit_w:
        out_shape.append(jax.ShapeDtypeStruct((K, n_main), BF16))
        out_specs.append(pl.BlockSpec((K, tn), lambda i, j: (0, j)))
    return pl.pallas_call(
        functools.partial(_proj_kernel, tuple(segs), tail_mode, tn, emit_w),
        out_shape=out_shape,
        grid=(M // tm, n_main // tn),
        in_specs=[
            pl.BlockSpec((tm, K), lambda i, j: (i, 0)),
            pl.BlockSpec((1, K), lambda i, j: (0, 0)),
            pl.BlockSpec((K, tn), lambda i, j: (0, j)),
            pl.BlockSpec((K, LANES), lambda i, j: (0, 0)),
            pl.BlockSpec((1, tn), lambda i, j: (0, j)),
            pl.BlockSpec((1, LANES), lambda i, j: (0, 0)),
            pl.BlockSpec((tm, HEAD_DIM), lambda i, j: (i % nseq, 0)),
            pl.BlockSpec((tm, HEAD_DIM), lambda i, j: (i % nseq, 0)),
        ],
        out_specs=out_specs,
        scratch_shapes=[pltpu.VMEM((tm, K), BF16)],
        compiler_params=_cparams(("arbitrary", "arbitrary")),
        name=name,
    )(x, norm_g, w, w_tail, gain_all, bias_tail, cosf, sinf)


def _outproj_kernel(group_sizes, *refs):
    n_lhs = sum(group_sizes)
    lhs = refs[:n_lhs]
    w_ref, res_ref, o_ref, xs_ref = refs[n_lhs:n_lhs + 4]

    @pl.when(pl.program_id(1) == 0)
    def _():
        k0, r0 = 0, 0
        for n in group_sizes:
            x = lhs[r0][...]
            for r in lhs[r0 + 1:r0 + n]:
                x = x + r[...]
            xs_ref[:, k0:k0 + x.shape[1]] = x.astype(BF16)
            k0, r0 = k0 + x.shape[1], r0 + n

    o_ref[...] = res_ref[...] + _dot(xs_ref[...], w_ref[...])


def _outproj(lhs_groups, w, res, *, tm, tn, name):
    M = res.shape[0]
    K, N = w.shape
    flat = [a for g in lhs_groups for a in g]
    assert sum(g[0].shape[1] for g in lhs_groups) == K
    return pl.pallas_call(
        functools.partial(_outproj_kernel, tuple(len(g) for g in lhs_groups)),
        out_shape=jax.ShapeDtypeStruct((M, N), F32),
        grid=(M // tm, N // tn),
        in_specs=[pl.BlockSpec((tm, a.shape[1]), lambda i, j: (i, 0)) for a in flat] + [
            pl.BlockSpec((K, tn), lambda i, j: (0, j)),
            pl.BlockSpec((tm, tn), lambda i, j: (i, j)),
        ],
        out_specs=pl.BlockSpec((tm, tn), lambda i, j: (i, j)),
        scratch_shapes=[pltpu.VMEM((tm, K), BF16)],
        compiler_params=_cparams(("arbitrary", "arbitrary")),
        name=name,
    )(*flat, w, res)


def _ffn_kernel(emit, x_ref, g_ref, wg_ref, wu_ref, wd_ref, o_ref, *rest):
    xn_ref = rest[-1]
    f = pl.program_id(1)

    @pl.when(f == 0)
    def _():
        xf = x_ref[...]
        ms = jnp.mean(xf * xf, axis=-1, keepdims=True)
        xn_ref[...] = (xf * lax.rsqrt(ms + EPS) * g_ref[...]).astype(BF16)
        o_ref[...] = xf

    xn = xn_ref[...]
    wg, wu, wd = wg_ref[...].astype(BF16), wu_ref[...].astype(BF16), wd_ref[...].astype(BF16)
    if emit:
        rest[0][...], rest[1][...], rest[2][...] = wg, wu, wd
    h = _silu(_dot(xn, wg)) * _dot(xn, wu)
    o_ref[...] += _dot(h.astype(BF16), wd)


def _ffn(x, norm_g, wg, wu, wd, *, tm, tf, emit=False, name):
    M, K = x.shape
    F = wg.shape[1]
    assert not emit or M == tm
    w_specs = [pl.BlockSpec((K, tf), lambda i, f: (0, f)), pl.BlockSpec((K, tf), lambda i, f: (0, f)),
               pl.BlockSpec((tf, K), lambda i, f: (f, 0))]
    out_shape = [jax.ShapeDtypeStruct((M, K), F32)]
    out_specs = [pl.BlockSpec((tm, K), lambda i, f: (i, 0))]
    if emit:
        out_shape += [jax.ShapeDtypeStruct(w.shape, BF16) for w in (wg, wu, wd)]
        out_specs += w_specs
    out = pl.pallas_call(
        functools.partial(_ffn_kernel, emit),
        out_shape=out_shape,
        grid=(M // tm, F // tf),
        in_specs=[pl.BlockSpec((tm, K), lambda i, f: (i, 0)), pl.BlockSpec((1, K), lambda i, f: (0, 0))] + w_specs,
        out_specs=out_specs,
        scratch_shapes=[pltpu.VMEM((tm, K), BF16)],
        compiler_params=_cparams(("arbitrary", "arbitrary")),
        name=name,
    )(x, norm_g, wg, wu, wd)
    return (out[0], tuple(out[1:])) if emit else out[0]


def _moe_kernel(emit, x_ref, g_ref, rw_ref, rb_ref, wg_ref, wu_ref, wd_ref, o_ref, *rest):
    xn_ref, gate_ref = rest[-2:]
    e = pl.program_id(1)
    f = pl.program_id(2)

    @pl.when((e == 0) & (f == 0))
    def _():
        xf = x_ref[...]
        ms = jnp.mean(xf * xf, axis=-1, keepdims=True)
        xn = (xf * lax.rsqrt(ms + EPS) * g_ref[...]).astype(BF16)
        xn_ref[...] = xn
        o_ref[...] = xf
        lane = lax.broadcasted_iota(jnp.int32, (xf.shape[0], LANES), 1)
        logits = jnp.where(lane < N_EXPERTS, _dot(xn, rw_ref[...]) + rb_ref[...], NEG)
        m1 = jnp.max(logits, axis=-1, keepdims=True)
        i1 = jnp.min(jnp.where(logits == m1, lane, LANES), axis=-1, keepdims=True)
        l2 = jnp.where(lane == i1, NEG, logits)
        m2 = jnp.max(l2, axis=-1, keepdims=True)
        i2 = jnp.min(jnp.where(l2 == m2, lane, LANES), axis=-1, keepdims=True)
        e2 = jnp.exp(m2 - m1)
        w1 = 1.0 / (1.0 + e2)
        gate_ref[...] = jnp.where(lane == i1, w1, 0.0) + jnp.where(lane == i2, e2 * w1, 0.0)

    xn = xn_ref[...]
    wg, wu, wd = wg_ref[0].astype(BF16), wu_ref[0].astype(BF16), wd_ref[0].astype(BF16)
    if emit:
        rest[0][0], rest[1][0], rest[2][0] = wg, wu, wd
    h = _silu(_dot(xn, wg)) * _dot(xn, wu)
    lane = lax.broadcasted_iota(jnp.int32, gate_ref.shape, 1)
    ge = jnp.sum(jnp.where(lane == e, gate_ref[...], 0.0), axis=-1, keepdims=True)
    o_ref[...] += ge * _dot(h.astype(BF16), wd)


def _moe(x, norm_g, rw, rb, wg, wu, wd, *, tm, tf, emit=False, name):
    M, K = x.shape
    E, _, F = wg.shape
    assert not emit or M == tm
    w_specs = [pl.BlockSpec((1, K, tf), lambda i, e, f: (e, 0, f)), pl.BlockSpec((1, K, tf), lambda i, e, f: (e, 0, f)),
               pl.BlockSpec((1, tf, K), lambda i, e, f: (e, f, 0))]
    out_shape = [jax.ShapeDtypeStruct((M, K), F32)]
    out_specs = [pl.BlockSpec((tm, K), lambda i, e, f: (i, 0))]
    if emit:
        out_shape += [jax.ShapeDtypeStruct(w.shape, BF16) for w in (wg, wu, wd)]
        out_specs += w_specs
    out = pl.pallas_call(
        functools.partial(_moe_kernel, emit),
        out_shape=out_shape,
        grid=(M // tm, E, F // tf),
        in_specs=[
            pl.BlockSpec((tm, K), lambda i, e, f: (i, 0)),
            pl.BlockSpec((1, K), lambda i, e, f: (0, 0)),
            pl.BlockSpec((K, LANES), lambda i, e, f: (0, 0)),
            pl.BlockSpec((1, LANES), lambda i, e, f: (0, 0)),
        ] + w_specs,
        out_specs=out_specs,
        scratch_shapes=[pltpu.VMEM((tm, K), BF16), pltpu.VMEM((tm, LANES), F32)],
        compiler_params=_cparams(("arbitrary", "arbitrary", "arbitrary")),
        name=name,
    )(x, norm_g, rw, rb, wg, wu, wd)
    return (out[0], tuple(out[1:])) if emit else out[0]


TM_E = 512
EXPERT_ISSUE_STEPS = 8


def _top2(logits, lane):
    m1 = jnp.max(logits, axis=-1, keepdims=True)
    i1 = jnp.min(jnp.where(logits == m1, lane, LANES), axis=-1, keepdims=True)
    l2 = jnp.where(lane == i1, NEG, logits)
    m2 = jnp.max(l2, axis=-1, keepdims=True)
    i2 = jnp.min(jnp.where(l2 == m2, lane, LANES), axis=-1, keepdims=True)
    e2 = jnp.exp(m2 - m1)
    w1 = 1.0 / (1.0 + e2)
    return i1, i2, w1, e2 * w1


def _route_kernel(x_ref, g_ref, rw_ref, rb_ref, xn_ref, route_ref, cnt_ref, carry):
    @pl.when(pl.program_id(0) == 0)
    def _():
        carry[...] = jnp.zeros_like(carry)

    xf = x_ref[...]
    tm = xf.shape[0]
    xn = xf * lax.rsqrt(jnp.mean(xf * xf, axis=-1, keepdims=True) + EPS) * g_ref[...]
    xn_ref[...] = xn
    lane = lax.broadcasted_iota(jnp.int32, (tm, LANES), 1)
    logits = jnp.where(lane < N_EXPERTS, _dot(xn.astype(BF16), rw_ref[...]) + rb_ref[...], NEG)
    i1, i2, w1, w2 = _top2(logits, lane)
    hit = jnp.where(lane == i1, 1.0, 0.0) + jnp.where(lane == i2, 1.0, 0.0)
    incl = _dot(_tril_bf16(tm), hit.astype(BF16)) + carry[...]
    before = incl - hit
    carry[...] = incl[tm - 1:tm, :]
    cnt_ref[...] = incl[tm - 1:tm, :]
    r1 = jnp.sum(jnp.where(lane == i1, before, 0.0), axis=-1, keepdims=True)
    r2 = jnp.sum(jnp.where(lane == i2, before, 0.0), axis=-1, keepdims=True)
    cols = (i1.astype(F32), i2.astype(F32), r1, r2, w1, w2)
    rec = jnp.zeros((tm, LANES), F32)
    for c, v in enumerate(cols):
        rec = jnp.where(lane == c, v, rec)
    route_ref[...] = rec


def _route(x, norm_g, rw, rb, *, tm):
    M, K = x.shape
    return pl.pallas_call(
        _route_kernel,
        out_shape=[jax.ShapeDtypeStruct((M, K), F32), jax.ShapeDtypeStruct((M, LANES), F32),
                   jax.ShapeDtypeStruct((1, LANES), F32)],
        grid=(M // tm,),
        in_specs=[pl.BlockSpec((tm, K), lambda i: (i, 0)), pl.BlockSpec((1, K), lambda i: (0, 0)),
                  pl.BlockSpec((K, LANES), lambda i: (0, 0)), pl.BlockSpec((1, LANES), lambda i: (0, 0))],
        out_specs=[pl.BlockSpec((tm, K), lambda i: (i, 0)), pl.BlockSpec((tm, LANES), lambda i: (i, 0)),
                   pl.BlockSpec((1, LANES), lambda i: (0, 0))],
        scratch_shapes=[pltpu.VMEM((1, LANES), F32)],
        compiler_params=_cparams(("arbitrary",)),
        name="moe_route",
    )(x, norm_g, rw, rb)


def _gather_start(src_hbm, idx_ref, first, n, dst, sem, unroll):
    def body(r, carry):
        pltpu.make_async_copy(src_hbm.at[pl.ds(idx_ref[first + r], 1)], dst.at[pl.ds(r, 1)], sem).start()
        return carry
    lax.fori_loop(0, n, body, 0, unroll=unroll)


def _gather_wait(dst, sem):
    pltpu.make_async_copy(dst, dst, sem).wait()


def _experts_kernel(te_ref, nu_ref, src_ref, x_hbm, wg_ref, wu_ref, wd_ref, o_ref, xbuf, xb_scr, acc_scr, sems):
    i, f = pl.program_id(0), pl.program_id(1)
    last = pl.num_programs(1) - 1
    live = i < nu_ref[0]
    slot = i % 2

    @pl.when(live & (f == 0))
    def _():
        @pl.when(i == 0)
        def _():
            _gather_start(x_hbm, src_ref, 0, TM_E, xbuf.at[0], sems.at[0], 8)

        _gather_wait(xbuf.at[slot], sems.at[slot])
        xb_scr[...] = xbuf[slot].astype(BF16)
        acc_scr[...] = jnp.zeros_like(acc_scr)

    @pl.when(live)
    def _():
        per_step = TM_E // EXPERT_ISSUE_STEPS
        first = (f - 1) * per_step

        @pl.when((i + 1 < nu_ref[0]) & (f >= 1) & (f <= EXPERT_ISSUE_STEPS))
        def _():
            for r in range(per_step):
                pltpu.make_async_copy(x_hbm.at[pl.ds(src_ref[(i + 1) * TM_E + first + r], 1)],
                                      xbuf.at[1 - slot, pl.ds(first + r, 1)], sems.at[1 - slot]).start()

        xb = xb_scr[...]
        h = _silu(_dot(xb, wg_ref[0])) * _dot(xb, wu_ref[0])
        acc_scr[...] += _dot(h.astype(BF16), wd_ref[0])

    @pl.when(f == last)
    def _():
        o_ref[...] = jnp.where(live, acc_scr[...], 0.0)


def _experts(tile_expert, n_used, src, xn, wg, wu, wd, *, tf):
    P = src.shape[0]
    E, K, F = wg.shape
    nf = F // tf
    col = lambda i, f, nu: jnp.where(i < nu[0], f, nf - 1)
    return pl.pallas_call(
        _experts_kernel,
        out_shape=jax.ShapeDtypeStruct((P, K), F32),
        grid_spec=pltpu.PrefetchScalarGridSpec(
            num_scalar_prefetch=3, grid=(P // TM_E, nf),
            in_specs=[pl.BlockSpec(memory_space=pl.ANY),
                      pl.BlockSpec((1, K, tf), lambda i, f, te, nu, sr: (te[i], 0, col(i, f, nu))),
                      pl.BlockSpec((1, K, tf), lambda i, f, te, nu, sr: (te[i], 0, col(i, f, nu))),
                      pl.BlockSpec((1, tf, K), lambda i, f, te, nu, sr: (te[i], col(i, f, nu), 0))],
            out_specs=pl.BlockSpec((TM_E, K), lambda i, f, te, nu, sr: (i, 0)),
            scratch_shapes=[pltpu.VMEM((2, TM_E, K), F32), pltpu.VMEM((TM_E, K), BF16), pltpu.VMEM((TM_E, K), F32),
                            pltpu.SemaphoreType.DMA((2,))]),
        compiler_params=_cparams(("arbitrary", "arbitrary")),
        name="moe_experts",
    )(tile_expert, n_used, src, xn, wg, wu, wd)


def _combine_kernel(tc, dest_ref, os_hbm, x_ref, route_ref, o_ref, buf, sems):
    i = pl.program_id(0)
    slot = i % 2

    @pl.when(i == 0)
    def _():
        _gather_start(os_hbm, dest_ref, 0, 2 * tc, buf.at[0], sems.at[0], 8)

    @pl.when(i + 1 < pl.num_programs(0))
    def _():
        _gather_start(os_hbm, dest_ref, (i + 1) * 2 * tc, 2 * tc, buf.at[1 - slot], sems.at[1 - slot], 8)

    _gather_wait(buf.at[slot], sems.at[slot])
    w1, w2 = route_ref[:, 4:5], route_ref[:, 5:6]
    o_ref[...] = x_ref[...] + (w1 * buf[slot, 0:tc, :] + w2 * buf[slot, tc:2 * tc, :])


def _combine(dest, os, x, route, *, tc):
    M, K = x.shape
    return pl.pallas_call(
        functools.partial(_combine_kernel, tc),
        out_shape=jax.ShapeDtypeStruct((M, K), F32),
        grid_spec=pltpu.PrefetchScalarGridSpec(
            num_scalar_prefetch=1, grid=(M // tc,),
            in_specs=[pl.BlockSpec(memory_space=pl.ANY),
                      pl.BlockSpec((tc, K), lambda i, d: (i, 0)), pl.BlockSpec((tc, LANES), lambda i, d: (i, 0))],
            out_specs=pl.BlockSpec((tc, K), lambda i, d: (i, 0)),
            scratch_shapes=[pltpu.VMEM((2, 2 * tc, K), F32), pltpu.SemaphoreType.DMA((2,))]),
        compiler_params=_cparams(("arbitrary",)),
        name="moe_combine",
    )(dest, os, x, route)


def _moe_routed(x, norm_g, rw, rb, wg, wu, wd, *, tm):
    M = x.shape[0]
    xn, route, counts = _route(x, norm_g, rw, rb, tm=tm)
    e = route[:, 0:2].astype(jnp.int32)
    rank = route[:, 2:4].astype(jnp.int32)
    cnt = counts[0, :N_EXPERTS].astype(jnp.int32)
    tiles = (cnt + TM_E - 1) // TM_E
    tile_end = jnp.cumsum(tiles)
    dest = ((tile_end - tiles) * TM_E)[e] + rank
    n_tiles = (2 * M) // TM_E + N_EXPERTS
    P = n_tiles * TM_E
    token = jnp.repeat(jnp.arange(M, dtype=jnp.int32), 2)
    src = jnp.zeros((P,), jnp.int32).at[dest.reshape(-1)].set(token)
    tile_expert = jnp.minimum(jnp.sum(tile_end[None, :] <= jnp.arange(n_tiles, dtype=jnp.int32)[:, None], axis=1),
                              N_EXPERTS - 1).astype(jnp.int32)
    os = _experts(tile_expert, tile_end[-1:].astype(jnp.int32), src, xn, wg, wu, wd, tf=256)
    tc = 256
    dest_tiles = dest.reshape(M // tc, tc, 2).transpose(0, 2, 1).reshape(-1).astype(jnp.int32)
    return _combine(dest_tiles, os, x, route, tc=tc)


HGRN_HEADS_PER_STEP = 4


def _hgrn_kernel(T, c, aq_ref, af_ref, ai_ref, ag_ref, lb_ref, gn_ref, o_ref, st_ref, st_scr):
    t = pl.program_id(2)

    @pl.when(t == 0)
    def _():
        st_scr[...] = jnp.zeros_like(st_scr)

    row = lax.broadcasted_iota(jnp.int32, (T, T), 0)
    col = lax.broadcasted_iota(jnp.int32, (T, T), 1)
    shift = c.bit_length() - 1
    same = (row >> shift) == (col >> shift)
    causal = same & (col <= row)
    tri = jnp.where(causal, 1.0, 0.0).astype(BF16)
    blk = jnp.where(same, 1.0, 0.0).astype(BF16)
    gn = gn_ref[...]
    for hh in range(HGRN_HEADS_PER_STEP):
        cols = slice(hh * HEAD_DIM, (hh + 1) * HEAD_DIM)
        lb = lb_ref[:, cols]
        gate = lb + (1.0 - lb) * _sigmoid(af_ref[:, cols])
        k = 1.0 - gate
        parts = _split3(jnp.log(gate))
        b = sum(_dot(tri, p) for p in parts)
        bend = sum(_dot(blk, p) for p in parts)
        qe = (_silu(aq_ref[:, cols]) * jnp.exp(b)).astype(BF16)
        ke = (k * jnp.exp(-b)).astype(BF16)
        kend = (k * jnp.exp(bend - b)).astype(BF16)
        vb = ai_ref[:, cols].astype(BF16)
        o_intra = _dot(jnp.where(causal, _dot_nt(qe, ke), 0.0).astype(BF16), vb)
        st = st_scr[hh]
        for u in range(T // c):
            rows = slice(u * c, (u + 1) * c)
            o = o_intra[rows] + _dot_nt(qe[rows], st.astype(BF16))
            st = st * jnp.exp(bend[u * c:u * c + 1]) + _dot_tn(vb[rows], kend[rows])
            o_ref[rows, cols] = _head_rms(o, gn) * _silu(ag_ref[rows, cols])
        st_scr[hh] = st

    @pl.when(t == pl.num_programs(2) - 1)
    def _():
        for hh in range(HGRN_HEADS_PER_STEP):
            st_ref[0, hh] = st_scr[hh].T


def _hgrn_prompt(hg, lb, gn, B, S, *, T, c):
    nt = S // T
    H = H_HGRN
    hp = HGRN_HEADS_PER_STEP
    W = hp * HEAD_DIM
    spec = lambda off: pl.BlockSpec((T, W), lambda b, h, t, off=off: (b * nt + t, off + h))
    return pl.pallas_call(
        functools.partial(_hgrn_kernel, T, c),
        out_shape=[jax.ShapeDtypeStruct((B * S, H * HEAD_DIM), F32),
                   jax.ShapeDtypeStruct((B, H, HEAD_DIM, HEAD_DIM), F32)],
        grid=(B, H // hp, nt),
        in_specs=[spec(0), spec(H // hp), spec(2 * H // hp), spec(3 * H // hp),
                  pl.BlockSpec((1, W), lambda b, h, t: (0, h)),
                  pl.BlockSpec((1, HEAD_DIM), lambda b, h, t: (0, 0))],
        out_specs=[pl.BlockSpec((T, W), lambda b, h, t: (b * nt + t, h)),
                   pl.BlockSpec((1, hp, HEAD_DIM, HEAD_DIM), lambda b, h, t: (b, h, 0, 0))],
        scratch_shapes=[pltpu.VMEM((hp, HEAD_DIM, HEAD_DIM), F32)],
        compiler_params=_cparams(("arbitrary", "arbitrary", "arbitrary")),
        name="hgrn_prompt",
    )(hg, hg, hg, hg, lb, gn)


BIAS_SHIFT = 4
BIAS_SLOT = 1 << BIAS_SHIFT


def _cumsum_kernel(lf_ref, pq_ref, pk_ref, oq_ref, ok_ref, aq_ref, ak_ref, carry):
    @pl.when(pl.program_id(1) == 0)
    def _():
        carry[...] = jnp.zeros_like(carry)

    tc = lf_ref.shape[0]
    c = _dot3(_tril_bf16(tc), lf_ref[...]) + carry[...]
    carry[...] = c[tc - 1:tc, :]
    parts = _split3(c * LOG2E)
    aq_ref[...] = sum(_dot(t, pq_ref[i]) for i, t in enumerate(parts)) + oq_ref[...]
    ak_ref[...] = (ok_ref[...] - sum(_dot(t, pk_ref[i]) for i, t in enumerate(parts))).astype(BF16)


def _bias_layout():
    pq, pk = np.zeros((3, LANES, LANES), np.float32), np.zeros((3, LANES, LANES), np.float32)
    oq, ok = np.zeros((1, LANES), np.float32), np.zeros((1, LANES), np.float32)
    for h in range(H_FOX):
        for i in range(3):
            pq[i, h, BIAS_SLOT * h + i] = 1.0
            pk[i, h, BIAS_SLOT * h + 3 + i] = 1.0
            oq[0, BIAS_SLOT * h + 3 + i] = 1.0
            ok[0, BIAS_SLOT * h + i] = 1.0
    return jnp.asarray(pq, BF16), jnp.asarray(pk, BF16), jnp.asarray(oq), jnp.asarray(ok)


def _seq_cumsum(lf, B, S, *, tc):
    n = S // tc
    const3 = pl.BlockSpec((3, LANES, LANES), lambda b, t: (0, 0, 0))
    const1 = pl.BlockSpec((1, LANES), lambda b, t: (0, 0))
    blk = pl.BlockSpec((tc, LANES), lambda b, t: (b * n + t, 0))
    return pl.pallas_call(
        _cumsum_kernel,
        out_shape=[jax.ShapeDtypeStruct((B * S, LANES), F32), jax.ShapeDtypeStruct((B * S, LANES), BF16)],
        grid=(B, n),
        in_specs=[blk, const3, const3, const1, const1],
        out_specs=[blk, blk],
        scratch_shapes=[pltpu.VMEM((1, LANES), F32)],
        compiler_params=_cparams(("arbitrary", "arbitrary")),
        name="seq_cumsum",
    )(lf, *_bias_layout())


def _online_init(m_scr, l_scr, acc_scr):
    m_scr[...] = jnp.full_like(m_scr, M_INIT)
    l_scr[...] = jnp.zeros_like(l_scr)
    acc_scr[...] = jnp.zeros_like(acc_scr)


def _online_update(s, vb, m_scr, l_scr, acc_scr):
    m_prev = m_scr[...]
    m_new = jnp.maximum(m_prev, jnp.max(s, axis=-1, keepdims=True))
    alpha = jnp.exp(m_prev - m_new)
    p = jnp.exp(s - m_new)
    l_scr[...] = alpha * l_scr[...] + jnp.sum(p, axis=-1, keepdims=True)
    acc_scr[...] = alpha * acc_scr[...] + _dot(p.astype(BF16), vb)
    m_scr[...] = m_new


def _online_result(l_scr, acc_scr):
    l = l_scr[...]
    return acc_scr[...] / jnp.where(l > 0.0, l, 1.0)


def _flash_t_step(sT, vT, m, l, acc_scr):
    m_new = jnp.maximum(m, jnp.max(sT, axis=0, keepdims=True))
    alpha = jnp.exp2(m - m_new)
    p = jnp.exp2(sT - m_new)
    acc_scr[...] = alpha * acc_scr[...] + _dot(vT, p.astype(BF16))
    return m_new, alpha * l + jnp.sum(p, axis=0, keepdims=True)


def _flash_t_causal(ka_scr, vt_scr, qa, acc_scr, n_full, n_diag, tk, qpos, unroll=1):
    R = qa.shape[1]
    acc_scr[...] = jnp.zeros_like(acc_scr)

    def body(it, carry):
        for u in range(unroll):
            kj = it * unroll + u
            carry = _flash_t_step(_dot(ka_scr[kj], qa), vt_scr[kj], *carry, acc_scr)
        return carry

    m, l = lax.fori_loop(0, n_full // unroll, body, (jnp.full((1, R), M_INIT, F32), jnp.zeros((1, R), F32)))
    for i in range(n_diag):
        kpos = (n_full + i) * tk + lax.broadcasted_iota(jnp.int32, (tk, R), 0)
        sT = jnp.where(kpos <= qpos, _dot(ka_scr[n_full + i], qa), NEG)
        m, l = _flash_t_step(sT, vt_scr[n_full + i], m, l, acc_scr)
    return acc_scr[...] / jnp.where(l > 0.0, l, 1.0)


def _fox_kernel(tq, tk, q_ref, k_ref, vt_ref, aq_ref, ak_ref, o_ref, ka_scr, vt_scr, acc_scr):
    h = pl.program_id(1)
    qi = pl.program_id(2)

    @pl.when(qi == 0)
    def _():
        for c in range(ka_scr.shape[0]):
            rows = slice(c * tk, (c + 1) * tk)
            ka_scr[c] = jnp.concatenate([k_ref[rows, :].astype(BF16), ak_ref[rows, :]], axis=1)
            vt_scr[c] = vt_ref[0, 0, :, rows].astype(BF16)

    lane = lax.broadcasted_iota(jnp.int32, (tq, LANES), 1)
    aq = jnp.where((lane >> BIAS_SHIFT) == h, aq_ref[...], 0.0)
    qa = jnp.concatenate([(q_ref[...] * QSCALE).T, aq.T], axis=0).astype(BF16)
    qpos = qi * tq + lax.broadcasted_iota(jnp.int32, (tk, tq), 1)
    o_ref[...] = _flash_t_causal(ka_scr, vt_scr, qa, acc_scr, qi * (tq // tk), tq // tk, tk, qpos,
                                 unroll=tq // tk).T


def _fox_prompt(fq, fk, fvt, aug_q, aug_k, B, S, *, tq, tk):
    nq = S // tq
    return pl.pallas_call(
        functools.partial(_fox_kernel, tq, tk),
        out_shape=jax.ShapeDtypeStruct(fq.shape, F32),
        grid=(B, H_FOX, nq),
        in_specs=[
            pl.BlockSpec((tq, HEAD_DIM), lambda b, h, qi: (b * nq + qi, h)),
            pl.BlockSpec((S, HEAD_DIM), lambda b, h, qi: (b, h)),
            pl.BlockSpec((1, 1, HEAD_DIM, S), lambda b, h, qi: (b, h, 0, 0)),
            pl.BlockSpec((tq, LANES), lambda b, h, qi: (b * nq + qi, 0)),
            pl.BlockSpec((S, LANES), lambda b, h, qi: (b, 0)),
        ],
        out_specs=pl.BlockSpec((tq, HEAD_DIM), lambda b, h, qi: (b * nq + qi, h)),
        scratch_shapes=[pltpu.VMEM((S // tk, tk, 2 * HEAD_DIM), BF16), pltpu.VMEM((S // tk, HEAD_DIM, tk), BF16),
                        pltpu.VMEM((HEAD_DIM, tq), F32)],
        compiler_params=_cparams(("arbitrary",) * 3),
        name="fox_prompt",
    )(fq, fk, fvt, aug_q, aug_k)


def _compress_kernel(nc, x_ref, pea_ref, peb_ref, w1a_ref, w1b_ref, w2_ref, o_ref):
    R = jnp.concatenate([x_ref[0, pl.ds(r, nc, stride=CMP_STRIDE), :] for r in range(CMP_STRIDE)], axis=1)
    p1 = _dot((R + pea_ref[...]).astype(BF16), w1a_ref[...])
    p2 = _dot((R + peb_ref[...]).astype(BF16), w1b_ref[...])
    pre = p1 + pltpu.roll(p2, nc - 1, 0)
    kc = _dot(_silu(pre).astype(BF16), w2_ref[...])
    row = lax.broadcasted_iota(jnp.int32, kc.shape, 0)
    o_ref[0, 0] = jnp.where(row < nc - 1, kc, 0.0)


def _cmp_weights(pe, w1, w2):
    half = CMP_STRIDE * HEAD_DIM
    return (pe[:CMP_STRIDE].reshape(1, half), pe[CMP_STRIDE:].reshape(1, half),
            w1[:half].astype(BF16), w1[half:].astype(BF16), w2.astype(BF16))


def _compress_prompt(x, cw, B, S):
    nc = S // CMP_STRIDE
    half = CMP_STRIDE * HEAD_DIM
    const = lambda shape: pl.BlockSpec(shape, lambda b, g: (0, 0))
    return pl.pallas_call(
        functools.partial(_compress_kernel, nc),
        out_shape=jax.ShapeDtypeStruct((B, NSA_GROUPS, nc, HEAD_DIM), F32),
        grid=(B, NSA_GROUPS),
        in_specs=[pl.BlockSpec((1, S, HEAD_DIM), lambda b, g: (b, 0, g)),
                  const((1, half)), const((1, half)), const((half, HEAD_DIM)), const((half, HEAD_DIM)),
                  const((HEAD_DIM, HEAD_DIM))],
        out_specs=pl.BlockSpec((1, 1, nc, HEAD_DIM), lambda b, g: (b, g, 0, 0)),
        compiler_params=_cparams(("arbitrary", "arbitrary")),
        name="nsa_compress",
    )(x, *cw)


def _overlap_t(n_cmp, n_sel):
    cs = np.arange(n_cmp)[None, :] * CMP_STRIDE
    ss = np.arange(n_sel)[:, None] * SEL_LEN
    ov = np.minimum(cs + CMP_LEN, ss + SEL_LEN) - np.maximum(cs, ss)
    return np.maximum(ov, 0).astype(np.float32) / CMP_LEN


def _group_rows(ref, Tq):
    return jnp.concatenate([ref[0, :, r * HEAD_DIM:(r + 1) * HEAD_DIM] for r in range(NSA_R)], axis=0)


def _store_gated(o_ref, o, gates, Tq, branch):
    for r in range(NSA_R):
        gcol = gates[:, 3 * r + branch:3 * r + branch + 1]
        o_ref[0, :, r * HEAD_DIM:(r + 1) * HEAD_DIM] = o[r * Tq:(r + 1) * Tq] * gcol


def _cmpsel_kernel(Tq, nsel, q_ref, kc_ref, vc_ref, ovt_ref, gate_ref, o_ref, sel_ref):
    qi = pl.program_id(2)
    q = (_group_rows(q_ref, Tq) * (HEAD_DIM ** -0.5)).astype(BF16)
    s = _dot_nt(q, kc_ref[0, 0].astype(BF16))
    qpos = qi * Tq + (lax.broadcasted_iota(jnp.int32, s.shape, 0) & (Tq - 1))
    cend = lax.broadcasted_iota(jnp.int32, s.shape, 1) * CMP_STRIDE + (CMP_LEN - 1)
    mask = cend <= qpos
    sm = jnp.where(mask, s, NEG)
    e = jnp.where(mask, jnp.exp(sm - jnp.max(sm, axis=-1, keepdims=True)), 0.0)
    den = jnp.sum(e, axis=-1, keepdims=True)
    p = e / jnp.where(den > 0.0, den, 1.0)
    o = _dot(p.astype(BF16), vc_ref[0, 0].astype(BF16))
    _store_gated(o_ref, o, gate_ref[0, 0], Tq, 0)

    psum = p[0:Tq]
    for r in range(1, NSA_R):
        psum = psum + p[r * Tq:(r + 1) * Tq]
    ovt = ovt_ref[...]
    sc = sum(_dot_nt(ovt, t) for t in _split3(psum))
    n = lax.broadcasted_iota(jnp.int32, sc.shape, 0)
    tpos = qi * Tq + lax.broadcasted_iota(jnp.int32, sc.shape, 1)
    cur = tpos >> SEL_SHIFT
    forced = (n == 0) | (n == cur) | (n == cur - 1)
    sc = jnp.where(n * SEL_LEN <= tpos, jnp.where(forced, -NEG, sc), NEG)
    ranks = []
    for v in range(nsel // 8):
        blk = sc[8 * v:8 * v + 8]
        nv = 8 * v + lax.broadcasted_iota(jnp.int32, blk.shape, 0)
        r = jnp.zeros(blk.shape, F32)
        for m in range(nsel):
            rowm = sc[m:m + 1, :]
            ge, gt = jnp.where(rowm >= blk, 1.0, 0.0), jnp.where(rowm > blk, 1.0, 0.0)
            if m < 8 * v:
                r = r + ge
            elif m > 8 * v + 7:
                r = r + gt
            else:
                r = r + jnp.where(nv > m, ge, gt)
        ranks.append(r)
    rank = jnp.concatenate(ranks, axis=0)
    selneg = jnp.where(rank < TOPN, 0.0, NEG)
    if nsel < LANES:
        selneg = jnp.concatenate([selneg, jnp.full((LANES - nsel, Tq), NEG, F32)], axis=0)
    sel_ref[0, 0, 0] = selneg.astype(BF16)


def _cmpsel_prompt(qn, kc, vc, gates_g, B, S, *, Tq):
    nq = S // Tq
    ncp = kc.shape[2]
    nsel = S // SEL_LEN
    ovt = jnp.asarray(_overlap_t(ncp, nsel), BF16)
    return pl.pallas_call(
        functools.partial(_cmpsel_kernel, Tq, nsel),
        out_shape=[jax.ShapeDtypeStruct((B, S, H_NSA * HEAD_DIM), F32),
                   jax.ShapeDtypeStruct((B, NSA_GROUPS, nq, LANES, Tq), BF16)],
        grid=(B, NSA_GROUPS, nq),
        in_specs=[pl.BlockSpec((1, Tq, NSA_R * HEAD_DIM), lambda b, g, qi: (b, qi, g)),
                  pl.BlockSpec((1, 1, ncp, HEAD_DIM), lambda b, g, qi: (b, g, 0, 0)),
                  pl.BlockSpec((1, 1, ncp, HEAD_DIM), lambda b, g, qi: (b, g, 0, 0)),
                  pl.BlockSpec((nsel, ncp), lambda b, g, qi: (0, 0)),
                  pl.BlockSpec((1, 1, Tq, 16), lambda b, g, qi: (b, g, qi, 0))],
        out_specs=[pl.BlockSpec((1, Tq, NSA_R * HEAD_DIM), lambda b, g, qi: (b, qi, g)),
                   pl.BlockSpec((1, 1, 1, LANES, Tq), lambda b, g, qi: (b, g, qi, 0, 0))],
        compiler_params=_cparams(("arbitrary",) * 3),
        name="nsa_cmpsel",
    )(qn, kc, vc, ovt, gates_g)


def _queries_t(q_ref, Tq):
    return jnp.concatenate([(q_ref[0, :, r * HEAD_DIM:(r + 1) * HEAD_DIM] * QSCALE).T for r in range(NSA_R)],
                           axis=1).astype(BF16)


def _store_gated_t(o_ref, oT, gates, Tq, branch):
    for r in range(NSA_R):
        gcol = gates[:, 3 * r + branch:3 * r + branch + 1]
        o_ref[0, :, r * HEAD_DIM:(r + 1) * HEAD_DIM] = oT[:, r * Tq:(r + 1) * Tq].T * gcol


def _sel_kernel(Tq, tk, q_ref, sn_ref, k_ref, vt_ref, e_ref, gate_ref, o_ref, ka_scr, vt_scr, acc_scr):
    qi = pl.program_id(2)

    @pl.when(qi == 0)
    def _():
        for c in range(ka_scr.shape[0]):
            rows = slice(c * tk, (c + 1) * tk)
            ka_scr[c] = jnp.concatenate([k_ref[0, rows, :].astype(BF16), e_ref[rows, :]], axis=1)
            vt_scr[c] = vt_ref[0, 0, :, rows].astype(BF16)

    sn = sn_ref[0, 0, 0]
    qa = jnp.concatenate([_queries_t(q_ref, Tq), jnp.concatenate([sn] * NSA_R, axis=1)], axis=0)
    R = NSA_R * Tq
    qpos = qi * Tq + (lax.broadcasted_iota(jnp.int32, (tk, R), 1) & (Tq - 1))
    oT = _flash_t_causal(ka_scr, vt_scr, qa, acc_scr, (qi * Tq) // tk, 1, tk, qpos)
    _store_gated_t(o_ref, oT, gate_ref[0, 0], Tq, 1)


def _sel_prompt(qr, selneg, sk, svt, gates_g, B, S, *, Tq, tk):
    nq = S // Tq
    key = np.arange(S)[:, None] // SEL_LEN
    e_all = jnp.asarray((key == np.arange(LANES)[None, :]).astype(np.float32), BF16)
    R = NSA_R * Tq
    return pl.pallas_call(
        functools.partial(_sel_kernel, Tq, tk),
        out_shape=jax.ShapeDtypeStruct((B, S, H_NSA * HEAD_DIM), F32),
        grid=(B, NSA_GROUPS, nq),
        in_specs=[pl.BlockSpec((1, Tq, NSA_R * HEAD_DIM), lambda b, g, qi: (b, qi, g)),
                  pl.BlockSpec((1, 1, 1, LANES, Tq), lambda b, g, qi: (b, g, qi, 0, 0)),
                  pl.BlockSpec((1, S, HEAD_DIM), lambda b, g, qi: (b, 0, g)),
                  pl.BlockSpec((1, 1, HEAD_DIM, S), lambda b, g, qi: (b, g, 0, 0)),
                  pl.BlockSpec((S, LANES), lambda b, g, qi: (0, 0)),
                  pl.BlockSpec((1, 1, Tq, 16), lambda b, g, qi: (b, g, qi, 0))],
        out_specs=pl.BlockSpec((1, Tq, NSA_R * HEAD_DIM), lambda b, g, qi: (b, qi, g)),
        scratch_shapes=[pltpu.VMEM((S // tk, tk, 2 * HEAD_DIM), BF16), pltpu.VMEM((S // tk, HEAD_DIM, tk), BF16),
                        pltpu.VMEM((HEAD_DIM, R), F32)],
        compiler_params=_cparams(("arbitrary",) * 3),
        name="nsa_sel",
    )(qr, selneg, sk, svt, e_all, gates_g)


def _win_kernel(Tq, nwb, q_ref, *refs):
    k_refs, vt_refs = refs[:nwb], refs[nwb:2 * nwb]
    gate_ref, o_ref = refs[2 * nwb:]
    qi = pl.program_id(2)
    R = NSA_R * Tq
    qT = _queries_t(q_ref, Tq)
    qpos = qi * Tq + (lax.broadcasted_iota(jnp.int32, (Tq, R), 1) & (Tq - 1))
    krow = lax.broadcasted_iota(jnp.int32, (Tq, R), 0)
    s = []
    for i in range(nwb):
        kb = qi - (nwb - 1) + i
        si = _dot(k_refs[i][0].astype(BF16), qT)
        if i == 0:
            si = jnp.where(qpos - (kb * Tq + krow) < WINDOW, si, NEG)
        if i == nwb - 1:
            si = jnp.where(kb * Tq + krow <= qpos, si, NEG)
        else:
            si = si + jnp.where(kb >= 0, 0.0, NEG)
        s.append(si)
    m = functools.reduce(jnp.maximum, [jnp.max(si, axis=0, keepdims=True) for si in s])
    p = [jnp.exp2(si - m) for si in s]
    l = sum(jnp.sum(pi, axis=0, keepdims=True) for pi in p)
    oT = sum(_dot(vt_refs[i][0, 0].astype(BF16), p[i].astype(BF16)) for i in range(nwb)) / l
    _store_gated_t(o_ref, oT, gate_ref[0, 0], Tq, 2)


def _win_prompt(qr, wk, wvt, gates_g, B, S, *, Tq):
    nq = S // Tq
    nwb = WINDOW // Tq + 1
    blk = lambda i: (lambda qi: jnp.maximum(qi - (nwb - 1) + i, 0))
    kspec = lambda i: pl.BlockSpec((1, Tq, HEAD_DIM), lambda b, g, qi, f=blk(i): (b, f(qi), g))
    vspec = lambda i: pl.BlockSpec((1, 1, HEAD_DIM, Tq), lambda b, g, qi, f=blk(i): (b, g, 0, f(qi)))
    return pl.pallas_call(
        functools.partial(_win_kernel, Tq, nwb),
        out_shape=jax.ShapeDtypeStruct((B, S, H_NSA * HEAD_DIM), F32),
        grid=(B, NSA_GROUPS, nq),
        in_specs=[pl.BlockSpec((1, Tq, NSA_R * HEAD_DIM), lambda b, g, qi: (b, qi, g))]
        + [kspec(i) for i in range(nwb)] + [vspec(i) for i in range(nwb)]
        + [pl.BlockSpec((1, 1, Tq, 16), lambda b, g, qi: (b, g, qi, 0))],
        out_specs=pl.BlockSpec((1, Tq, NSA_R * HEAD_DIM), lambda b, g, qi: (b, qi, g)),
        compiler_params=_cparams(("arbitrary",) * 3),
        name="nsa_win",
    )(qr, *([wk] * nwb), *([wvt] * nwb), gates_g)


def _hgrn_step_kernel(aq_ref, af_ref, ai_ref, ag_ref, lb_ref, gn_ref, s_ref, o_ref, so_ref):
    for h in range(s_ref.shape[1]):
        lb = lb_ref[h]
        gate = lb + (1.0 - lb) * _sigmoid(af_ref[0, h])
        s_new = s_ref[0, h] * gate + (1.0 - gate) * ai_ref[0, h]
        so_ref[0, h] = s_new
        o = jnp.sum(_silu(aq_ref[0, h]) * s_new, axis=0, keepdims=True)
        o_ref[0, h] = _head_rms(o, gn_ref[...]) * _silu(ag_ref[0, h])


def _hgrn_step(hg, lb, gn, state):
    D, H = state.shape[:2]
    W = H * HEAD_DIM
    colv = lambda a: a.reshape(D, H, HEAD_DIM, 1)
    rowv = lambda a: a.reshape(D, H, 1, HEAD_DIM)
    cspec = pl.BlockSpec((1, H, HEAD_DIM, 1), lambda d: (d, 0, 0, 0))
    rspec = pl.BlockSpec((1, H, 1, HEAD_DIM), lambda d: (d, 0, 0, 0))
    sspec = pl.BlockSpec((1, H, HEAD_DIM, HEAD_DIM), lambda d: (d, 0, 0, 0))
    o, s_new = pl.pallas_call(
        _hgrn_step_kernel,
        out_shape=[jax.ShapeDtypeStruct((D, H, 1, HEAD_DIM), F32), jax.ShapeDtypeStruct(state.shape, F32)],
        grid=(D,),
        in_specs=[cspec, cspec, rspec, rspec,
                  pl.BlockSpec((H, HEAD_DIM, 1), lambda d: (0, 0, 0)),
                  pl.BlockSpec((1, HEAD_DIM), lambda d: (0, 0)), sspec],
        out_specs=[rspec, sspec],
        compiler_params=_cparams(("arbitrary",)),
        name="hgrn_step",
    )(colv(hg[:, :W]), colv(hg[:, W:2 * W]), rowv(hg[:, 2 * W:3 * W]), rowv(hg[:, 3 * W:]),
      lb.reshape(H, HEAD_DIM, 1), gn, state)
    return o.reshape(D, W), s_new


def _fox_step_kernel(PG, pt_ref, q_ref, kn_ref, vn_ref, lfn_ref, *refs):
    k_refs, v_refs, lf_refs = refs[:PG], refs[PG:2 * PG], refs[2 * PG:3 * PG]
    o_ref, m_scr, l_scr, acc_scr, carry = refs[3 * PG:]
    W = H_FOX * PAGE
    q = q_ref[0] * (HEAD_DIM ** -0.5)

    @pl.when(pl.program_id(1) == 0)
    def _():
        m_scr[...] = jnp.sum(q * kn_ref[0], axis=-1, keepdims=True)
        l_scr[...] = jnp.ones_like(l_scr)
        acc_scr[...] = vn_ref[0]
        carry[...] = jnp.broadcast_to(lfn_ref[0], carry.shape)

    qb = q.astype(BF16)
    lane = lax.broadcasted_iota(jnp.int32, (H_FOX, W), 1)
    head = lax.broadcasted_iota(jnp.int32, (H_FOX, W), 0)
    plane = lax.broadcasted_iota(jnp.int32, (PG, W), 1)
    lf = jnp.concatenate([lf_refs[i][0] for i in range(PG)], axis=0)
    suf, tot = lf, lf
    step = H_FOX
    while step < W:
        suf = suf + jnp.where(plane + step < W, pltpu.roll(suf, W - step, 1), 0.0)
        tot = tot + pltpu.roll(tot, step, 1)
        step *= 2
    c = carry[...]
    s = []
    for i in range(PG):
        si = _dot_nt(qb, k_refs[i][0].astype(BF16)) + (c + (suf[i:i + 1] - lf[i:i + 1]))
        s.append(jnp.where((lane & (H_FOX - 1)) == head, si, NEG))
        c = c + tot[i:i + 1]
    carry[...] = c
    m_prev = m_scr[...]
    m_new = functools.reduce(jnp.maximum, [jnp.max(si, axis=-1, keepdims=True) for si in s] + [m_prev])
    alpha = jnp.exp(m_prev - m_new)
    p = [jnp.exp(si - m_new) for si in s]
    l_scr[...] = alpha * l_scr[...] + sum(jnp.sum(pi, axis=-1, keepdims=True) for pi in p)
    acc_scr[...] = alpha * acc_scr[...] + sum(_dot(p[i].astype(BF16), v_refs[i][0].astype(BF16)) for i in range(PG))
    m_scr[...] = m_new

    @pl.when(pl.program_id(1) == pl.num_programs(1) - 1)
    def _():
        o_ref[0] = _online_result(l_scr, acc_scr)


def _fox_step(page_table, q, k_new, v_new, lf_new, cache_k, cache_v, cache_lf, *, PG):
    D, NP = page_table.shape
    n_pool = cache_k.shape[0]
    W = H_FOX * PAGE
    k2 = cache_k.reshape(n_pool, W, HEAD_DIM)
    v2 = cache_v.reshape(n_pool, W, HEAD_DIM)
    lf2 = cache_lf.astype(F32).reshape(n_pool, 1, W)
    lfn = jnp.tile(lf_new, (1, PAGE)).reshape(D, 1, W)
    page = lambda i: (lambda d, j, pt: (pt[d, NP - 1 - (j * PG + i)], 0, 0))
    hspec = pl.BlockSpec((1, H_FOX, HEAD_DIM), lambda d, j, pt: (d, 0, 0))
    in_specs = [hspec, hspec, hspec, pl.BlockSpec((1, 1, W), lambda d, j, pt: (d, 0, 0))]
    in_specs += [pl.BlockSpec((1, W, HEAD_DIM), page(i)) for i in range(PG)]
    in_specs += [pl.BlockSpec((1, W, HEAD_DIM), page(i)) for i in range(PG)]
    in_specs += [pl.BlockSpec((1, 1, W), page(i)) for i in range(PG)]
    return pl.pallas_call(
        functools.partial(_fox_step_kernel, PG),
        out_shape=jax.ShapeDtypeStruct((D, H_FOX, HEAD_DIM), F32),
        grid_spec=pltpu.PrefetchScalarGridSpec(
            num_scalar_prefetch=1, grid=(D, NP // PG), in_specs=in_specs, out_specs=hspec,
            scratch_shapes=[pltpu.VMEM((H_FOX, 1), F32), pltpu.VMEM((H_FOX, 1), F32),
                            pltpu.VMEM((H_FOX, HEAD_DIM), F32), pltpu.VMEM((H_FOX, W), F32)]),
        compiler_params=_cparams(("arbitrary", "arbitrary")),
        name="fox_step",
    )(page_table, q, k_new, v_new, lfn, *([k2] * PG), *([v2] * PG), *([lf2] * PG))


def _cmp_pages_kernel(PG, pt_ref, *refs):
    pages = refs[:PG]
    pea_ref, peb_ref, w1a_ref, w1b_ref, p1_ref, p2_ref, r_scr = refs[PG:]
    per_page = PAGE // CMP_STRIDE
    G = NSA_GROUPS
    for i in range(PG):
        for m in range(per_page):
            dst = (i * per_page + m) * G
            for r in range(CMP_STRIDE):
                src = (m * CMP_STRIDE + r) * G
                r_scr[dst:dst + G, r * HEAD_DIM:(r + 1) * HEAD_DIM] = pages[i][0, src:src + G, :]
    R = r_scr[...]
    p1_ref[0] = _dot((R + pea_ref[...]).astype(BF16), w1a_ref[...])
    p2_ref[0] = _dot((R + peb_ref[...]).astype(BF16), w1b_ref[...])


def _cmp_pages(page_table, cache, cw, *, PG):
    D, NP = page_table.shape
    n_pool = cache.shape[0]
    c2 = cache.reshape(n_pool, PAGE * NSA_GROUPS, HEAD_DIM)
    rows = PG * (PAGE // CMP_STRIDE) * NSA_GROUPS
    half = CMP_STRIDE * HEAD_DIM
    page = lambda i: (lambda d, j, pt: (pt[d, j * PG + i], 0, 0))
    const = lambda shape: pl.BlockSpec(shape, lambda d, j, pt: (0, 0))
    in_specs = [pl.BlockSpec((1, PAGE * NSA_GROUPS, HEAD_DIM), page(i)) for i in range(PG)]
    in_specs += [const((1, half)), const((1, half)), const((half, HEAD_DIM)), const((half, HEAD_DIM))]
    ospec = pl.BlockSpec((1, rows, HEAD_DIM), lambda d, j, pt: (d, j, 0))
    shp = jax.ShapeDtypeStruct((D, (NP // PG) * rows, HEAD_DIM), F32)
    return pl.pallas_call(
        functools.partial(_cmp_pages_kernel, PG),
        out_shape=[shp, shp],
        grid_spec=pltpu.PrefetchScalarGridSpec(
            num_scalar_prefetch=1, grid=(D, NP // PG), in_specs=in_specs, out_specs=[ospec, ospec],
            scratch_shapes=[pltpu.VMEM((rows, half), F32)]),
        compiler_params=_cparams(("arbitrary", "arbitrary")),
        name="nsa_cmp_pages",
    )(page_table, *([c2] * PG), *cw[:4])


def _cmpsel_step_kernel(qpos, nsp, q_ref, p1k_ref, p2k_ref, p1v_ref, p2v_ref, w2k_ref, w2v_ref,
                        ov_ref, g_ref, o_ref, idx_ref):
    nc = p1k_ref.shape[1] // NSA_GROUPS
    rows = pl.ds(pl.program_id(1), nc, stride=NSA_GROUPS)

    def finish(p1_ref, p2_ref, w2_ref):
        pre = p1_ref[0, rows, :] + pltpu.roll(p2_ref[0, rows, :], nc - 1, 0)
        return _dot(_silu(pre).astype(BF16), w2_ref[...]).astype(BF16)

    kc = finish(p1k_ref, p2k_ref, w2k_ref)
    vc = finish(p1v_ref, p2v_ref, w2v_ref)
    q = (q_ref[0, 0] * (HEAD_DIM ** -0.5)).astype(BF16)
    s = _dot_nt(q, kc)
    cend = lax.broadcasted_iota(jnp.int32, s.shape, 1) * CMP_STRIDE + (CMP_LEN - 1)
    mask = cend <= qpos
    sm = jnp.where(mask, s, NEG)
    e = jnp.where(mask, jnp.exp(sm - jnp.max(sm, axis=-1, keepdims=True)), 0.0)
    den = jnp.sum(e, axis=-1, keepdims=True)
    p = e / jnp.where(den > 0.0, den, 1.0)
    o_ref[0, 0] = _dot(p.astype(BF16), vc) * g_ref[0, 0][:, 0:1]

    psum = jnp.broadcast_to(jnp.sum(p[0:NSA_R], axis=0, keepdims=True), (8, nc))
    ov = ov_ref[...]
    sc_row = sum(_dot(t, ov) for t in _split3(psum))[0:1]
    n_lane = lax.broadcasted_iota(jnp.int32, (1, nsp), 1)
    cur = qpos // SEL_LEN
    forced = (n_lane == 0) | (n_lane == cur) | (n_lane == cur - 1)
    sc_row = jnp.where(n_lane * SEL_LEN <= qpos, jnp.where(forced, -NEG, sc_row), NEG)
    mi = lax.broadcasted_iota(jnp.int32, (nsp, nsp), 0)
    ni = lax.broadcasted_iota(jnp.int32, (nsp, nsp), 1)
    sc_col = jnp.sum(jnp.where(mi == ni, sc_row, 0.0), axis=-1, keepdims=True)
    beats = jnp.where(mi < ni, jnp.where(sc_col >= sc_row, 1.0, 0.0), jnp.where(sc_col > sc_row, 1.0, 0.0))
    rank = jnp.sum(beats, axis=0, keepdims=True)
    lane = lax.broadcasted_iota(jnp.int32, (1, LANES), 1)
    out = jnp.zeros((1, LANES), F32)
    for k in range(TOPN):
        nk = jnp.sum(jnp.where(rank == float(k), n_lane.astype(F32), 0.0), axis=-1, keepdims=True)
        out = jnp.where(lane == k, nk, out)
    idx_ref[0, 0] = jnp.broadcast_to(out, (8, LANES)).astype(jnp.int32)


def _cmpsel_step(q16, p1k, p2k, p1v, p2v, cwk, cwv, gate_rows, qpos):
    D, G = q16.shape[:2]
    nc = p1k.shape[1] // G
    n_sel = -(-(qpos + 1) // SEL_LEN)
    nsp = -(-n_sel // LANES) * LANES
    ov = jnp.asarray(np.pad(_overlap_t(nc, n_sel).T, ((0, 0), (0, nsp - n_sel))), BF16)
    big = pl.BlockSpec((1, nc * G, HEAD_DIM), lambda d, g: (d, 0, 0))
    qspec = pl.BlockSpec((1, 1, 16, HEAD_DIM), lambda d, g: (d, g, 0, 0))
    w2spec = pl.BlockSpec((HEAD_DIM, HEAD_DIM), lambda d, g: (0, 0))
    return pl.pallas_call(
        functools.partial(_cmpsel_step_kernel, qpos, nsp),
        out_shape=[jax.ShapeDtypeStruct((D, G, 16, HEAD_DIM), F32),
                   jax.ShapeDtypeStruct((D, G, 8, LANES), jnp.int32)],
        grid=(D, G),
        in_specs=[qspec, big, big, big, big, w2spec, w2spec,
                  pl.BlockSpec((nc, nsp), lambda d, g: (0, 0)),
                  pl.BlockSpec((1, 1, 16, 3), lambda d, g: (d, g, 0, 0))],
        out_specs=[qspec, pl.BlockSpec((1, 1, 8, LANES), lambda d, g: (d, g, 0, 0))],
        compiler_params=_cparams(("arbitrary", "arbitrary")),
        name="nsa_cmpsel_step",
    )(q16, p1k, p2k, p1v, p2v, cwk[4], cwv[4], ov, gate_rows)


def _sel_step_kernel(n_past, pt_ref, idx_ref, q_ref, kn_ref, vn_ref, *refs):
    k_refs, v_refs = refs[:TOPN], refs[TOPN:2 * TOPN]
    g_ref, o_ref = refs[2 * TOPN:]
    d, g = pl.program_id(0), pl.program_id(1)
    q = q_ref[0, 0] * (HEAD_DIM ** -0.5)
    qb = q.astype(BF16)
    rows = pl.ds(g, SEL_LEN, stride=NSA_GROUPS)
    s_new = jnp.sum(q * kn_ref[0, 0], axis=-1, keepdims=True)
    s = [_dot_nt(qb, k_refs[k][rows, :].astype(BF16)) + jnp.where(idx_ref[d, g, k] < n_past, 0.0, NEG)
         for k in range(TOPN)]
    m = functools.reduce(jnp.maximum, [jnp.max(sk, axis=-1, keepdims=True) for sk in s] + [s_new])
    p = [jnp.exp(sk - m) for sk in s]
    p_new = jnp.exp(s_new - m)
    l = sum(jnp.sum(pk, axis=-1, keepdims=True) for pk in p) + p_new
    o = sum(_dot(p[k].astype(BF16), v_refs[k][rows, :].astype(BF16)) for k in range(TOPN)) + p_new * vn_ref[0, 0]
    o_ref[0, 0] = o / l * g_ref[0, 0][:, 1:2]


def _sel_step(page_table, idx, q16, k_new, v_new, cache_k, cache_v, gate_rows):
    D, NP = page_table.shape
    G = NSA_GROUPS
    n_pool = cache_k.shape[0]
    per_page = PAGE // SEL_LEN
    n_past = NP * per_page
    blk_rows = SEL_LEN * G
    k2 = cache_k.reshape(n_pool * PAGE * G, HEAD_DIM)
    v2 = cache_v.reshape(n_pool * PAGE * G, HEAD_DIM)

    def blk(k):
        def index(d, g, pt, ix):
            n = jnp.minimum(ix[d, g, k], n_past - 1)
            return (pt[d, n // per_page] * per_page + n % per_page, 0)
        return pl.BlockSpec((blk_rows, HEAD_DIM), index)

    qspec = pl.BlockSpec((1, 1, 16, HEAD_DIM), lambda d, g, pt, ix: (d, g, 0, 0))
    nspec = pl.BlockSpec((1, 1, 1, HEAD_DIM), lambda d, g, pt, ix: (d, g, 0, 0))
    return pl.pallas_call(
        functools.partial(_sel_step_kernel, n_past),
        out_shape=jax.ShapeDtypeStruct((D, G, 16, HEAD_DIM), F32),
        grid_spec=pltpu.PrefetchScalarGridSpec(
            num_scalar_prefetch=2, grid=(D, G),
            in_specs=[qspec, nspec, nspec] + [blk(k) for k in range(TOPN)] * 2
            + [pl.BlockSpec((1, 1, 16, 3), lambda d, g, pt, ix: (d, g, 0, 0))],
            out_specs=qspec),
        compiler_params=_cparams(("arbitrary",) * 2),
        name="nsa_sel_step",
    )(page_table, idx, q16, k_new, v_new, *([k2] * TOPN), *([v2] * TOPN), gate_rows)


def _win_step_kernel(q_ref, k_ref, v_ref, g_ref, o_ref):
    q = (q_ref[0, 0] * (HEAD_DIM ** -0.5)).astype(BF16)
    s = _dot_nt(q, k_ref[0].astype(BF16))
    e = jnp.exp(s - jnp.max(s, axis=-1, keepdims=True))
    p = e / jnp.sum(e, axis=-1, keepdims=True)
    o_ref[0, 0] = _dot(p.astype(BF16), v_ref[0].astype(BF16)) * g_ref[0, 0][:, 2:3]


def _win_step(q16, kw, vw, gate_rows):
    D, G = q16.shape[:2]
    L = kw.shape[1]
    qspec = pl.BlockSpec((1, 1, 16, HEAD_DIM), lambda d, g: (d, g, 0, 0))
    kspec = pl.BlockSpec((1, L, HEAD_DIM), lambda d, g: (d, 0, g))
    return pl.pallas_call(
        _win_step_kernel,
        out_shape=jax.ShapeDtypeStruct((D, G, 16, HEAD_DIM), F32),
        grid=(D, G),
        in_specs=[qspec, kspec, kspec, pl.BlockSpec((1, 1, 16, 3), lambda d, g: (d, g, 0, 0))],
        out_specs=qspec,
        compiler_params=_cparams(("arbitrary", "arbitrary")),
        name="nsa_win_step",
    )(q16, kw, vw, gate_rows)


TN = 512
TM = 512
TM_STEP = 8


def _pad_cols(a, n):
    return jnp.pad(a, ((0, 0), (0, n - a.shape[1])))


TN_PROJ = 1024


def _segments(widths_modes):
    segs, start = [], 0
    for width, mode in widths_modes:
        segs.append((start, width, mode))
        start += width
    return segs, start


def _proj_weights(w_in, n_main):
    return w_in, _pad_cols(w_in[:, n_main:].astype(BF16), LANES)


def _even_params(e_norm_mix, e_w_in, lb, out_norm, f_bias, q_norm, k_norm, e_w_out, e_norm_ffn):
    d = e_w_in.shape[0]
    aw, bw = H_HGRN * HEAD_DIM, H_FOX * HEAD_DIM
    segs, n_main = _segments(((4 * aw, 'raw'), (bw, 'hnorm'), (bw, 'hnorm'), (bw, 'raw')))
    gain = jnp.concatenate([jnp.ones((4 * aw,), F32), jnp.tile(q_norm, H_FOX), jnp.tile(k_norm, H_FOX),
                            jnp.ones((bw,), F32)])
    w_in, w_tail = _proj_weights(e_w_in, n_main)
    return dict(norm=e_norm_mix.reshape(1, d), w_in=w_in, w_tail=w_tail, gain=gain.reshape(1, -1),
                bias=_pad_cols(f_bias.astype(F32).reshape(1, -1), LANES), segs=segs, lb=lb.reshape(1, aw),
                out_norm=out_norm.reshape(1, HEAD_DIM), w_out=e_w_out.astype(BF16), norm_ffn=e_norm_ffn.reshape(1, d))


def _even_proj(x2, p, w, tm, emit_w=False):
    zeros = jnp.zeros((tm, HEAD_DIM), F32)
    return _proj(x2, p['norm'], w, p['w_tail'], p['gain'], p['bias'], zeros, zeros, p['segs'], 'logsig',
                 tm=tm, tn=TN_PROJ, emit_w=emit_w, name="even_proj")


def _even_finish(x2, oa, of, p, ffn_w, tm, emit=False):
    x2 = _outproj([[oa], [of]], p['w_out'], x2, tm=tm, tn=TN_PROJ, name="even_out")
    return _ffn(x2, p['norm_ffn'], *ffn_w, tm=tm, tf=TN, emit=emit, name="ffn")


def _odd_params(o_norm_mix, o_w_in, q_norm, ck_norm, sk_norm, wk_norm, pe_k, w1_k, w2_k, pe_v, w1_v, w2_v,
                o_w_out, o_norm_ffn, router_w, router_b):
    d = o_w_in.shape[0]
    qw, kvw = H_NSA * HEAD_DIM, NSA_GROUPS * HEAD_DIM
    segs, n_main = _segments(((qw, 'hnorm_both'), (kvw, 'hnorm'), (kvw, 'raw'), (kvw, 'hnorm_rope'), (kvw, 'raw'),
                              (kvw, 'hnorm_rope'), (kvw, 'raw')))
    ones = jnp.ones((kvw,), F32)
    gain = jnp.concatenate([jnp.tile(q_norm, H_NSA), jnp.tile(ck_norm, NSA_GROUPS), ones,
                            jnp.tile(sk_norm, NSA_GROUPS), ones, jnp.tile(wk_norm, NSA_GROUPS), ones])
    w_in, w_tail = _proj_weights(o_w_in, n_main)
    return dict(norm=o_norm_mix.reshape(1, d), w_in=w_in, w_tail=w_tail, gain=gain.reshape(1, -1),
                bias=jnp.zeros((1, LANES), F32), segs=segs,
                cwk=_cmp_weights(pe_k, w1_k, w2_k), cwv=_cmp_weights(pe_v, w1_v, w2_v),
                w_out=o_w_out.astype(BF16), norm_ffn=o_norm_ffn.reshape(1, d),
                rw=_pad_cols(router_w.astype(BF16), LANES), rb=_pad_cols(router_b.reshape(1, -1).astype(F32), LANES))


def _odd_proj(x2, p, w, pos, tm, emit_w=False):
    cosf, sinf = _rope_tables(pos)
    return _proj(x2, p['norm'], w, p['w_tail'], p['gain'], p['bias'], cosf, sinf, p['segs'], 'sigmoid',
                 tm=tm, tn=TN_PROJ, emit_w=emit_w, name="odd_proj")


def _odd_finish(x2, branches, p, moe_w, tm, emit=False):
    x2 = _outproj([branches], p['w_out'], x2, tm=tm, tn=TN_PROJ, name="odd_out")
    args = (x2, p['norm_ffn'], p['rw'], p['rb'], *moe_w)
    if emit:
        return _moe(*args, tm=tm, tf=256, emit=True, name="moe")
    return _moe_routed(*args, tm=tm)


def kernel(x_prompt, x_sample, state_hgrn, cache_fox_k, cache_fox_v, cache_fox_logf, cache_nsa_cmp_k, cache_nsa_cmp_v, cache_nsa_sel_k, cache_nsa_sel_v, cache_nsa_win_k, cache_nsa_win_v, page_table, e_norm_mix, e_w_in, hgrn_lb_logits, hgrn_out_norm, fox_f_bias, fox_q_norm, fox_k_norm, e_w_out, e_norm_ffn, ffn_w_gate, ffn_w_up, ffn_w_down, o_norm_mix, o_w_in, nsa_q_norm, nsa_cmp_k_norm, nsa_sel_k_norm, nsa_win_k_norm, cmp_pe_k, cmp_w1_k, cmp_w2_k, cmp_pe_v, cmp_w1_v, cmp_w2_v, o_w_out, o_norm_ffn, router_w, router_b, moe_w_gate, moe_w_up, moe_w_down):
    B, S, d = x_prompt.shape
    D, T, _ = x_sample.shape
    NP = page_table.shape[1]
    past = NP * PAGE
    w_buf = cache_nsa_win_k.shape[2]
    assert T == 1 and w_buf == WINDOW and S >= WINDOW and cache_fox_k.shape[2] == PAGE
    G = NSA_GROUPS
    lbs = jnp.cumsum(jax.nn.softmax(hgrn_lb_logits.astype(F32), axis=0), axis=0)
    xp, xd = x_prompt.reshape(B * S, d), x_sample.reshape(D, d)
    tm_p = min(TM, S)

    li = 0
    p = _even_params(e_norm_mix[li], e_w_in[li], lbs[li], hgrn_out_norm[li], fox_f_bias[li], fox_q_norm[li],
                     fox_k_norm[li], e_w_out[li], e_norm_ffn[li])
    hg, fq, fk, fv, fl, w_in_bf16 = _even_proj(xd, p, p['w_in'], TM_STEP, emit_w=True)
    oa, st_d = _hgrn_step(hg, p['lb'], p['out_norm'], state_hgrn[li].astype(F32))
    heads = lambda a: a.reshape(D, H_FOX, HEAD_DIM)
    of = _fox_step(page_table, heads(fq), heads(fk), heads(fv), fl[:, :H_FOX],
                   cache_fox_k[li], cache_fox_v[li], cache_fox_logf[li], PG=16)
    xd, ffn_bf16 = _even_finish(xd, oa, of.reshape(D, H_FOX * HEAD_DIM), p,
                                (ffn_w_gate[li], ffn_w_up[li], ffn_w_down[li]), TM_STEP, emit=True)
    fox_d = (fk.reshape(1, D, 1, H_FOX, HEAD_DIM), fv.reshape(1, D, 1, H_FOX, HEAD_DIM),
             fl[:, :H_FOX].reshape(1, D, 1, H_FOX))

    hg, fq, fk, fv, fl = _even_proj(xp, p, w_in_bf16, tm_p)
    aug_q, aug_k = _seq_cumsum(fl, B, S, tc=tm_p)
    oa, st_p = _hgrn_prompt(hg, p['lb'], p['out_norm'], B, S, T=min(256, S), c=32)
    fvt = fv.reshape(B, S, H_FOX, HEAD_DIM).transpose(0, 2, 3, 1)
    of = _fox_prompt(fq, fk, fvt, aug_q, aug_k, B, S, tq=min(1024, S), tk=tm_p)
    xp = _even_finish(xp, oa, of, p, ffn_bf16, tm_p)
    fox_p = (fk.reshape(1, B, S, H_FOX, HEAD_DIM), fv.reshape(1, B, S, H_FOX, HEAD_DIM),
             fl[:, :H_FOX].reshape(1, B, S, H_FOX))

    p = _odd_params(o_norm_mix[li], o_w_in[li], nsa_q_norm[li], nsa_cmp_k_norm[li], nsa_sel_k_norm[li],
                    nsa_win_k_norm[li], cmp_pe_k[li], cmp_w1_k[li], cmp_w2_k[li], cmp_pe_v[li], cmp_w1_v[li],
                    cmp_w2_v[li], o_w_out[li], o_norm_ffn[li], router_w[li], router_b[li])
    qn, qr, ck, cv, sk, sv, wk, wv, gt, w_in_bf16 = _odd_proj(xd, p, p['w_in'], jnp.full((TM_STEP,), past), TM_STEP,
                                                              emit_w=True)
    rows16 = lambda a, w: jnp.pad(a.reshape(D, G, NSA_R, w), ((0, 0), (0, 0), (0, 16 - NSA_R), (0, 0)))
    gate_rows = rows16(gt[:, :3 * H_NSA], 3)
    q16n, q16r = rows16(qn, HEAD_DIM), rows16(qr, HEAD_DIM)
    p1k, p2k = _cmp_pages(page_table, cache_nsa_cmp_k[li], p['cwk'], PG=16)
    p1v, p2v = _cmp_pages(page_table, cache_nsa_cmp_v[li], p['cwv'], PG=16)
    o_cmp, idx = _cmpsel_step(q16n, p1k, p2k, p1v, p2v, p['cwk'], p['cwv'], gate_rows, past)
    new = lambda a: a.reshape(D, G, 1, HEAD_DIM)
    o_sel = _sel_step(page_table, idx[:, :, 0, :TOPN], q16r, new(sk), new(sv),
                      cache_nsa_sel_k[li], cache_nsa_sel_v[li], gate_rows)
    kvd = lambda a: a.reshape(D, 1, G, HEAD_DIM)
    win_k = jnp.concatenate([cache_nsa_win_k[li], kvd(wk)], axis=1)[:, -w_buf:]
    win_v = jnp.concatenate([cache_nsa_win_v[li], kvd(wv)], axis=1)[:, -w_buf:]
    o_win = _win_step(q16r, win_k.reshape(D, w_buf, G * HEAD_DIM), win_v.reshape(D, w_buf, G * HEAD_DIM), gate_rows)
    unrow = lambda a: a[:, :, :NSA_R].reshape(D, H_NSA * HEAD_DIM)
    xd, moe_bf16 = _odd_finish(xd, [unrow(o_cmp), unrow(o_sel), unrow(o_win)], p,
                               (moe_w_gate[li], moe_w_up[li], moe_w_down[li]), TM_STEP, emit=True)
    kv1 = lambda a: a.reshape(1, D, 1, G, HEAD_DIM)
    nsa_d = (kv1(ck), kv1(cv), kv1(sk), kv1(sv), win_k[None], win_v[None])

    qn, qr, ck, cv, sk, sv, wk, wv, gt = _odd_proj(xp, p, w_in_bf16, jnp.arange(S), tm_p)
    seq = lambda a: a.reshape(B, S, -1)
    gates_g = jnp.pad(gt[:, :3 * H_NSA].reshape(B, S, G, 3 * NSA_R).transpose(0, 2, 1, 3),
                      ((0, 0), (0, 0), (0, 0), (0, 16 - 3 * NSA_R)))
    kc = _compress_prompt(seq(ck), p['cwk'], B, S)
    vc = _compress_prompt(seq(cv), p['cwv'], B, S)
    tq = min(256, S)
    o_cmp, selneg = _cmpsel_prompt(seq(qn), kc, vc, gates_g, B, S, Tq=tq)
    keys_last = lambda a: a.reshape(B, S, G, HEAD_DIM).transpose(0, 2, 3, 1)
    o_sel = _sel_prompt(seq(qr), selneg, seq(sk), keys_last(sv), gates_g, B, S, Tq=tq, tk=min(512, S))
    o_win = _win_prompt(seq(qr), seq(wk), keys_last(wv), gates_g, B, S, Tq=tq)
    flat = lambda a: a.reshape(B * S, -1)
    xp = _odd_finish(xp, [flat(o_cmp), flat(o_sel), flat(o_win)], p, moe_bf16, tm_p)
    kv = lambda a: a.reshape(1, B, S, G, HEAD_DIM)
    nsa_p = (kv(ck), kv(cv), kv(sk), kv(sv), kv(wk)[:, :, S - WINDOW:], kv(wv)[:, :, S - WINDOW:])

    return (xp.reshape(B, S, d), xd.reshape(D, T, d), st_p[None].astype(state_hgrn.dtype),
            st_d[None].astype(state_hgrn.dtype), *fox_p, *fox_d, *nsa_p, *nsa_d)
```

```python
import functools

import numpy as np
import jax
import jax.numpy as jnp
from jax import lax
from jax.experimental import pallas as pl
from jax.experimental.pallas import tpu as pltpu

F32 = jnp.float32
BF16 = jnp.bfloat16

LANES = 128
HEAD_DIM = 128
EPS = 1e-6
ROPE_THETA = 10000.0
LOG2E = 1.4426950408889634
QSCALE = HEAD_DIM ** -0.5 * LOG2E
NEG = -1e30
M_INIT = -1e20
VMEM_LIMIT = 56 * 1024 * 1024

H_HGRN = 8
H_FOX = 8
H_NSA = 16
NSA_GROUPS = 4
NSA_R = H_NSA // NSA_GROUPS
CMP_STRIDE = 16
CMP_LEN = 32
SEL_LEN = 64
SEL_SHIFT = 6
TOPN = 16
WINDOW = 512
N_EXPERTS = 8
PAGE = 128


def _cparams(sem):
    return pltpu.CompilerParams(dimension_semantics=sem, vmem_limit_bytes=VMEM_LIMIT)


def _dot(a, b):
    return jnp.dot(a, b, preferred_element_type=F32)


def _dot_nt(a, b):
    return lax.dot_general(a, b, (((1,), (1,)), ((), ())), preferred_element_type=F32)


def _dot_tn(a, b):
    return lax.dot_general(a, b, (((0,), (0,)), ((), ())), preferred_element_type=F32)


def _split3(x):
    hi = x.astype(BF16)
    r1 = x - hi.astype(F32)
    mid = r1.astype(BF16)
    lo = (r1 - mid.astype(F32)).astype(BF16)
    return hi, mid, lo


def _dot3(w_bf16, x):
    hi, mid, lo = _split3(x)
    return _dot(w_bf16, hi) + _dot(w_bf16, mid) + _dot(w_bf16, lo)


def _sigmoid(x):
    return 1.0 / (1.0 + jnp.exp(-x))


def _silu(x):
    return x * _sigmoid(x)


def _log_sigmoid(x):
    return jnp.minimum(x, 0.0) - jnp.log(1.0 + jnp.exp(-jnp.abs(x)))


def _tril_bf16(n):
    r = lax.broadcasted_iota(jnp.int32, (n, n), 0)
    c = lax.broadcasted_iota(jnp.int32, (n, n), 1)
    return jnp.where(c <= r, 1.0, 0.0).astype(BF16)


def _head_rms(a, gain):
    ms = jnp.mean(a * a, axis=-1, keepdims=True)
    return a * lax.rsqrt(ms + EPS) * gain


def _rope(y, cosf, sinf):
    return y * cosf + pltpu.roll(y, HEAD_DIM // 2, 1) * sinf


def _rope_tables(pos):
    half = HEAD_DIM // 2
    inv = ROPE_THETA ** (-jnp.arange(half, dtype=F32) / half)
    ang = pos.astype(F32)[:, None] * inv[None, :]
    cos, sin = jnp.cos(ang), jnp.sin(ang)
    return jnp.concatenate([cos, cos], axis=-1), jnp.concatenate([-sin, sin], axis=-1)


PROJ_PIECE = 512


def _n_outs(mode):
    return 2 if mode == 'hnorm_both' else 1


def _seg_block(width, tn):
    return min(width, tn)


def _proj_kernel(segs, tail_mode, tn, *refs):
    n_out = sum(_n_outs(s[2]) for s in segs)
    x_ref, g_ref, w_ref, wt_ref, gain_ref, bias_ref, cos_ref, sin_ref = refs[:8]
    out_refs = refs[8:8 + n_out]
    tail_ref = refs[8 + n_out]
    xn_ref = refs[9 + n_out]
    j = pl.program_id(1)

    @pl.when(j == 0)
    def _():
        xf = x_ref[...]
        ms = jnp.mean(xf * xf, axis=-1, keepdims=True)
        xn_ref[...] = (xf * lax.rsqrt(ms + EPS) * g_ref[...]).astype(BF16)

    def emit(mode, outs, a, b, dst):
        for p0 in range(a, b, PROJ_PIECE):
            p1 = min(p0 + PROJ_PIECE, b)
            acc = _dot(xn_ref[...], w_ref[:, p0:p1])
            for h in range((p1 - p0) // HEAD_DIM):
                sub = slice(h * HEAD_DIM, (h + 1) * HEAD_DIM)
                src = slice(p0 + h * HEAD_DIM, p0 + (h + 1) * HEAD_DIM)
                to = slice(dst + p0 - a + h * HEAD_DIM, dst + p0 - a + (h + 1) * HEAD_DIM)
                if mode == 'raw':
                    outs[0][:, to] = acc[:, sub]
                    continue
                y = _head_rms(acc[:, sub], gain_ref[:, src])
                if mode in ('hnorm', 'hnorm_both'):
                    outs[0][:, to] = y
                if mode in ('hnorm_rope', 'hnorm_both'):
                    outs[-1][:, to] = _rope(y, cos_ref[...], sin_ref[...])

    ntiles = pl.num_programs(1)
    for tile in range((segs[-1][0] + segs[-1][1]) // tn):
        @pl.when(j == tile)
        def _(tile=tile):
            oi = 0
            for (start, width, mode) in segs:
                outs = out_refs[oi:oi + _n_outs(mode)]
                oi += _n_outs(mode)
                a, b = max(start, tile * tn), min(start + width, (tile + 1) * tn)
                if a < b:
                    emit(mode, outs, a - tile * tn, b - tile * tn, (a - start) % _seg_block(width, tn))

    @pl.when(j == ntiles - 1)
    def _():
        t = _dot(xn_ref[...], wt_ref[...]) + bias_ref[...]
        tail_ref[...] = _sigmoid(t) if tail_mode == 'sigmoid' else _log_sigmoid(t)


def _proj(x, norm_g, w, w_tail, gain_all, bias_tail, cosf, sinf, segs, tail_mode, *, tm, tn, name):
    M, K = x.shape
    n_main = segs[-1][0] + segs[-1][1]
    assert M % tm == 0 and n_main % tn == 0
    nseq = cosf.shape[0] // tm
    out_shape, out_specs = [], []
    for (start, width, mode) in segs:
        bw = _seg_block(width, tn)
        assert width % bw == 0 and start % bw == 0
        for _ in range(_n_outs(mode)):
            out_shape.append(jax.ShapeDtypeStruct((M, width), F32))
            out_specs.append(pl.BlockSpec(
                (tm, bw), lambda i, j, s=start, bw=bw, n=width // bw: (i, jnp.clip((j * tn - s) // bw, 0, n - 1))))
    out_shape.append(jax.ShapeDtypeStruct((M, LANES), F32))
    out_specs.append(pl.BlockSpec((tm, LANES), lambda i, j: (i, 0)))
    return pl.pallas_call(
        functools.partial(_proj_kernel, tuple(segs), tail_mode, tn),
        out_shape=out_shape,
        grid=(M // tm, n_main // tn),
        in_specs=[
            pl.BlockSpec((tm, K), lambda i, j: (i, 0)),
            pl.BlockSpec((1, K), lambda i, j: (0, 0)),
            pl.BlockSpec((K, tn), lambda i, j: (0, j)),
            pl.BlockSpec((K, LANES), lambda i, j: (0, 0)),
            pl.BlockSpec((1, tn), lambda i, j: (0, j)),
            pl.BlockSpec((1, LANES), lambda i, j: (0, 0)),
            pl.BlockSpec((tm, HEAD_DIM), lambda i, j: (i % nseq, 0)),
            pl.BlockSpec((tm, HEAD_DIM), lambda i, j: (i % nseq, 0)),
        ],
        out_specs=out_specs,
        scratch_shapes=[pltpu.VMEM((tm, K), BF16)],
        compiler_params=_cparams(("arbitrary", "arbitrary")),
        name=name,
    )(x, norm_g, w, w_tail, gain_all, bias_tail, cosf, sinf)


def _outproj_kernel(group_sizes, *refs):
    n_lhs = sum(group_sizes)
    lhs = refs[:n_lhs]
    w_ref, res_ref, o_ref, xs_ref = refs[n_lhs:n_lhs + 4]

    @pl.when(pl.program_id(1) == 0)
    def _():
        k0, r0 = 0, 0
        for n in group_sizes:
            x = lhs[r0][...]
            for r in lhs[r0 + 1:r0 + n]:
                x = x + r[...]
            xs_ref[:, k0:k0 + x.shape[1]] = x.astype(BF16)
            k0, r0 = k0 + x.shape[1], r0 + n

    o_ref[...] = res_ref[...] + _dot(xs_ref[...], w_ref[...])


def _outproj(lhs_groups, w, res, *, tm, tn, name):
    M = res.shape[0]
    K, N = w.shape
    flat = [a for g in lhs_groups for a in g]
    assert sum(g[0].shape[1] for g in lhs_groups) == K
    return pl.pallas_call(
        functools.partial(_outproj_kernel, tuple(len(g) for g in lhs_groups)),
        out_shape=jax.ShapeDtypeStruct((M, N), F32),
        grid=(M // tm, N // tn),
        in_specs=[pl.BlockSpec((tm, a.shape[1]), lambda i, j: (i, 0)) for a in flat] + [
            pl.BlockSpec((K, tn), lambda i, j: (0, j)),
            pl.BlockSpec((tm, tn), lambda i, j: (i, j)),
        ],
        out_specs=pl.BlockSpec((tm, tn), lambda i, j: (i, j)),
        scratch_shapes=[pltpu.VMEM((tm, K), BF16)],
        compiler_params=_cparams(("arbitrary", "arbitrary")),
        name=name,
    )(*flat, w, res)


def _ffn_kernel(emit, x_ref, g_ref, wg_ref, wu_ref, wd_ref, o_ref, *rest):
    xn_ref = rest[-1]
    f = pl.program_id(1)

    @pl.when(f == 0)
    def _():
        xf = x_ref[...]
        ms = jnp.mean(xf * xf, axis=-1, keepdims=True)
        xn_ref[...] = (xf * lax.rsqrt(ms + EPS) * g_ref[...]).astype(BF16)
        o_ref[...] = xf

    xn = xn_ref[...]
    wg, wu, wd = wg_ref[...].astype(BF16), wu_ref[...].astype(BF16), wd_ref[...].astype(BF16)
    if emit:
        rest[0][...], rest[1][...], rest[2][...] = wg, wu, wd
    h = _silu(_dot(xn, wg)) * _dot(xn, wu)
    o_ref[...] += _dot(h.astype(BF16), wd)


def _ffn(x, norm_g, wg, wu, wd, *, tm, tf, emit=False, name):
    M, K = x.shape
    F = wg.shape[1]
    assert not emit or M == tm
    w_specs = [pl.BlockSpec((K, tf), lambda i, f: (0, f)), pl.BlockSpec((K, tf), lambda i, f: (0, f)),
               pl.BlockSpec((tf, K), lambda i, f: (f, 0))]
    out_shape = [jax.ShapeDtypeStruct((M, K), F32)]
    out_specs = [pl.BlockSpec((tm, K), lambda i, f: (i, 0))]
    if emit:
        out_shape += [jax.ShapeDtypeStruct(w.shape, BF16) for w in (wg, wu, wd)]
        out_specs += w_specs
    out = pl.pallas_call(
        functools.partial(_ffn_kernel, emit),
        out_shape=out_shape,
        grid=(M // tm, F // tf),
        in_specs=[pl.BlockSpec((tm, K), lambda i, f: (i, 0)), pl.BlockSpec((1, K), lambda i, f: (0, 0))] + w_specs,
        out_specs=out_specs,
        scratch_shapes=[pltpu.VMEM((tm, K), BF16)],
        compiler_params=_cparams(("arbitrary", "arbitrary")),
        name=name,
    )(x, norm_g, wg, wu, wd)
    return (out[0], tuple(out[1:])) if emit else out[0]


def _moe_kernel(emit, x_ref, g_ref, rw_ref, rb_ref, wg_ref, wu_ref, wd_ref, o_ref, *rest):
    xn_ref, gate_ref = rest[-2:]
    e = pl.program_id(1)
    f = pl.program_id(2)

    @pl.when((e == 0) & (f == 0))
    def _():
        xf = x_ref[...]
        ms = jnp.mean(xf * xf, axis=-1, keepdims=True)
        xn = (xf * lax.rsqrt(ms + EPS) * g_ref[...]).astype(BF16)
        xn_ref[...] = xn
        o_ref[...] = xf
        lane = lax.broadcasted_iota(jnp.int32, (xf.shape[0], LANES), 1)
        logits = jnp.where(lane < N_EXPERTS, _dot(xn, rw_ref[...]) + rb_ref[...], NEG)
        m1 = jnp.max(logits, axis=-1, keepdims=True)
        i1 = jnp.min(jnp.where(logits == m1, lane, LANES), axis=-1, keepdims=True)
        l2 = jnp.where(lane == i1, NEG, logits)
        m2 = jnp.max(l2, axis=-1, keepdims=True)
        i2 = jnp.min(jnp.where(l2 == m2, lane, LANES), axis=-1, keepdims=True)
        e2 = jnp.exp(m2 - m1)
        w1 = 1.0 / (1.0 + e2)
        gate_ref[...] = jnp.where(lane == i1, w1, 0.0) + jnp.where(lane == i2, e2 * w1, 0.0)

    xn = xn_ref[...]
    wg, wu, wd = wg_ref[0].astype(BF16), wu_ref[0].astype(BF16), wd_ref[0].astype(BF16)
    if emit:
        rest[0][0], rest[1][0], rest[2][0] = wg, wu, wd
    h = _silu(_dot(xn, wg)) * _dot(xn, wu)
    lane = lax.broadcasted_iota(jnp.int32, gate_ref.shape, 1)
    ge = jnp.sum(jnp.where(lane == e, gate_ref[...], 0.0), axis=-1, keepdims=True)
    o_ref[...] += ge * _dot(h.astype(BF16), wd)


def _moe(x, norm_g, rw, rb, wg, wu, wd, *, tm, tf, emit=False, name):
    M, K = x.shape
    E, _, F = wg.shape
    assert not emit or M == tm
    w_specs = [pl.BlockSpec((1, K, tf), lambda i, e, f: (e, 0, f)), pl.BlockSpec((1, K, tf), lambda i, e, f: (e, 0, f)),
               pl.BlockSpec((1, tf, K), lambda i, e, f: (e, f, 0))]
    out_shape = [jax.ShapeDtypeStruct((M, K), F32)]
    out_specs = [pl.BlockSpec((tm, K), lambda i, e, f: (i, 0))]
    if emit:
        out_shape += [jax.ShapeDtypeStruct(w.shape, BF16) for w in (wg, wu, wd)]
        out_specs += w_specs
    out = pl.pallas_call(
        functools.partial(_moe_kernel, emit),
        out_shape=out_shape,
        grid=(M // tm, E, F // tf),
        in_specs=[
            pl.BlockSpec((tm, K), lambda i, e, f: (i, 0)),
            pl.BlockSpec((1, K), lambda i, e, f: (0, 0)),
            pl.BlockSpec((K, LANES), lambda i, e, f: (0, 0)),
            pl.BlockSpec((1, LANES), lambda i, e, f: (0, 0)),
        ] + w_specs,
        out_specs=out_specs,
        scratch_shapes=[pltpu.VMEM((tm, K), BF16), pltpu.VMEM((tm, LANES), F32)],
        compiler_params=_cparams(("arbitrary", "arbitrary", "arbitrary")),
        name=name,
    )(x, norm_g, rw, rb, wg, wu, wd)
    return (out[0], tuple(out[1:])) if emit else out[0]


TM_E = 512
EXPERT_ISSUE_STEPS = 8


def _top2(logits, lane):
    m1 = jnp.max(logits, axis=-1, keepdims=True)
    i1 = jnp.min(jnp.where(logits == m1, lane, LANES), axis=-1, keepdims=True)
    l2 = jnp.where(lane == i1, NEG, logits)
    m2 = jnp.max(l2, axis=-1, keepdims=True)
    i2 = jnp.min(jnp.where(l2 == m2, lane, LANES), axis=-1, keepdims=True)
    e2 = jnp.exp(m2 - m1)
    w1 = 1.0 / (1.0 + e2)
    return i1, i2, w1, e2 * w1


def _route_kernel(x_ref, g_ref, rw_ref, rb_ref, xn_ref, route_ref, cnt_ref, carry):
    @pl.when(pl.program_id(0) == 0)
    def _():
        carry[...] = jnp.zeros_like(carry)

    xf = x_ref[...]
    tm = xf.shape[0]
    xn = xf * lax.rsqrt(jnp.mean(xf * xf, axis=-1, keepdims=True) + EPS) * g_ref[...]
    xn_ref[...] = xn
    lane = lax.broadcasted_iota(jnp.int32, (tm, LANES), 1)
    logits = jnp.where(lane < N_EXPERTS, _dot(xn.astype(BF16), rw_ref[...]) + rb_ref[...], NEG)
    i1, i2, w1, w2 = _top2(logits, lane)
    hit = jnp.where(lane == i1, 1.0, 0.0) + jnp.where(lane == i2, 1.0, 0.0)
    incl = _dot(_tril_bf16(tm), hit.astype(BF16)) + carry[...]
    before = incl - hit
    carry[...] = incl[tm - 1:tm, :]
    cnt_ref[...] = incl[tm - 1:tm, :]
    r1 = jnp.sum(jnp.where(lane == i1, before, 0.0), axis=-1, keepdims=True)
    r2 = jnp.sum(jnp.where(lane == i2, before, 0.0), axis=-1, keepdims=True)
    cols = (i1.astype(F32), i2.astype(F32), r1, r2, w1, w2)
    rec = jnp.zeros((tm, LANES), F32)
    for c, v in enumerate(cols):
        rec = jnp.where(lane == c, v, rec)
    route_ref[...] = rec


def _route(x, norm_g, rw, rb, *, tm):
    M, K = x.shape
    return pl.pallas_call(
        _route_kernel,
        out_shape=[jax.ShapeDtypeStruct((M, K), F32), jax.ShapeDtypeStruct((M, LANES), F32),
                   jax.ShapeDtypeStruct((1, LANES), F32)],
        grid=(M // tm,),
        in_specs=[pl.BlockSpec((tm, K), lambda i: (i, 0)), pl.BlockSpec((1, K), lambda i: (0, 0)),
                  pl.BlockSpec((K, LANES), lambda i: (0, 0)), pl.BlockSpec((1, LANES), lambda i: (0, 0))],
        out_specs=[pl.BlockSpec((tm, K), lambda i: (i, 0)), pl.BlockSpec((tm, LANES), lambda i: (i, 0)),
                   pl.BlockSpec((1, LANES), lambda i: (0, 0))],
        scratch_shapes=[pltpu.VMEM((1, LANES), F32)],
        compiler_params=_cparams(("arbitrary",)),
        name="moe_route",
    )(x, norm_g, rw, rb)


def _gather_start(src_hbm, idx_ref, first, n, dst, sem, unroll):
    def body(r, carry):
        pltpu.make_async_copy(src_hbm.at[pl.ds(idx_ref[first + r], 1)], dst.at[pl.ds(r, 1)], sem).start()
        return carry
    lax.fori_loop(0, n, body, 0, unroll=unroll)


def _gather_wait(dst, sem):
    pltpu.make_async_copy(dst, dst, sem).wait()


def _experts_kernel(te_ref, nu_ref, src_ref, x_hbm, wg_ref, wu_ref, wd_ref, o_ref, xbuf, xb_scr, acc_scr, sems):
    i, f = pl.program_id(0), pl.program_id(1)
    last = pl.num_programs(1) - 1
    live = i < nu_ref[0]
    slot = i % 2

    @pl.when(live & (f == 0))
    def _():
        @pl.when(i == 0)
        def _():
            _gather_start(x_hbm, src_ref, 0, TM_E, xbuf.at[0], sems.at[0], 8)

        _gather_wait(xbuf.at[slot], sems.at[slot])
        xb_scr[...] = xbuf[slot].astype(BF16)
        acc_scr[...] = jnp.zeros_like(acc_scr)

    @pl.when(live)
    def _():
        per_step = TM_E // EXPERT_ISSUE_STEPS
        first = (f - 1) * per_step

        @pl.when((i + 1 < nu_ref[0]) & (f >= 1) & (f <= EXPERT_ISSUE_STEPS))
        def _():
            for r in range(per_step):
                pltpu.make_async_copy(x_hbm.at[pl.ds(src_ref[(i + 1) * TM_E + first + r], 1)],
                                      xbuf.at[1 - slot, pl.ds(first + r, 1)], sems.at[1 - slot]).start()

        xb = xb_scr[...]
        h = _silu(_dot(xb, wg_ref[0])) * _dot(xb, wu_ref[0])
        acc_scr[...] += _dot(h.astype(BF16), wd_ref[0])

    @pl.when(f == last)
    def _():
        o_ref[...] = jnp.where(live, acc_scr[...], 0.0)


def _experts(tile_expert, n_used, src, xn, wg, wu, wd, *, tf):
    P = src.shape[0]
    E, K, F = wg.shape
    nf = F // tf
    col = lambda i, f, nu: jnp.where(i < nu[0], f, nf - 1)
    return pl.pallas_call(
        _experts_kernel,
        out_shape=jax.ShapeDtypeStruct((P, K), F32),
        grid_spec=pltpu.PrefetchScalarGridSpec(
            num_scalar_prefetch=3, grid=(P // TM_E, nf),
            in_specs=[pl.BlockSpec(memory_space=pl.ANY),
                      pl.BlockSpec((1, K, tf), lambda i, f, te, nu, sr: (te[i], 0, col(i, f, nu))),
                      pl.BlockSpec((1, K, tf), lambda i, f, te, nu, sr: (te[i], 0, col(i, f, nu))),
                      pl.BlockSpec((1, tf, K), lambda i, f, te, nu, sr: (te[i], col(i, f, nu), 0))],
            out_specs=pl.BlockSpec((TM_E, K), lambda i, f, te, nu, sr: (i, 0)),
            scratch_shapes=[pltpu.VMEM((2, TM_E, K), F32), pltpu.VMEM((TM_E, K), BF16), pltpu.VMEM((TM_E, K), F32),
                            pltpu.SemaphoreType.DMA((2,))]),
        compiler_params=_cparams(("arbitrary", "arbitrary")),
        name="moe_experts",
    )(tile_expert, n_used, src, xn, wg, wu, wd)


def _combine_kernel(tc, dest_ref, os_hbm, x_ref, route_ref, o_ref, buf, sems):
    i = pl.program_id(0)
    slot = i % 2

    @pl.when(i == 0)
    def _():
        _gather_start(os_hbm, dest_ref, 0, 2 * tc, buf.at[0], sems.at[0], 8)

    @pl.when(i + 1 < pl.num_programs(0))
    def _():
        _gather_start(os_hbm, dest_ref, (i + 1) * 2 * tc, 2 * tc, buf.at[1 - slot], sems.at[1 - slot], 8)

    _gather_wait(buf.at[slot], sems.at[slot])
    w1, w2 = route_ref[:, 4:5], route_ref[:, 5:6]
    o_ref[...] = x_ref[...] + (w1 * buf[slot, 0:tc, :] + w2 * buf[slot, tc:2 * tc, :])


def _combine(dest, os, x, route, *, tc):
    M, K = x.shape
    return pl.pallas_call(
        functools.partial(_combine_kernel, tc),
        out_shape=jax.ShapeDtypeStruct((M, K), F32),
        grid_spec=pltpu.PrefetchScalarGridSpec(
            num_scalar_prefetch=1, grid=(M // tc,),
            in_specs=[pl.BlockSpec(memory_space=pl.ANY),
                      pl.BlockSpec((tc, K), lambda i, d: (i, 0)), pl.BlockSpec((tc, LANES), lambda i, d: (i, 0))],
            out_specs=pl.BlockSpec((tc, K), lambda i, d: (i, 0)),
            scratch_shapes=[pltpu.VMEM((2, 2 * tc, K), F32), pltpu.SemaphoreType.DMA((2,))]),
        compiler_params=_cparams(("arbitrary",)),
        name="moe_combine",
    )(dest, os, x, route)


def _moe_routed(x, norm_g, rw, rb, wg, wu, wd, *, tm):
    M = x.shape[0]
    xn, route, counts = _route(x, norm_g, rw, rb, tm=tm)
    e = route[:, 0:2].astype(jnp.int32)
    rank = route[:, 2:4].astype(jnp.int32)
    cnt = counts[0, :N_EXPERTS].astype(jnp.int32)
    tiles = (cnt + TM_E - 1) // TM_E
    tile_end = jnp.cumsum(tiles)
    dest = ((tile_end - tiles) * TM_E)[e] + rank
    n_tiles = (2 * M) // TM_E + N_EXPERTS
    P = n_tiles * TM_E
    token = jnp.repeat(jnp.arange(M, dtype=jnp.int32), 2)
    src = jnp.zeros((P,), jnp.int32).at[dest.reshape(-1)].set(token)
    tile_expert = jnp.minimum(jnp.sum(tile_end[None, :] <= jnp.arange(n_tiles, dtype=jnp.int32)[:, None], axis=1),
                              N_EXPERTS - 1).astype(jnp.int32)
    os = _experts(tile_expert, tile_end[-1:].astype(jnp.int32), src, xn, wg, wu, wd, tf=256)
    tc = 256
    dest_tiles = dest.reshape(M // tc, tc, 2).transpose(0, 2, 1).reshape(-1).astype(jnp.int32)
    return _combine(dest_tiles, os, x, route, tc=tc)


HGRN_HEADS_PER_STEP = 4


def _hgrn_kernel(T, c, aq_ref, af_ref, ai_ref, ag_ref, lb_ref, gn_ref, o_ref, st_ref, st_scr):
    t = pl.program_id(2)

    @pl.when(t == 0)
    def _():
        st_scr[...] = jnp.zeros_like(st_scr)

    row = lax.broadcasted_iota(jnp.int32, (T, T), 0)
    col = lax.broadcasted_iota(jnp.int32, (T, T), 1)
    shift = c.bit_length() - 1
    same = (row >> shift) == (col >> shift)
    causal = same & (col <= row)
    tri = jnp.where(causal, 1.0, 0.0).astype(BF16)
    blk = jnp.where(same, 1.0, 0.0).astype(BF16)
    gn = gn_ref[...]
    for hh in range(HGRN_HEADS_PER_STEP):
        cols = slice(hh * HEAD_DIM, (hh + 1) * HEAD_DIM)
        lb = lb_ref[:, cols]
        gate = lb + (1.0 - lb) * _sigmoid(af_ref[:, cols])
        k = 1.0 - gate
        parts = _split3(jnp.log(gate))
        b = sum(_dot(tri, p) for p in parts)
        bend = sum(_dot(blk, p) for p in parts)
        qe = (_silu(aq_ref[:, cols]) * jnp.exp(b)).astype(BF16)
        ke = (k * jnp.exp(-b)).astype(BF16)
        kend = (k * jnp.exp(bend - b)).astype(BF16)
        vb = ai_ref[:, cols].astype(BF16)
        o_intra = _dot(jnp.where(causal, _dot_nt(qe, ke), 0.0).astype(BF16), vb)
        st = st_scr[hh]
        for u in range(T // c):
            rows = slice(u * c, (u + 1) * c)
            o = o_intra[rows] + _dot_nt(qe[rows], st.astype(BF16))
            st = st * jnp.exp(bend[u * c:u * c + 1]) + _dot_tn(vb[rows], kend[rows])
            o_ref[rows, cols] = _head_rms(o, gn) * _silu(ag_ref[rows, cols])
        st_scr[hh] = st

    @pl.when(t == pl.num_programs(2) - 1)
    def _():
        for hh in range(HGRN_HEADS_PER_STEP):
            st_ref[0, hh] = st_scr[hh].T


def _hgrn_prompt(hg, lb, gn, B, S, *, T, c):
    nt = S // T
    H = H_HGRN
    hp = HGRN_HEADS_PER_STEP
    W = hp * HEAD_DIM
    spec = lambda off: pl.BlockSpec((T, W), lambda b, h, t, off=off: (b * nt + t, off + h))
    return pl.pallas_call(
        functools.partial(_hgrn_kernel, T, c),
        out_shape=[jax.ShapeDtypeStruct((B * S, H * HEAD_DIM), F32),
                   jax.ShapeDtypeStruct((B, H, HEAD_DIM, HEAD_DIM), F32)],
        grid=(B, H // hp, nt),
        in_specs=[spec(0), spec(H // hp), spec(2 * H // hp), spec(3 * H // hp),
                  pl.BlockSpec((1, W), lambda b, h, t: (0, h)),
                  pl.BlockSpec((1, HEAD_DIM), lambda b, h, t: (0, 0))],
        out_specs=[pl.BlockSpec((T, W), lambda b, h, t: (b * nt + t, h)),
                   pl.BlockSpec((1, hp, HEAD_DIM, HEAD_DIM), lambda b, h, t: (b, h, 0, 0))],
        scratch_shapes=[pltpu.VMEM((hp, HEAD_DIM, HEAD_DIM), F32)],
        compiler_params=_cparams(("arbitrary", "arbitrary", "arbitrary")),
        name="hgrn_prompt",
    )(hg, hg, hg, hg, lb, gn)


BIAS_SHIFT = 4
BIAS_SLOT = 1 << BIAS_SHIFT


def _cumsum_kernel(lf_ref, pq_ref, pk_ref, oq_ref, ok_ref, aq_ref, ak_ref, carry):
    @pl.when(pl.program_id(1) == 0)
    def _():
        carry[...] = jnp.zeros_like(carry)

    tc = lf_ref.shape[0]
    c = _dot3(_tril_bf16(tc), lf_ref[...]) + carry[...]
    carry[...] = c[tc - 1:tc, :]
    parts = _split3(c * LOG2E)
    aq_ref[...] = sum(_dot(t, pq_ref[i]) for i, t in enumerate(parts)) + oq_ref[...]
    ak_ref[...] = (ok_ref[...] - sum(_dot(t, pk_ref[i]) for i, t in enumerate(parts))).astype(BF16)


def _bias_layout():
    pq, pk = np.zeros((3, LANES, LANES), np.float32), np.zeros((3, LANES, LANES), np.float32)
    oq, ok = np.zeros((1, LANES), np.float32), np.zeros((1, LANES), np.float32)
    for h in range(H_FOX):
        for i in range(3):
            pq[i, h, BIAS_SLOT * h + i] = 1.0
            pk[i, h, BIAS_SLOT * h + 3 + i] = 1.0
            oq[0, BIAS_SLOT * h + 3 + i] = 1.0
            ok[0, BIAS_SLOT * h + i] = 1.0
    return jnp.asarray(pq, BF16), jnp.asarray(pk, BF16), jnp.asarray(oq), jnp.asarray(ok)


def _seq_cumsum(lf, B, S, *, tc):
    n = S // tc
    const3 = pl.BlockSpec((3, LANES, LANES), lambda b, t: (0, 0, 0))
    const1 = pl.BlockSpec((1, LANES), lambda b, t: (0, 0))
    blk = pl.BlockSpec((tc, LANES), lambda b, t: (b * n + t, 0))
    return pl.pallas_call(
        _cumsum_kernel,
        out_shape=[jax.ShapeDtypeStruct((B * S, LANES), F32), jax.ShapeDtypeStruct((B * S, LANES), BF16)],
        grid=(B, n),
        in_specs=[blk, const3, const3, const1, const1],
        out_specs=[blk, blk],
        scratch_shapes=[pltpu.VMEM((1, LANES), F32)],
        compiler_params=_cparams(("arbitrary", "arbitrary")),
        name="seq_cumsum",
    )(lf, *_bias_layout())


def _online_init(m_scr, l_scr, acc_scr):
    m_scr[...] = jnp.full_like(m_scr, M_INIT)
    l_scr[...] = jnp.zeros_like(l_scr)
    acc_scr[...] = jnp.zeros_like(acc_scr)


def _online_update(s, vb, m_scr, l_scr, acc_scr):
    m_prev = m_scr[...]
    m_new = jnp.maximum(m_prev, jnp.max(s, axis=-1, keepdims=True))
    alpha = jnp.exp(m_prev - m_new)
    p = jnp.exp(s - m_new)
    l_scr[...] = alpha * l_scr[...] + jnp.sum(p, axis=-1, keepdims=True)
    acc_scr[...] = alpha * acc_scr[...] + _dot(p.astype(BF16), vb)
    m_scr[...] = m_new


def _online_result(l_scr, acc_scr):
    l = l_scr[...]
    return acc_scr[...] / jnp.where(l > 0.0, l, 1.0)


def _flash_t_step(sT, vT, m, l, acc_scr):
    m_new = jnp.maximum(m, jnp.max(sT, axis=0, keepdims=True))
    alpha = jnp.exp2(m - m_new)
    p = jnp.exp2(sT - m_new)
    acc_scr[...] = alpha * acc_scr[...] + _dot(vT, p.astype(BF16))
    return m_new, alpha * l + jnp.sum(p, axis=0, keepdims=True)


def _flash_t_causal(ka_scr, vt_scr, qa, acc_scr, n_full, n_diag, tk, qpos, unroll=1):
    R = qa.shape[1]
    acc_scr[...] = jnp.zeros_like(acc_scr)

    def body(it, carry):
        for u in range(unroll):
            kj = it * unroll + u
            carry = _flash_t_step(_dot(ka_scr[kj], qa), vt_scr[kj], *carry, acc_scr)
        return carry

    m, l = lax.fori_loop(0, n_full // unroll, body, (jnp.full((1, R), M_INIT, F32), jnp.zeros((1, R), F32)))
    for i in range(n_diag):
        kpos = (n_full + i) * tk + lax.broadcasted_iota(jnp.int32, (tk, R), 0)
        sT = jnp.where(kpos <= qpos, _dot(ka_scr[n_full + i], qa), NEG)
        m, l = _flash_t_step(sT, vt_scr[n_full + i], m, l, acc_scr)
    return acc_scr[...] / jnp.where(l > 0.0, l, 1.0)


def _fox_kernel(tq, tk, q_ref, k_ref, vt_ref, aq_ref, ak_ref, o_ref, ka_scr, vt_scr, acc_scr):
    h = pl.program_id(1)
    qi = pl.program_id(2)

    @pl.when(qi == 0)
    def _():
        for c in range(ka_scr.shape[0]):
            rows = slice(c * tk, (c + 1) * tk)
            ka_scr[c] = jnp.concatenate([k_ref[rows, :].astype(BF16), ak_ref[rows, :]], axis=1)
            vt_scr[c] = vt_ref[0, 0, :, rows].astype(BF16)

    lane = lax.broadcasted_iota(jnp.int32, (tq, LANES), 1)
    aq = jnp.where((lane >> BIAS_SHIFT) == h, aq_ref[...], 0.0)
    qa = jnp.concatenate([(q_ref[...] * QSCALE).T, aq.T], axis=0).astype(BF16)
    qpos = qi * tq + lax.broadcasted_iota(jnp.int32, (tk, tq), 1)
    o_ref[...] = _flash_t_causal(ka_scr, vt_scr, qa, acc_scr, qi * (tq // tk), tq // tk, tk, qpos,
                                 unroll=tq // tk).T


def _fox_prompt(fq, fk, fvt, aug_q, aug_k, B, S, *, tq, tk):
    nq = S // tq
    return pl.pallas_call(
        functools.partial(_fox_kernel, tq, tk),
        out_shape=jax.ShapeDtypeStruct(fq.shape, F32),
        grid=(B, H_FOX, nq),
        in_specs=[
            pl.BlockSpec((tq, HEAD_DIM), lambda b, h, qi: (b * nq + qi, h)),
            pl.BlockSpec((S, HEAD_DIM), lambda b, h, qi: (b, h)),
            pl.BlockSpec((1, 1, HEAD_DIM, S), lambda b, h, qi: (b, h, 0, 0)),
            pl.BlockSpec((tq, LANES), lambda b, h, qi: (b * nq + qi, 0)),
            pl.BlockSpec((S, LANES), lambda b, h, qi: (b, 0)),
        ],
        out_specs=pl.BlockSpec((tq, HEAD_DIM), lambda b, h, qi: (b * nq + qi, h)),
        scratch_shapes=[pltpu.VMEM((S // tk, tk, 2 * HEAD_DIM), BF16), pltpu.VMEM((S // tk, HEAD_DIM, tk), BF16),
                        pltpu.VMEM((HEAD_DIM, tq), F32)],
        compiler_params=_cparams(("arbitrary",) * 3),
        name="fox_prompt",
    )(fq, fk, fvt, aug_q, aug_k)


def _compress_kernel(nc, x_ref, pea_ref, peb_ref, w1a_ref, w1b_ref, w2_ref, o_ref):
    R = jnp.concatenate([x_ref[0, pl.ds(r, nc, stride=CMP_STRIDE), :] for r in range(CMP_STRIDE)], axis=1)
    p1 = _dot((R + pea_ref[...]).astype(BF16), w1a_ref[...])
    p2 = _dot((R + peb_ref[...]).astype(BF16), w1b_ref[...])
    pre = p1 + pltpu.roll(p2, nc - 1, 0)
    kc = _dot(_silu(pre).astype(BF16), w2_ref[...])
    row = lax.broadcasted_iota(jnp.int32, kc.shape, 0)
    o_ref[0, 0] = jnp.where(row < nc - 1, kc, 0.0)


def _cmp_weights(pe, w1, w2):
    half = CMP_STRIDE * HEAD_DIM
    return (pe[:CMP_STRIDE].reshape(1, half), pe[CMP_STRIDE:].reshape(1, half),
            w1[:half].astype(BF16), w1[half:].astype(BF16), w2.astype(BF16))


def _compress_prompt(x, cw, B, S):
    nc = S // CMP_STRIDE
    half = CMP_STRIDE * HEAD_DIM
    const = lambda shape: pl.BlockSpec(shape, lambda b, g: (0, 0))
    return pl.pallas_call(
        functools.partial(_compress_kernel, nc),
        out_shape=jax.ShapeDtypeStruct((B, NSA_GROUPS, nc, HEAD_DIM), F32),
        grid=(B, NSA_GROUPS),
        in_specs=[pl.BlockSpec((1, S, HEAD_DIM), lambda b, g: (b, 0, g)),
                  const((1, half)), const((1, half)), const((half, HEAD_DIM)), const((half, HEAD_DIM)),
                  const((HEAD_DIM, HEAD_DIM))],
        out_specs=pl.BlockSpec((1, 1, nc, HEAD_DIM), lambda b, g: (b, g, 0, 0)),
        compiler_params=_cparams(("arbitrary", "arbitrary")),
        name="nsa_compress",
    )(x, *cw)


def _overlap_t(n_cmp, n_sel):
    cs = np.arange(n_cmp)[None, :] * CMP_STRIDE
    ss = np.arange(n_sel)[:, None] * SEL_LEN
    ov = np.minimum(cs + CMP_LEN, ss + SEL_LEN) - np.maximum(cs, ss)
    return np.maximum(ov, 0).astype(np.float32) / CMP_LEN


def _group_rows(ref, Tq):
    return jnp.concatenate([ref[0, :, r * HEAD_DIM:(r + 1) * HEAD_DIM] for r in range(NSA_R)], axis=0)


def _store_gated(o_ref, o, gates, Tq, branch):
    for r in range(NSA_R):
        gcol = gates[:, 3 * r + branch:3 * r + branch + 1]
        o_ref[0, :, r * HEAD_DIM:(r + 1) * HEAD_DIM] = o[r * Tq:(r + 1) * Tq] * gcol


def _cmpsel_kernel(Tq, nsel, q_ref, kc_ref, vc_ref, ovt_ref, gate_ref, o_ref, sel_ref):
    qi = pl.program_id(2)
    q = (_group_rows(q_ref, Tq) * (HEAD_DIM ** -0.5)).astype(BF16)
    s = _dot_nt(q, kc_ref[0, 0].astype(BF16))
    qpos = qi * Tq + (lax.broadcasted_iota(jnp.int32, s.shape, 0) & (Tq - 1))
    cend = lax.broadcasted_iota(jnp.int32, s.shape, 1) * CMP_STRIDE + (CMP_LEN - 1)
    mask = cend <= qpos
    sm = jnp.where(mask, s, NEG)
    e = jnp.where(mask, jnp.exp(sm - jnp.max(sm, axis=-1, keepdims=True)), 0.0)
    den = jnp.sum(e, axis=-1, keepdims=True)
    p = e / jnp.where(den > 0.0, den, 1.0)
    o = _dot(p.astype(BF16), vc_ref[0, 0].astype(BF16))
    _store_gated(o_ref, o, gate_ref[0, 0], Tq, 0)

    psum = p[0:Tq]
    for r in range(1, NSA_R):
        psum = psum + p[r * Tq:(r + 1) * Tq]
    ovt = ovt_ref[...]
    sc = sum(_dot_nt(ovt, t) for t in _split3(psum))
    n = lax.broadcasted_iota(jnp.int32, sc.shape, 0)
    tpos = qi * Tq + lax.broadcasted_iota(jnp.int32, sc.shape, 1)
    cur = tpos >> SEL_SHIFT
    forced = (n == 0) | (n == cur) | (n == cur - 1)
    sc = jnp.where(n * SEL_LEN <= tpos, jnp.where(forced, -NEG, sc), NEG)
    ranks = []
    for v in range(nsel // 8):
        blk = sc[8 * v:8 * v + 8]
        nv = 8 * v + lax.broadcasted_iota(jnp.int32, blk.shape, 0)
        r = jnp.zeros(blk.shape, F32)
        for m in range(nsel):
            rowm = sc[m:m + 1, :]
            ge, gt = jnp.where(rowm >= blk, 1.0, 0.0), jnp.where(rowm > blk, 1.0, 0.0)
            if m < 8 * v:
                r = r + ge
            elif m > 8 * v + 7:
                r = r + gt
            else:
                r = r + jnp.where(nv > m, ge, gt)
        ranks.append(r)
    rank = jnp.concatenate(ranks, axis=0)
    selneg = jnp.where(rank < TOPN, 0.0, NEG)
    if nsel < LANES:
        selneg = jnp.concatenate([selneg, jnp.full((LANES - nsel, Tq), NEG, F32)], axis=0)
    sel_ref[0, 0, 0] = selneg.astype(BF16)


def _cmpsel_prompt(qn, kc, vc, gates_g, B, S, *, Tq):
    nq = S // Tq
    ncp = kc.shape[2]
    nsel = S // SEL_LEN
    ovt = jnp.asarray(_overlap_t(ncp, nsel), BF16)
    return pl.pallas_call(
        functools.partial(_cmpsel_kernel, Tq, nsel),
        out_shape=[jax.ShapeDtypeStruct((B, S, H_NSA * HEAD_DIM), F32),
                   jax.ShapeDtypeStruct((B, NSA_GROUPS, nq, LANES, Tq), BF16)],
        grid=(B, NSA_GROUPS, nq),
        in_specs=[pl.BlockSpec((1, Tq, NSA_R * HEAD_DIM), lambda b, g, qi: (b, qi, g)),
                  pl.BlockSpec((1, 1, ncp, HEAD_DIM), lambda b, g, qi: (b, g, 0, 0)),
                  pl.BlockSpec((1, 1, ncp, HEAD_DIM), lambda b, g, qi: (b, g, 0, 0)),
                  pl.BlockSpec((nsel, ncp), lambda b, g, qi: (0, 0)),
                  pl.BlockSpec((1, 1, Tq, 16), lambda b, g, qi: (b, g, qi, 0))],
        out_specs=[pl.BlockSpec((1, Tq, NSA_R * HEAD_DIM), lambda b, g, qi: (b, qi, g)),
                   pl.BlockSpec((1, 1, 1, LANES, Tq), lambda b, g, qi: (b, g, qi, 0, 0))],
        compiler_params=_cparams(("arbitrary",) * 3),
        name="nsa_cmpsel",
    )(qn, kc, vc, ovt, gates_g)


def _queries_t(q_ref, Tq):
    return jnp.concatenate([(q_ref[0, :, r * HEAD_DIM:(r + 1) * HEAD_DIM] * QSCALE).T for r in range(NSA_R)],
                           axis=1).astype(BF16)


def _store_gated_t(o_ref, oT, gates, Tq, branch):
    for r in range(NSA_R):
        gcol = gates[:, 3 * r + branch:3 * r + branch + 1]
        o_ref[0, :, r * HEAD_DIM:(r + 1) * HEAD_DIM] = oT[:, r * Tq:(r + 1) * Tq].T * gcol


def _sel_kernel(Tq, tk, q_ref, sn_ref, k_ref, vt_ref, e_ref, gate_ref, o_ref, ka_scr, vt_scr, acc_scr):
    qi = pl.program_id(2)

    @pl.when(qi == 0)
    def _():
        for c in range(ka_scr.shape[0]):
            rows = slice(c * tk, (c + 1) * tk)
            ka_scr[c] = jnp.concatenate([k_ref[0, rows, :].astype(BF16), e_ref[rows, :]], axis=1)
            vt_scr[c] = vt_ref[0, 0, :, rows].astype(BF16)

    sn = sn_ref[0, 0, 0]
    qa = jnp.concatenate([_queries_t(q_ref, Tq), jnp.concatenate([sn] * NSA_R, axis=1)], axis=0)
    R = NSA_R * Tq
    qpos = qi * Tq + (lax.broadcasted_iota(jnp.int32, (tk, R), 1) & (Tq - 1))
    oT = _flash_t_causal(ka_scr, vt_scr, qa, acc_scr, (qi * Tq) // tk, 1, tk, qpos)
    _store_gated_t(o_ref, oT, gate_ref[0, 0], Tq, 1)


def _sel_prompt(qr, selneg, sk, svt, gates_g, B, S, *, Tq, tk):
    nq = S // Tq
    key = np.arange(S)[:, None] // SEL_LEN
    e_all = jnp.asarray((key == np.arange(LANES)[None, :]).astype(np.float32), BF16)
    R = NSA_R * Tq
    return pl.pallas_call(
        functools.partial(_sel_kernel, Tq, tk),
        out_shape=jax.ShapeDtypeStruct((B, S, H_NSA * HEAD_DIM), F32),
        grid=(B, NSA_GROUPS, nq),
        in_specs=[pl.BlockSpec((1, Tq, NSA_R * HEAD_DIM), lambda b, g, qi: (b, qi, g)),
                  pl.BlockSpec((1, 1, 1, LANES, Tq), lambda b, g, qi: (b, g, qi, 0, 0)),
                  pl.BlockSpec((1, S, HEAD_DIM), lambda b, g, qi: (b, 0, g)),
                  pl.BlockSpec((1, 1, HEAD_DIM, S), lambda b, g, qi: (b, g, 0, 0)),
                  pl.BlockSpec((S, LANES), lambda b, g, qi: (0, 0)),
                  pl.BlockSpec((1, 1, Tq, 16), lambda b, g, qi: (b, g, qi, 0))],
        out_specs=pl.BlockSpec((1, Tq, NSA_R * HEAD_DIM), lambda b, g, qi: (b, qi, g)),
        scratch_shapes=[pltpu.VMEM((S // tk, tk, 2 * HEAD_DIM), BF16), pltpu.VMEM((S // tk, HEAD_DIM, tk), BF16),
                        pltpu.VMEM((HEAD_DIM, R), F32)],
        compiler_params=_cparams(("arbitrary",) * 3),
        name="nsa_sel",
    )(qr, selneg, sk, svt, e_all, gates_g)


def _win_kernel(Tq, nwb, q_ref, *refs):
    k_refs, vt_refs = refs[:nwb], refs[nwb:2 * nwb]
    gate_ref, o_ref = refs[2 * nwb:]
    qi = pl.program_id(2)
    R = NSA_R * Tq
    qT = _queries_t(q_ref, Tq)
    qpos = qi * Tq + (lax.broadcasted_iota(jnp.int32, (Tq, R), 1) & (Tq - 1))
    krow = lax.broadcasted_iota(jnp.int32, (Tq, R), 0)
    s = []
    for i in range(nwb):
        kb = qi - (nwb - 1) + i
        si = _dot(k_refs[i][0].astype(BF16), qT)
        if i == 0:
            si = jnp.where(qpos - (kb * Tq + krow) < WINDOW, si, NEG)
        if i == nwb - 1:
            si = jnp.where(kb * Tq + krow <= qpos, si, NEG)
        else:
            si = si + jnp.where(kb >= 0, 0.0, NEG)
        s.append(si)
    m = functools.reduce(jnp.maximum, [jnp.max(si, axis=0, keepdims=True) for si in s])
    p = [jnp.exp2(si - m) for si in s]
    l = sum(jnp.sum(pi, axis=0, keepdims=True) for pi in p)
    oT = sum(_dot(vt_refs[i][0, 0].astype(BF16), p[i].astype(BF16)) for i in range(nwb)) / l
    _store_gated_t(o_ref, oT, gate_ref[0, 0], Tq, 2)


def _win_prompt(qr, wk, wvt, gates_g, B, S, *, Tq):
    nq = S // Tq
    nwb = WINDOW // Tq + 1
    blk = lambda i: (lambda qi: jnp.maximum(qi - (nwb - 1) + i, 0))
    kspec = lambda i: pl.BlockSpec((1, Tq, HEAD_DIM), lambda b, g, qi, f=blk(i): (b, f(qi), g))
    vspec = lambda i: pl.BlockSpec((1, 1, HEAD_DIM, Tq), lambda b, g, qi, f=blk(i): (b, g, 0, f(qi)))
    return pl.pallas_call(
        functools.partial(_win_kernel, Tq, nwb),
        out_shape=jax.ShapeDtypeStruct((B, S, H_NSA * HEAD_DIM), F32),
        grid=(B, NSA_GROUPS, nq),
        in_specs=[pl.BlockSpec((1, Tq, NSA_R * HEAD_DIM), lambda b, g, qi: (b, qi, g))]
        + [kspec(i) for i in range(nwb)] + [vspec(i) for i in range(nwb)]
        + [pl.BlockSpec((1, 1, Tq, 16), lambda b, g, qi: (b, g, qi, 0))],
        out_specs=pl.BlockSpec((1, Tq, NSA_R * HEAD_DIM), lambda b, g, qi: (b, qi, g)),
        compiler_params=_cparams(("arbitrary",) * 3),
        name="nsa_win",
    )(qr, *([wk] * nwb), *([wvt] * nwb), gates_g)


def _hgrn_step_kernel(aq_ref, af_ref, ai_ref, ag_ref, lb_ref, gn_ref, s_ref, o_ref, so_ref):
    for h in range(s_ref.shape[1]):
        lb = lb_ref[h]
        gate = lb + (1.0 - lb) * _sigmoid(af_ref[0, h])
        s_new = s_ref[0, h] * gate + (1.0 - gate) * ai_ref[0, h]
        so_ref[0, h] = s_new
        o = jnp.sum(_silu(aq_ref[0, h]) * s_new, axis=0, keepdims=True)
        o_ref[0, h] = _head_rms(o, gn_ref[...]) * _silu(ag_ref[0, h])


def _hgrn_step(hg, lb, gn, state):
    D, H = state.shape[:2]
    W = H * HEAD_DIM
    colv = lambda a: a.reshape(D, H, HEAD_DIM, 1)
    rowv = lambda a: a.reshape(D, H, 1, HEAD_DIM)
    cspec = pl.BlockSpec((1, H, HEAD_DIM, 1), lambda d: (d, 0, 0, 0))
    rspec = pl.BlockSpec((1, H, 1, HEAD_DIM), lambda d: (d, 0, 0, 0))
    sspec = pl.BlockSpec((1, H, HEAD_DIM, HEAD_DIM), lambda d: (d, 0, 0, 0))
    o, s_new = pl.pallas_call(
        _hgrn_step_kernel,
        out_shape=[jax.ShapeDtypeStruct((D, H, 1, HEAD_DIM), F32), jax.ShapeDtypeStruct(state.shape, F32)],
        grid=(D,),
        in_specs=[cspec, cspec, rspec, rspec,
                  pl.BlockSpec((H, HEAD_DIM, 1), lambda d: (0, 0, 0)),
                  pl.BlockSpec((1, HEAD_DIM), lambda d: (0, 0)), sspec],
        out_specs=[rspec, sspec],
        compiler_params=_cparams(("arbitrary",)),
        name="hgrn_step",
    )(colv(hg[:, :W]), colv(hg[:, W:2 * W]), rowv(hg[:, 2 * W:3 * W]), rowv(hg[:, 3 * W:]),
      lb.reshape(H, HEAD_DIM, 1), gn, state)
    return o.reshape(D, W), s_new


def _fox_step_kernel(PG, pt_ref, q_ref, kn_ref, vn_ref, lfn_ref, *refs):
    k_refs, v_refs, lf_refs = refs[:PG], refs[PG:2 * PG], refs[2 * PG:3 * PG]
    o_ref, m_scr, l_scr, acc_scr, carry = refs[3 * PG:]
    W = H_FOX * PAGE
    q = q_ref[0] * (HEAD_DIM ** -0.5)

    @pl.when(pl.program_id(1) == 0)
    def _():
        m_scr[...] = jnp.sum(q * kn_ref[0], axis=-1, keepdims=True)
        l_scr[...] = jnp.ones_like(l_scr)
        acc_scr[...] = vn_ref[0]
        carry[...] = jnp.broadcast_to(lfn_ref[0], carry.shape)

    qb = q.astype(BF16)
    lane = lax.broadcasted_iota(jnp.int32, (H_FOX, W), 1)
    head = lax.broadcasted_iota(jnp.int32, (H_FOX, W), 0)
    plane = lax.broadcasted_iota(jnp.int32, (PG, W), 1)
    lf = jnp.concatenate([lf_refs[i][0] for i in range(PG)], axis=0)
    suf, tot = lf, lf
    step = H_FOX
    while step < W:
        suf = suf + jnp.where(plane + step < W, pltpu.roll(suf, W - step, 1), 0.0)
        tot = tot + pltpu.roll(tot, step, 1)
        step *= 2
    c = carry[...]
    s = []
    for i in range(PG):
        si = _dot_nt(qb, k_refs[i][0].astype(BF16)) + (c + (suf[i:i + 1] - lf[i:i + 1]))
        s.append(jnp.where((lane & (H_FOX - 1)) == head, si, NEG))
        c = c + tot[i:i + 1]
    carry[...] = c
    m_prev = m_scr[...]
    m_new = functools.reduce(jnp.maximum, [jnp.max(si, axis=-1, keepdims=True) for si in s] + [m_prev])
    alpha = jnp.exp(m_prev - m_new)
    p = [jnp.exp(si - m_new) for si in s]
    l_scr[...] = alpha * l_scr[...] + sum(jnp.sum(pi, axis=-1, keepdims=True) for pi in p)
    acc_scr[...] = alpha * acc_scr[...] + sum(_dot(p[i].astype(BF16), v_refs[i][0].astype(BF16)) for i in range(PG))
    m_scr[...] = m_new

    @pl.when(pl.program_id(1) == pl.num_programs(1) - 1)
    def _():
        o_ref[0] = _online_result(l_scr, acc_scr)


def _fox_step(page_table, q, k_new, v_new, lf_new, cache_k, cache_v, cache_lf, *, PG):
    D, NP = page_table.shape
    n_pool = cache_k.shape[0]
    W = H_FOX * PAGE
    k2 = cache_k.reshape(n_pool, W, HEAD_DIM)
    v2 = cache_v.reshape(n_pool, W, HEAD_DIM)
    lf2 = cache_lf.astype(F32).reshape(n_pool, 1, W)
    lfn = jnp.tile(lf_new, (1, PAGE)).reshape(D, 1, W)
    page = lambda i: (lambda d, j, pt: (pt[d, NP - 1 - (j * PG + i)], 0, 0))
    hspec = pl.BlockSpec((1, H_FOX, HEAD_DIM), lambda d, j, pt: (d, 0, 0))
    in_specs = [hspec, hspec, hspec, pl.BlockSpec((1, 1, W), lambda d, j, pt: (d, 0, 0))]
    in_specs += [pl.BlockSpec((1, W, HEAD_DIM), page(i)) for i in range(PG)]
    in_specs += [pl.BlockSpec((1, W, HEAD_DIM), page(i)) for i in range(PG)]
    in_specs += [pl.BlockSpec((1, 1, W), page(i)) for i in range(PG)]
    return pl.pallas_call(
        functools.partial(_fox_step_kernel, PG),
        out_shape=jax.ShapeDtypeStruct((D, H_FOX, HEAD_DIM), F32),
        grid_spec=pltpu.PrefetchScalarGridSpec(
            num_scalar_prefetch=1, grid=(D, NP // PG), in_specs=in_specs, out_specs=hspec,
            scratch_shapes=[pltpu.VMEM((H_FOX, 1), F32), pltpu.VMEM((H_FOX, 1), F32),
                            pltpu.VMEM((H_FOX, HEAD_DIM), F32), pltpu.VMEM((H_FOX, W), F32)]),
        compiler_params=_cparams(("arbitrary", "arbitrary")),
        name="fox_step",
    )(page_table, q, k_new, v_new, lfn, *([k2] * PG), *([v2] * PG), *([lf2] * PG))


def _cmp_pages_kernel(PG, pt_ref, *refs):
    pages = refs[:PG]
    pea_ref, peb_ref, w1a_ref, w1b_ref, p1_ref, p2_ref, r_scr = refs[PG:]
    per_page = PAGE // CMP_STRIDE
    G = NSA_GROUPS
    for i in range(PG):
        for m in range(per_page):
            dst = (i * per_page + m) * G
            for r in range(CMP_STRIDE):
                src = (m * CMP_STRIDE + r) * G
                r_scr[dst:dst + G, r * HEAD_DIM:(r + 1) * HEAD_DIM] = pages[i][0, src:src + G, :]
    R = r_scr[...]
    p1_ref[0] = _dot((R + pea_ref[...]).astype(BF16), w1a_ref[...])
    p2_ref[0] = _dot((R + peb_ref[...]).astype(BF16), w1b_ref[...])


def _cmp_pages(page_table, cache, cw, *, PG):
    D, NP = page_table.shape
    n_pool = cache.shape[0]
    c2 = cache.reshape(n_pool, PAGE * NSA_GROUPS, HEAD_DIM)
    rows = PG * (PAGE // CMP_STRIDE) * NSA_GROUPS
    half = CMP_STRIDE * HEAD_DIM
    page = lambda i: (lambda d, j, pt: (pt[d, j * PG + i], 0, 0))
    const = lambda shape: pl.BlockSpec(shape, lambda d, j, pt: (0, 0))
    in_specs = [pl.BlockSpec((1, PAGE * NSA_GROUPS, HEAD_DIM), page(i)) for i in range(PG)]
    in_specs += [const((1, half)), const((1, half)), const((half, HEAD_DIM)), const((half, HEAD_DIM))]
    ospec = pl.BlockSpec((1, rows, HEAD_DIM), lambda d, j, pt: (d, j, 0))
    shp = jax.ShapeDtypeStruct((D, (NP // PG) * rows, HEAD_DIM), F32)
    return pl.pallas_call(
        functools.partial(_cmp_pages_kernel, PG),
        out_shape=[shp, shp],
        grid_spec=pltpu.PrefetchScalarGridSpec(
            num_scalar_prefetch=1, grid=(D, NP // PG), in_specs=in_specs, out_specs=[ospec, ospec],
            scratch_shapes=[pltpu.VMEM((rows, half), F32)]),
        compiler_params=_cparams(("arbitrary", "arbitrary")),
        name="nsa_cmp_pages",
    )(page_table, *([c2] * PG), *cw[:4])


def _cmpsel_step_kernel(qpos, nsp, q_ref, p1k_ref, p2k_ref, p1v_ref, p2v_ref, w2k_ref, w2v_ref,
                        ov_ref, g_ref, o_ref, idx_ref):
    nc = p1k_ref.shape[1] // NSA_GROUPS
    rows = pl.ds(pl.program_id(1), nc, stride=NSA_GROUPS)

    def finish(p1_ref, p2_ref, w2_ref):
        pre = p1_ref[0, rows, :] + pltpu.roll(p2_ref[0, rows, :], nc - 1, 0)
        return _dot(_silu(pre).astype(BF16), w2_ref[...]).astype(BF16)

    kc = finish(p1k_ref, p2k_ref, w2k_ref)
    vc = finish(p1v_ref, p2v_ref, w2v_ref)
    q = (q_ref[0, 0] * (HEAD_DIM ** -0.5)).astype(BF16)
    s = _dot_nt(q, kc)
    cend = lax.broadcasted_iota(jnp.int32, s.shape, 1) * CMP_STRIDE + (CMP_LEN - 1)
    mask = cend <= qpos
    sm = jnp.where(mask, s, NEG)
    e = jnp.where(mask, jnp.exp(sm - jnp.max(sm, axis=-1, keepdims=True)), 0.0)
    den = jnp.sum(e, axis=-1, keepdims=True)
    p = e / jnp.where(den > 0.0, den, 1.0)
    o_ref[0, 0] = _dot(p.astype(BF16), vc) * g_ref[0, 0][:, 0:1]

    psum = jnp.broadcast_to(jnp.sum(p[0:NSA_R], axis=0, keepdims=True), (8, nc))
    ov = ov_ref[...]
    sc_row = sum(_dot(t, ov) for t in _split3(psum))[0:1]
    n_lane = lax.broadcasted_iota(jnp.int32, (1, nsp), 1)
    cur = qpos // SEL_LEN
    forced = (n_lane == 0) | (n_lane == cur) | (n_lane == cur - 1)
    sc_row = jnp.where(n_lane * SEL_LEN <= qpos, jnp.where(forced, -NEG, sc_row), NEG)
    mi = lax.broadcasted_iota(jnp.int32, (nsp, nsp), 0)
    ni = lax.broadcasted_iota(jnp.int32, (nsp, nsp), 1)
    sc_col = jnp.sum(jnp.where(mi == ni, sc_row, 0.0), axis=-1, keepdims=True)
    beats = jnp.where(mi < ni, jnp.where(sc_col >= sc_row, 1.0, 0.0), jnp.where(sc_col > sc_row, 1.0, 0.0))
    rank = jnp.sum(beats, axis=0, keepdims=True)
    lane = lax.broadcasted_iota(jnp.int32, (1, LANES), 1)
    out = jnp.zeros((1, LANES), F32)
    for k in range(TOPN):
        nk = jnp.sum(jnp.where(rank == float(k), n_lane.astype(F32), 0.0), axis=-1, keepdims=True)
        out = jnp.where(lane == k, nk, out)
    idx_ref[0, 0] = jnp.broadcast_to(out, (8, LANES)).astype(jnp.int32)


def _cmpsel_step(q16, p1k, p2k, p1v, p2v, cwk, cwv, gate_rows, qpos):
    D, G = q16.shape[:2]
    nc = p1k.shape[1] // G
    n_sel = -(-(qpos + 1) // SEL_LEN)
    nsp = -(-n_sel // LANES) * LANES
    ov = jnp.asarray(np.pad(_overlap_t(nc, n_sel).T, ((0, 0), (0, nsp - n_sel))), BF16)
    big = pl.BlockSpec((1, nc * G, HEAD_DIM), lambda d, g: (d, 0, 0))
    qspec = pl.BlockSpec((1, 1, 16, HEAD_DIM), lambda d, g: (d, g, 0, 0))
    w2spec = pl.BlockSpec((HEAD_DIM, HEAD_DIM), lambda d, g: (0, 0))
    return pl.pallas_call(
        functools.partial(_cmpsel_step_kernel, qpos, nsp),
        out_shape=[jax.ShapeDtypeStruct((D, G, 16, HEAD_DIM), F32),
                   jax.ShapeDtypeStruct((D, G, 8, LANES), jnp.int32)],
        grid=(D, G),
        in_specs=[qspec, big, big, big, big, w2spec, w2spec,
                  pl.BlockSpec((nc, nsp), lambda d, g: (0, 0)),
                  pl.BlockSpec((1, 1, 16, 3), lambda d, g: (d, g, 0, 0))],
        out_specs=[qspec, pl.BlockSpec((1, 1, 8, LANES), lambda d, g: (d, g, 0, 0))],
        compiler_params=_cparams(("arbitrary", "arbitrary")),
        name="nsa_cmpsel_step",
    )(q16, p1k, p2k, p1v, p2v, cwk[4], cwv[4], ov, gate_rows)


def _sel_step_kernel(n_past, pt_ref, idx_ref, q_ref, kn_ref, vn_ref, *refs):
    k_refs, v_refs = refs[:TOPN], refs[TOPN:2 * TOPN]
    g_ref, o_ref = refs[2 * TOPN:]
    d, g = pl.program_id(0), pl.program_id(1)
    q = q_ref[0, 0] * (HEAD_DIM ** -0.5)
    qb = q.astype(BF16)
    rows = pl.ds(g, SEL_LEN, stride=NSA_GROUPS)
    s_new = jnp.sum(q * kn_ref[0, 0], axis=-1, keepdims=True)
    s = [_dot_nt(qb, k_refs[k][rows, :].astype(BF16)) + jnp.where(idx_ref[d, g, k] < n_past, 0.0, NEG)
         for k in range(TOPN)]
    m = functools.reduce(jnp.maximum, [jnp.max(sk, axis=-1, keepdims=True) for sk in s] + [s_new])
    p = [jnp.exp(sk - m) for sk in s]
    p_new = jnp.exp(s_new - m)
    l = sum(jnp.sum(pk, axis=-1, keepdims=True) for pk in p) + p_new
    o = sum(_dot(p[k].astype(BF16), v_refs[k][rows, :].astype(BF16)) for k in range(TOPN)) + p_new * vn_ref[0, 0]
    o_ref[0, 0] = o / l * g_ref[0, 0][:, 1:2]


def _sel_step(page_table, idx, q16, k_new, v_new, cache_k, cache_v, gate_rows):
    D, NP = page_table.shape
    G = NSA_GROUPS
    n_pool = cache_k.shape[0]
    per_page = PAGE // SEL_LEN
    n_past = NP * per_page
    blk_rows = SEL_LEN * G
    k2 = cache_k.reshape(n_pool * PAGE * G, HEAD_DIM)
    v2 = cache_v.reshape(n_pool * PAGE * G, HEAD_DIM)

    def blk(k):
        def index(d, g, pt, ix):
            n = jnp.minimum(ix[d, g, k], n_past - 1)
            return (pt[d, n // per_page] * per_page + n % per_page, 0)
        return pl.BlockSpec((blk_rows, HEAD_DIM), index)

    qspec = pl.BlockSpec((1, 1, 16, HEAD_DIM), lambda d, g, pt, ix: (d, g, 0, 0))
    nspec = pl.BlockSpec((1, 1, 1, HEAD_DIM), lambda d, g, pt, ix: (d, g, 0, 0))
    return pl.pallas_call(
        functools.partial(_sel_step_kernel, n_past),
        out_shape=jax.ShapeDtypeStruct((D, G, 16, HEAD_DIM), F32),
        grid_spec=pltpu.PrefetchScalarGridSpec(
            num_scalar_prefetch=2, grid=(D, G),
            in_specs=[qspec, nspec, nspec] + [blk(k) for k in range(TOPN)] * 2
            + [pl.BlockSpec((1, 1, 16, 3), lambda d, g, pt, ix: (d, g, 0, 0))],
            out_specs=qspec),
        compiler_params=_cparams(("arbitrary",) * 2),
        name="nsa_sel_step",
    )(page_table, idx, q16, k_new, v_new, *([k2] * TOPN), *([v2] * TOPN), gate_rows)


def _win_step_kernel(q_ref, k_ref, v_ref, g_ref, o_ref):
    q = (q_ref[0, 0] * (HEAD_DIM ** -0.5)).astype(BF16)
    s = _dot_nt(q, k_ref[0].astype(BF16))
    e = jnp.exp(s - jnp.max(s, axis=-1, keepdims=True))
    p = e / jnp.sum(e, axis=-1, keepdims=True)
    o_ref[0, 0] = _dot(p.astype(BF16), v_ref[0].astype(BF16)) * g_ref[0, 0][:, 2:3]


def _win_step(q16, kw, vw, gate_rows):
    D, G = q16.shape[:2]
    L = kw.shape[1]
    qspec = pl.BlockSpec((1, 1, 16, HEAD_DIM), lambda d, g: (d, g, 0, 0))
    kspec = pl.BlockSpec((1, L, HEAD_DIM), lambda d, g: (d, 0, g))
    return pl.pallas_call(
        _win_step_kernel,
        out_shape=jax.ShapeDtypeStruct((D, G, 16, HEAD_DIM), F32),
        grid=(D, G),
        in_specs=[qspec, kspec, kspec, pl.BlockSpec((1, 1, 16, 3), lambda d, g: (d, g, 0, 0))],
        out_specs=qspec,
        compiler_params=_cparams(("arbitrary", "arbitrary")),
        name="nsa_win_step",
    )(q16, kw, vw, gate_rows)


TN = 512
TM = 512
TM_STEP = 8


def _pad_cols(a, n):
    return jnp.pad(a, ((0, 0), (0, n - a.shape[1])))


TN_PROJ = 1024


def _segments(widths_modes):
    segs, start = [], 0
    for width, mode in widths_modes:
        segs.append((start, width, mode))
        start += width
    return segs, start


def _proj_weights(w_stack, layer, n_main):
    wb = w_stack[layer].astype(BF16)
    return wb, _pad_cols(wb[:, n_main:], LANES)


def _even_params(e_norm_mix, e_w_in, layer, lb, out_norm, f_bias, q_norm, k_norm, e_w_out, e_norm_ffn):
    d = e_w_in.shape[1]
    aw, bw = H_HGRN * HEAD_DIM, H_FOX * HEAD_DIM
    segs, n_main = _segments(((4 * aw, 'raw'), (bw, 'hnorm'), (bw, 'hnorm'), (bw, 'raw')))
    gain = jnp.concatenate([jnp.ones((4 * aw,), F32), jnp.tile(q_norm, H_FOX), jnp.tile(k_norm, H_FOX),
                            jnp.ones((bw,), F32)])
    w_in, w_tail = _proj_weights(e_w_in, layer, n_main)
    return dict(norm=e_norm_mix.reshape(1, d), w_in=w_in, w_tail=w_tail, gain=gain.reshape(1, -1),
                bias=_pad_cols(f_bias.astype(F32).reshape(1, -1), LANES), segs=segs, lb=lb.reshape(1, aw),
                out_norm=out_norm.reshape(1, HEAD_DIM), w_out=e_w_out.astype(BF16), norm_ffn=e_norm_ffn.reshape(1, d))


def _even_proj(x2, p, tm):
    zeros = jnp.zeros((tm, HEAD_DIM), F32)
    return _proj(x2, p['norm'], p['w_in'], p['w_tail'], p['gain'], p['bias'], zeros, zeros, p['segs'], 'logsig',
                 tm=tm, tn=TN_PROJ, name="even_proj")


def _even_finish(x2, oa, of, p, ffn_w, tm, emit=False):
    x2 = _outproj([[oa], [of]], p['w_out'], x2, tm=tm, tn=TN_PROJ, name="even_out")
    return _ffn(x2, p['norm_ffn'], *ffn_w, tm=tm, tf=TN, emit=emit, name="ffn")


def _odd_params(o_norm_mix, o_w_in, layer, q_norm, ck_norm, sk_norm, wk_norm, pe_k, w1_k, w2_k, pe_v, w1_v, w2_v,
                o_w_out, o_norm_ffn, router_w, router_b):
    d = o_w_in.shape[1]
    qw, kvw = H_NSA * HEAD_DIM, NSA_GROUPS * HEAD_DIM
    segs, n_main = _segments(((qw, 'hnorm_both'), (kvw, 'hnorm'), (kvw, 'raw'), (kvw, 'hnorm_rope'), (kvw, 'raw'),
                              (kvw, 'hnorm_rope'), (kvw, 'raw')))
    ones = jnp.ones((kvw,), F32)
    gain = jnp.concatenate([jnp.tile(q_norm, H_NSA), jnp.tile(ck_norm, NSA_GROUPS), ones,
                            jnp.tile(sk_norm, NSA_GROUPS), ones, jnp.tile(wk_norm, NSA_GROUPS), ones])
    w_in, w_tail = _proj_weights(o_w_in, layer, n_main)
    return dict(norm=o_norm_mix.reshape(1, d), w_in=w_in, w_tail=w_tail, gain=gain.reshape(1, -1),
                bias=jnp.zeros((1, LANES), F32), segs=segs,
                cwk=_cmp_weights(pe_k, w1_k, w2_k), cwv=_cmp_weights(pe_v, w1_v, w2_v),
                w_out=o_w_out.astype(BF16), norm_ffn=o_norm_ffn.reshape(1, d),
                rw=_pad_cols(router_w.astype(BF16), LANES), rb=_pad_cols(router_b.reshape(1, -1).astype(F32), LANES))


def _odd_proj(x2, p, pos, tm):
    cosf, sinf = _rope_tables(pos)
    return _proj(x2, p['norm'], p['w_in'], p['w_tail'], p['gain'], p['bias'], cosf, sinf, p['segs'], 'sigmoid',
                 tm=tm, tn=TN_PROJ, name="odd_proj")


def _odd_finish(x2, branches, p, moe_w, tm, emit=False):
    x2 = _outproj([branches], p['w_out'], x2, tm=tm, tn=TN_PROJ, name="odd_out")
    args = (x2, p['norm_ffn'], p['rw'], p['rb'], *moe_w)
    if emit:
        return _moe(*args, tm=tm, tf=256, emit=True, name="moe")
    return _moe_routed(*args, tm=tm)


def kernel(x_prompt, x_sample, state_hgrn, cache_fox_k, cache_fox_v, cache_fox_logf, cache_nsa_cmp_k, cache_nsa_cmp_v, cache_nsa_sel_k, cache_nsa_sel_v, cache_nsa_win_k, cache_nsa_win_v, page_table, e_norm_mix, e_w_in, hgrn_lb_logits, hgrn_out_norm, fox_f_bias, fox_q_norm, fox_k_norm, e_w_out, e_norm_ffn, ffn_w_gate, ffn_w_up, ffn_w_down, o_norm_mix, o_w_in, nsa_q_norm, nsa_cmp_k_norm, nsa_sel_k_norm, nsa_win_k_norm, cmp_pe_k, cmp_w1_k, cmp_w2_k, cmp_pe_v, cmp_w1_v, cmp_w2_v, o_w_out, o_norm_ffn, router_w, router_b, moe_w_gate, moe_w_up, moe_w_down):
    B, S, d = x_prompt.shape
    D, T, _ = x_sample.shape
    NP = page_table.shape[1]
    past = NP * PAGE
    w_buf = cache_nsa_win_k.shape[2]
    assert T == 1 and w_buf == WINDOW and S >= WINDOW and cache_fox_k.shape[2] == PAGE
    G = NSA_GROUPS
    lbs = jnp.cumsum(jax.nn.softmax(hgrn_lb_logits.astype(F32), axis=0), axis=0)
    xp, xd = x_prompt.reshape(B * S, d), x_sample.reshape(D, d)
    tm_p = min(TM, S)

    li = 0
    p = _even_params(e_norm_mix[li], e_w_in, li, lbs[li], hgrn_out_norm[li], fox_f_bias[li], fox_q_norm[li],
                     fox_k_norm[li], e_w_out[li], e_norm_ffn[li])
    hg, fq, fk, fv, fl = _even_proj(xd, p, TM_STEP)
    oa, st_d = _hgrn_step(hg, p['lb'], p['out_norm'], state_hgrn[li].astype(F32))
    heads = lambda a: a.reshape(D, H_FOX, HEAD_DIM)
    of = _fox_step(page_table, heads(fq), heads(fk), heads(fv), fl[:, :H_FOX],
                   cache_fox_k[li], cache_fox_v[li], cache_fox_logf[li], PG=16)
    xd, ffn_bf16 = _even_finish(xd, oa, of.reshape(D, H_FOX * HEAD_DIM), p,
                                (ffn_w_gate[li], ffn_w_up[li], ffn_w_down[li]), TM_STEP, emit=True)
    fox_d = (fk.reshape(1, D, 1, H_FOX, HEAD_DIM), fv.reshape(1, D, 1, H_FOX, HEAD_DIM),
             fl[:, :H_FOX].reshape(1, D, 1, H_FOX))

    hg, fq, fk, fv, fl = _even_proj(xp, p, tm_p)
    aug_q, aug_k = _seq_cumsum(fl, B, S, tc=tm_p)
    oa, st_p = _hgrn_prompt(hg, p['lb'], p['out_norm'], B, S, T=min(256, S), c=32)
    fvt = fv.reshape(B, S, H_FOX, HEAD_DIM).transpose(0, 2, 3, 1)
    of = _fox_prompt(fq, fk, fvt, aug_q, aug_k, B, S, tq=min(1024, S), tk=tm_p)
    xp = _even_finish(xp, oa, of, p, ffn_bf16, tm_p)
    fox_p = (fk.reshape(1, B, S, H_FOX, HEAD_DIM), fv.reshape(1, B, S, H_FOX, HEAD_DIM),
             fl[:, :H_FOX].reshape(1, B, S, H_FOX))

    p = _odd_params(o_norm_mix[li], o_w_in, li, nsa_q_norm[li], nsa_cmp_k_norm[li], nsa_sel_k_norm[li],
                    nsa_win_k_norm[li], cmp_pe_k[li], cmp_w1_k[li], cmp_w2_k[li], cmp_pe_v[li], cmp_w1_v[li],
                    cmp_w2_v[li], o_w_out[li], o_norm_ffn[li], router_w[li], router_b[li])
    qn, qr, ck, cv, sk, sv, wk, wv, gt = _odd_proj(xd, p, jnp.full((TM_STEP,), past), TM_STEP)
    rows16 = lambda a, w: jnp.pad(a.reshape(D, G, NSA_R, w), ((0, 0), (0, 0), (0, 16 - NSA_R), (0, 0)))
    gate_rows = rows16(gt[:, :3 * H_NSA], 3)
    q16n, q16r = rows16(qn, HEAD_DIM), rows16(qr, HEAD_DIM)
    p1k, p2k = _cmp_pages(page_table, cache_nsa_cmp_k[li], p['cwk'], PG=16)
    p1v, p2v = _cmp_pages(page_table, cache_nsa_cmp_v[li], p['cwv'], PG=16)
    o_cmp, idx = _cmpsel_step(q16n, p1k, p2k, p1v, p2v, p['cwk'], p['cwv'], gate_rows, past)
    new = lambda a: a.reshape(D, G, 1, HEAD_DIM)
    o_sel = _sel_step(page_table, idx[:, :, 0, :TOPN], q16r, new(sk), new(sv),
                      cache_nsa_sel_k[li], cache_nsa_sel_v[li], gate_rows)
    kvd = lambda a: a.reshape(D, 1, G, HEAD_DIM)
    win_k = jnp.concatenate([cache_nsa_win_k[li], kvd(wk)], axis=1)[:, -w_buf:]
    win_v = jnp.concatenate([cache_nsa_win_v[li], kvd(wv)], axis=1)[:, -w_buf:]
    o_win = _win_step(q16r, win_k.reshape(D, w_buf, G * HEAD_DIM), win_v.reshape(D, w_buf, G * HEAD_DIM), gate_rows)
    unrow = lambda a: a[:, :, :NSA_R].reshape(D, H_NSA * HEAD_DIM)
    xd, moe_bf16 = _odd_finish(xd, [unrow(o_cmp), unrow(o_sel), unrow(o_win)], p,
                               (moe_w_gate[li], moe_w_up[li], moe_w_down[li]), TM_STEP, emit=True)
    kv1 = lambda a: a.reshape(1, D, 1, G, HEAD_DIM)
    nsa_d = (kv1(ck), kv1(cv), kv1(sk), kv1(sv), win_k[None], win_v[None])

    qn, qr, ck, cv, sk, sv, wk, wv, gt = _odd_proj(xp, p, jnp.arange(S), tm_p)
    seq = lambda a: a.reshape(B, S, -1)
    gates_g = jnp.pad(gt[:, :3 * H_NSA].reshape(B, S, G, 3 * NSA_R).transpose(0, 2, 1, 3),
                      ((0, 0), (0, 0), (0, 0), (0, 16 - 3 * NSA_R)))
    kc = _compress_prompt(seq(ck), p['cwk'], B, S)
    vc = _compress_prompt(seq(cv), p['cwv'], B, S)
    tq = min(256, S)
    o_cmp, selneg = _cmpsel_prompt(seq(qn), kc, vc, gates_g, B, S, Tq=tq)
    keys_last = lambda a: a.reshape(B, S, G, HEAD_DIM).transpose(0, 2, 3, 1)
    o_sel = _sel_prompt(seq(qr), selneg, seq(sk), keys_last(sv), gates_g, B, S, Tq=tq, tk=min(512, S))
    o_win = _win_prompt(seq(qr), seq(wk), keys_last(wv), gates_g, B, S, Tq=tq)
    flat = lambda a: a.reshape(B * S, -1)
    xp = _odd_finish(xp, [flat(o_cmp), flat(o_sel), flat(o_win)], p, moe_bf16, tm_p)
    kv = lambda a: a.reshape(1, B, S, G, HEAD_DIM)
    nsa_p = (kv(ck), kv(cv), kv(sk), kv(sv), kv(wk)[:, :, S - WINDOW:], kv(wv)[:, :, S - WINDOW:])

    return (xp.reshape(B, S, d), xd.reshape(D, T, d), st_p[None].astype(state_hgrn.dtype),
            st_d[None].astype(state_hgrn.dtype), *fox_p, *fox_d, *nsa_p, *nsa_d)
```

```python
import functools

import numpy as np
import jax
import jax.numpy as jnp
from jax import lax
from jax.experimental import pallas as pl
from jax.experimental.pallas import tpu as pltpu

F32 = jnp.float32
BF16 = jnp.bfloat16

LANES = 128
HEAD_DIM = 128
EPS = 1e-6
ROPE_THETA = 10000.0
LOG2E = 1.4426950408889634
QSCALE = HEAD_DIM ** -0.5 * LOG2E
NEG = -1e30
M_INIT = -1e20
VMEM_LIMIT = 56 * 1024 * 1024

H_HGRN = 8
H_FOX = 8
H_NSA = 16
NSA_GROUPS = 4
NSA_R = H_NSA // NSA_GROUPS
CMP_STRIDE = 16
CMP_LEN = 32
SEL_LEN = 64
SEL_SHIFT = 6
TOPN = 16
WINDOW = 512
N_EXPERTS = 8
PAGE = 128


def _cparams(sem):
    return pltpu.CompilerParams(dimension_semantics=sem, vmem_limit_bytes=VMEM_LIMIT)


def _dot(a, b):
    return jnp.dot(a, b, preferred_element_type=F32)


def _dot_nt(a, b):
    return lax.dot_general(a, b, (((1,), (1,)), ((), ())), preferred_element_type=F32)


def _dot_tn(a, b):
    return lax.dot_general(a, b, (((0,), (0,)), ((), ())), preferred_element_type=F32)


def _split3(x):
    hi = x.astype(BF16)
    r1 = x - hi.astype(F32)
    mid = r1.astype(BF16)
    lo = (r1 - mid.astype(F32)).astype(BF16)
    return hi, mid, lo


def _dot3(w_bf16, x):
    hi, mid, lo = _split3(x)
    return _dot(w_bf16, hi) + _dot(w_bf16, mid) + _dot(w_bf16, lo)


def _sigmoid(x):
    return 1.0 / (1.0 + jnp.exp(-x))


def _silu(x):
    return x * _sigmoid(x)


def _log_sigmoid(x):
    return jnp.minimum(x, 0.0) - jnp.log(1.0 + jnp.exp(-jnp.abs(x)))


def _tril_bf16(n):
    r = lax.broadcasted_iota(jnp.int32, (n, n), 0)
    c = lax.broadcasted_iota(jnp.int32, (n, n), 1)
    return jnp.where(c <= r, 1.0, 0.0).astype(BF16)


def _head_rms(a, gain):
    ms = jnp.mean(a * a, axis=-1, keepdims=True)
    return a * lax.rsqrt(ms + EPS) * gain


def _rope(y, cosf, sinf):
    return y * cosf + pltpu.roll(y, HEAD_DIM // 2, 1) * sinf


def _rope_tables(pos):
    half = HEAD_DIM // 2
    inv = ROPE_THETA ** (-jnp.arange(half, dtype=F32) / half)
    ang = pos.astype(F32)[:, None] * inv[None, :]
    cos, sin = jnp.cos(ang), jnp.sin(ang)
    return jnp.concatenate([cos, cos], axis=-1), jnp.concatenate([-sin, sin], axis=-1)


PROJ_PIECE = 512


def _n_outs(mode):
    return 2 if mode == 'hnorm_both' else 1


def _seg_block(width, tn):
    return min(width, tn)


def _proj_kernel(segs, tail_mode, tn, *refs):
    n_out = sum(_n_outs(s[2]) for s in segs)
    x_ref, g_ref, w_ref, wt_ref, gain_ref, bias_ref, cos_ref, sin_ref = refs[:8]
    out_refs = refs[8:8 + n_out]
    tail_ref = refs[8 + n_out]
    xn_ref = refs[9 + n_out]
    j = pl.program_id(1)

    @pl.when(j == 0)
    def _():
        xf = x_ref[...]
        ms = jnp.mean(xf * xf, axis=-1, keepdims=True)
        xn_ref[...] = (xf * lax.rsqrt(ms + EPS) * g_ref[...]).astype(BF16)

    def emit(mode, outs, a, b, dst):
        for p0 in range(a, b, PROJ_PIECE):
            p1 = min(p0 + PROJ_PIECE, b)
            acc = _dot(xn_ref[...], w_ref[:, p0:p1])
            for h in range((p1 - p0) // HEAD_DIM):
                sub = slice(h * HEAD_DIM, (h + 1) * HEAD_DIM)
                src = slice(p0 + h * HEAD_DIM, p0 + (h + 1) * HEAD_DIM)
                to = slice(dst + p0 - a + h * HEAD_DIM, dst + p0 - a + (h + 1) * HEAD_DIM)
                if mode == 'raw':
                    outs[0][:, to] = acc[:, sub]
                    continue
                y = _head_rms(acc[:, sub], gain_ref[:, src])
                if mode in ('hnorm', 'hnorm_both'):
                    outs[0][:, to] = y
                if mode in ('hnorm_rope', 'hnorm_both'):
                    outs[-1][:, to] = _rope(y, cos_ref[...], sin_ref[...])

    ntiles = pl.num_programs(1)
    for tile in range((segs[-1][0] + segs[-1][1]) // tn):
        @pl.when(j == tile)
        def _(tile=tile):
            oi = 0
            for (start, width, mode) in segs:
                outs = out_refs[oi:oi + _n_outs(mode)]
                oi += _n_outs(mode)
                a, b = max(start, tile * tn), min(start + width, (tile + 1) * tn)
                if a < b:
                    emit(mode, outs, a - tile * tn, b - tile * tn, (a - start) % _seg_block(width, tn))

    @pl.when(j == ntiles - 1)
    def _():
        t = _dot(xn_ref[...], wt_ref[...]) + bias_ref[...]
        tail_ref[...] = _sigmoid(t) if tail_mode == 'sigmoid' else _log_sigmoid(t)


def _proj(x, norm_g, w, w_tail, gain_all, bias_tail, cosf, sinf, segs, tail_mode, *, tm, tn, name):
    M, K = x.shape
    n_main = segs[-1][0] + segs[-1][1]
    assert M % tm == 0 and n_main % tn == 0
    nseq = cosf.shape[0] // tm
    out_shape, out_specs = [], []
    for (start, width, mode) in segs:
        bw = _seg_block(width, tn)
        assert width % bw == 0 and start % bw == 0
        for _ in range(_n_outs(mode)):
            out_shape.append(jax.ShapeDtypeStruct((M, width), F32))
            out_specs.append(pl.BlockSpec(
                (tm, bw), lambda i, j, s=start, bw=bw, n=width // bw: (i, jnp.clip((j * tn - s) // bw, 0, n - 1))))
    out_shape.append(jax.ShapeDtypeStruct((M, LANES), F32))
    out_specs.append(pl.BlockSpec((tm, LANES), lambda i, j: (i, 0)))
    return pl.pallas_call(
        functools.partial(_proj_kernel, tuple(segs), tail_mode, tn),
        out_shape=out_shape,
        grid=(M // tm, n_main // tn),
        in_specs=[
            pl.BlockSpec((tm, K), lambda i, j: (i, 0)),
            pl.BlockSpec((1, K), lambda i, j: (0, 0)),
            pl.BlockSpec((K, tn), lambda i, j: (0, j)),
            pl.BlockSpec((K, LANES), lambda i, j: (0, 0)),
            pl.BlockSpec((1, tn), lambda i, j: (0, j)),
            pl.BlockSpec((1, LANES), lambda i, j: (0, 0)),
            pl.BlockSpec((tm, HEAD_DIM), lambda i, j: (i % nseq, 0)),
            pl.BlockSpec((tm, HEAD_DIM), lambda i, j: (i % nseq, 0)),
        ],
        out_specs=out_specs,
        scratch_shapes=[pltpu.VMEM((tm, K), BF16)],
        compiler_params=_cparams(("arbitrary", "arbitrary")),
        name=name,
    )(x, norm_g, w, w_tail, gain_all, bias_tail, cosf, sinf)


def _outproj_kernel(group_sizes, *refs):
    n_lhs = sum(group_sizes)
    lhs = refs[:n_lhs]
    w_ref, res_ref, o_ref, xs_ref = refs[n_lhs:n_lhs + 4]

    @pl.when(pl.program_id(1) == 0)
    def _():
        k0, r0 = 0, 0
        for n in group_sizes:
            x = lhs[r0][...]
            for r in lhs[r0 + 1:r0 + n]:
                x = x + r[...]
            xs_ref[:, k0:k0 + x.shape[1]] = x.astype(BF16)
            k0, r0 = k0 + x.shape[1], r0 + n

    o_ref[...] = res_ref[...] + _dot(xs_ref[...], w_ref[...])


def _outproj(lhs_groups, w, res, *, tm, tn, name):
    M = res.shape[0]
    K, N = w.shape
    flat = [a for g in lhs_groups for a in g]
    assert sum(g[0].shape[1] for g in lhs_groups) == K
    return pl.pallas_call(
        functools.partial(_outproj_kernel, tuple(len(g) for g in lhs_groups)),
        out_shape=jax.ShapeDtypeStruct((M, N), F32),
        grid=(M // tm, N // tn),
        in_specs=[pl.BlockSpec((tm, a.shape[1]), lambda i, j: (i, 0)) for a in flat] + [
            pl.BlockSpec((K, tn), lambda i, j: (0, j)),
            pl.BlockSpec((tm, tn), lambda i, j: (i, j)),
        ],
        out_specs=pl.BlockSpec((tm, tn), lambda i, j: (i, j)),
        scratch_shapes=[pltpu.VMEM((tm, K), BF16)],
        compiler_params=_cparams(("arbitrary", "arbitrary")),
        name=name,
    )(*flat, w, res)


def _ffn_kernel(emit, x_ref, g_ref, wg_ref, wu_ref, wd_ref, o_ref, *rest):
    xn_ref = rest[-1]
    f = pl.program_id(1)

    @pl.when(f == 0)
    def _():
        xf = x_ref[...]
        ms = jnp.mean(xf * xf, axis=-1, keepdims=True)
        xn_ref[...] = (xf * lax.rsqrt(ms + EPS) * g_ref[...]).astype(BF16)
        o_ref[...] = xf

    xn = xn_ref[...]
    wg, wu, wd = wg_ref[...].astype(BF16), wu_ref[...].astype(BF16), wd_ref[...].astype(BF16)
    if emit:
        rest[0][...], rest[1][...], rest[2][...] = wg, wu, wd
    h = _silu(_dot(xn, wg)) * _dot(xn, wu)
    o_ref[...] += _dot(h.astype(BF16), wd)


def _ffn(x, norm_g, wg, wu, wd, *, tm, tf, emit=False, name):
    M, K = x.shape
    F = wg.shape[1]
    assert not emit or M == tm
    w_specs = [pl.BlockSpec((K, tf), lambda i, f: (0, f)), pl.BlockSpec((K, tf), lambda i, f: (0, f)),
               pl.BlockSpec((tf, K), lambda i, f: (f, 0))]
    out_shape = [jax.ShapeDtypeStruct((M, K), F32)]
    out_specs = [pl.BlockSpec((tm, K), lambda i, f: (i, 0))]
    if emit:
        out_shape += [jax.ShapeDtypeStruct(w.shape, BF16) for w in (wg, wu, wd)]
        out_specs += w_specs
    out = pl.pallas_call(
        functools.partial(_ffn_kernel, emit),
        out_shape=out_shape,
        grid=(M // tm, F // tf),
        in_specs=[pl.BlockSpec((tm, K), lambda i, f: (i, 0)), pl.BlockSpec((1, K), lambda i, f: (0, 0))] + w_specs,
        out_specs=out_specs,
        scratch_shapes=[pltpu.VMEM((tm, K), BF16)],
        compiler_params=_cparams(("arbitrary", "arbitrary")),
        name=name,
    )(x, norm_g, wg, wu, wd)
    return (out[0], tuple(out[1:])) if emit else out[0]


def _moe_kernel(emit, x_ref, g_ref, rw_ref, rb_ref, wg_ref, wu_ref, wd_ref, o_ref, *rest):
    xn_ref, gate_ref = rest[-2:]
    e = pl.program_id(1)
    f = pl.program_id(2)

    @pl.when((e == 0) & (f == 0))
    def _():
        xf = x_ref[...]
        ms = jnp.mean(xf * xf, axis=-1, keepdims=True)
        xn = (xf * lax.rsqrt(ms + EPS) * g_ref[...]).astype(BF16)
        xn_ref[...] = xn
        o_ref[...] = xf
        lane = lax.broadcasted_iota(jnp.int32, (xf.shape[0], LANES), 1)
        logits = jnp.where(lane < N_EXPERTS, _dot(xn, rw_ref[...]) + rb_ref[...], NEG)
        m1 = jnp.max(logits, axis=-1, keepdims=True)
        i1 = jnp.min(jnp.where(logits == m1, lane, LANES), axis=-1, keepdims=True)
        l2 = jnp.where(lane == i1, NEG, logits)
        m2 = jnp.max(l2, axis=-1, keepdims=True)
        i2 = jnp.min(jnp.where(l2 == m2, lane, LANES), axis=-1, keepdims=True)
        e2 = jnp.exp(m2 - m1)
        w1 = 1.0 / (1.0 + e2)
        gate_ref[...] = jnp.where(lane == i1, w1, 0.0) + jnp.where(lane == i2, e2 * w1, 0.0)

    xn = xn_ref[...]
    wg, wu, wd = wg_ref[0].astype(BF16), wu_ref[0].astype(BF16), wd_ref[0].astype(BF16)
    if emit:
        rest[0][0], rest[1][0], rest[2][0] = wg, wu, wd
    h = _silu(_dot(xn, wg)) * _dot(xn, wu)
    lane = lax.broadcasted_iota(jnp.int32, gate_ref.shape, 1)
    ge = jnp.sum(jnp.where(lane == e, gate_ref[...], 0.0), axis=-1, keepdims=True)
    o_ref[...] += ge * _dot(h.astype(BF16), wd)


def _moe(x, norm_g, rw, rb, wg, wu, wd, *, tm, tf, emit=False, name):
    M, K = x.shape
    E, _, F = wg.shape
    assert not emit or M == tm
    w_specs = [pl.BlockSpec((1, K, tf), lambda i, e, f: (e, 0, f)), pl.BlockSpec((1, K, tf), lambda i, e, f: (e, 0, f)),
               pl.BlockSpec((1, tf, K), lambda i, e, f: (e, f, 0))]
    out_shape = [jax.ShapeDtypeStruct((M, K), F32)]
    out_specs = [pl.BlockSpec((tm, K), lambda i, e, f: (i, 0))]
    if emit:
        out_shape += [jax.ShapeDtypeStruct(w.shape, BF16) for w in (wg, wu, wd)]
        out_specs += w_specs
    out = pl.pallas_call(
        functools.partial(_moe_kernel, emit),
        out_shape=out_shape,
        grid=(M // tm, E, F // tf),
        in_specs=[
            pl.BlockSpec((tm, K), lambda i, e, f: (i, 0)),
            pl.BlockSpec((1, K), lambda i, e, f: (0, 0)),
            pl.BlockSpec((K, LANES), lambda i, e, f: (0, 0)),
            pl.BlockSpec((1, LANES), lambda i, e, f: (0, 0)),
        ] + w_specs,
        out_specs=out_specs,
        scratch_shapes=[pltpu.VMEM((tm, K), BF16), pltpu.VMEM((tm, LANES), F32)],
        compiler_params=_cparams(("arbitrary", "arbitrary", "arbitrary")),
        name=name,
    )(x, norm_g, rw, rb, wg, wu, wd)
    return (out[0], tuple(out[1:])) if emit else out[0]


TM_E = 512
EXPERT_ISSUE_STEPS = 8


def _top2(logits, lane):
    m1 = jnp.max(logits, axis=-1, keepdims=True)
    i1 = jnp.min(jnp.where(logits == m1, lane, LANES), axis=-1, keepdims=True)
    l2 = jnp.where(lane == i1, NEG, logits)
    m2 = jnp.max(l2, axis=-1, keepdims=True)
    i2 = jnp.min(jnp.where(l2 == m2, lane, LANES), axis=-1, keepdims=True)
    e2 = jnp.exp(m2 - m1)
    w1 = 1.0 / (1.0 + e2)
    return i1, i2, w1, e2 * w1


def _route_kernel(x_ref, g_ref, rw_ref, rb_ref, xn_ref, route_ref, cnt_ref, carry):
    @pl.when(pl.program_id(0) == 0)
    def _():
        carry[...] = jnp.zeros_like(carry)

    xf = x_ref[...]
    tm = xf.shape[0]
    xn = xf * lax.rsqrt(jnp.mean(xf * xf, axis=-1, keepdims=True) + EPS) * g_ref[...]
    xn_ref[...] = xn
    lane = lax.broadcasted_iota(jnp.int32, (tm, LANES), 1)
    logits = jnp.where(lane < N_EXPERTS, _dot(xn.astype(BF16), rw_ref[...]) + rb_ref[...], NEG)
    i1, i2, w1, w2 = _top2(logits, lane)
    hit = jnp.where(lane == i1, 1.0, 0.0) + jnp.where(lane == i2, 1.0, 0.0)
    incl = _dot(_tril_bf16(tm), hit.astype(BF16)) + carry[...]
    before = incl - hit
    carry[...] = incl[tm - 1:tm, :]
    cnt_ref[...] = incl[tm - 1:tm, :]
    r1 = jnp.sum(jnp.where(lane == i1, before, 0.0), axis=-1, keepdims=True)
    r2 = jnp.sum(jnp.where(lane == i2, before, 0.0), axis=-1, keepdims=True)
    cols = (i1.astype(F32), i2.astype(F32), r1, r2, w1, w2)
    rec = jnp.zeros((tm, LANES), F32)
    for c, v in enumerate(cols):
        rec = jnp.where(lane == c, v, rec)
    route_ref[...] = rec


def _route(x, norm_g, rw, rb, *, tm):
    M, K = x.shape
    return pl.pallas_call(
        _route_kernel,
        out_shape=[jax.ShapeDtypeStruct((M, K), F32), jax.ShapeDtypeStruct((M, LANES), F32),
                   jax.ShapeDtypeStruct((1, LANES), F32)],
        grid=(M // tm,),
        in_specs=[pl.BlockSpec((tm, K), lambda i: (i, 0)), pl.BlockSpec((1, K), lambda i: (0, 0)),
                  pl.BlockSpec((K, LANES), lambda i: (0, 0)), pl.BlockSpec((1, LANES), lambda i: (0, 0))],
        out_specs=[pl.BlockSpec((tm, K), lambda i: (i, 0)), pl.BlockSpec((tm, LANES), lambda i: (i, 0)),
                   pl.BlockSpec((1, LANES), lambda i: (0, 0))],
        scratch_shapes=[pltpu.VMEM((1, LANES), F32)],
        compiler_params=_cparams(("arbitrary",)),
        name="moe_route",
    )(x, norm_g, rw, rb)


def _gather_start(src_hbm, idx_ref, first, n, dst, sem, unroll):
    def body(it, carry):
        for p in range(2):
            r = 2 * it + p
            pltpu.make_async_copy(src_hbm.at[pl.ds(idx_ref[first + r], 1)], dst.at[pl.ds(r, 1)], sem).start(priority=p)
        return carry
    lax.fori_loop(0, n // 2, body, 0, unroll=unroll // 2)


def _gather_wait(dst, sem):
    pltpu.make_async_copy(dst, dst, sem).wait()


def _experts_kernel(te_ref, nu_ref, src_ref, x_hbm, wg_ref, wu_ref, wd_ref, o_ref, xbuf, xb_scr, acc_scr, sems):
    i, f = pl.program_id(0), pl.program_id(1)
    last = pl.num_programs(1) - 1
    live = i < nu_ref[0]
    slot = i % 2

    @pl.when(live & (f == 0))
    def _():
        @pl.when(i == 0)
        def _():
            _gather_start(x_hbm, src_ref, 0, TM_E, xbuf.at[0], sems.at[0], 8)

        _gather_wait(xbuf.at[slot], sems.at[slot])
        xb_scr[...] = xbuf[slot].astype(BF16)
        acc_scr[...] = jnp.zeros_like(acc_scr)

    @pl.when(live)
    def _():
        per_step = TM_E // EXPERT_ISSUE_STEPS
        first = (f - 1) * per_step

        @pl.when((i + 1 < nu_ref[0]) & (f >= 1) & (f <= EXPERT_ISSUE_STEPS))
        def _():
            for r in range(per_step):
                pltpu.make_async_copy(x_hbm.at[pl.ds(src_ref[(i + 1) * TM_E + first + r], 1)],
                                      xbuf.at[1 - slot, pl.ds(first + r, 1)], sems.at[1 - slot]).start(priority=r % 2)

        xb = xb_scr[...]
        h = _silu(_dot(xb, wg_ref[0])) * _dot(xb, wu_ref[0])
        acc_scr[...] += _dot(h.astype(BF16), wd_ref[0])

    @pl.when(f == last)
    def _():
        o_ref[...] = jnp.where(live, acc_scr[...], 0.0)


def _experts(tile_expert, n_used, src, xn, wg, wu, wd, *, tf):
    P = src.shape[0]
    E, K, F = wg.shape
    nf = F // tf
    col = lambda i, f, nu: jnp.where(i < nu[0], f, nf - 1)
    return pl.pallas_call(
        _experts_kernel,
        out_shape=jax.ShapeDtypeStruct((P, K), F32),
        grid_spec=pltpu.PrefetchScalarGridSpec(
            num_scalar_prefetch=3, grid=(P // TM_E, nf),
            in_specs=[pl.BlockSpec(memory_space=pl.ANY),
                      pl.BlockSpec((1, K, tf), lambda i, f, te, nu, sr: (te[i], 0, col(i, f, nu))),
                      pl.BlockSpec((1, K, tf), lambda i, f, te, nu, sr: (te[i], 0, col(i, f, nu))),
                      pl.BlockSpec((1, tf, K), lambda i, f, te, nu, sr: (te[i], col(i, f, nu), 0))],
            out_specs=pl.BlockSpec((TM_E, K), lambda i, f, te, nu, sr: (i, 0)),
            scratch_shapes=[pltpu.VMEM((2, TM_E, K), F32), pltpu.VMEM((TM_E, K), BF16), pltpu.VMEM((TM_E, K), F32),
                            pltpu.SemaphoreType.DMA((2,))]),
        compiler_params=_cparams(("arbitrary", "arbitrary")),
        name="moe_experts",
    )(tile_expert, n_used, src, xn, wg, wu, wd)


def _combine_kernel(tc, dest_ref, os_hbm, x_ref, route_ref, o_ref, buf, sems):
    i = pl.program_id(0)
    slot = i % 2

    @pl.when(i == 0)
    def _():
        _gather_start(os_hbm, dest_ref, 0, 2 * tc, buf.at[0], sems.at[0], 8)

    @pl.when(i + 1 < pl.num_programs(0))
    def _():
        _gather_start(os_hbm, dest_ref, (i + 1) * 2 * tc, 2 * tc, buf.at[1 - slot], sems.at[1 - slot], 8)

    _gather_wait(buf.at[slot], sems.at[slot])
    w1, w2 = route_ref[:, 4:5], route_ref[:, 5:6]
    o_ref[...] = x_ref[...] + (w1 * buf[slot, 0:tc, :] + w2 * buf[slot, tc:2 * tc, :])


def _combine(dest, os, x, route, *, tc):
    M, K = x.shape
    return pl.pallas_call(
        functools.partial(_combine_kernel, tc),
        out_shape=jax.ShapeDtypeStruct((M, K), F32),
        grid_spec=pltpu.PrefetchScalarGridSpec(
            num_scalar_prefetch=1, grid=(M // tc,),
            in_specs=[pl.BlockSpec(memory_space=pl.ANY),
                      pl.BlockSpec((tc, K), lambda i, d: (i, 0)), pl.BlockSpec((tc, LANES), lambda i, d: (i, 0))],
            out_specs=pl.BlockSpec((tc, K), lambda i, d: (i, 0)),
            scratch_shapes=[pltpu.VMEM((2, 2 * tc, K), F32), pltpu.SemaphoreType.DMA((2,))]),
        compiler_params=_cparams(("arbitrary",)),
        name="moe_combine",
    )(dest, os, x, route)


def _moe_routed(x, norm_g, rw, rb, wg, wu, wd, *, tm):
    M = x.shape[0]
    xn, route, counts = _route(x, norm_g, rw, rb, tm=tm)
    e = route[:, 0:2].astype(jnp.int32)
    rank = route[:, 2:4].astype(jnp.int32)
    cnt = counts[0, :N_EXPERTS].astype(jnp.int32)
    tiles = (cnt + TM_E - 1) // TM_E
    tile_end = jnp.cumsum(tiles)
    dest = ((tile_end - tiles) * TM_E)[e] + rank
    n_tiles = (2 * M) // TM_E + N_EXPERTS
    P = n_tiles * TM_E
    token = jnp.repeat(jnp.arange(M, dtype=jnp.int32), 2)
    src = jnp.zeros((P,), jnp.int32).at[dest.reshape(-1)].set(token)
    tile_expert = jnp.minimum(jnp.sum(tile_end[None, :] <= jnp.arange(n_tiles, dtype=jnp.int32)[:, None], axis=1),
                              N_EXPERTS - 1).astype(jnp.int32)
    os = _experts(tile_expert, tile_end[-1:].astype(jnp.int32), src, xn, wg, wu, wd, tf=256)
    tc = 256
    dest_tiles = dest.reshape(M // tc, tc, 2).transpose(0, 2, 1).reshape(-1).astype(jnp.int32)
    return _combine(dest_tiles, os, x, route, tc=tc)


HGRN_HEADS_PER_STEP = 4


def _hgrn_kernel(T, c, aq_ref, af_ref, ai_ref, ag_ref, lb_ref, gn_ref, o_ref, st_ref, st_scr):
    t = pl.program_id(2)

    @pl.when(t == 0)
    def _():
        st_scr[...] = jnp.zeros_like(st_scr)

    row = lax.broadcasted_iota(jnp.int32, (T, T), 0)
    col = lax.broadcasted_iota(jnp.int32, (T, T), 1)
    shift = c.bit_length() - 1
    same = (row >> shift) == (col >> shift)
    causal = same & (col <= row)
    tri = jnp.where(causal, 1.0, 0.0).astype(BF16)
    blk = jnp.where(same, 1.0, 0.0).astype(BF16)
    gn = gn_ref[...]
    for hh in range(HGRN_HEADS_PER_STEP):
        cols = slice(hh * HEAD_DIM, (hh + 1) * HEAD_DIM)
        lb = lb_ref[:, cols]
        gate = lb + (1.0 - lb) * _sigmoid(af_ref[:, cols])
        k = 1.0 - gate
        parts = _split3(jnp.log(gate))
        b = sum(_dot(tri, p) for p in parts)
        bend = sum(_dot(blk, p) for p in parts)
        qe = (_silu(aq_ref[:, cols]) * jnp.exp(b)).astype(BF16)
        ke = (k * jnp.exp(-b)).astype(BF16)
        kend = (k * jnp.exp(bend - b)).astype(BF16)
        vb = ai_ref[:, cols].astype(BF16)
        o_intra = _dot(jnp.where(causal, _dot_nt(qe, ke), 0.0).astype(BF16), vb)
        st = st_scr[hh]
        for u in range(T // c):
            rows = slice(u * c, (u + 1) * c)
            o = o_intra[rows] + _dot_nt(qe[rows], st.astype(BF16))
            st = st * jnp.exp(bend[u * c:u * c + 1]) + _dot_tn(vb[rows], kend[rows])
            o_ref[rows, cols] = _head_rms(o, gn) * _silu(ag_ref[rows, cols])
        st_scr[hh] = st

    @pl.when(t == pl.num_programs(2) - 1)
    def _():
        for hh in range(HGRN_HEADS_PER_STEP):
            st_ref[0, hh] = st_scr[hh].T


def _hgrn_prompt(hg, lb, gn, B, S, *, T, c):
    nt = S // T
    H = H_HGRN
    hp = HGRN_HEADS_PER_STEP
    W = hp * HEAD_DIM
    spec = lambda off: pl.BlockSpec((T, W), lambda b, h, t, off=off: (b * nt + t, off + h))
    return pl.pallas_call(
        functools.partial(_hgrn_kernel, T, c),
        out_shape=[jax.ShapeDtypeStruct((B * S, H * HEAD_DIM), F32),
                   jax.ShapeDtypeStruct((B, H, HEAD_DIM, HEAD_DIM), F32)],
        grid=(B, H // hp, nt),
        in_specs=[spec(0), spec(H // hp), spec(2 * H // hp), spec(3 * H // hp),
                  pl.BlockSpec((1, W), lambda b, h, t: (0, h)),
                  pl.BlockSpec((1, HEAD_DIM), lambda b, h, t: (0, 0))],
        out_specs=[pl.BlockSpec((T, W), lambda b, h, t: (b * nt + t, h)),
                   pl.BlockSpec((1, hp, HEAD_DIM, HEAD_DIM), lambda b, h, t: (b, h, 0, 0))],
        scratch_shapes=[pltpu.VMEM((hp, HEAD_DIM, HEAD_DIM), F32)],
        compiler_params=_cparams(("arbitrary", "arbitrary", "arbitrary")),
        name="hgrn_prompt",
    )(hg, hg, hg, hg, lb, gn)


BIAS_SHIFT = 4
BIAS_SLOT = 1 << BIAS_SHIFT


def _cumsum_kernel(lf_ref, pq_ref, pk_ref, oq_ref, ok_ref, aq_ref, ak_ref, carry):
    @pl.when(pl.program_id(1) == 0)
    def _():
        carry[...] = jnp.zeros_like(carry)

    tc = lf_ref.shape[0]
    c = _dot3(_tril_bf16(tc), lf_ref[...]) + carry[...]
    carry[...] = c[tc - 1:tc, :]
    parts = _split3(c * LOG2E)
    aq_ref[...] = sum(_dot(t, pq_ref[i]) for i, t in enumerate(parts)) + oq_ref[...]
    ak_ref[...] = (ok_ref[...] - sum(_dot(t, pk_ref[i]) for i, t in enumerate(parts))).astype(BF16)


def _bias_layout():
    pq, pk = np.zeros((3, LANES, LANES), np.float32), np.zeros((3, LANES, LANES), np.float32)
    oq, ok = np.zeros((1, LANES), np.float32), np.zeros((1, LANES), np.float32)
    for h in range(H_FOX):
        for i in range(3):
            pq[i, h, BIAS_SLOT * h + i] = 1.0
            pk[i, h, BIAS_SLOT * h + 3 + i] = 1.0
            oq[0, BIAS_SLOT * h + 3 + i] = 1.0
            ok[0, BIAS_SLOT * h + i] = 1.0
    return jnp.asarray(pq, BF16), jnp.asarray(pk, BF16), jnp.asarray(oq), jnp.asarray(ok)


def _seq_cumsum(lf, B, S, *, tc):
    n = S // tc
    const3 = pl.BlockSpec((3, LANES, LANES), lambda b, t: (0, 0, 0))
    const1 = pl.BlockSpec((1, LANES), lambda b, t: (0, 0))
    blk = pl.BlockSpec((tc, LANES), lambda b, t: (b * n + t, 0))
    return pl.pallas_call(
        _cumsum_kernel,
        out_shape=[jax.ShapeDtypeStruct((B * S, LANES), F32), jax.ShapeDtypeStruct((B * S, LANES), BF16)],
        grid=(B, n),
        in_specs=[blk, const3, const3, const1, const1],
        out_specs=[blk, blk],
        scratch_shapes=[pltpu.VMEM((1, LANES), F32)],
        compiler_params=_cparams(("arbitrary", "arbitrary")),
        name="seq_cumsum",
    )(lf, *_bias_layout())


def _online_init(m_scr, l_scr, acc_scr):
    m_scr[...] = jnp.full_like(m_scr, M_INIT)
    l_scr[...] = jnp.zeros_like(l_scr)
    acc_scr[...] = jnp.zeros_like(acc_scr)


def _online_update(s, vb, m_scr, l_scr, acc_scr):
    m_prev = m_scr[...]
    m_new = jnp.maximum(m_prev, jnp.max(s, axis=-1, keepdims=True))
    alpha = jnp.exp(m_prev - m_new)
    p = jnp.exp(s - m_new)
    l_scr[...] = alpha * l_scr[...] + jnp.sum(p, axis=-1, keepdims=True)
    acc_scr[...] = alpha * acc_scr[...] + _dot(p.astype(BF16), vb)
    m_scr[...] = m_new


def _online_result(l_scr, acc_scr):
    l = l_scr[...]
    return acc_scr[...] / jnp.where(l > 0.0, l, 1.0)


def _flash_t_step(sT, vT, m, l, acc_scr):
    m_new = jnp.maximum(m, jnp.max(sT, axis=0, keepdims=True))
    alpha = jnp.exp2(m - m_new)
    p = jnp.exp2(sT - m_new)
    acc_scr[...] = alpha * acc_scr[...] + _dot(vT, p.astype(BF16))
    return m_new, alpha * l + jnp.sum(p, axis=0, keepdims=True)


def _flash_t_causal(ka_scr, vt_scr, qa, acc_scr, n_full, n_diag, tk, qpos, unroll=1):
    R = qa.shape[1]
    acc_scr[...] = jnp.zeros_like(acc_scr)

    def body(it, carry):
        for u in range(unroll):
            kj = it * unroll + u
            carry = _flash_t_step(_dot(ka_scr[kj], qa), vt_scr[kj], *carry, acc_scr)
        return carry

    m, l = lax.fori_loop(0, n_full // unroll, body, (jnp.full((1, R), M_INIT, F32), jnp.zeros((1, R), F32)))
    for i in range(n_diag):
        kpos = (n_full + i) * tk + lax.broadcasted_iota(jnp.int32, (tk, R), 0)
        sT = jnp.where(kpos <= qpos, _dot(ka_scr[n_full + i], qa), NEG)
        m, l = _flash_t_step(sT, vt_scr[n_full + i], m, l, acc_scr)
    return acc_scr[...] / jnp.where(l > 0.0, l, 1.0)


def _fox_kernel(tq, tk, q_ref, k_ref, vt_ref, aq_ref, ak_ref, o_ref, ka_scr, vt_scr, acc_scr):
    h = pl.program_id(1)
    qi = pl.program_id(2)

    @pl.when(qi == 0)
    def _():
        for c in range(ka_scr.shape[0]):
            rows = slice(c * tk, (c + 1) * tk)
            ka_scr[c] = jnp.concatenate([k_ref[rows, :].astype(BF16), ak_ref[rows, :]], axis=1)
            vt_scr[c] = vt_ref[0, 0, :, rows].astype(BF16)

    lane = lax.broadcasted_iota(jnp.int32, (tq, LANES), 1)
    aq = jnp.where((lane >> BIAS_SHIFT) == h, aq_ref[...], 0.0)
    qa = jnp.concatenate([(q_ref[...] * QSCALE).T, aq.T], axis=0).astype(BF16)
    qpos = qi * tq + lax.broadcasted_iota(jnp.int32, (tk, tq), 1)
    o_ref[...] = _flash_t_causal(ka_scr, vt_scr, qa, acc_scr, qi * (tq // tk), tq // tk, tk, qpos,
                                 unroll=tq // tk).T


def _fox_prompt(fq, fk, fvt, aug_q, aug_k, B, S, *, tq, tk):
    nq = S // tq
    return pl.pallas_call(
        functools.partial(_fox_kernel, tq, tk),
        out_shape=jax.ShapeDtypeStruct(fq.shape, F32),
        grid=(B, H_FOX, nq),
        in_specs=[
            pl.BlockSpec((tq, HEAD_DIM), lambda b, h, qi: (b * nq + qi, h)),
            pl.BlockSpec((S, HEAD_DIM), lambda b, h, qi: (b, h)),
            pl.BlockSpec((1, 1, HEAD_DIM, S), lambda b, h, qi: (b, h, 0, 0)),
            pl.BlockSpec((tq, LANES), lambda b, h, qi: (b * nq + qi, 0)),
            pl.BlockSpec((S, LANES), lambda b, h, qi: (b, 0)),
        ],
        out_specs=pl.BlockSpec((tq, HEAD_DIM), lambda b, h, qi: (b * nq + qi, h)),
        scratch_shapes=[pltpu.VMEM((S // tk, tk, 2 * HEAD_DIM), BF16), pltpu.VMEM((S // tk, HEAD_DIM, tk), BF16),
                        pltpu.VMEM((HEAD_DIM, tq), F32)],
        compiler_params=_cparams(("arbitrary",) * 3),
        name="fox_prompt",
    )(fq, fk, fvt, aug_q, aug_k)


def _compress_kernel(nc, x_ref, pea_ref, peb_ref, w1a_ref, w1b_ref, w2_ref, o_ref):
    R = jnp.concatenate([x_ref[0, pl.ds(r, nc, stride=CMP_STRIDE), :] for r in range(CMP_STRIDE)], axis=1)
    p1 = _dot((R + pea_ref[...]).astype(BF16), w1a_ref[...])
    p2 = _dot((R + peb_ref[...]).astype(BF16), w1b_ref[...])
    pre = p1 + pltpu.roll(p2, nc - 1, 0)
    kc = _dot(_silu(pre).astype(BF16), w2_ref[...])
    row = lax.broadcasted_iota(jnp.int32, kc.shape, 0)
    o_ref[0, 0] = jnp.where(row < nc - 1, kc, 0.0)


def _cmp_weights(pe, w1, w2):
    half = CMP_STRIDE * HEAD_DIM
    return (pe[:CMP_STRIDE].reshape(1, half), pe[CMP_STRIDE:].reshape(1, half),
            w1[:half].astype(BF16), w1[half:].astype(BF16), w2.astype(BF16))


def _compress_prompt(x, cw, B, S):
    nc = S // CMP_STRIDE
    half = CMP_STRIDE * HEAD_DIM
    const = lambda shape: pl.BlockSpec(shape, lambda b, g: (0, 0))
    return pl.pallas_call(
        functools.partial(_compress_kernel, nc),
        out_shape=jax.ShapeDtypeStruct((B, NSA_GROUPS, nc, HEAD_DIM), F32),
        grid=(B, NSA_GROUPS),
        in_specs=[pl.BlockSpec((1, S, HEAD_DIM), lambda b, g: (b, 0, g)),
                  const((1, half)), const((1, half)), const((half, HEAD_DIM)), const((half, HEAD_DIM)),
                  const((HEAD_DIM, HEAD_DIM))],
        out_specs=pl.BlockSpec((1, 1, nc, HEAD_DIM), lambda b, g: (b, g, 0, 0)),
        compiler_params=_cparams(("arbitrary", "arbitrary")),
        name="nsa_compress",
    )(x, *cw)


def _overlap_t(n_cmp, n_sel):
    cs = np.arange(n_cmp)[None, :] * CMP_STRIDE
    ss = np.arange(n_sel)[:, None] * SEL_LEN
    ov = np.minimum(cs + CMP_LEN, ss + SEL_LEN) - np.maximum(cs, ss)
    return np.maximum(ov, 0).astype(np.float32) / CMP_LEN


def _group_rows(ref, Tq):
    return jnp.concatenate([ref[0, :, r * HEAD_DIM:(r + 1) * HEAD_DIM] for r in range(NSA_R)], axis=0)


def _store_gated(o_ref, o, gates, Tq, branch):
    for r in range(NSA_R):
        gcol = gates[:, 3 * r + branch:3 * r + branch + 1]
        o_ref[0, :, r * HEAD_DIM:(r + 1) * HEAD_DIM] = o[r * Tq:(r + 1) * Tq] * gcol


def _cmpsel_kernel(Tq, nsel, q_ref, kc_ref, vc_ref, ovt_ref, gate_ref, o_ref, sel_ref):
    qi = pl.program_id(2)
    q = (_group_rows(q_ref, Tq) * (HEAD_DIM ** -0.5)).astype(BF16)
    s = _dot_nt(q, kc_ref[0, 0].astype(BF16))
    qpos = qi * Tq + (lax.broadcasted_iota(jnp.int32, s.shape, 0) & (Tq - 1))
    cend = lax.broadcasted_iota(jnp.int32, s.shape, 1) * CMP_STRIDE + (CMP_LEN - 1)
    mask = cend <= qpos
    sm = jnp.where(mask, s, NEG)
    e = jnp.where(mask, jnp.exp(sm - jnp.max(sm, axis=-1, keepdims=True)), 0.0)
    den = jnp.sum(e, axis=-1, keepdims=True)
    p = e / jnp.where(den > 0.0, den, 1.0)
    o = _dot(p.astype(BF16), vc_ref[0, 0].astype(BF16))
    _store_gated(o_ref, o, gate_ref[0, 0], Tq, 0)

    psum = p[0:Tq]
    for r in range(1, NSA_R):
        psum = psum + p[r * Tq:(r + 1) * Tq]
    ovt = ovt_ref[...]
    sc = sum(_dot_nt(ovt, t) for t in _split3(psum))
    n = lax.broadcasted_iota(jnp.int32, sc.shape, 0)
    tpos = qi * Tq + lax.broadcasted_iota(jnp.int32, sc.shape, 1)
    cur = tpos >> SEL_SHIFT
    forced = (n == 0) | (n == cur) | (n == cur - 1)
    sc = jnp.where(n * SEL_LEN <= tpos, jnp.where(forced, -NEG, sc), NEG)
    ranks = []
    for v in range(nsel // 8):
        blk = sc[8 * v:8 * v + 8]
        nv = 8 * v + lax.broadcasted_iota(jnp.int32, blk.shape, 0)
        r = jnp.zeros(blk.shape, F32)
        for m in range(nsel):
            rowm = sc[m:m + 1, :]
            ge, gt = jnp.where(rowm >= blk, 1.0, 0.0), jnp.where(rowm > blk, 1.0, 0.0)
            if m < 8 * v:
                r = r + ge
            elif m > 8 * v + 7:
                r = r + gt
            else:
                r = r + jnp.where(nv > m, ge, gt)
        ranks.append(r)
    rank = jnp.concatenate(ranks, axis=0)
    selneg = jnp.where(rank < TOPN, 0.0, NEG)
    if nsel < LANES:
        selneg = jnp.concatenate([selneg, jnp.full((LANES - nsel, Tq), NEG, F32)], axis=0)
    sel_ref[0, 0, 0] = selneg.astype(BF16)


def _cmpsel_prompt(qn, kc, vc, gates_g, B, S, *, Tq):
    nq = S // Tq
    ncp = kc.shape[2]
    nsel = S // SEL_LEN
    ovt = jnp.asarray(_overlap_t(ncp, nsel), BF16)
    return pl.pallas_call(
        functools.partial(_cmpsel_kernel, Tq, nsel),
        out_shape=[jax.ShapeDtypeStruct((B, S, H_NSA * HEAD_DIM), F32),
                   jax.ShapeDtypeStruct((B, NSA_GROUPS, nq, LANES, Tq), BF16)],
        grid=(B, NSA_GROUPS, nq),
        in_specs=[pl.BlockSpec((1, Tq, NSA_R * HEAD_DIM), lambda b, g, qi: (b, qi, g)),
                  pl.BlockSpec((1, 1, ncp, HEAD_DIM), lambda b, g, qi: (b, g, 0, 0)),
                  pl.BlockSpec((1, 1, ncp, HEAD_DIM), lambda b, g, qi: (b, g, 0, 0)),
                  pl.BlockSpec((nsel, ncp), lambda b, g, qi: (0, 0)),
                  pl.BlockSpec((1, 1, Tq, 16), lambda b, g, qi: (b, g, qi, 0))],
        out_specs=[pl.BlockSpec((1, Tq, NSA_R * HEAD_DIM), lambda b, g, qi: (b, qi, g)),
                   pl.BlockSpec((1, 1, 1, LANES, Tq), lambda b, g, qi: (b, g, qi, 0, 0))],
        compiler_params=_cparams(("arbitrary",) * 3),
        name="nsa_cmpsel",
    )(qn, kc, vc, ovt, gates_g)


def _queries_t(q_ref, Tq):
    return jnp.concatenate([(q_ref[0, :, r * HEAD_DIM:(r + 1) * HEAD_DIM] * QSCALE).T for r in range(NSA_R)],
                           axis=1).astype(BF16)


def _store_gated_t(o_ref, oT, gates, Tq, branch):
    for r in range(NSA_R):
        gcol = gates[:, 3 * r + branch:3 * r + branch + 1]
        o_ref[0, :, r * HEAD_DIM:(r + 1) * HEAD_DIM] = oT[:, r * Tq:(r + 1) * Tq].T * gcol


def _sel_kernel(Tq, tk, q_ref, sn_ref, k_ref, vt_ref, e_ref, gate_ref, o_ref, ka_scr, vt_scr, acc_scr):
    qi = pl.program_id(2)

    @pl.when(qi == 0)
    def _():
        for c in range(ka_scr.shape[0]):
            rows = slice(c * tk, (c + 1) * tk)
            ka_scr[c] = jnp.concatenate([k_ref[0, rows, :].astype(BF16), e_ref[rows, :]], axis=1)
            vt_scr[c] = vt_ref[0, 0, :, rows].astype(BF16)

    sn = sn_ref[0, 0, 0]
    qa = jnp.concatenate([_queries_t(q_ref, Tq), jnp.concatenate([sn] * NSA_R, axis=1)], axis=0)
    R = NSA_R * Tq
    qpos = qi * Tq + (lax.broadcasted_iota(jnp.int32, (tk, R), 1) & (Tq - 1))
    oT = _flash_t_causal(ka_scr, vt_scr, qa, acc_scr, (qi * Tq) // tk, 1, tk, qpos)
    _store_gated_t(o_ref, oT, gate_ref[0, 0], Tq, 1)


def _sel_prompt(qr, selneg, sk, svt, gates_g, B, S, *, Tq, tk):
    nq = S // Tq
    key = np.arange(S)[:, None] // SEL_LEN
    e_all = jnp.asarray((key == np.arange(LANES)[None, :]).astype(np.float32), BF16)
    R = NSA_R * Tq
    return pl.pallas_call(
        functools.partial(_sel_kernel, Tq, tk),
        out_shape=jax.ShapeDtypeStruct((B, S, H_NSA * HEAD_DIM), F32),
        grid=(B, NSA_GROUPS, nq),
        in_specs=[pl.BlockSpec((1, Tq, NSA_R * HEAD_DIM), lambda b, g, qi: (b, qi, g)),
                  pl.BlockSpec((1, 1, 1, LANES, Tq), lambda b, g, qi: (b, g, qi, 0, 0)),
                  pl.BlockSpec((1, S, HEAD_DIM), lambda b, g, qi: (b, 0, g)),
                  pl.BlockSpec((1, 1, HEAD_DIM, S), lambda b, g, qi: (b, g, 0, 0)),
                  pl.BlockSpec((S, LANES), lambda b, g, qi: (0, 0)),
                  pl.BlockSpec((1, 1, Tq, 16), lambda b, g, qi: (b, g, qi, 0))],
        out_specs=pl.BlockSpec((1, Tq, NSA_R * HEAD_DIM), lambda b, g, qi: (b, qi, g)),
        scratch_shapes=[pltpu.VMEM((S // tk, tk, 2 * HEAD_DIM), BF16), pltpu.VMEM((S // tk, HEAD_DIM, tk), BF16),
                        pltpu.VMEM((HEAD_DIM, R), F32)],
        compiler_params=_cparams(("arbitrary",) * 3),
        name="nsa_sel",
    )(qr, selneg, sk, svt, e_all, gates_g)


def _win_kernel(Tq, nwb, q_ref, *refs):
    k_refs, vt_refs = refs[:nwb], refs[nwb:2 * nwb]
    gate_ref, o_ref = refs[2 * nwb:]
    qi = pl.program_id(2)
    R = NSA_R * Tq
    qT = _queries_t(q_ref, Tq)
    qpos = qi * Tq + (lax.broadcasted_iota(jnp.int32, (Tq, R), 1) & (Tq - 1))
    krow = lax.broadcasted_iota(jnp.int32, (Tq, R), 0)
    s = []
    for i in range(nwb):
        kb = qi - (nwb - 1) + i
        si = _dot(k_refs[i][0].astype(BF16), qT)
        if i == 0:
            si = jnp.where(qpos - (kb * Tq + krow) < WINDOW, si, NEG)
        if i == nwb - 1:
            si = jnp.where(kb * Tq + krow <= qpos, si, NEG)
        else:
            si = si + jnp.where(kb >= 0, 0.0, NEG)
        s.append(si)
    m = functools.reduce(jnp.maximum, [jnp.max(si, axis=0, keepdims=True) for si in s])
    p = [jnp.exp2(si - m) for si in s]
    l = sum(jnp.sum(pi, axis=0, keepdims=True) for pi in p)
    oT = sum(_dot(vt_refs[i][0, 0].astype(BF16), p[i].astype(BF16)) for i in range(nwb)) / l
    _store_gated_t(o_ref, oT, gate_ref[0, 0], Tq, 2)


def _win_prompt(qr, wk, wvt, gates_g, B, S, *, Tq):
    nq = S // Tq
    nwb = WINDOW // Tq + 1
    blk = lambda i: (lambda qi: jnp.maximum(qi - (nwb - 1) + i, 0))
    kspec = lambda i: pl.BlockSpec((1, Tq, HEAD_DIM), lambda b, g, qi, f=blk(i): (b, f(qi), g))
    vspec = lambda i: pl.BlockSpec((1, 1, HEAD_DIM, Tq), lambda b, g, qi, f=blk(i): (b, g, 0, f(qi)))
    return pl.pallas_call(
        functools.partial(_win_kernel, Tq, nwb),
        out_shape=jax.ShapeDtypeStruct((B, S, H_NSA * HEAD_DIM), F32),
        grid=(B, NSA_GROUPS, nq),
        in_specs=[pl.BlockSpec((1, Tq, NSA_R * HEAD_DIM), lambda b, g, qi: (b, qi, g))]
        + [kspec(i) for i in range(nwb)] + [vspec(i) for i in range(nwb)]
        + [pl.BlockSpec((1, 1, Tq, 16), lambda b, g, qi: (b, g, qi, 0))],
        out_specs=pl.BlockSpec((1, Tq, NSA_R * HEAD_DIM), lambda b, g, qi: (b, qi, g)),
        compiler_params=_cparams(("arbitrary",) * 3),
        name="nsa_win",
    )(qr, *([wk] * nwb), *([wvt] * nwb), gates_g)


def _hgrn_step_kernel(aq_ref, af_ref, ai_ref, ag_ref, lb_ref, gn_ref, s_ref, o_ref, so_ref):
    for h in range(s_ref.shape[1]):
        lb = lb_ref[h]
        gate = lb + (1.0 - lb) * _sigmoid(af_ref[0, h])
        s_new = s_ref[0, h] * gate + (1.0 - gate) * ai_ref[0, h]
        so_ref[0, h] = s_new
        o = jnp.sum(_silu(aq_ref[0, h]) * s_new, axis=0, keepdims=True)
        o_ref[0, h] = _head_rms(o, gn_ref[...]) * _silu(ag_ref[0, h])


def _hgrn_step(hg, lb, gn, state):
    D, H = state.shape[:2]
    W = H * HEAD_DIM
    colv = lambda a: a.reshape(D, H, HEAD_DIM, 1)
    rowv = lambda a: a.reshape(D, H, 1, HEAD_DIM)
    cspec = pl.BlockSpec((1, H, HEAD_DIM, 1), lambda d: (d, 0, 0, 0))
    rspec = pl.BlockSpec((1, H, 1, HEAD_DIM), lambda d: (d, 0, 0, 0))
    sspec = pl.BlockSpec((1, H, HEAD_DIM, HEAD_DIM), lambda d: (d, 0, 0, 0))
    o, s_new = pl.pallas_call(
        _hgrn_step_kernel,
        out_shape=[jax.ShapeDtypeStruct((D, H, 1, HEAD_DIM), F32), jax.ShapeDtypeStruct(state.shape, F32)],
        grid=(D,),
        in_specs=[cspec, cspec, rspec, rspec,
                  pl.BlockSpec((H, HEAD_DIM, 1), lambda d: (0, 0, 0)),
                  pl.BlockSpec((1, HEAD_DIM), lambda d: (0, 0)), sspec],
        out_specs=[rspec, sspec],
        compiler_params=_cparams(("arbitrary",)),
        name="hgrn_step",
    )(colv(hg[:, :W]), colv(hg[:, W:2 * W]), rowv(hg[:, 2 * W:3 * W]), rowv(hg[:, 3 * W:]),
      lb.reshape(H, HEAD_DIM, 1), gn, state)
    return o.reshape(D, W), s_new


def _fox_step_kernel(PG, pt_ref, q_ref, kn_ref, vn_ref, lfn_ref, *refs):
    k_refs, v_refs, lf_refs = refs[:PG], refs[PG:2 * PG], refs[2 * PG:3 * PG]
    o_ref, m_scr, l_scr, acc_scr, carry = refs[3 * PG:]
    W = H_FOX * PAGE
    q = q_ref[0] * (HEAD_DIM ** -0.5)

    @pl.when(pl.program_id(1) == 0)
    def _():
        m_scr[...] = jnp.sum(q * kn_ref[0], axis=-1, keepdims=True)
        l_scr[...] = jnp.ones_like(l_scr)
        acc_scr[...] = vn_ref[0]
        carry[...] = jnp.broadcast_to(lfn_ref[0], carry.shape)

    qb = q.astype(BF16)
    lane = lax.broadcasted_iota(jnp.int32, (H_FOX, W), 1)
    head = lax.broadcasted_iota(jnp.int32, (H_FOX, W), 0)
    plane = lax.broadcasted_iota(jnp.int32, (PG, W), 1)
    lf = jnp.concatenate([lf_refs[i][0] for i in range(PG)], axis=0)
    suf, tot = lf, lf
    step = H_FOX
    while step < W:
        suf = suf + jnp.where(plane + step < W, pltpu.roll(suf, W - step, 1), 0.0)
        tot = tot + pltpu.roll(tot, step, 1)
        step *= 2
    c = carry[...]
    s = []
    for i in range(PG):
        si = _dot_nt(qb, k_refs[i][0].astype(BF16)) + (c + (suf[i:i + 1] - lf[i:i + 1]))
        s.append(jnp.where((lane & (H_FOX - 1)) == head, si, NEG))
        c = c + tot[i:i + 1]
    carry[...] = c
    m_prev = m_scr[...]
    m_new = functools.reduce(jnp.maximum, [jnp.max(si, axis=-1, keepdims=True) for si in s] + [m_prev])
    alpha = jnp.exp(m_prev - m_new)
    p = [jnp.exp(si - m_new) for si in s]
    l_scr[...] = alpha * l_scr[...] + sum(jnp.sum(pi, axis=-1, keepdims=True) for pi in p)
    acc_scr[...] = alpha * acc_scr[...] + sum(_dot(p[i].astype(BF16), v_refs[i][0].astype(BF16)) for i in range(PG))
    m_scr[...] = m_new

    @pl.when(pl.program_id(1) == pl.num_programs(1) - 1)
    def _():
        o_ref[0] = _online_result(l_scr, acc_scr)


def _fox_step(page_table, q, k_new, v_new, lf_new, cache_k, cache_v, cache_lf, *, PG):
    D, NP = page_table.shape
    n_pool = cache_k.shape[0]
    W = H_FOX * PAGE
    k2 = cache_k.reshape(n_pool, W, HEAD_DIM)
    v2 = cache_v.reshape(n_pool, W, HEAD_DIM)
    lf2 = cache_lf.astype(F32).reshape(n_pool, 1, W)
    lfn = jnp.tile(lf_new, (1, PAGE)).reshape(D, 1, W)
    page = lambda i: (lambda d, j, pt: (pt[d, NP - 1 - (j * PG + i)], 0, 0))
    hspec = pl.BlockSpec((1, H_FOX, HEAD_DIM), lambda d, j, pt: (d, 0, 0))
    in_specs = [hspec, hspec, hspec, pl.BlockSpec((1, 1, W), lambda d, j, pt: (d, 0, 0))]
    in_specs += [pl.BlockSpec((1, W, HEAD_DIM), page(i)) for i in range(PG)]
    in_specs += [pl.BlockSpec((1, W, HEAD_DIM), page(i)) for i in range(PG)]
    in_specs += [pl.BlockSpec((1, 1, W), page(i)) for i in range(PG)]
    return pl.pallas_call(
        functools.partial(_fox_step_kernel, PG),
        out_shape=jax.ShapeDtypeStruct((D, H_FOX, HEAD_DIM), F32),
        grid_spec=pltpu.PrefetchScalarGridSpec(
            num_scalar_prefetch=1, grid=(D, NP // PG), in_specs=in_specs, out_specs=hspec,
            scratch_shapes=[pltpu.VMEM((H_FOX, 1), F32), pltpu.VMEM((H_FOX, 1), F32),
                            pltpu.VMEM((H_FOX, HEAD_DIM), F32), pltpu.VMEM((H_FOX, W), F32)]),
        compiler_params=_cparams(("arbitrary", "arbitrary")),
        name="fox_step",
    )(page_table, q, k_new, v_new, lfn, *([k2] * PG), *([v2] * PG), *([lf2] * PG))


def _cmp_pages_kernel(PG, pt_ref, *refs):
    pages = refs[:PG]
    pea_ref, peb_ref, w1a_ref, w1b_ref, p1_ref, p2_ref, r_scr = refs[PG:]
    per_page = PAGE // CMP_STRIDE
    G = NSA_GROUPS
    for i in range(PG):
        for m in range(per_page):
            dst = (i * per_page + m) * G
            for r in range(CMP_STRIDE):
                src = (m * CMP_STRIDE + r) * G
                r_scr[dst:dst + G, r * HEAD_DIM:(r + 1) * HEAD_DIM] = pages[i][0, src:src + G, :]
    R = r_scr[...]
    p1_ref[0] = _dot((R + pea_ref[...]).astype(BF16), w1a_ref[...])
    p2_ref[0] = _dot((R + peb_ref[...]).astype(BF16), w1b_ref[...])


def _cmp_pages(page_table, cache, cw, *, PG):
    D, NP = page_table.shape
    n_pool = cache.shape[0]
    c2 = cache.reshape(n_pool, PAGE * NSA_GROUPS, HEAD_DIM)
    rows = PG * (PAGE // CMP_STRIDE) * NSA_GROUPS
    half = CMP_STRIDE * HEAD_DIM
    page = lambda i: (lambda d, j, pt: (pt[d, j * PG + i], 0, 0))
    const = lambda shape: pl.BlockSpec(shape, lambda d, j, pt: (0, 0))
    in_specs = [pl.BlockSpec((1, PAGE * NSA_GROUPS, HEAD_DIM), page(i)) for i in range(PG)]
    in_specs += [const((1, half)), const((1, half)), const((half, HEAD_DIM)), const((half, HEAD_DIM))]
    ospec = pl.BlockSpec((1, rows, HEAD_DIM), lambda d, j, pt: (d, j, 0))
    shp = jax.ShapeDtypeStruct((D, (NP // PG) * rows, HEAD_DIM), F32)
    return pl.pallas_call(
        functools.partial(_cmp_pages_kernel, PG),
        out_shape=[shp, shp],
        grid_spec=pltpu.PrefetchScalarGridSpec(
            num_scalar_prefetch=1, grid=(D, NP // PG), in_specs=in_specs, out_specs=[ospec, ospec],
            scratch_shapes=[pltpu.VMEM((rows, half), F32)]),
        compiler_params=_cparams(("arbitrary", "arbitrary")),
        name="nsa_cmp_pages",
    )(page_table, *([c2] * PG), *cw[:4])


def _cmpsel_step_kernel(qpos, nsp, q_ref, p1k_ref, p2k_ref, p1v_ref, p2v_ref, w2k_ref, w2v_ref,
                        ov_ref, g_ref, o_ref, idx_ref):
    nc = p1k_ref.shape[1] // NSA_GROUPS
    rows = pl.ds(pl.program_id(1), nc, stride=NSA_GROUPS)

    def finish(p1_ref, p2_ref, w2_ref):
        pre = p1_ref[0, rows, :] + pltpu.roll(p2_ref[0, rows, :], nc - 1, 0)
        return _dot(_silu(pre).astype(BF16), w2_ref[...]).astype(BF16)

    kc = finish(p1k_ref, p2k_ref, w2k_ref)
    vc = finish(p1v_ref, p2v_ref, w2v_ref)
    q = (q_ref[0, 0] * (HEAD_DIM ** -0.5)).astype(BF16)
    s = _dot_nt(q, kc)
    cend = lax.broadcasted_iota(jnp.int32, s.shape, 1) * CMP_STRIDE + (CMP_LEN - 1)
    mask = cend <= qpos
    sm = jnp.where(mask, s, NEG)
    e = jnp.where(mask, jnp.exp(sm - jnp.max(sm, axis=-1, keepdims=True)), 0.0)
    den = jnp.sum(e, axis=-1, keepdims=True)
    p = e / jnp.where(den > 0.0, den, 1.0)
    o_ref[0, 0] = _dot(p.astype(BF16), vc) * g_ref[0, 0][:, 0:1]

    psum = jnp.broadcast_to(jnp.sum(p[0:NSA_R], axis=0, keepdims=True), (8, nc))
    ov = ov_ref[...]
    sc_row = sum(_dot(t, ov) for t in _split3(psum))[0:1]
    n_lane = lax.broadcasted_iota(jnp.int32, (1, nsp), 1)
    cur = qpos // SEL_LEN
    forced = (n_lane == 0) | (n_lane == cur) | (n_lane == cur - 1)
    sc_row = jnp.where(n_lane * SEL_LEN <= qpos, jnp.where(forced, -NEG, sc_row), NEG)
    mi = lax.broadcasted_iota(jnp.int32, (nsp, nsp), 0)
    ni = lax.broadcasted_iota(jnp.int32, (nsp, nsp), 1)
    sc_col = jnp.sum(jnp.where(mi == ni, sc_row, 0.0), axis=-1, keepdims=True)
    beats = jnp.where(mi < ni, jnp.where(sc_col >= sc_row, 1.0, 0.0), jnp.where(sc_col > sc_row, 1.0, 0.0))
    rank = jnp.sum(beats, axis=0, keepdims=True)
    lane = lax.broadcasted_iota(jnp.int32, (1, LANES), 1)
    out = jnp.zeros((1, LANES), F32)
    for k in range(TOPN):
        nk = jnp.sum(jnp.where(rank == float(k), n_lane.astype(F32), 0.0), axis=-1, keepdims=True)
        out = jnp.where(lane == k, nk, out)
    idx_ref[0, 0] = jnp.broadcast_to(out, (8, LANES)).astype(jnp.int32)


def _cmpsel_step(q16, p1k, p2k, p1v, p2v, cwk, cwv, gate_rows, qpos):
    D, G = q16.shape[:2]
    nc = p1k.shape[1] // G
    n_sel = -(-(qpos + 1) // SEL_LEN)
    nsp = -(-n_sel // LANES) * LANES
    ov = jnp.asarray(np.pad(_overlap_t(nc, n_sel).T, ((0, 0), (0, nsp - n_sel))), BF16)
    big = pl.BlockSpec((1, nc * G, HEAD_DIM), lambda d, g: (d, 0, 0))
    qspec = pl.BlockSpec((1, 1, 16, HEAD_DIM), lambda d, g: (d, g, 0, 0))
    w2spec = pl.BlockSpec((HEAD_DIM, HEAD_DIM), lambda d, g: (0, 0))
    return pl.pallas_call(
        functools.partial(_cmpsel_step_kernel, qpos, nsp),
        out_shape=[jax.ShapeDtypeStruct((D, G, 16, HEAD_DIM), F32),
                   jax.ShapeDtypeStruct((D, G, 8, LANES), jnp.int32)],
        grid=(D, G),
        in_specs=[qspec, big, big, big, big, w2spec, w2spec,
                  pl.BlockSpec((nc, nsp), lambda d, g: (0, 0)),
                  pl.BlockSpec((1, 1, 16, 3), lambda d, g: (d, g, 0, 0))],
        out_specs=[qspec, pl.BlockSpec((1, 1, 8, LANES), lambda d, g: (d, g, 0, 0))],
        compiler_params=_cparams(("arbitrary", "arbitrary")),
        name="nsa_cmpsel_step",
    )(q16, p1k, p2k, p1v, p2v, cwk[4], cwv[4], ov, gate_rows)


def _sel_step_kernel(n_past, pt_ref, idx_ref, q_ref, kn_ref, vn_ref, *refs):
    k_refs, v_refs = refs[:TOPN], refs[TOPN:2 * TOPN]
    g_ref, o_ref = refs[2 * TOPN:]
    d, g = pl.program_id(0), pl.program_id(1)
    q = q_ref[0, 0] * (HEAD_DIM ** -0.5)
    qb = q.astype(BF16)
    rows = pl.ds(g, SEL_LEN, stride=NSA_GROUPS)
    s_new = jnp.sum(q * kn_ref[0, 0], axis=-1, keepdims=True)
    s = [_dot_nt(qb, k_refs[k][rows, :].astype(BF16)) + jnp.where(idx_ref[d, g, k] < n_past, 0.0, NEG)
         for k in range(TOPN)]
    m = functools.reduce(jnp.maximum, [jnp.max(sk, axis=-1, keepdims=True) for sk in s] + [s_new])
    p = [jnp.exp(sk - m) for sk in s]
    p_new = jnp.exp(s_new - m)
    l = sum(jnp.sum(pk, axis=-1, keepdims=True) for pk in p) + p_new
    o = sum(_dot(p[k].astype(BF16), v_refs[k][rows, :].astype(BF16)) for k in range(TOPN)) + p_new * vn_ref[0, 0]
    o_ref[0, 0] = o / l * g_ref[0, 0][:, 1:2]


def _sel_step(page_table, idx, q16, k_new, v_new, cache_k, cache_v, gate_rows):
    D, NP = page_table.shape
    G = NSA_GROUPS
    n_pool = cache_k.shape[0]
    per_page = PAGE // SEL_LEN
    n_past = NP * per_page
    blk_rows = SEL_LEN * G
    k2 = cache_k.reshape(n_pool * PAGE * G, HEAD_DIM)
    v2 = cache_v.reshape(n_pool * PAGE * G, HEAD_DIM)

    def blk(k):
        def index(d, g, pt, ix):
            n = jnp.minimum(ix[d, g, k], n_past - 1)
            return (pt[d, n // per_page] * per_page + n % per_page, 0)
        return pl.BlockSpec((blk_rows, HEAD_DIM), index)

    qspec = pl.BlockSpec((1, 1, 16, HEAD_DIM), lambda d, g, pt, ix: (d, g, 0, 0))
    nspec = pl.BlockSpec((1, 1, 1, HEAD_DIM), lambda d, g, pt, ix: (d, g, 0, 0))
    return pl.pallas_call(
        functools.partial(_sel_step_kernel, n_past),
        out_shape=jax.ShapeDtypeStruct((D, G, 16, HEAD_DIM), F32),
        grid_spec=pltpu.PrefetchScalarGridSpec(
            num_scalar_prefetch=2, grid=(D, G),
            in_specs=[qspec, nspec, nspec] + [blk(k) for k in range(TOPN)] * 2
            + [pl.BlockSpec((1, 1, 16, 3), lambda d, g, pt, ix: (d, g, 0, 0))],
            out_specs=qspec),
        compiler_params=_cparams(("arbitrary",) * 2),
        name="nsa_sel_step",
    )(page_table, idx, q16, k_new, v_new, *([k2] * TOPN), *([v2] * TOPN), gate_rows)


def _win_step_kernel(q_ref, k_ref, v_ref, g_ref, o_ref):
    q = (q_ref[0, 0] * (HEAD_DIM ** -0.5)).astype(BF16)
    s = _dot_nt(q, k_ref[0].astype(BF16))
    e = jnp.exp(s - jnp.max(s, axis=-1, keepdims=True))
    p = e / jnp.sum(e, axis=-1, keepdims=True)
    o_ref[0, 0] = _dot(p.astype(BF16), v_ref[0].astype(BF16)) * g_ref[0, 0][:, 2:3]


def _win_step(q16, kw, vw, gate_rows):
    D, G = q16.shape[:2]
    L = kw.shape[1]
    qspec = pl.BlockSpec((1, 1, 16, HEAD_DIM), lambda d, g: (d, g, 0, 0))
    kspec = pl.BlockSpec((1, L, HEAD_DIM), lambda d, g: (d, 0, g))
    return pl.pallas_call(
        _win_step_kernel,
        out_shape=jax.ShapeDtypeStruct((D, G, 16, HEAD_DIM), F32),
        grid=(D, G),
        in_specs=[qspec, kspec, kspec, pl.BlockSpec((1, 1, 16, 3), lambda d, g: (d, g, 0, 0))],
        out_specs=qspec,
        compiler_params=_cparams(("arbitrary", "arbitrary")),
        name="nsa_win_step",
    )(q16, kw, vw, gate_rows)


TN = 512
TM = 512
TM_STEP = 8


def _pad_cols(a, n):
    return jnp.pad(a, ((0, 0), (0, n - a.shape[1])))


TN_PROJ = 1024


def _segments(widths_modes):
    segs, start = [], 0
    for width, mode in widths_modes:
        segs.append((start, width, mode))
        start += width
    return segs, start


def _proj_weights(w_stack, layer, n_main):
    wb = w_stack[layer].astype(BF16)
    return wb, _pad_cols(wb[:, n_main:], LANES)


def _even_params(e_norm_mix, e_w_in, layer, lb, out_norm, f_bias, q_norm, k_norm, e_w_out, e_norm_ffn):
    d = e_w_in.shape[1]
    aw, bw = H_HGRN * HEAD_DIM, H_FOX * HEAD_DIM
    segs, n_main = _segments(((4 * aw, 'raw'), (bw, 'hnorm'), (bw, 'hnorm'), (bw, 'raw')))
    gain = jnp.concatenate([jnp.ones((4 * aw,), F32), jnp.tile(q_norm, H_FOX), jnp.tile(k_norm, H_FOX),
                            jnp.ones((bw,), F32)])
    w_in, w_tail = _proj_weights(e_w_in, layer, n_main)
    return dict(norm=e_norm_mix.reshape(1, d), w_in=w_in, w_tail=w_tail, gain=gain.reshape(1, -1),
                bias=_pad_cols(f_bias.astype(F32).reshape(1, -1), LANES), segs=segs, lb=lb.reshape(1, aw),
                out_norm=out_norm.reshape(1, HEAD_DIM), w_out=e_w_out.astype(BF16), norm_ffn=e_norm_ffn.reshape(1, d))


def _even_proj(x2, p, tm):
    zeros = jnp.zeros((tm, HEAD_DIM), F32)
    return _proj(x2, p['norm'], p['w_in'], p['w_tail'], p['gain'], p['bias'], zeros, zeros, p['segs'], 'logsig',
                 tm=tm, tn=TN_PROJ, name="even_proj")


def _even_finish(x2, oa, of, p, ffn_w, tm, emit=False):
    x2 = _outproj([[oa], [of]], p['w_out'], x2, tm=tm, tn=TN_PROJ, name="even_out")
    return _ffn(x2, p['norm_ffn'], *ffn_w, tm=tm, tf=TN, emit=emit, name="ffn")


def _odd_params(o_norm_mix, o_w_in, layer, q_norm, ck_norm, sk_norm, wk_norm, pe_k, w1_k, w2_k, pe_v, w1_v, w2_v,
                o_w_out, o_norm_ffn, router_w, router_b):
    d = o_w_in.shape[1]
    qw, kvw = H_NSA * HEAD_DIM, NSA_GROUPS * HEAD_DIM
    segs, n_main = _segments(((qw, 'hnorm_both'), (kvw, 'hnorm'), (kvw, 'raw'), (kvw, 'hnorm_rope'), (kvw, 'raw'),
                              (kvw, 'hnorm_rope'), (kvw, 'raw')))
    ones = jnp.ones((kvw,), F32)
    gain = jnp.concatenate([jnp.tile(q_norm, H_NSA), jnp.tile(ck_norm, NSA_GROUPS), ones,
                            jnp.tile(sk_norm, NSA_GROUPS), ones, jnp.tile(wk_norm, NSA_GROUPS), ones])
    w_in, w_tail = _proj_weights(o_w_in, layer, n_main)
    return dict(norm=o_norm_mix.reshape(1, d), w_in=w_in, w_tail=w_tail, gain=gain.reshape(1, -1),
                bias=jnp.zeros((1, LANES), F32), segs=segs,
                cwk=_cmp_weights(pe_k, w1_k, w2_k), cwv=_cmp_weights(pe_v, w1_v, w2_v),
                w_out=o_w_out.astype(BF16), norm_ffn=o_norm_ffn.reshape(1, d),
                rw=_pad_cols(router_w.astype(BF16), LANES), rb=_pad_cols(router_b.reshape(1, -1).astype(F32), LANES))


def _odd_proj(x2, p, pos, tm):
    cosf, sinf = _rope_tables(pos)
    return _proj(x2, p['norm'], p['w_in'], p['w_tail'], p['gain'], p['bias'], cosf, sinf, p['segs'], 'sigmoid',
                 tm=tm, tn=TN_PROJ, name="odd_proj")


def _odd_finish(x2, branches, p, moe_w, tm, emit=False):
    x2 = _outproj([branches], p['w_out'], x2, tm=tm, tn=TN_PROJ, name="odd_out")
    args = (x2, p['norm_ffn'], p['rw'], p['rb'], *moe_w)
    if emit:
        return _moe(*args, tm=tm, tf=256, emit=True, name="moe")
    return _moe_routed(*args, tm=tm)


def kernel(x_prompt, x_sample, state_hgrn, cache_fox_k, cache_fox_v, cache_fox_logf, cache_nsa_cmp_k, cache_nsa_cmp_v, cache_nsa_sel_k, cache_nsa_sel_v, cache_nsa_win_k, cache_nsa_win_v, page_table, e_norm_mix, e_w_in, hgrn_lb_logits, hgrn_out_norm, fox_f_bias, fox_q_norm, fox_k_norm, e_w_out, e_norm_ffn, ffn_w_gate, ffn_w_up, ffn_w_down, o_norm_mix, o_w_in, nsa_q_norm, nsa_cmp_k_norm, nsa_sel_k_norm, nsa_win_k_norm, cmp_pe_k, cmp_w1_k, cmp_w2_k, cmp_pe_v, cmp_w1_v, cmp_w2_v, o_w_out, o_norm_ffn, router_w, router_b, moe_w_gate, moe_w_up, moe_w_down):
    B, S, d = x_prompt.shape
    D, T, _ = x_sample.shape
    NP = page_table.shape[1]
    past = NP * PAGE
    w_buf = cache_nsa_win_k.shape[2]
    assert T == 1 and w_buf == WINDOW and S >= WINDOW and cache_fox_k.shape[2] == PAGE
    G = NSA_GROUPS
    lbs = jnp.cumsum(jax.nn.softmax(hgrn_lb_logits.astype(F32), axis=0), axis=0)
    xp, xd = x_prompt.reshape(B * S, d), x_sample.reshape(D, d)
    tm_p = min(TM, S)

    li = 0
    p = _even_params(e_norm_mix[li], e_w_in, li, lbs[li], hgrn_out_norm[li], fox_f_bias[li], fox_q_norm[li],
                     fox_k_norm[li], e_w_out[li], e_norm_ffn[li])
    hg, fq, fk, fv, fl = _even_proj(xd, p, TM_STEP)
    oa, st_d = _hgrn_step(hg, p['lb'], p['out_norm'], state_hgrn[li].astype(F32))
    heads = lambda a: a.reshape(D, H_FOX, HEAD_DIM)
    of = _fox_step(page_table, heads(fq), heads(fk), heads(fv), fl[:, :H_FOX],
                   cache_fox_k[li], cache_fox_v[li], cache_fox_logf[li], PG=16)
    xd, ffn_bf16 = _even_finish(xd, oa, of.reshape(D, H_FOX * HEAD_DIM), p,
                                (ffn_w_gate[li], ffn_w_up[li], ffn_w_down[li]), TM_STEP, emit=True)
    fox_d = (fk.reshape(1, D, 1, H_FOX, HEAD_DIM), fv.reshape(1, D, 1, H_FOX, HEAD_DIM),
             fl[:, :H_FOX].reshape(1, D, 1, H_FOX))

    hg, fq, fk, fv, fl = _even_proj(xp, p, tm_p)
    aug_q, aug_k = _seq_cumsum(fl, B, S, tc=tm_p)
    oa, st_p = _hgrn_prompt(hg, p['lb'], p['out_norm'], B, S, T=min(256, S), c=32)
    fvt = fv.reshape(B, S, H_FOX, HEAD_DIM).transpose(0, 2, 3, 1)
    of = _fox_prompt(fq, fk, fvt, aug_q, aug_k, B, S, tq=min(1024, S), tk=tm_p)
    xp = _even_finish(xp, oa, of, p, ffn_bf16, tm_p)
    fox_p = (fk.reshape(1, B, S, H_FOX, HEAD_DIM), fv.reshape(1, B, S, H_FOX, HEAD_DIM),
             fl[:, :H_FOX].reshape(1, B, S, H_FOX))

    p = _odd_params(o_norm_mix[li], o_w_in, li, nsa_q_norm[li], nsa_cmp_k_norm[li], nsa_sel_k_norm[li],
                    nsa_win_k_norm[li], cmp_pe_k[li], cmp_w1_k[li], cmp_w2_k[li], cmp_pe_v[li], cmp_w1_v[li],
                    cmp_w2_v[li], o_w_out[li], o_norm_ffn[li], router_w[li], router_b[li])
    qn, qr, ck, cv, sk, sv, wk, wv, gt = _odd_proj(xd, p, jnp.full((TM_STEP,), past), TM_STEP)
    rows16 = lambda a, w: jnp.pad(a.reshape(D, G, NSA_R, w), ((0, 0), (0, 0), (0, 16 - NSA_R), (0, 0)))
    gate_rows = rows16(gt[:, :3 * H_NSA], 3)
    q16n, q16r = rows16(qn, HEAD_DIM), rows16(qr, HEAD_DIM)
    p1k, p2k = _cmp_pages(page_table, cache_nsa_cmp_k[li], p['cwk'], PG=16)
    p1v, p2v = _cmp_pages(page_table, cache_nsa_cmp_v[li], p['cwv'], PG=16)
    o_cmp, idx = _cmpsel_step(q16n, p1k, p2k, p1v, p2v, p['cwk'], p['cwv'], gate_rows, past)
    new = lambda a: a.reshape(D, G, 1, HEAD_DIM)
    o_sel = _sel_step(page_table, idx[:, :, 0, :TOPN], q16r, new(sk), new(sv),
                      cache_nsa_sel_k[li], cache_nsa_sel_v[li], gate_rows)
    kvd = lambda a: a.reshape(D, 1, G, HEAD_DIM)
    win_k = jnp.concatenate([cache_nsa_win_k[li], kvd(wk)], axis=1)[:, -w_buf:]
    win_v = jnp.concatenate([cache_nsa_win_v[li], kvd(wv)], axis=1)[:, -w_buf:]
    o_win = _win_step(q16r, win_k.reshape(D, w_buf, G * HEAD_DIM), win_v.reshape(D, w_buf, G * HEAD_DIM), gate_rows)
    unrow = lambda a: a[:, :, :NSA_R].reshape(D, H_NSA * HEAD_DIM)
    xd, moe_bf16 = _odd_finish(xd, [unrow(o_cmp), unrow(o_sel), unrow(o_win)], p,
                               (moe_w_gate[li], moe_w_up[li], moe_w_down[li]), TM_STEP, emit=True)
    kv1 = lambda a: a.reshape(1, D, 1, G, HEAD_DIM)
    nsa_d = (kv1(ck), kv1(cv), kv1(sk), kv1(sv), win_k[None], win_v[None])

    qn, qr, ck, cv, sk, sv, wk, wv, gt = _odd_proj(xp, p, jnp.arange(S), tm_p)
    seq = lambda a: a.reshape(B, S, -1)
    gates_g = jnp.pad(gt[:, :3 * H_NSA].reshape(B, S, G, 3 * NSA_R).transpose(0, 2, 1, 3),
                      ((0, 0), (0, 0), (0, 0), (0, 16 - 3 * NSA_R)))
    kc = _compress_prompt(seq(ck), p['cwk'], B, S)
    vc = _compress_prompt(seq(cv), p['cwv'], B, S)
    tq = min(256, S)
    o_cmp, selneg = _cmpsel_prompt(seq(qn), kc, vc, gates_g, B, S, Tq=tq)
    keys_last = lambda a: a.reshape(B, S, G, HEAD_DIM).transpose(0, 2, 3, 1)
    o_sel = _sel_prompt(seq(qr), selneg, seq(sk), keys_last(sv), gates_g, B, S, Tq=tq, tk=min(512, S))
    o_win = _win_prompt(seq(qr), seq(wk), keys_last(wv), gates_g, B, S, Tq=tq)
    flat = lambda a: a.reshape(B * S, -1)
    xp = _odd_finish(xp, [flat(o_cmp), flat(o_sel), flat(o_win)], p, moe_bf16, tm_p)
    kv = lambda a: a.reshape(1, B, S, G, HEAD_DIM)
    nsa_p = (kv(ck), kv(cv), kv(sk), kv(sv), kv(wk)[:, :, S - WINDOW:], kv(wv)[:, :, S - WINDOW:])

    return (xp.reshape(B, S, d), xd.reshape(D, T, d), st_p[None].astype(state_hgrn.dtype),
            st_d[None].astype(state_hgrn.dtype), *fox_p, *fox_d, *nsa_p, *nsa_d)
```
